```python
import jax, jax.numpy as jnp
from jax import lax
import numpy as np

D_MODEL = 1024
BATCH = 8
SEQ = 2048
DEPTH = 2

D_RNN = 512
N_RNN_BLOCKS = 8
RNN_BLOCK = D_RNN // N_RNN_BLOCKS
CONV_WIDTH = 4
LRU_C = 8.0
N_HEADS = 8
HEAD_DIM = 64
D_ATTN = N_HEADS * HEAD_DIM
N_IDX_HEADS = 8
IDX_DIM = 64
MAX_TOPK = 256
Q_BLOCK = 128
D_MIX = D_RNN + D_ATTN
SPLITS = (D_RNN, D_RNN, D_ATTN, HEAD_DIM, HEAD_DIM, D_ATTN, N_IDX_HEADS * IDX_DIM, IDX_DIM, N_IDX_HEADS)
D_IN = sum(SPLITS)
EPS = 1e-6

kernel_name = "hymba_rglru_dsa_adaln_block"


def rmsnorm(x, g):
    xf = x.astype(jnp.float32)
    y = xf * lax.rsqrt(jnp.mean(xf * xf, axis=-1, keepdims=True) + EPS)
    return (y * g.astype(jnp.float32)).astype(x.dtype)


def causal_depthwise_conv(x, w, b):
    C = x.shape[-1]
    y = lax.conv_general_dilated(
        x, w[:, None, :].astype(x.dtype), window_strides=(1,), padding=[(CONV_WIDTH - 1, 0)],
        dimension_numbers=("NWC", "WIO", "NWC"), feature_group_count=C)
    return y + b


def rg_lru(x, w_x, b_x, w_a, b_a, a_param):
    B, S, D = x.shape
    xb = x.reshape(B, S, N_RNN_BLOCKS, RNN_BLOCK)
    gate_x = jax.nn.sigmoid((jnp.einsum("bsnc,ncd->bsnd", xb, w_x).reshape(B, S, D) + b_x).astype(jnp.float32))
    gate_a = jax.nn.sigmoid((jnp.einsum("bsnc,ncd->bsnd", xb, w_a).reshape(B, S, D) + b_a).astype(jnp.float32))
    log_a = -LRU_C * gate_a * jax.nn.softplus(-a_param.astype(jnp.float32))
    a = jnp.exp(log_a)
    mult = jnp.sqrt(-jnp.expm1(2.0 * log_a))
    u = mult * gate_x * x.astype(jnp.float32)

    def combine(left, right):
        a1, b1 = left
        a2, b2 = right
        return a1 * a2, a2 * b1 + b2

    _, h = lax.associative_scan(combine, (a, u), axis=1)
    return h.astype(x.dtype)


def dsa_attention(q, k, v, qi, ki, wi):
    B, S, H, Dh = q.shape
    topk = min(MAX_TOPK, S // 4)
    n_blocks = S // Q_BLOCK
    key_pos = jnp.arange(S)
    kif = ki.astype(jnp.float32)

    def block(i):
        start = i * Q_BLOCK
        qb = lax.dynamic_slice_in_dim(q, start, Q_BLOCK, axis=1)
        qib = lax.dynamic_slice_in_dim(qi, start, Q_BLOCK, axis=1).astype(jnp.float32)
        wib = lax.dynamic_slice_in_dim(wi, start, Q_BLOCK, axis=1).astype(jnp.float32)
        q_pos = start + jnp.arange(Q_BLOCK)
        logits = jnp.einsum("bthd,bsd->bths", qib, kif) * (IDX_DIM ** -0.5)
        score = jnp.einsum("bths,bth->bts", jax.nn.relu(logits), wib * (N_IDX_HEADS ** -0.5))
        causal = key_pos[None, :] <= q_pos[:, None]
        score = jnp.where(causal[None], score, -jnp.inf)
        _, idx = lax.top_k(score, topk)
        valid = idx <= q_pos[None, :, None]
        kg = jax.vmap(lambda kk, ii: kk[ii])(k, idx)
        vg = jax.vmap(lambda vv, ii: vv[ii])(v, idx)
        s = jnp.einsum("bthd,btkd->bthk", qb, kg).astype(jnp.float32) * (Dh ** -0.5)
        s = jnp.where(valid[:, :, None, :], s, -jnp.inf)
        p = jax.nn.softmax(s, axis=-1).astype(vg.dtype)
        return jnp.einsum("bthk,btkd->bthd", p, vg)

    out = lax.map(block, jnp.arange(n_blocks))
    return out.transpose(1, 0, 2, 3, 4).reshape(B, S, H * Dh)


def setup_inputs(seed: int = 0) -> dict:
    key = jax.random.key(seed)
    ks = jax.random.split(key, 16)
    f32 = jnp.float32
    x = jax.random.normal(ks[0], (BATCH, SEQ, D_MODEL), f32)
    c = jax.random.normal(ks[1], (BATCH, D_MODEL), f32)
    norm_g = 1.0 + 0.02 * jax.random.normal(ks[2], (DEPTH, D_MODEL), f32)
    ada_w = jax.random.normal(ks[3], (DEPTH, D_MODEL, 3 * D_MODEL), f32) * (0.5 * D_MODEL ** -0.5)
    ada_b = 0.02 * jax.random.normal(ks[4], (DEPTH, 3 * D_MODEL), f32)
    w_in = jax.random.normal(ks[5], (DEPTH, D_MODEL, D_IN), f32) * (D_MODEL ** -0.5)
    conv_w = jax.random.normal(ks[6], (DEPTH, CONV_WIDTH, D_RNN), f32) * (CONV_WIDTH ** -0.5)
    conv_b = 0.02 * jax.random.normal(ks[7], (DEPTH, D_RNN), f32)
    lru_wx = jax.random.normal(ks[8], (DEPTH, N_RNN_BLOCKS, RNN_BLOCK, RNN_BLOCK), f32) * (RNN_BLOCK ** -0.5)
    lru_bx = 0.02 * jax.random.normal(ks[9], (DEPTH, D_RNN), f32)
    lru_wa = jax.random.normal(ks[10], (DEPTH, N_RNN_BLOCKS, RNN_BLOCK, RNN_BLOCK), f32) * (RNN_BLOCK ** -0.5)
    lru_ba = 0.02 * jax.random.normal(ks[11], (DEPTH, D_RNN), f32)
    a_c = jax.random.uniform(ks[12], (DEPTH, D_RNN), f32, 0.9, 0.999)
    a0 = a_c ** (1.0 / LRU_C)
    lru_a = jnp.log(a0) - jnp.log1p(-a0)
    w_out = jax.random.normal(ks[13], (DEPTH, D_MIX, D_MODEL), f32) * (D_MIX ** -0.5)
    final_g = 1.0 + 0.02 * jax.random.normal(ks[14], (D_MODEL,), f32)
    return {"x": x, "c": c, "norm_g": norm_g, "ada_w": ada_w, "ada_b": ada_b, "w_in": w_in,
            "conv_w": conv_w, "conv_b": conv_b, "lru_wx": lru_wx, "lru_bx": lru_bx,
            "lru_wa": lru_wa, "lru_ba": lru_ba, "lru_a": lru_a, "w_out": w_out, "final_g": final_g}


def reference(x, c, norm_g, ada_w, ada_b, w_in, conv_w, conv_b, lru_wx, lru_bx, lru_wa, lru_ba, lru_a, w_out, final_g):
    B, S, _ = x.shape
    split_points = np.cumsum(SPLITS)[:-1].tolist()
    c_act = jax.nn.silu(c)
    for l in range(DEPTH):
        mod = c_act @ ada_w[l] + ada_b[l]
        shift, scale, gate = jnp.split(mod, 3, axis=-1)
        h = rmsnorm(x, norm_g[l]) * (1.0 + scale[:, None, :]) + shift[:, None, :]
        z = h @ w_in[l]
        xr, gr, q, k, v, ga, qi, ki, wi = jnp.split(z, split_points, axis=-1)
        xr = causal_depthwise_conv(xr, conv_w[l], conv_b[l])
        y_r = rg_lru(xr, lru_wx[l], lru_bx[l], lru_wa[l], lru_ba[l], lru_a[l]) * jax.nn.silu(gr)
        y_a = dsa_attention(q.reshape(B, S, N_HEADS, HEAD_DIM), k, v,
                            qi.reshape(B, S, N_IDX_HEADS, IDX_DIM), ki, wi) * jax.nn.silu(ga)
        y = jnp.concatenate([y_r, y_a], axis=-1) @ w_out[l]
        x = x + gate[:, None, :] * y
    return rmsnorm(x, final_g)
```

```python
import functools

import jax
import jax.numpy as jnp
from jax import lax
from jax.experimental import pallas as pl
from jax.experimental.pallas import tpu as pltpu

D_MODEL = 1024
D_RNN = 512
N_RNN_BLOCKS = 8
RNN_BLOCK = D_RNN // N_RNN_BLOCKS
CONV_WIDTH = 4
LRU_C = 8.0
N_HEADS = 8
HEAD_DIM = 64
D_ATTN = N_HEADS * HEAD_DIM
N_IDX_HEADS = 8
IDX_DIM = 64
MAX_TOPK = 256
EPS = 1e-6

Z_RNN = 0
Z_QQ = 1024
Z_GA = 2048
Z_KV = 2560
Z_COLS = 2816
WI_LANE = 64

ROW_TILE = 256
Q_TILE = 256
K_TILE = 512
VT_ROWS = 80
MASKED = -1e30
PLAIN_BISECT_ROUNDS = 10
VMEM_LIMIT = 48 * 1024 * 1024


def _silu(x):
    return x * jax.nn.sigmoid(x)


def _rms(x, g):
    return x * lax.rsqrt(jnp.mean(x * x, axis=-1, keepdims=True) + EPS) * g


def _mod_kernel(c_ref, w_ref, b_ref, o_ref):
    c_act = _silu(c_ref[...])
    o_ref[...] = jnp.dot(c_act, w_ref[...], precision=lax.Precision.HIGHEST,
                         preferred_element_type=jnp.float32) + b_ref[...]


def _modulation(c, ada_w, ada_b):
    depth = ada_w.shape[0]
    batch = c.shape[0]
    out = pl.pallas_call(
        _mod_kernel,
        grid=(depth, 3),
        in_specs=[
            pl.BlockSpec((batch, D_MODEL), lambda l, j: (0, 0)),
            pl.BlockSpec((None, D_MODEL, D_MODEL), lambda l, j: (l, 0, j)),
            pl.BlockSpec((None, None, 1, D_MODEL), lambda l, j: (l, j, 0, 0)),
        ],
        out_specs=pl.BlockSpec((None, None, batch, D_MODEL), lambda l, j: (l, j, 0, 0)),
        out_shape=jax.ShapeDtypeStruct((depth, 3, batch, D_MODEL), jnp.float32),
        compiler_params=pltpu.CompilerParams(vmem_limit_bytes=VMEM_LIMIT),
        name="adaln_mod",
    )(c, ada_w, ada_b.reshape(depth, 3, 1, D_MODEL))
    return out.transpose(0, 2, 1, 3)


def _inproj_kernel(x_ref, mod_ref, g_ref, w_ref, rnn_ref, qq_ref, ga_ref, kv_ref, wi_ref):
    x = x_ref[...]
    shift = mod_ref[0:1, :]
    scale = mod_ref[1:2, :]
    h = _rms(x, g_ref[...]) * (1.0 + scale) + shift
    z = jnp.dot(h.astype(jnp.bfloat16), w_ref[...], preferred_element_type=jnp.float32)
    rnn_ref[...] = z[:, Z_RNN:Z_QQ]
    qq_ref[...] = z[:, Z_QQ:Z_GA].astype(jnp.bfloat16)
    ga_ref[...] = z[:, Z_GA:Z_KV]
    kv_ref[...] = z[:, Z_KV:Z_COLS].astype(jnp.bfloat16)
    wi_ref[...] = z[:, Z_COLS - 128:Z_COLS]


def _in_projection(x, mod_l, norm_g, w_perm):
    batch, seq, _ = x.shape
    row = lambda b, i: (b, i, 0)
    return pl.pallas_call(
        _inproj_kernel,
        grid=(batch, seq // ROW_TILE),
        in_specs=[
            pl.BlockSpec((None, ROW_TILE, D_MODEL), row),
            pl.BlockSpec((None, 3, D_MODEL), lambda b, i: (b, 0, 0)),
            pl.BlockSpec((1, D_MODEL), lambda b, i: (0, 0)),
            pl.BlockSpec((D_MODEL, Z_COLS), lambda b, i: (0, 0)),
        ],
        out_specs=[
            pl.BlockSpec((None, ROW_TILE, 1024), row),
            pl.BlockSpec((None, ROW_TILE, 1024), row),
            pl.BlockSpec((None, ROW_TILE, 512), row),
            pl.BlockSpec((None, ROW_TILE, 256), row),
            pl.BlockSpec((None, ROW_TILE, 128), row),
        ],
        out_shape=[
            jax.ShapeDtypeStruct((batch, seq, 1024), jnp.float32),
            jax.ShapeDtypeStruct((batch, seq, 1024), jnp.bfloat16),
            jax.ShapeDtypeStruct((batch, seq, 512), jnp.float32),
            jax.ShapeDtypeStruct((batch, seq, 256), jnp.bfloat16),
            jax.ShapeDtypeStruct((batch, seq, 128), jnp.float32),
        ],
        compiler_params=pltpu.CompilerParams(
            dimension_semantics=("parallel", "parallel"), vmem_limit_bytes=VMEM_LIMIT),
        name="in_proj",
    )(x, mod_l, norm_g.reshape(1, D_MODEL), w_perm)


def _lru_kernel(rnn_ref, cw_ref, cb_ref, wg_ref, bg_ref, ap_ref, y_ref, xbuf_ref, h_ref):
    ts = ROW_TILE

    @pl.when(pl.program_id(1) == 0)
    def _():
        xbuf_ref[0:8, :] = jnp.zeros((8, D_RNN), jnp.float32)
        h_ref[...] = jnp.zeros((1, D_RNN), jnp.float32)

    xr = rnn_ref[:, 0:D_RNN]
    gr = rnn_ref[:, D_RNN:2 * D_RNN]
    xbuf_ref[8:8 + ts, :] = xr
    xc = cw_ref[3:4, :] * xr + cb_ref[...]
    for k in range(CONV_WIDTH - 1):
        back = CONV_WIDTH - 1 - k
        xc = xc + cw_ref[k:k + 1, :] * xbuf_ref[8 - back:8 - back + ts, :]
    xbuf_ref[0:8, :] = xbuf_ref[ts:ts + 8, :]

    gates = jnp.dot(xc.astype(jnp.bfloat16), wg_ref[...], preferred_element_type=jnp.float32) + bg_ref[...]
    gate_x = jax.nn.sigmoid(gates[:, 0:D_RNN])
    gate_a = jax.nn.sigmoid(gates[:, D_RNN:2 * D_RNN])
    neg_ap = -ap_ref[...]
    softplus = jnp.maximum(neg_ap, 0.0) + jnp.log(1.0 + jnp.exp(-jnp.abs(neg_ap)))
    log_a = (-LRU_C) * gate_a * softplus
    a = jnp.exp(log_a)
    mult = jnp.sqrt(1.0 - a * a)
    u = mult * gate_x * xc

    row = lax.broadcasted_iota(jnp.int32, (ts, D_RNN), 0)
    d = 1
    while d < ts:
        a_sh = pltpu.roll(a, d, 0)
        u_sh = pltpu.roll(u, d, 0)
        keep = row >= d
        u = jnp.where(keep, u + a * u_sh, u)
        a = jnp.where(keep, a * a_sh, a)
        d *= 2
    h = u + a * h_ref[...]
    h_ref[...] = h[ts - 1:ts, :]
    y_ref[...] = (h * _silu(gr)).astype(y_ref.dtype)


def _rg_lru(rnn, conv_w, conv_b, w_gates, b_gates, a_param):
    batch, seq, _ = rnn.shape
    const = lambda b, i: (0, 0)
    return pl.pallas_call(
        _lru_kernel,
        grid=(batch, seq // ROW_TILE),
        in_specs=[
            pl.BlockSpec((None, ROW_TILE, 2 * D_RNN), lambda b, i: (b, i, 0)),
            pl.BlockSpec((CONV_WIDTH, D_RNN), const),
            pl.BlockSpec((1, D_RNN), const),
            pl.BlockSpec((D_RNN, 2 * D_RNN), const),
            pl.BlockSpec((1, 2 * D_RNN), const),
            pl.BlockSpec((1, D_RNN), const),
        ],
        out_specs=pl.BlockSpec((None, ROW_TILE, D_RNN), lambda b, i: (b, i, 0)),
        out_shape=jax.ShapeDtypeStruct((batch, seq, D_RNN), jnp.bfloat16),
        scratch_shapes=[
            pltpu.VMEM((ROW_TILE + 8, D_RNN), jnp.float32),
            pltpu.VMEM((1, D_RNN), jnp.float32),
        ],
        compiler_params=pltpu.CompilerParams(
            dimension_semantics=("parallel", "arbitrary"), vmem_limit_bytes=VMEM_LIMIT),
        name="rg_lru",
    )(rnn, conv_w, conv_b.reshape(1, D_RNN), w_gates, b_gates, a_param.reshape(1, D_RNN))


def _attn_kernel(qq_ref, kv_ref, vt_ref, wi_ref, ga_ref, tri_ref, y_ref,
                 qt_ref, qit_ref, sc_ref, bias_ref, eqb_ref, s_ref, m_ref, mnew_ref, acc_ref, out_ref):
    tq, tk = Q_TILE, K_TILE
    topk = float(MAX_TOPK)
    qb = pl.program_id(1)
    n_chunks = ((qb + 1) * tq + tk - 1) // tk

    for h in range(N_HEADS):
        q_h = qq_ref[:, h * HEAD_DIM:(h + 1) * HEAD_DIM].astype(jnp.float32) * (HEAD_DIM ** -0.5)
        qt_ref[h] = q_h.T.astype(jnp.bfloat16)
        qi_h = qq_ref[:, D_ATTN + h * IDX_DIM:D_ATTN + (h + 1) * IDX_DIM].astype(jnp.float32)
        qit_ref[h] = qi_h.T.astype(jnp.bfloat16)
    w_t = wi_ref[...].T[WI_LANE:WI_LANE + N_IDX_HEADS, :] * ((IDX_DIM ** -0.5) * (N_IDX_HEADS ** -0.5))

    q_pos = qb * tq + lax.broadcasted_iota(jnp.int32, (tk, tq), 1)
    key_iota = lax.broadcasted_iota(jnp.int32, (tk, tq), 0)

    def score_chunk(c, carry):
        rmax, rmin = carry
        k0 = pl.multiple_of(c * tk, tk)
        ki = kv_ref[pl.ds(k0, tk), 128:128 + IDX_DIM]
        score = jnp.zeros((tk, tq), jnp.float32)
        for h in range(N_IDX_HEADS):
            logits = jnp.dot(ki, qit_ref[h], preferred_element_type=jnp.float32)
            score = score + jnp.maximum(logits, 0.0) * w_t[h:h + 1, :]
        causal = (k0 + key_iota) <= q_pos
        sc_ref[pl.ds(k0, tk), :] = jnp.where(causal, score, -jnp.inf)
        rmax = jnp.maximum(rmax, jnp.max(jnp.where(causal, score, -jnp.inf), axis=0, keepdims=True))
        rmin = jnp.minimum(rmin, jnp.min(jnp.where(causal, score, jnp.inf), axis=0, keepdims=True))
        return rmax, rmin

    rmax, rmin = lax.fori_loop(
        0, n_chunks, score_chunk,
        (jnp.full((1, tq), -jnp.inf, jnp.float32), jnp.full((1, tq), jnp.inf, jnp.float32)))

    groups = tk // 8

    def probe(mid, with_ge, with_snap):
        def body(c, carry):
            k0 = pl.multiple_of(c * tk, tk)
            s = sc_ref[pl.ds(k0, tk), :].reshape(groups, 8, tq)
            gt = s > mid
            out = [carry[0] + jnp.sum(jnp.where(gt, 1.0, 0.0), axis=0)]
            if with_ge:
                out.append(carry[1] + jnp.sum(jnp.where(s >= mid, 1.0, 0.0), axis=0))
            if with_snap:
                out.append(jnp.minimum(carry[1], jnp.min(jnp.where(gt, s, jnp.inf), axis=0)))
                out.append(jnp.maximum(carry[2], jnp.max(jnp.where(gt, -jnp.inf, s), axis=0)))
            return tuple(out)
        zero = jnp.zeros((8, tq), jnp.float32)
        init = [zero]
        if with_ge:
            init.append(zero)
        if with_snap:
            init += [jnp.full((8, tq), jnp.inf, jnp.float32), jnp.full((8, tq), -jnp.inf, jnp.float32)]
        res = lax.fori_loop(0, n_chunks, body, tuple(init))
        out = [jnp.sum(r, axis=0, keepdims=True) for r in res[:2 if with_ge else 1]]
        if with_snap:
            out.append(jnp.min(res[1], axis=0, keepdims=True))
            out.append(jnp.max(res[2], axis=0, keepdims=True))
        return out

    def update(state, mid, cnt, tie, new_lo, new_hi):
        lo, hi, clo, chi, act = state
        on = act > 0.5
        fin = jnp.logical_and(on, jnp.logical_or(cnt == topk, tie))
        go_on = jnp.logical_and(on, jnp.logical_not(fin))
        up = jnp.logical_and(go_on, cnt > topk)
        dn = jnp.logical_and(go_on, cnt < topk)
        lo = jnp.where(fin, mid, jnp.where(up, new_lo, lo))
        hi = jnp.where(fin, mid, jnp.where(dn, new_hi, hi))
        clo = jnp.where(up, cnt, clo)
        chi = jnp.where(jnp.logical_or(fin, dn), cnt, chi)
        act = jnp.where(jnp.logical_and(go_on, lo < hi), 1.0, 0.0)
        return lo, hi, clo, chi, act

    def midpoint(state):
        return 0.5 * state[0] + 0.5 * state[1]

    def interpolated(state):
        lo, hi, clo, chi, _ = state
        frac = (topk + 0.5 - chi) / jnp.maximum(clo - chi, 1.0)
        return jnp.minimum(jnp.maximum(hi - (hi - lo) * frac, lo), hi)

    n_valid = (q_pos[0:1, :] + 1).astype(jnp.float32)
    few = n_valid <= topk
    state = (rmin,
             jnp.where(few, -jnp.inf, rmax),
             n_valid,
             jnp.where(few, topk, 0.0),
             jnp.where(jnp.logical_or(few, rmin >= rmax), 0.0, 1.0))

    mid = jnp.minimum(jnp.maximum(0.0, state[0]), state[1])
    cnt, cge = probe(mid, True, False)
    state = update(state, mid, cnt, jnp.logical_and(cnt < topk, cge >= topk), mid, mid)

    def plain_cond(st):
        return jnp.logical_and(st[1] > 0.5, st[0] < PLAIN_BISECT_ROUNDS)

    def plain_body(st):
        state = st[2:]
        for pick in (interpolated, interpolated, midpoint):
            mid = pick(state)
            (cnt,) = probe(mid, False, False)
            state = update(state, mid, cnt, False, mid, mid)
        return (st[0] + 1, jnp.max(state[4])) + state

    st = lax.while_loop(plain_cond, plain_body, (jnp.int32(0), jnp.max(state[4])) + state)

    def snap_cond(st):
        return jnp.logical_and(st[1] > 0.5, st[0] < 4096)

    def snap_body(st):
        state = st[2:]
        mid = midpoint(state)
        mid = jnp.where(mid >= state[1], state[0], mid)
        cnt, above, below = probe(mid, False, True)
        state = update(state, mid, cnt, False, above, below)
        return (st[0] + 1, jnp.max(state[4])) + state

    st = lax.while_loop(snap_cond, snap_body, (jnp.int32(0),) + st[1:])
    thr = st[3]
    need = topk - st[5]
    any_tie = jnp.max(need) > 0.5

    m_ref[...] = jnp.full(m_ref.shape, MASKED, jnp.float32)
    acc_ref[...] = jnp.zeros(acc_ref.shape, jnp.float32)
    eqb_ref[...] = jnp.zeros(eqb_ref.shape, jnp.float32)

    def attend_chunk(c, _):
        k0 = pl.multiple_of(c * tk, tk)
        s_idx = sc_ref[pl.ds(k0, tk), :]
        bias_ref[...] = jnp.where(s_idx > thr, 0.0, MASKED)

        @pl.when(any_tie)
        def _():
            eq = s_idx == thr
            eq_f = jnp.where(eq, 1.0, 0.0)
            rank = jnp.dot(tri_ref[...], eq_f.astype(jnp.bfloat16),
                           preferred_element_type=jnp.float32) + eqb_ref[...]
            tie_bias = jnp.where(jnp.logical_and(eq, rank < need), 0.0, MASKED)
            bias_ref[...] = jnp.where(s_idx > thr, 0.0, tie_bias)
            eqb_ref[...] = eqb_ref[...] + jnp.sum(eq_f, axis=0, keepdims=True)

        kc = kv_ref[pl.ds(k0, tk), 0:HEAD_DIM]
        vt = vt_ref[c]
        for h in range(N_HEADS):
            s = jnp.dot(kc, qt_ref[h], preferred_element_type=jnp.float32) + bias_ref[...]
            s_ref[h] = s
            mnew_ref[h] = jnp.maximum(m_ref[h], jnp.max(s, axis=0, keepdims=True))
        for h in range(N_HEADS):
            m_new = mnew_ref[h]
            alpha = jnp.exp(m_ref[h] - m_new)
            p = jnp.exp(s_ref[h] - m_new)
            acc_ref[h] = alpha * acc_ref[h] + jnp.dot(
                vt, p.astype(jnp.bfloat16), preferred_element_type=jnp.float32)
            m_ref[h] = m_new
        return 0

    lax.fori_loop(0, n_chunks, attend_chunk, 0)

    for h in range(N_HEADS):
        rows = slice(h * HEAD_DIM, (h + 1) * HEAD_DIM)
        out_ref[rows, :] = acc_ref[h, 0:HEAD_DIM, :] / acc_ref[h, HEAD_DIM:HEAD_DIM + 1, :]
    y = out_ref[...].T * _silu(ga_ref[...])
    y_ref[...] = y.astype(y_ref.dtype)


def _dsa_attention(qq, kv, vt, wi, ga, tri):
    batch, seq, _ = qq.shape
    row = lambda b, i: (b, i, 0)
    return pl.pallas_call(
        _attn_kernel,
        grid=(batch, seq // Q_TILE),
        in_specs=[
            pl.BlockSpec((None, Q_TILE, 1024), row),
            pl.BlockSpec((None, seq, 256), lambda b, i: (b, 0, 0)),
            pl.BlockSpec((None, seq // K_TILE, VT_ROWS, K_TILE), lambda b, i: (b, 0, 0, 0)),
            pl.BlockSpec((None, Q_TILE, 128), row),
            pl.BlockSpec((None, Q_TILE, D_ATTN), row),
            pl.BlockSpec((K_TILE, K_TILE), lambda b, i: (0, 0)),
        ],
        out_specs=pl.BlockSpec((None, Q_TILE, D_ATTN), row),
        out_shape=jax.ShapeDtypeStruct((batch, seq, D_ATTN), jnp.bfloat16),
        scratch_shapes=[
            pltpu.VMEM((N_HEADS, HEAD_DIM, Q_TILE), jnp.bfloat16),
            pltpu.VMEM((N_IDX_HEADS, IDX_DIM, Q_TILE), jnp.bfloat16),
            pltpu.VMEM((seq, Q_TILE), jnp.float32),
            pltpu.VMEM((K_TILE, Q_TILE), jnp.float32),
            pltpu.VMEM((1, Q_TILE), jnp.float32),
            pltpu.VMEM((N_HEADS, K_TILE, Q_TILE), jnp.float32),
            pltpu.VMEM((N_HEADS, 1, Q_TILE), jnp.float32),
            pltpu.VMEM((N_HEADS, 1, Q_TILE), jnp.float32),
            pltpu.VMEM((N_HEADS, VT_ROWS, Q_TILE), jnp.float32),
            pltpu.VMEM((D_ATTN, Q_TILE), jnp.float32),
        ],
        compiler_params=pltpu.CompilerParams(
            dimension_semantics=("parallel", "parallel"), vmem_limit_bytes=VMEM_LIMIT),
        name="dsa_attention",
    )(qq, kv, vt, wi, ga, tri)


def _outproj_kernel(yr_ref, ya_ref, x_ref, mod_ref, wr_ref, wa_ref, g_ref, o_ref, *, final_norm):
    y = jnp.dot(yr_ref[...], wr_ref[...], preferred_element_type=jnp.float32)
    y = y + jnp.dot(ya_ref[...], wa_ref[...], preferred_element_type=jnp.float32)
    x_new = x_ref[...] + mod_ref[2:3, :] * y
    if final_norm:
        x_new = _rms(x_new, g_ref[...])
    o_ref[...] = x_new


def _out_projection(yr, ya, x, mod_l, w_r, w_a, final_g, final_norm):
    batch, seq, _ = x.shape
    row = lambda b, i: (b, i, 0)
    const = lambda b, i: (0, 0)
    return pl.pallas_call(
        functools.partial(_outproj_kernel, final_norm=final_norm),
        grid=(batch, seq // ROW_TILE),
        in_specs=[
            pl.BlockSpec((None, ROW_TILE, D_RNN), row),
            pl.BlockSpec((None, ROW_TILE, D_ATTN), row),
            pl.BlockSpec((None, ROW_TILE, D_MODEL), row),
            pl.BlockSpec((None, 3, D_MODEL), lambda b, i: (b, 0, 0)),
            pl.BlockSpec((D_RNN, D_MODEL), const),
            pl.BlockSpec((D_ATTN, D_MODEL), const),
            pl.BlockSpec((1, D_MODEL), const),
        ],
        out_specs=pl.BlockSpec((None, ROW_TILE, D_MODEL), row),
        out_shape=jax.ShapeDtypeStruct((batch, seq, D_MODEL), jnp.float32),
        compiler_params=pltpu.CompilerParams(
            dimension_semantics=("parallel", "parallel"), vmem_limit_bytes=VMEM_LIMIT),
        name="out_proj",
    )(yr, ya, x, mod_l, w_r, w_a, final_g.reshape(1, D_MODEL))


def _permute_w_in(w):
    xr, gr, q, k, v, ga, qi, ki, wi = jnp.split(
        w, [512, 1024, 1536, 1600, 1664, 2176, 2688, 2752], axis=-1)
    pad = jnp.zeros((w.shape[0], Z_COLS - 2760), w.dtype)
    return jnp.concatenate([xr, gr, q, qi, ga, k, v, ki, wi, pad], axis=-1).astype(jnp.bfloat16)


def _block_diag(w):
    n, c, d = w.shape
    eye = jnp.eye(n, dtype=w.dtype)
    return (eye[:, None, :, None] * w[:, :, None, :]).reshape(n * c, n * d)


def kernel(x, c, norm_g, ada_w, ada_b, w_in, conv_w, conv_b, lru_wx, lru_bx, lru_wa, lru_ba, lru_a, w_out, final_g):
    depth = w_in.shape[0]
    batch, seq, _ = x.shape
    mod = _modulation(c, ada_w, ada_b)
    idx = jnp.arange(K_TILE)
    tri = (idx[None, :] < idx[:, None]).astype(jnp.bfloat16)
    ones_rows = jnp.zeros((batch, seq // K_TILE, VT_ROWS - HEAD_DIM, K_TILE), jnp.bfloat16).at[:, :, 0, :].set(1.0)
    for l in range(depth):
        w_perm = _permute_w_in(w_in[l])
        w_gates = jnp.concatenate([_block_diag(lru_wx[l]), _block_diag(lru_wa[l])], axis=-1).astype(jnp.bfloat16)
        b_gates = jnp.concatenate([lru_bx[l], lru_ba[l]]).reshape(1, 2 * D_RNN)
        w_o = w_out[l].astype(jnp.bfloat16)
        rnn, qq, ga, kv, wi = _in_projection(x, mod[l], norm_g[l], w_perm)
        y_r = _rg_lru(rnn, conv_w[l], conv_b[l], w_gates, b_gates, lru_a[l])
        vt = kv[:, :, HEAD_DIM:2 * HEAD_DIM].reshape(batch, seq // K_TILE, K_TILE, HEAD_DIM).swapaxes(2, 3)
        vt = jnp.concatenate([vt, ones_rows], axis=2)
        y_a = _dsa_attention(qq, kv, vt, wi, ga, tri)
        x = _out_projection(y_r, y_a, x, mod[l], w_o[:D_RNN], w_o[D_RNN:], final_g, l == depth - 1)
    return x
```

```python
import functools

import jax
import jax.numpy as jnp
from jax import lax
from jax.experimental import pallas as pl
from jax.experimental.pallas import tpu as pltpu

D_MODEL = 1024
D_RNN = 512
N_RNN_BLOCKS = 8
RNN_BLOCK = D_RNN // N_RNN_BLOCKS
CONV_WIDTH = 4
LRU_C = 8.0
N_HEADS = 8
HEAD_DIM = 64
D_ATTN = N_HEADS * HEAD_DIM
N_IDX_HEADS = 8
IDX_DIM = 64
MAX_TOPK = 256
EPS = 1e-6

Z_RNN = 0
Z_QQ = 1024
Z_GA = 2048
Z_KV = 2560
Z_COLS = 2816
WI_LANE = 64

ROW_TILE = 512
Q_TILE = 256
K_TILE = 512
VT_ROWS = 80
MASKED = -1e30
PROBES_PER_ROUND = 4
PLAIN_BISECT_ROUNDS = 8
PARTIAL_ROWS = 32
VMEM_LIMIT = 48 * 1024 * 1024


def _silu(x):
    return x * jax.nn.sigmoid(x)


def _rms(x, g):
    return x * lax.rsqrt(jnp.mean(x * x, axis=-1, keepdims=True) + EPS) * g


def _reduce_keys(x, op):
    keys, queries = x.shape
    part = op(x.reshape(keys // PARTIAL_ROWS, PARTIAL_ROWS, queries), axis=0)
    return op(part, axis=0, keepdims=True)


def _mod_kernel(c_ref, w_ref, b_ref, o_ref):
    c_act = _silu(c_ref[...])
    o_ref[...] = jnp.dot(c_act, w_ref[...], precision=lax.Precision.HIGHEST,
                         preferred_element_type=jnp.float32) + b_ref[...]


def _modulation(c, ada_w, ada_b):
    depth = ada_w.shape[0]
    batch = c.shape[0]
    out = pl.pallas_call(
        _mod_kernel,
        grid=(depth, 3),
        in_specs=[
            pl.BlockSpec((batch, D_MODEL), lambda l, j: (0, 0)),
            pl.BlockSpec((None, D_MODEL, D_MODEL), lambda l, j: (l, 0, j)),
            pl.BlockSpec((None, None, 1, D_MODEL), lambda l, j: (l, j, 0, 0)),
        ],
        out_specs=pl.BlockSpec((None, None, batch, D_MODEL), lambda l, j: (l, j, 0, 0)),
        out_shape=jax.ShapeDtypeStruct((depth, 3, batch, D_MODEL), jnp.float32),
        compiler_params=pltpu.CompilerParams(vmem_limit_bytes=VMEM_LIMIT),
        name="adaln_mod",
    )(c, ada_w, ada_b.reshape(depth, 3, 1, D_MODEL))
    return out.transpose(0, 2, 1, 3)


def _inproj_kernel(x_ref, mod_ref, g_ref, w_ref, cw_ref, cb_ref, wg_ref, bg_ref, ap_ref,
                   yr_ref, qq_ref, ga_ref, kv_ref, wi_ref, xbuf_ref, h_ref):
    @pl.when(pl.program_id(1) == 0)
    def _():
        xbuf_ref[0:8, :] = jnp.zeros((8, D_RNN), jnp.float32)
        h_ref[...] = jnp.zeros((1, D_RNN), jnp.float32)

    x = x_ref[...]
    shift = mod_ref[0:1, :]
    scale = mod_ref[1:2, :]
    h = _rms(x, g_ref[...]) * (1.0 + scale) + shift
    h = h.astype(jnp.bfloat16)
    z_rnn = jnp.dot(h, w_ref[:, Z_RNN:Z_QQ], preferred_element_type=jnp.float32)
    y_r = _rg_lru_tile(z_rnn[:, 0:D_RNN], z_rnn[:, D_RNN:2 * D_RNN],
                       cw_ref, cb_ref, wg_ref, bg_ref, ap_ref, xbuf_ref, h_ref)
    yr_ref[...] = y_r.astype(yr_ref.dtype)
    z = jnp.dot(h, w_ref[:, Z_QQ:Z_COLS], preferred_element_type=jnp.float32)
    qq_ref[...] = z[:, 0:Z_GA - Z_QQ].astype(jnp.bfloat16)
    ga_ref[...] = z[:, Z_GA - Z_QQ:Z_KV - Z_QQ]
    kv_ref[...] = z[:, Z_KV - Z_QQ:Z_COLS - Z_QQ].astype(jnp.bfloat16)
    wi_ref[...] = z[:, Z_COLS - Z_QQ - 128:Z_COLS - Z_QQ]


def _in_projection(x, mod_l, norm_g, w_perm, conv_w, conv_b, w_gates, b_gates, a_param):
    batch, seq, _ = x.shape
    row = lambda b, i: (b, i, 0)
    const = lambda b, i: (0, 0)
    return pl.pallas_call(
        _inproj_kernel,
        grid=(batch, seq // ROW_TILE),
        in_specs=[
            pl.BlockSpec((None, ROW_TILE, D_MODEL), row),
            pl.BlockSpec((None, 3, D_MODEL), lambda b, i: (b, 0, 0)),
            pl.BlockSpec((1, D_MODEL), const),
            pl.BlockSpec((D_MODEL, Z_COLS), const),
            pl.BlockSpec((CONV_WIDTH, D_RNN), const),
            pl.BlockSpec((1, D_RNN), const),
            pl.BlockSpec((D_RNN, 2 * D_RNN), const),
            pl.BlockSpec((1, 2 * D_RNN), const),
            pl.BlockSpec((1, D_RNN), const),
        ],
        out_specs=[
            pl.BlockSpec((None, ROW_TILE, D_RNN), row),
            pl.BlockSpec((None, ROW_TILE, 1024), row),
            pl.BlockSpec((None, ROW_TILE, 512), row),
            pl.BlockSpec((None, ROW_TILE, 256), row),
            pl.BlockSpec((None, ROW_TILE, 128), row),
        ],
        out_shape=[
            jax.ShapeDtypeStruct((batch, seq, D_RNN), jnp.bfloat16),
            jax.ShapeDtypeStruct((batch, seq, 1024), jnp.bfloat16),
            jax.ShapeDtypeStruct((batch, seq, 512), jnp.float32),
            jax.ShapeDtypeStruct((batch, seq, 256), jnp.bfloat16),
            jax.ShapeDtypeStruct((batch, seq, 128), jnp.float32),
        ],
        scratch_shapes=[
            pltpu.VMEM((ROW_TILE + 8, D_RNN), jnp.float32),
            pltpu.VMEM((1, D_RNN), jnp.float32),
        ],
        compiler_params=pltpu.CompilerParams(
            dimension_semantics=("parallel", "arbitrary"), vmem_limit_bytes=VMEM_LIMIT),
        name="in_proj_rg_lru",
    )(x, mod_l, norm_g.reshape(1, D_MODEL), w_perm, conv_w, conv_b.reshape(1, D_RNN),
      w_gates, b_gates, a_param.reshape(1, D_RNN))


def _rg_lru_tile(xr, gr, cw_ref, cb_ref, wg_ref, bg_ref, ap_ref, xbuf_ref, h_ref):
    ts = ROW_TILE
    xbuf_ref[8:8 + ts, :] = xr
    xc = cw_ref[3:4, :] * xr + cb_ref[...]
    for k in range(CONV_WIDTH - 1):
        back = CONV_WIDTH - 1 - k
        xc = xc + cw_ref[k:k + 1, :] * xbuf_ref[8 - back:8 - back + ts, :]
    xbuf_ref[0:8, :] = xbuf_ref[ts:ts + 8, :]

    gates = jnp.dot(xc.astype(jnp.bfloat16), wg_ref[...], preferred_element_type=jnp.float32) + bg_ref[...]
    gate_x = jax.nn.sigmoid(gates[:, 0:D_RNN])
    gate_a = jax.nn.sigmoid(gates[:, D_RNN:2 * D_RNN])
    neg_ap = -ap_ref[...]
    softplus = jnp.maximum(neg_ap, 0.0) + jnp.log(1.0 + jnp.exp(-jnp.abs(neg_ap)))
    log_a = (-LRU_C) * gate_a * softplus
    a = jnp.exp(log_a)
    mult = jnp.sqrt(1.0 - a * a)
    u = mult * gate_x * xc

    groups = ts // 8
    a = a.reshape(groups, 8, D_RNN)
    u = u.reshape(groups, 8, D_RNN)
    sub = lax.broadcasted_iota(jnp.int32, (groups, 8, D_RNN), 1)
    for d in (1, 2, 4):
        keep = sub >= d
        u = jnp.where(keep, u + a * pltpu.roll(u, d, 1), u)
        a = jnp.where(keep, a * pltpu.roll(a, d, 1), a)
    carry = h_ref[...]
    hs = []
    for g in range(groups):
        h_g = u[g] + a[g] * carry
        hs.append(h_g)
        carry = h_g[7:8, :]
    h_ref[...] = carry
    return jnp.concatenate(hs, axis=0) * _silu(gr)


def _attn_kernel(qq_ref, kv_ref, vt_ref, wi_ref, ga_ref, tri_ref, y_ref,
                 qt_ref, qit_ref, sc_ref, bias_ref, eqb_ref, s_ref, m_ref, mnew_ref, acc_ref, out_ref):
    tq, tk = Q_TILE, K_TILE
    topk = float(MAX_TOPK)
    qb = pl.program_id(1)
    n_chunks = ((qb + 1) * tq + tk - 1) // tk

    for h in range(N_HEADS):
        q_h = qq_ref[:, h * HEAD_DIM:(h + 1) * HEAD_DIM].astype(jnp.float32) * (HEAD_DIM ** -0.5)
        qt_ref[h] = q_h.T.astype(jnp.bfloat16)
        qi_h = qq_ref[:, D_ATTN + h * IDX_DIM:D_ATTN + (h + 1) * IDX_DIM].astype(jnp.float32)
        qit_ref[h] = qi_h.T.astype(jnp.bfloat16)
    w_t = wi_ref[...].T[WI_LANE:WI_LANE + N_IDX_HEADS, :] * ((IDX_DIM ** -0.5) * (N_IDX_HEADS ** -0.5))

    q_pos = qb * tq + lax.broadcasted_iota(jnp.int32, (tk, tq), 1)
    key_iota = lax.broadcasted_iota(jnp.int32, (tk, tq), 0)

    def score_chunk(c, carry):
        rmax, rmin = carry
        k0 = pl.multiple_of(c * tk, tk)
        ki = kv_ref[pl.ds(k0, tk), 128:128 + IDX_DIM]
        score = jnp.zeros((tk, tq), jnp.float32)
        for h in range(N_IDX_HEADS):
            logits = jnp.dot(ki, qit_ref[h], preferred_element_type=jnp.float32)
            score = score + jnp.maximum(logits, 0.0) * w_t[h:h + 1, :]
        causal = (k0 + key_iota) <= q_pos
        sc_ref[pl.ds(k0, tk), :] = jnp.where(causal, score, -jnp.inf)
        rmax = jnp.maximum(rmax, _reduce_keys(jnp.where(causal, score, -jnp.inf), jnp.max))
        rmin = jnp.minimum(rmin, _reduce_keys(jnp.where(causal, score, jnp.inf), jnp.min))
        return rmax, rmin

    rmax, rmin = lax.fori_loop(
        0, n_chunks, score_chunk,
        (jnp.full((1, tq), -jnp.inf, jnp.float32), jnp.full((1, tq), jnp.inf, jnp.float32)))

    def probe(mid, with_ge, with_snap):
        def body(c, carry):
            k0 = pl.multiple_of(c * tk, tk)
            s = sc_ref[pl.ds(k0, tk), :].reshape(tk // PARTIAL_ROWS, PARTIAL_ROWS, tq)
            gt = s > mid
            out = [carry[0] + jnp.sum(jnp.where(gt, 1.0, 0.0), axis=0)]
            if with_ge:
                out.append(carry[1] + jnp.sum(jnp.where(s >= mid, 1.0, 0.0), axis=0))
            if with_snap:
                out.append(jnp.minimum(carry[1], jnp.min(jnp.where(gt, s, jnp.inf), axis=0)))
                out.append(jnp.maximum(carry[2], jnp.max(jnp.where(gt, -jnp.inf, s), axis=0)))
            return tuple(out)
        zero = jnp.zeros((PARTIAL_ROWS, tq), jnp.float32)
        init = [zero]
        if with_ge:
            init.append(zero)
        if with_snap:
            init += [jnp.full((PARTIAL_ROWS, tq), jnp.inf, jnp.float32),
                     jnp.full((PARTIAL_ROWS, tq), -jnp.inf, jnp.float32)]
        res = lax.fori_loop(0, n_chunks, body, tuple(init))
        out = [jnp.sum(r, axis=0, keepdims=True) for r in res[:2 if with_ge else 1]]
        if with_snap:
            out.append(jnp.min(res[1], axis=0, keepdims=True))
            out.append(jnp.max(res[2], axis=0, keepdims=True))
        return out

    def update(state, mid, cnt, tie, new_lo, new_hi):
        lo, hi, chi, act = state
        on = act > 0.5
        fin = jnp.logical_and(on, jnp.logical_or(cnt == topk, tie))
        go_on = jnp.logical_and(on, jnp.logical_not(fin))
        up = jnp.logical_and(go_on, cnt > topk)
        dn = jnp.logical_and(go_on, cnt < topk)
        lo = jnp.where(fin, mid, jnp.where(up, new_lo, lo))
        hi = jnp.where(fin, mid, jnp.where(dn, new_hi, hi))
        chi = jnp.where(jnp.logical_or(fin, dn), cnt, chi)
        act = jnp.where(jnp.logical_and(go_on, lo < hi), 1.0, 0.0)
        return lo, hi, chi, act

    def midpoint(state):
        return 0.5 * state[0] + 0.5 * state[1]

    n_valid = (q_pos[0:1, :] + 1).astype(jnp.float32)
    few = n_valid <= topk
    state = (rmin,
             jnp.where(few, -jnp.inf, rmax),
             jnp.where(few, topk, 0.0),
             jnp.where(jnp.logical_or(few, rmin >= rmax), 0.0, 1.0))

    mid = jnp.minimum(jnp.maximum(0.0, state[0]), state[1])
    cnt, cge = probe(mid, True, False)
    state = update(state, mid, cnt, jnp.logical_and(cnt < topk, cge >= topk), mid, mid)

    def plain_cond(st):
        return jnp.logical_and(st[1] > 0.5, st[0] < PLAIN_BISECT_ROUNDS)

    def plain_body(st):
        state = st[2:]
        for _ in range(PROBES_PER_ROUND):
            mid = midpoint(state)
            (cnt,) = probe(mid, False, False)
            state = update(state, mid, cnt, False, mid, mid)
        return (st[0] + 1, jnp.max(state[3])) + state

    st = lax.while_loop(plain_cond, plain_body, (jnp.int32(0), jnp.max(state[3])) + state)

    def snap_cond(st):
        return jnp.logical_and(st[1] > 0.5, st[0] < 4096)

    def snap_body(st):
        state = st[2:]
        mid = midpoint(state)
        mid = jnp.where(mid >= state[1], state[0], mid)
        cnt, above, below = probe(mid, False, True)
        state = update(state, mid, cnt, False, above, below)
        return (st[0] + 1, jnp.max(state[3])) + state

    st = lax.while_loop(snap_cond, snap_body, (jnp.int32(0),) + st[1:])
    thr = st[3]
    need = topk - st[4]
    any_tie = jnp.max(need) > 0.5

    m_ref[...] = jnp.full(m_ref.shape, MASKED, jnp.float32)
    acc_ref[...] = jnp.zeros(acc_ref.shape, jnp.float32)
    eqb_ref[...] = jnp.zeros(eqb_ref.shape, jnp.float32)

    def attend_chunk(c, _):
        k0 = pl.multiple_of(c * tk, tk)
        s_idx = sc_ref[pl.ds(k0, tk), :]
        bias_ref[...] = jnp.where(s_idx > thr, 0.0, MASKED)

        @pl.when(any_tie)
        def _():
            eq = s_idx == thr
            eq_f = jnp.where(eq, 1.0, 0.0)
            rank = jnp.dot(tri_ref[...], eq_f.astype(jnp.bfloat16),
                           preferred_element_type=jnp.float32) + eqb_ref[...]
            tie_bias = jnp.where(jnp.logical_and(eq, rank < need), 0.0, MASKED)
            bias_ref[...] = jnp.where(s_idx > thr, 0.0, tie_bias)
            eqb_ref[...] = eqb_ref[...] + _reduce_keys(eq_f, jnp.sum)

        kc = kv_ref[pl.ds(k0, tk), 0:HEAD_DIM]
        vt = vt_ref[c]
        for h in range(N_HEADS):
            s = jnp.dot(kc, qt_ref[h], preferred_element_type=jnp.float32) + bias_ref[...]
            s_ref[h] = s
            mnew_ref[h] = jnp.maximum(m_ref[h], _reduce_keys(s, jnp.max))
        for h in range(N_HEADS):
            m_new = mnew_ref[h]
            alpha = jnp.exp(m_ref[h] - m_new)
            p = jnp.exp(s_ref[h] - m_new)
            acc_ref[h] = alpha * acc_ref[h] + jnp.dot(
                vt, p.astype(jnp.bfloat16), preferred_element_type=jnp.float32)
            m_ref[h] = m_new
        return 0

    lax.fori_loop(0, n_chunks, attend_chunk, 0)

    for h in range(N_HEADS):
        rows = slice(h * HEAD_DIM, (h + 1) * HEAD_DIM)
        out_ref[rows, :] = acc_ref[h, 0:HEAD_DIM, :] / acc_ref[h, HEAD_DIM:HEAD_DIM + 1, :]
    y = out_ref[...].T * _silu(ga_ref[...])
    y_ref[...] = y.astype(y_ref.dtype)


def _dsa_attention(qq, kv, vt, wi, ga, tri):
    batch, seq, _ = qq.shape
    row = lambda b, i: (b, i, 0)
    return pl.pallas_call(
        _attn_kernel,
        grid=(batch, seq // Q_TILE),
        in_specs=[
            pl.BlockSpec((None, Q_TILE, 1024), row),
            pl.BlockSpec((None, seq, 256), lambda b, i: (b, 0, 0)),
            pl.BlockSpec((None, seq // K_TILE, VT_ROWS, K_TILE), lambda b, i: (b, 0, 0, 0)),
            pl.BlockSpec((None, Q_TILE, 128), row),
            pl.BlockSpec((None, Q_TILE, D_ATTN), row),
            pl.BlockSpec((K_TILE, K_TILE), lambda b, i: (0, 0)),
        ],
        out_specs=pl.BlockSpec((None, Q_TILE, D_ATTN), row),
        out_shape=jax.ShapeDtypeStruct((batch, seq, D_ATTN), jnp.bfloat16),
        scratch_shapes=[
            pltpu.VMEM((N_HEADS, HEAD_DIM, Q_TILE), jnp.bfloat16),
            pltpu.VMEM((N_IDX_HEADS, IDX_DIM, Q_TILE), jnp.bfloat16),
            pltpu.VMEM((seq, Q_TILE), jnp.float32),
            pltpu.VMEM((K_TILE, Q_TILE), jnp.float32),
            pltpu.VMEM((1, Q_TILE), jnp.float32),
            pltpu.VMEM((N_HEADS, K_TILE, Q_TILE), jnp.float32),
            pltpu.VMEM((N_HEADS, 1, Q_TILE), jnp.float32),
            pltpu.VMEM((N_HEADS, 1, Q_TILE), jnp.float32),
            pltpu.VMEM((N_HEADS, VT_ROWS, Q_TILE), jnp.float32),
            pltpu.VMEM((D_ATTN, Q_TILE), jnp.float32),
        ],
        compiler_params=pltpu.CompilerParams(
            dimension_semantics=("parallel", "parallel"), vmem_limit_bytes=VMEM_LIMIT),
        name="dsa_attention",
    )(qq, kv, vt, wi, ga, tri)


def _outproj_kernel(yr_ref, ya_ref, x_ref, mod_ref, wr_ref, wa_ref, g_ref, o_ref, *, final_norm):
    y = jnp.dot(yr_ref[...], wr_ref[...], preferred_element_type=jnp.float32)
    y = y + jnp.dot(ya_ref[...], wa_ref[...], preferred_element_type=jnp.float32)
    x_new = x_ref[...] + mod_ref[2:3, :] * y
    if final_norm:
        x_new = _rms(x_new, g_ref[...])
    o_ref[...] = x_new


def _out_projection(yr, ya, x, mod_l, w_r, w_a, final_g, final_norm):
    batch, seq, _ = x.shape
    row = lambda b, i: (b, i, 0)
    const = lambda b, i: (0, 0)
    return pl.pallas_call(
        functools.partial(_outproj_kernel, final_norm=final_norm),
        grid=(batch, seq // ROW_TILE),
        in_specs=[
            pl.BlockSpec((None, ROW_TILE, D_RNN), row),
            pl.BlockSpec((None, ROW_TILE, D_ATTN), row),
            pl.BlockSpec((None, ROW_TILE, D_MODEL), row),
            pl.BlockSpec((None, 3, D_MODEL), lambda b, i: (b, 0, 0)),
            pl.BlockSpec((D_RNN, D_MODEL), const),
            pl.BlockSpec((D_ATTN, D_MODEL), const),
            pl.BlockSpec((1, D_MODEL), const),
        ],
        out_specs=pl.BlockSpec((None, ROW_TILE, D_MODEL), row),
        out_shape=jax.ShapeDtypeStruct((batch, seq, D_MODEL), jnp.float32),
        compiler_params=pltpu.CompilerParams(
            dimension_semantics=("parallel", "parallel"), vmem_limit_bytes=VMEM_LIMIT),
        name="out_proj",
    )(yr, ya, x, mod_l, w_r, w_a, final_g.reshape(1, D_MODEL))


def _permute_w_in(w):
    xr, gr, q, k, v, ga, qi, ki, wi = jnp.split(
        w, [512, 1024, 1536, 1600, 1664, 2176, 2688, 2752], axis=-1)
    pad = jnp.zeros((w.shape[0], Z_COLS - 2760), w.dtype)
    return jnp.concatenate([xr, gr, q, qi, ga, k, v, ki, wi, pad], axis=-1).astype(jnp.bfloat16)


def _block_diag(w):
    n, c, d = w.shape
    eye = jnp.eye(n, dtype=w.dtype)
    return (eye[:, None, :, None] * w[:, :, None, :]).reshape(n * c, n * d)


def kernel(x, c, norm_g, ada_w, ada_b, w_in, conv_w, conv_b, lru_wx, lru_bx, lru_wa, lru_ba, lru_a, w_out, final_g):
    depth = w_in.shape[0]
    batch, seq, _ = x.shape
    mod = _modulation(c, ada_w, ada_b)
    idx = jnp.arange(K_TILE)
    tri = (idx[None, :] < idx[:, None]).astype(jnp.bfloat16)
    ones_rows = jnp.zeros((batch, seq // K_TILE, VT_ROWS - HEAD_DIM, K_TILE), jnp.bfloat16).at[:, :, 0, :].set(1.0)
    for l in range(depth):
        w_perm = _permute_w_in(w_in[l])
        w_gates = jnp.concatenate([_block_diag(lru_wx[l]), _block_diag(lru_wa[l])], axis=-1).astype(jnp.bfloat16)
        b_gates = jnp.concatenate([lru_bx[l], lru_ba[l]]).reshape(1, 2 * D_RNN)
        w_o = w_out[l].astype(jnp.bfloat16)
        y_r, qq, ga, kv, wi = _in_projection(x, mod[l], norm_g[l], w_perm, conv_w[l], conv_b[l],
                                             w_gates, b_gates, lru_a[l])
        vt = kv[:, :, HEAD_DIM:2 * HEAD_DIM].reshape(batch, seq // K_TILE, K_TILE, HEAD_DIM).swapaxes(2, 3)
        vt = jnp.concatenate([vt, ones_rows], axis=2)
        y_a = _dsa_attention(qq, kv, vt, wi, ga, tri)
        x = _out_projection(y_r, y_a, x, mod[l], w_o[:D_RNN], w_o[D_RNN:], final_g, l == depth - 1)
    return x
```

```python
import functools

import jax
import jax.numpy as jnp
from jax import lax
from jax.experimental import pallas as pl
from jax.experimental.pallas import tpu as pltpu

D_MODEL = 1024
D_RNN = 512
N_RNN_BLOCKS = 8
RNN_BLOCK = D_RNN // N_RNN_BLOCKS
CONV_WIDTH = 4
LRU_C = 8.0
N_HEADS = 8
HEAD_DIM = 64
D_ATTN = N_HEADS * HEAD_DIM
N_IDX_HEADS = 8
IDX_DIM = 64
MAX_TOPK = 256
EPS = 1e-6
LOG2_E = 1.4426950408889634

Z_RNN = 0
Z_Q = 1024
Z_KV = 1536
Z_GA = 1664
Z_QI = 2176
Z_KI = 2688
Z_COLS = 2816
D_IN = 2760
WI_LANE = 64

ROW_TILE = 512
Q_TILE = 256
K_TILE = 512
VT_ROWS = 80
MASKED = -1e30
PROBES_PER_ROUND = 4
PLAIN_BISECT_ROUNDS = 8
PARTIAL_ROWS = 32
SWEEP_TILE = 256
VMEM_LIMIT = 48 * 1024 * 1024


def _silu(x):
    return x * jax.nn.sigmoid(x)


def _rms(x, g):
    return x * lax.rsqrt(jnp.mean(x * x, axis=-1, keepdims=True) + EPS) * g


def _reduce_keys(x, op):
    keys, queries = x.shape
    part = op(x.reshape(keys // PARTIAL_ROWS, PARTIAL_ROWS, queries), axis=0)
    return op(part, axis=0, keepdims=True)


def _mod_kernel(c_ref, w_ref, b_ref, o_ref):
    c_act = _silu(c_ref[...])
    o_ref[...] = jnp.dot(c_act, w_ref[...], precision=lax.Precision.HIGHEST,
                         preferred_element_type=jnp.float32) + b_ref[...]


def _modulation(c, ada_w, ada_b):
    depth = ada_w.shape[0]
    batch = c.shape[0]
    out = pl.pallas_call(
        _mod_kernel,
        grid=(depth, 3),
        in_specs=[
            pl.BlockSpec((batch, D_MODEL), lambda l, j: (0, 0)),
            pl.BlockSpec((None, D_MODEL, D_MODEL), lambda l, j: (l, 0, j)),
            pl.BlockSpec((None, None, 1, D_MODEL), lambda l, j: (l, j, 0, 0)),
        ],
        out_specs=pl.BlockSpec((None, None, batch, D_MODEL), lambda l, j: (l, j, 0, 0)),
        out_shape=jax.ShapeDtypeStruct((depth, 3, batch, D_MODEL), jnp.float32),
        compiler_params=pltpu.CompilerParams(vmem_limit_bytes=VMEM_LIMIT),
        name="adaln_mod",
    )(c, ada_w, ada_b.reshape(depth, 3, 1, D_MODEL))
    return out.transpose(0, 2, 1, 3)


def _inproj_kernel(x_ref, mod_ref, g_ref, w_ref, cw_ref, cb_ref, wg_ref, bg_ref, ap_ref,
                   yr_ref, qq_ref, ga_ref, kv_ref, wi_ref, vt_ref, xbuf_ref, h_ref):
    @pl.when(pl.program_id(1) == 0)
    def _():
        xbuf_ref[0:8, :] = jnp.zeros((8, D_RNN), jnp.float32)
        h_ref[...] = jnp.zeros((1, D_RNN), jnp.float32)

    x = x_ref[...]
    shift = mod_ref[0:1, :]
    scale = mod_ref[1:2, :]
    h = _rms(x, g_ref[...]) * (1.0 + scale) + shift
    h = h.astype(jnp.bfloat16)
    z_rnn = jnp.dot(h, w_ref[:, Z_RNN:Z_Q], preferred_element_type=jnp.float32)
    y_r = _rg_lru_tile(z_rnn[:, 0:D_RNN], z_rnn[:, D_RNN:2 * D_RNN],
                       cw_ref, cb_ref, wg_ref, bg_ref, ap_ref, xbuf_ref, h_ref)
    yr_ref[...] = y_r.astype(yr_ref.dtype)
    z = jnp.dot(h, w_ref[:, Z_Q:Z_COLS], preferred_element_type=jnp.float32)
    col = lambda start, width: z[:, start - Z_Q:start - Z_Q + width]
    qq_ref[...] = jnp.concatenate([col(Z_Q, D_ATTN), col(Z_QI, N_IDX_HEADS * IDX_DIM)], axis=-1).astype(jnp.bfloat16)
    ga_ref[...] = col(Z_GA, D_ATTN)
    kv_ref[...] = jnp.concatenate([col(Z_KV, 128), col(Z_KI, 128)], axis=-1).astype(jnp.bfloat16)
    wi_ref[...] = col(Z_KI, 128)
    pad_rows = lax.broadcasted_iota(jnp.int32, (VT_ROWS - HEAD_DIM, K_TILE), 0)
    for j in range(ROW_TILE // K_TILE):
        v_t = col(Z_KV + HEAD_DIM, HEAD_DIM)[j * K_TILE:(j + 1) * K_TILE, :].T
        vt_ref[j, 0:HEAD_DIM, :] = v_t.astype(jnp.bfloat16)
        vt_ref[j, HEAD_DIM:VT_ROWS, :] = jnp.where(pad_rows == 0, 1.0, 0.0).astype(jnp.bfloat16)


def _in_projection(x, mod_l, norm_g, w_pad, conv_w, conv_b, w_gates, b_gates, a_param):
    batch, seq, _ = x.shape
    row = lambda b, i: (b, i, 0)
    const = lambda b, i: (0, 0)
    return pl.pallas_call(
        _inproj_kernel,
        grid=(batch, seq // ROW_TILE),
        in_specs=[
            pl.BlockSpec((None, ROW_TILE, D_MODEL), row),
            pl.BlockSpec((None, 3, D_MODEL), lambda b, i: (b, 0, 0)),
            pl.BlockSpec((1, D_MODEL), const),
            pl.BlockSpec((D_MODEL, Z_COLS), const),
            pl.BlockSpec((CONV_WIDTH, D_RNN), const),
            pl.BlockSpec((1, D_RNN), const),
            pl.BlockSpec((D_RNN, 2 * D_RNN), const),
            pl.BlockSpec((1, 2 * D_RNN), const),
            pl.BlockSpec((1, D_RNN), const),
        ],
        out_specs=[
            pl.BlockSpec((None, ROW_TILE, D_RNN), row),
            pl.BlockSpec((None, ROW_TILE, 1024), row),
            pl.BlockSpec((None, ROW_TILE, 512), row),
            pl.BlockSpec((None, ROW_TILE, 256), row),
            pl.BlockSpec((None, ROW_TILE, 128), row),
            pl.BlockSpec((None, ROW_TILE // K_TILE, VT_ROWS, K_TILE), lambda b, i: (b, i, 0, 0)),
        ],
        out_shape=[
            jax.ShapeDtypeStruct((batch, seq, D_RNN), jnp.bfloat16),
            jax.ShapeDtypeStruct((batch, seq, 1024), jnp.bfloat16),
            jax.ShapeDtypeStruct((batch, seq, 512), jnp.float32),
            jax.ShapeDtypeStruct((batch, seq, 256), jnp.bfloat16),
            jax.ShapeDtypeStruct((batch, seq, 128), jnp.float32),
            jax.ShapeDtypeStruct((batch, seq // K_TILE, VT_ROWS, K_TILE), jnp.bfloat16),
        ],
        scratch_shapes=[
            pltpu.VMEM((ROW_TILE + 8, D_RNN), jnp.float32),
            pltpu.VMEM((1, D_RNN), jnp.float32),
        ],
        compiler_params=pltpu.CompilerParams(
            dimension_semantics=("parallel", "arbitrary"), vmem_limit_bytes=VMEM_LIMIT),
        name="in_proj_rg_lru",
    )(x, mod_l, norm_g.reshape(1, D_MODEL), w_pad, conv_w, conv_b.reshape(1, D_RNN),
      w_gates, b_gates, a_param.reshape(1, D_RNN))


def _rg_lru_tile(xr, gr, cw_ref, cb_ref, wg_ref, bg_ref, ap_ref, xbuf_ref, h_ref):
    ts = ROW_TILE
    xbuf_ref[8:8 + ts, :] = xr
    xc = cw_ref[3:4, :] * xr + cb_ref[...]
    for k in range(CONV_WIDTH - 1):
        back = CONV_WIDTH - 1 - k
        xc = xc + cw_ref[k:k + 1, :] * xbuf_ref[8 - back:8 - back + ts, :]
    xbuf_ref[0:8, :] = xbuf_ref[ts:ts + 8, :]

    gates = jnp.dot(xc.astype(jnp.bfloat16), wg_ref[...], preferred_element_type=jnp.float32) + bg_ref[...]
    gate_x = jax.nn.sigmoid(gates[:, 0:D_RNN])
    gate_a = jax.nn.sigmoid(gates[:, D_RNN:2 * D_RNN])
    neg_ap = -ap_ref[...]
    softplus = jnp.maximum(neg_ap, 0.0) + jnp.log(1.0 + jnp.exp(-jnp.abs(neg_ap)))
    log_a = (-LRU_C) * gate_a * softplus
    a = jnp.exp(log_a)
    mult = jnp.sqrt(1.0 - a * a)
    u = mult * gate_x * xc

    groups = ts // 8
    a = a.reshape(groups, 8, D_RNN)
    u = u.reshape(groups, 8, D_RNN)
    sub = lax.broadcasted_iota(jnp.int32, (groups, 8, D_RNN), 1)
    for d in (1, 2, 4):
        keep = sub >= d
        u = jnp.where(keep, u + a * pltpu.roll(u, d, 1), u)
        a = jnp.where(keep, a * pltpu.roll(a, d, 1), a)
    carry = h_ref[...]
    hs = []
    for g in range(groups):
        h_g = u[g] + a[g] * carry
        hs.append(h_g)
        carry = h_g[7:8, :]
    h_ref[...] = carry
    return jnp.concatenate(hs, axis=0) * _silu(gr)


def _attn_kernel(qq_ref, kv_ref, vt_ref, wi_ref, ga_ref, tri_ref, y_ref,
                 qt_ref, qit_ref, sc_ref, bias_ref, eqb_ref, s_ref, m_ref, mnew_ref, acc_ref, out_ref):
    tq, tk = Q_TILE, K_TILE
    topk = float(MAX_TOPK)
    qb = pl.program_id(1)
    n_chunks = ((qb + 1) * tq + tk - 1) // tk

    for h in range(N_HEADS):
        q_h = qq_ref[:, h * HEAD_DIM:(h + 1) * HEAD_DIM].astype(jnp.float32) * (HEAD_DIM ** -0.5 * LOG2_E)
        qt_ref[h] = q_h.T.astype(jnp.bfloat16)
        qi_h = qq_ref[:, D_ATTN + h * IDX_DIM:D_ATTN + (h + 1) * IDX_DIM].astype(jnp.float32)
        qit_ref[h] = qi_h.T.astype(jnp.bfloat16)
    w_t = wi_ref[...].T[WI_LANE:WI_LANE + N_IDX_HEADS, :] * ((IDX_DIM ** -0.5) * (N_IDX_HEADS ** -0.5))

    q_pos = qb * tq + lax.broadcasted_iota(jnp.int32, (tk, tq), 1)
    key_iota = lax.broadcasted_iota(jnp.int32, (tk, tq), 0)

    def score_chunk(c, carry):
        rmax, rmin = carry
        k0 = pl.multiple_of(c * tk, tk)
        ki = kv_ref[pl.ds(k0, tk), 128:128 + IDX_DIM]
        score = jnp.zeros((tk, tq), jnp.float32)
        for h in range(N_IDX_HEADS):
            logits = jnp.dot(ki, qit_ref[h], preferred_element_type=jnp.float32)
            score = score + jnp.maximum(logits, 0.0) * w_t[h:h + 1, :]
        causal = (k0 + key_iota) <= q_pos
        sc_ref[pl.ds(k0, tk), :] = jnp.where(causal, score, -jnp.inf)
        rmax = jnp.maximum(rmax, _reduce_keys(jnp.where(causal, score, -jnp.inf), jnp.max))
        rmin = jnp.minimum(rmin, _reduce_keys(jnp.where(causal, score, jnp.inf), jnp.min))
        return rmax, rmin

    rmax, rmin = lax.fori_loop(
        0, n_chunks, score_chunk,
        (jnp.full((1, tq), -jnp.inf, jnp.float32), jnp.full((1, tq), jnp.inf, jnp.float32)))

    n_sweep_tiles = ((qb + 1) * tq) // SWEEP_TILE

    def sweep(mid, want):
        kinds = {"gt": (jnp.sum, 0.0), "ge": (jnp.sum, 0.0), "above": (jnp.min, jnp.inf), "below": (jnp.max, -jnp.inf)}

        def body(c, carry):
            k0 = pl.multiple_of(c * SWEEP_TILE, SWEEP_TILE)
            s = sc_ref[pl.ds(k0, SWEEP_TILE), :].reshape(SWEEP_TILE // PARTIAL_ROWS, PARTIAL_ROWS, tq)
            gt = s > mid
            terms = {"gt": lambda: jnp.where(gt, 1.0, 0.0), "ge": lambda: jnp.where(s >= mid, 1.0, 0.0),
                     "above": lambda: jnp.where(gt, s, jnp.inf), "below": lambda: jnp.where(gt, -jnp.inf, s)}
            out = []
            for name, acc in zip(want, carry):
                op = kinds[name][0]
                part = op(terms[name](), axis=0)
                out.append(acc + part if op is jnp.sum else
                           (jnp.minimum(acc, part) if op is jnp.min else jnp.maximum(acc, part)))
            return tuple(out)

        init = tuple(jnp.full((PARTIAL_ROWS, tq), kinds[name][1], jnp.float32) for name in want)
        res = lax.fori_loop(0, n_sweep_tiles, body, init)
        return [kinds[name][0](r, axis=0, keepdims=True) for name, r in zip(want, res)]

    def update(state, mid, cnt, tie, new_lo, new_hi):
        lo, hi, chi, act = state
        on = act > 0.5
        fin = jnp.logical_and(on, jnp.logical_or(cnt == topk, tie))
        go_on = jnp.logical_and(on, jnp.logical_not(fin))
        up = jnp.logical_and(go_on, cnt > topk)
        dn = jnp.logical_and(go_on, cnt < topk)
        lo = jnp.where(fin, mid, jnp.where(up, new_lo, lo))
        hi = jnp.where(fin, mid, jnp.where(dn, new_hi, hi))
        chi = jnp.where(jnp.logical_or(fin, dn), cnt, chi)
        act = jnp.where(jnp.logical_and(go_on, lo < hi), 1.0, 0.0)
        return lo, hi, chi, act

    def midpoint(state):
        return 0.5 * state[0] + 0.5 * state[1]

    n_valid = (q_pos[0:1, :] + 1).astype(jnp.float32)
    few = n_valid <= topk
    state = (rmin,
             jnp.where(few, -jnp.inf, rmax),
             jnp.where(few, topk, 0.0),
             jnp.where(jnp.logical_or(few, rmin >= rmax), 0.0, 1.0))

    mid = jnp.minimum(jnp.maximum(0.0, state[0]), state[1])
    cnt, cge = sweep(mid, ("gt", "ge"))
    state = update(state, mid, cnt, jnp.logical_and(cnt < topk, cge >= topk), mid, mid)

    def pending(state):
        return jnp.where(jnp.logical_and(state[3] > 0.5, topk - state[2] != 1.0), 1.0, 0.0)

    def plain_cond(st):
        return jnp.logical_and(st[1] > 0.5, st[0] < PLAIN_BISECT_ROUNDS)

    def plain_body(st):
        state = st[2:]
        for _ in range(PROBES_PER_ROUND):
            mid = midpoint(state)
            (cnt,) = sweep(mid, ("gt",))
            state = update(state, mid, cnt, False, mid, mid)
        return (st[0] + 1, jnp.max(pending(state))) + state

    st = lax.while_loop(plain_cond, plain_body, (jnp.int32(0), jnp.max(pending(state))) + state)
    lo, hi, chi, act = st[2:]
    (largest_below,) = sweep(hi, ("below",))
    one_short = jnp.logical_and(act > 0.5, topk - chi == 1.0)
    state = (jnp.where(one_short, largest_below, lo), jnp.where(one_short, largest_below, hi), chi,
             jnp.where(one_short, 0.0, act))

    def snap_cond(st):
        return jnp.logical_and(st[1] > 0.5, st[0] < 4096)

    def snap_body(st):
        state = st[2:]
        mid = midpoint(state)
        mid = jnp.where(mid >= state[1], state[0], mid)
        cnt, above, below = sweep(mid, ("gt", "above", "below"))
        state = update(state, mid, cnt, False, above, below)
        return (st[0] + 1, jnp.max(state[3])) + state

    st = lax.while_loop(snap_cond, snap_body, (jnp.int32(0), jnp.max(state[3])) + state)
    thr = st[3]
    need = topk - st[4]
    any_tie = jnp.max(need) > 0.5

    m_ref[...] = jnp.full(m_ref.shape, MASKED, jnp.float32)
    acc_ref[...] = jnp.zeros(acc_ref.shape, jnp.float32)
    eqb_ref[...] = jnp.zeros(eqb_ref.shape, jnp.float32)

    def attend_chunk(c, _):
        k0 = pl.multiple_of(c * tk, tk)
        s_idx = sc_ref[pl.ds(k0, tk), :]
        bias_ref[...] = jnp.where(s_idx > thr, 0.0, MASKED)

        @pl.when(any_tie)
        def _():
            eq = s_idx == thr
            eq_f = jnp.where(eq, 1.0, 0.0)
            rank = jnp.dot(tri_ref[...], eq_f.astype(jnp.bfloat16),
                           preferred_element_type=jnp.float32) + eqb_ref[...]
            tie_bias = jnp.where(jnp.logical_and(eq, rank < need), 0.0, MASKED)
            bias_ref[...] = jnp.where(s_idx > thr, 0.0, tie_bias)
            eqb_ref[...] = eqb_ref[...] + _reduce_keys(eq_f, jnp.sum)

        kc = kv_ref[pl.ds(k0, tk), 0:HEAD_DIM]
        vt = vt_ref[c]

        def logits(h):
            s = jnp.dot(kc, qt_ref[h], preferred_element_type=jnp.float32) + bias_ref[...]
            s_ref[h] = s
            mnew_ref[h] = jnp.maximum(m_ref[h], _reduce_keys(s, jnp.max))

        def weigh(h):
            m_new = mnew_ref[h]
            alpha = jnp.exp2(m_ref[h] - m_new)
            p = jnp.exp2(s_ref[h] - m_new)
            acc_ref[h] = alpha * acc_ref[h] + jnp.dot(
                vt, p.astype(jnp.bfloat16), preferred_element_type=jnp.float32)
            m_ref[h] = m_new

        for h in range(N_HEADS):
            logits(h)
        for h in range(N_HEADS):
            weigh(h)
        return 0

    lax.fori_loop(0, n_chunks, attend_chunk, 0)

    for h in range(N_HEADS):
        rows = slice(h * HEAD_DIM, (h + 1) * HEAD_DIM)
        out_ref[rows, :] = acc_ref[h, 0:HEAD_DIM, :] / acc_ref[h, HEAD_DIM:HEAD_DIM + 1, :]
    y = out_ref[...].T * _silu(ga_ref[...])
    y_ref[...] = y.astype(y_ref.dtype)


def _dsa_attention(qq, kv, vt, wi, ga, tri):
    batch, seq, _ = qq.shape
    row = lambda b, i: (b, i, 0)
    return pl.pallas_call(
        _attn_kernel,
        grid=(batch, seq // Q_TILE),
        in_specs=[
            pl.BlockSpec((None, Q_TILE, 1024), row),
            pl.BlockSpec((None, seq, 256), lambda b, i: (b, 0, 0)),
            pl.BlockSpec((None, seq // K_TILE, VT_ROWS, K_TILE), lambda b, i: (b, 0, 0, 0)),
            pl.BlockSpec((None, Q_TILE, 128), row),
            pl.BlockSpec((None, Q_TILE, D_ATTN), row),
            pl.BlockSpec((K_TILE, K_TILE), lambda b, i: (0, 0)),
        ],
        out_specs=pl.BlockSpec((None, Q_TILE, D_ATTN), row),
        out_shape=jax.ShapeDtypeStruct((batch, seq, D_ATTN), jnp.bfloat16),
        scratch_shapes=[
            pltpu.VMEM((N_HEADS, HEAD_DIM, Q_TILE), jnp.bfloat16),
            pltpu.VMEM((N_IDX_HEADS, IDX_DIM, Q_TILE), jnp.bfloat16),
            pltpu.VMEM((seq, Q_TILE), jnp.float32),
            pltpu.VMEM((K_TILE, Q_TILE), jnp.float32),
            pltpu.VMEM((1, Q_TILE), jnp.float32),
            pltpu.VMEM((N_HEADS, K_TILE, Q_TILE), jnp.float32),
            pltpu.VMEM((N_HEADS, 1, Q_TILE), jnp.float32),
            pltpu.VMEM((N_HEADS, 1, Q_TILE), jnp.float32),
            pltpu.VMEM((N_HEADS, VT_ROWS, Q_TILE), jnp.float32),
            pltpu.VMEM((D_ATTN, Q_TILE), jnp.float32),
        ],
        compiler_params=pltpu.CompilerParams(
            dimension_semantics=("parallel", "parallel"), vmem_limit_bytes=VMEM_LIMIT),
        name="dsa_attention",
    )(qq, kv, vt, wi, ga, tri)


def _outproj_kernel(yr_ref, ya_ref, x_ref, mod_ref, wr_ref, wa_ref, g_ref, o_ref, *, final_norm):
    y = jnp.dot(yr_ref[...], wr_ref[...], preferred_element_type=jnp.float32)
    y = y + jnp.dot(ya_ref[...], wa_ref[...], preferred_element_type=jnp.float32)
    x_new = x_ref[...] + mod_ref[2:3, :] * y
    if final_norm:
        x_new = _rms(x_new, g_ref[...])
    o_ref[...] = x_new


def _out_projection(yr, ya, x, mod_l, w_r, w_a, final_g, final_norm):
    batch, seq, _ = x.shape
    row = lambda b, i: (b, i, 0)
    const = lambda b, i: (0, 0)
    return pl.pallas_call(
        functools.partial(_outproj_kernel, final_norm=final_norm),
        grid=(batch, seq // ROW_TILE),
        in_specs=[
            pl.BlockSpec((None, ROW_TILE, D_RNN), row),
            pl.BlockSpec((None, ROW_TILE, D_ATTN), row),
            pl.BlockSpec((None, ROW_TILE, D_MODEL), row),
            pl.BlockSpec((None, 3, D_MODEL), lambda b, i: (b, 0, 0)),
            pl.BlockSpec((D_RNN, D_MODEL), const),
            pl.BlockSpec((D_ATTN, D_MODEL), const),
            pl.BlockSpec((1, D_MODEL), const),
        ],
        out_specs=pl.BlockSpec((None, ROW_TILE, D_MODEL), row),
        out_shape=jax.ShapeDtypeStruct((batch, seq, D_MODEL), jnp.float32),
        compiler_params=pltpu.CompilerParams(
            dimension_semantics=("parallel", "parallel"), vmem_limit_bytes=VMEM_LIMIT),
        name="out_proj",
    )(yr, ya, x, mod_l, w_r, w_a, final_g.reshape(1, D_MODEL))


def _pad_w_in(w):
    return jnp.pad(w.astype(jnp.bfloat16), ((0, 0), (0, Z_COLS - D_IN)))


def _block_diag(w):
    n, c, d = w.shape
    eye = jnp.eye(n, dtype=w.dtype)
    return (eye[:, None, :, None] * w[:, :, None, :]).reshape(n * c, n * d)


def kernel(x, c, norm_g, ada_w, ada_b, w_in, conv_w, conv_b, lru_wx, lru_bx, lru_wa, lru_ba, lru_a, w_out, final_g):
    depth = w_in.shape[0]
    mod = _modulation(c, ada_w, ada_b)
    idx = jnp.arange(K_TILE)
    tri = (idx[None, :] < idx[:, None]).astype(jnp.bfloat16)
    for l in range(depth):
        w_pad = _pad_w_in(w_in[l])
        w_gates = jnp.concatenate([_block_diag(lru_wx[l]), _block_diag(lru_wa[l])], axis=-1).astype(jnp.bfloat16)
        b_gates = jnp.concatenate([lru_bx[l], lru_ba[l]]).reshape(1, 2 * D_RNN)
        w_o = w_out[l].astype(jnp.bfloat16)
        y_r, qq, ga, kv, wi, vt = _in_projection(x, mod[l], norm_g[l], w_pad, conv_w[l], conv_b[l],
                                                 w_gates, b_gates, lru_a[l])
        y_a = _dsa_attention(qq, kv, vt, wi, ga, tri)
        x = _out_projection(y_r, y_a, x, mod[l], w_o[:D_RNN], w_o[D_RNN:], final_g, l == depth - 1)
    return x
```

```python
import functools

import jax
import jax.numpy as jnp
from jax import lax
from jax.experimental import pallas as pl
from jax.experimental.pallas import tpu as pltpu

D_MODEL = 1024
D_RNN = 512
N_RNN_BLOCKS = 8
RNN_BLOCK = D_RNN // N_RNN_BLOCKS
CONV_WIDTH = 4
LRU_C = 8.0
N_HEADS = 8
HEAD_DIM = 64
D_ATTN = N_HEADS * HEAD_DIM
N_IDX_HEADS = 8
IDX_DIM = 64
MAX_TOPK = 256
EPS = 1e-6
LOG2_E = 1.4426950408889634

Z_RNN = 0
Z_Q = 1024
Z_KV = 1536
Z_GA = 1664
Z_QI = 2176
Z_KI = 2688
Z_COLS = 2816
D_IN = 2760
WI_LANE = 64

ROW_TILE = 512
Q_TILE = 256
K_TILE = 512
VT_ROWS = 80
MASKED = -1e30
PROBES_PER_ROUND = 7
PLAIN_BISECT_ROUNDS = 5
PARTIAL_ROWS = 32
SWEEP_TILE = 256
TAIL_KEYS = K_TILE - Q_TILE
assert K_TILE == 2 * Q_TILE
VMEM_LIMIT = 48 * 1024 * 1024


def _silu(x):
    return x * jax.nn.sigmoid(x)


def _rms(x, g):
    return x * lax.rsqrt(jnp.mean(x * x, axis=-1, keepdims=True) + EPS) * g


def _reduce_keys(x, op):
    keys, queries = x.shape
    part = op(x.reshape(keys // PARTIAL_ROWS, PARTIAL_ROWS, queries), axis=0)
    return op(part, axis=0, keepdims=True)


def _mod_kernel(c_ref, w_ref, b_ref, o_ref):
    c_act = _silu(c_ref[...])
    o_ref[...] = jnp.dot(c_act, w_ref[...], precision=lax.Precision.HIGHEST,
                         preferred_element_type=jnp.float32) + b_ref[...]


def _modulation(c, ada_w, ada_b):
    depth = ada_w.shape[0]
    batch = c.shape[0]
    out = pl.pallas_call(
        _mod_kernel,
        grid=(depth, 3),
        in_specs=[
            pl.BlockSpec((batch, D_MODEL), lambda l, j: (0, 0)),
            pl.BlockSpec((None, D_MODEL, D_MODEL), lambda l, j: (l, 0, j)),
            pl.BlockSpec((None, None, 1, D_MODEL), lambda l, j: (l, j, 0, 0)),
        ],
        out_specs=pl.BlockSpec((None, None, batch, D_MODEL), lambda l, j: (l, j, 0, 0)),
        out_shape=jax.ShapeDtypeStruct((depth, 3, batch, D_MODEL), jnp.float32),
        compiler_params=pltpu.CompilerParams(vmem_limit_bytes=VMEM_LIMIT),
        name="adaln_mod",
    )(c, ada_w, ada_b.reshape(depth, 3, 1, D_MODEL))
    return out.transpose(0, 2, 1, 3)


def _inproj_kernel(x_ref, mod_ref, g_ref, w_ref, cw_ref, cb_ref, wg_ref, bg_ref, ap_ref,
                   yr_ref, qt_ref, ga_ref, kv_ref, wi_ref, vt_ref, xbuf_ref, h_ref):
    @pl.when(pl.program_id(1) == 0)
    def _():
        xbuf_ref[0:8, :] = jnp.zeros((8, D_RNN), jnp.float32)
        h_ref[...] = jnp.zeros((1, D_RNN), jnp.float32)

    x = x_ref[...]
    shift = mod_ref[0:1, :]
    scale = mod_ref[1:2, :]
    h = _rms(x, g_ref[...]) * (1.0 + scale) + shift
    h = h.astype(jnp.bfloat16)
    z_rnn = jnp.dot(h, w_ref[:, Z_RNN:Z_Q], preferred_element_type=jnp.float32)
    y_r = _rg_lru_tile(z_rnn[:, 0:D_RNN], z_rnn[:, D_RNN:2 * D_RNN],
                       cw_ref, cb_ref, wg_ref, bg_ref, ap_ref, xbuf_ref, h_ref)
    yr_ref[...] = y_r.astype(yr_ref.dtype)
    z = jnp.dot(h, w_ref[:, Z_Q:Z_COLS], preferred_element_type=jnp.float32)
    col = lambda start, width: z[:, start - Z_Q:start - Z_Q + width]
    for h in range(N_HEADS):
        q_h = col(Z_Q + h * HEAD_DIM, HEAD_DIM) * (HEAD_DIM ** -0.5 * LOG2_E)
        qt_ref[h] = q_h.T.astype(jnp.bfloat16)
    for h in range(N_IDX_HEADS):
        qt_ref[N_HEADS + h] = col(Z_QI + h * IDX_DIM, IDX_DIM).T.astype(jnp.bfloat16)
    ga_ref[...] = col(Z_GA, D_ATTN)
    kv_ref[...] = jnp.concatenate([col(Z_KV, 128), col(Z_KI, 128)], axis=-1).astype(jnp.bfloat16)
    wi_ref[...] = col(Z_KI, 128)
    pad_rows = lax.broadcasted_iota(jnp.int32, (VT_ROWS - HEAD_DIM, K_TILE), 0)
    for j in range(ROW_TILE // K_TILE):
        v_t = col(Z_KV + HEAD_DIM, HEAD_DIM)[j * K_TILE:(j + 1) * K_TILE, :].T
        vt_ref[j, 0:HEAD_DIM, :] = v_t.astype(jnp.bfloat16)
        vt_ref[j, HEAD_DIM:VT_ROWS, :] = jnp.where(pad_rows == 0, 1.0, 0.0).astype(jnp.bfloat16)


def _in_projection(x, mod_l, norm_g, w_pad, conv_w, conv_b, w_gates, b_gates, a_param):
    batch, seq, _ = x.shape
    row = lambda b, i: (b, i, 0)
    const = lambda b, i: (0, 0)
    return pl.pallas_call(
        _inproj_kernel,
        grid=(batch, seq // ROW_TILE),
        in_specs=[
            pl.BlockSpec((None, ROW_TILE, D_MODEL), row),
            pl.BlockSpec((None, 3, D_MODEL), lambda b, i: (b, 0, 0)),
            pl.BlockSpec((1, D_MODEL), const),
            pl.BlockSpec((D_MODEL, Z_COLS), const),
            pl.BlockSpec((CONV_WIDTH, D_RNN), const),
            pl.BlockSpec((1, D_RNN), const),
            pl.BlockSpec((D_RNN, 2 * D_RNN), const),
            pl.BlockSpec((1, 2 * D_RNN), const),
            pl.BlockSpec((1, D_RNN), const),
        ],
        out_specs=[
            pl.BlockSpec((None, ROW_TILE, D_RNN), row),
            pl.BlockSpec((None, N_HEADS + N_IDX_HEADS, HEAD_DIM, ROW_TILE), lambda b, i: (b, 0, 0, i)),
            pl.BlockSpec((None, ROW_TILE, 512), row),
            pl.BlockSpec((None, ROW_TILE, 256), row),
            pl.BlockSpec((None, ROW_TILE, 128), row),
            pl.BlockSpec((None, ROW_TILE // K_TILE, VT_ROWS, K_TILE), lambda b, i: (b, i, 0, 0)),
        ],
        out_shape=[
            jax.ShapeDtypeStruct((batch, seq, D_RNN), jnp.bfloat16),
            jax.ShapeDtypeStruct((batch, N_HEADS + N_IDX_HEADS, HEAD_DIM, seq), jnp.bfloat16),
            jax.ShapeDtypeStruct((batch, seq, 512), jnp.float32),
            jax.ShapeDtypeStruct((batch, seq, 256), jnp.bfloat16),
            jax.ShapeDtypeStruct((batch, seq, 128), jnp.float32),
            jax.ShapeDtypeStruct((batch, seq // K_TILE, VT_ROWS, K_TILE), jnp.bfloat16),
        ],
        scratch_shapes=[
            pltpu.VMEM((ROW_TILE + 8, D_RNN), jnp.float32),
            pltpu.VMEM((1, D_RNN), jnp.float32),
        ],
        compiler_params=pltpu.CompilerParams(
            dimension_semantics=("parallel", "arbitrary"), vmem_limit_bytes=VMEM_LIMIT),
        name="in_proj_rg_lru",
    )(x, mod_l, norm_g.reshape(1, D_MODEL), w_pad, conv_w, conv_b.reshape(1, D_RNN),
      w_gates, b_gates, a_param.reshape(1, D_RNN))


def _rg_lru_tile(xr, gr, cw_ref, cb_ref, wg_ref, bg_ref, ap_ref, xbuf_ref, h_ref):
    ts = ROW_TILE
    xbuf_ref[8:8 + ts, :] = xr
    xc = cw_ref[3:4, :] * xr + cb_ref[...]
    for k in range(CONV_WIDTH - 1):
        back = CONV_WIDTH - 1 - k
        xc = xc + cw_ref[k:k + 1, :] * xbuf_ref[8 - back:8 - back + ts, :]
    xbuf_ref[0:8, :] = xbuf_ref[ts:ts + 8, :]

    gates = jnp.dot(xc.astype(jnp.bfloat16), wg_ref[...], preferred_element_type=jnp.float32) + bg_ref[...]
    gate_x = jax.nn.sigmoid(gates[:, 0:D_RNN])
    gate_a = jax.nn.sigmoid(gates[:, D_RNN:2 * D_RNN])
    neg_ap = -ap_ref[...]
    softplus = jnp.maximum(neg_ap, 0.0) + jnp.log(1.0 + jnp.exp(-jnp.abs(neg_ap)))
    log_a = (-LRU_C) * gate_a * softplus
    a = jnp.exp(log_a)
    mult = jnp.sqrt(1.0 - a * a)
    u = mult * gate_x * xc

    groups = ts // 8
    a = a.reshape(groups, 8, D_RNN)
    u = u.reshape(groups, 8, D_RNN)
    sub = lax.broadcasted_iota(jnp.int32, (groups, 8, D_RNN), 1)
    for d in (1, 2, 4):
        keep = sub >= d
        u = jnp.where(keep, u + a * pltpu.roll(u, d, 1), u)
        a = jnp.where(keep, a * pltpu.roll(a, d, 1), a)
    carry = h_ref[...]
    hs = []
    for g in range(groups):
        h_g = u[g] + a[g] * carry
        hs.append(h_g)
        carry = h_g[7:8, :]
    h_ref[...] = carry
    return jnp.concatenate(hs, axis=0) * _silu(gr)


def _attn_kernel(qt_ref, kv_ref, vt_ref, wi_ref, ga_ref, tri_ref, y_ref,
                 sc_ref, bias_ref, eqb_ref, s_ref, m_ref, mnew_ref, acc_ref, out_ref):
    tq, tk = Q_TILE, K_TILE
    topk = float(MAX_TOPK)
    qb = pl.program_id(1)
    n_keys = (qb + 1) * tq
    n_full = n_keys // tk
    has_tail = n_keys - n_full * tk > 0

    w_t = wi_ref[...].T[WI_LANE:WI_LANE + N_IDX_HEADS, :] * ((IDX_DIM ** -0.5) * (N_IDX_HEADS ** -0.5))

    def score_chunk(c, carry, size):
        rmax, rmin = carry
        k0 = pl.multiple_of(c * tk, tk)
        ki = kv_ref[pl.ds(k0, size), 128:128 + IDX_DIM]
        score = jnp.zeros((size, tq), jnp.float32)
        for h in range(N_IDX_HEADS):
            logits = jnp.dot(ki, qt_ref[N_HEADS + h], preferred_element_type=jnp.float32)
            score = score + jnp.maximum(logits, 0.0) * w_t[h:h + 1, :]
        key_pos = k0 + lax.broadcasted_iota(jnp.int32, (size, tq), 0)
        causal = key_pos <= qb * tq + lax.broadcasted_iota(jnp.int32, (size, tq), 1)
        sc_ref[pl.ds(k0, size), :] = jnp.where(causal, score, -jnp.inf)
        rmax = jnp.maximum(rmax, _reduce_keys(jnp.where(causal, score, -jnp.inf), jnp.max))
        rmin = jnp.minimum(rmin, _reduce_keys(jnp.where(causal, score, jnp.inf), jnp.min))
        return rmax, rmin

    extremes = lax.fori_loop(
        0, n_full, functools.partial(score_chunk, size=tk),
        (jnp.full((1, tq), -jnp.inf, jnp.float32), jnp.full((1, tq), jnp.inf, jnp.float32)))
    rmax, rmin = lax.cond(has_tail, lambda e: score_chunk(n_full, e, TAIL_KEYS), lambda e: e, extremes)

    n_sweep_tiles = ((qb + 1) * tq) // SWEEP_TILE

    def sweep(mid, want):
        kinds = {"gt": (jnp.sum, 0.0), "ge": (jnp.sum, 0.0), "above": (jnp.min, jnp.inf), "below": (jnp.max, -jnp.inf)}

        def body(c, carry):
            k0 = pl.multiple_of(c * SWEEP_TILE, SWEEP_TILE)
            s = sc_ref[pl.ds(k0, SWEEP_TILE), :].reshape(SWEEP_TILE // PARTIAL_ROWS, PARTIAL_ROWS, tq)
            gt = s > mid
            terms = {"gt": lambda: jnp.where(gt, 1.0, 0.0), "ge": lambda: jnp.where(s >= mid, 1.0, 0.0),
                     "above": lambda: jnp.where(gt, s, jnp.inf), "below": lambda: jnp.where(gt, -jnp.inf, s)}
            out = []
            for name, acc in zip(want, carry):
                op = kinds[name][0]
                part = op(terms[name](), axis=0)
                out.append(acc + part if op is jnp.sum else
                           (jnp.minimum(acc, part) if op is jnp.min else jnp.maximum(acc, part)))
            return tuple(out)

        init = tuple(jnp.full((PARTIAL_ROWS, tq), kinds[name][1], jnp.float32) for name in want)
        res = lax.fori_loop(0, n_sweep_tiles, body, init)
        return [kinds[name][0](r, axis=0, keepdims=True) for name, r in zip(want, res)]

    def update(state, mid, cnt, tie, new_lo, new_hi):
        lo, hi, chi, act = state
        on = act > 0.5
        fin = jnp.logical_and(on, jnp.logical_or(cnt == topk, tie))
        go_on = jnp.logical_and(on, jnp.logical_not(fin))
        up = jnp.logical_and(go_on, cnt > topk)
        dn = jnp.logical_and(go_on, cnt < topk)
        lo = jnp.where(fin, mid, jnp.where(up, new_lo, lo))
        hi = jnp.where(fin, mid, jnp.where(dn, new_hi, hi))
        chi = jnp.where(jnp.logical_or(fin, dn), cnt, chi)
        act = jnp.where(jnp.logical_and(go_on, lo < hi), 1.0, 0.0)
        return lo, hi, chi, act

    def midpoint(state):
        return 0.5 * state[0] + 0.5 * state[1]

    n_valid = (qb * tq + 1 + lax.broadcasted_iota(jnp.int32, (1, tq), 1)).astype(jnp.float32)
    few = n_valid <= topk
    state = (rmin,
             jnp.where(few, -jnp.inf, rmax),
             jnp.where(few, topk, 0.0),
             jnp.where(jnp.logical_or(few, rmin >= rmax), 0.0, 1.0))

    mid = jnp.minimum(jnp.maximum(0.0, state[0]), state[1])
    cnt, cge = sweep(mid, ("gt", "ge"))
    state = update(state, mid, cnt, jnp.logical_and(cnt < topk, cge >= topk), mid, mid)

    def pending(state):
        return jnp.where(jnp.logical_and(state[3] > 0.5, topk - state[2] != 1.0), 1.0, 0.0)

    def plain_cond(st):
        return jnp.logical_and(st[1] > 0.5, st[0] < PLAIN_BISECT_ROUNDS)

    def plain_body(st):
        state = st[2:]
        for _ in range(PROBES_PER_ROUND):
            mid = midpoint(state)
            (cnt,) = sweep(mid, ("gt",))
            state = update(state, mid, cnt, False, mid, mid)
        return (st[0] + 1, jnp.max(pending(state))) + state

    st = lax.while_loop(plain_cond, plain_body, (jnp.int32(0), jnp.max(pending(state))) + state)
    lo, hi, chi, act = st[2:]
    (largest_below,) = sweep(hi, ("below",))
    one_short = jnp.logical_and(act > 0.5, topk - chi == 1.0)
    state = (jnp.where(one_short, largest_below, lo), jnp.where(one_short, largest_below, hi), chi,
             jnp.where(one_short, 0.0, act))

    def snap_cond(st):
        return jnp.logical_and(st[1] > 0.5, st[0] < 4096)

    def snap_body(st):
        state = st[2:]
        mid = midpoint(state)
        mid = jnp.where(mid >= state[1], state[0], mid)
        cnt, above, below = sweep(mid, ("gt", "above", "below"))
        state = update(state, mid, cnt, False, above, below)
        return (st[0] + 1, jnp.max(state[3])) + state

    st = lax.while_loop(snap_cond, snap_body, (jnp.int32(0), jnp.max(state[3])) + state)
    thr = st[3]
    need = topk - st[4]
    any_tie = jnp.max(need) > 0.5

    m_ref[...] = jnp.full(m_ref.shape, MASKED, jnp.float32)
    acc_ref[...] = jnp.zeros(acc_ref.shape, jnp.float32)
    eqb_ref[...] = jnp.zeros(eqb_ref.shape, jnp.float32)

    def attend_chunk(c, size):
        k0 = pl.multiple_of(c * tk, tk)
        s_idx = sc_ref[pl.ds(k0, size), :]
        bias_ref[0:size, :] = jnp.where(s_idx > thr, 0.0, MASKED)

        @pl.when(any_tie)
        def _():
            eq = s_idx == thr
            eq_f = jnp.where(eq, 1.0, 0.0)
            rank = jnp.dot(tri_ref[0:size, 0:size], eq_f.astype(jnp.bfloat16),
                           preferred_element_type=jnp.float32) + eqb_ref[...]
            tie_bias = jnp.where(jnp.logical_and(eq, rank < need), 0.0, MASKED)
            bias_ref[0:size, :] = jnp.where(s_idx > thr, 0.0, tie_bias)
            eqb_ref[...] = eqb_ref[...] + _reduce_keys(eq_f, jnp.sum)

        kc = kv_ref[pl.ds(k0, size), 0:HEAD_DIM]
        vt = vt_ref[c, :, 0:size]

        def logits(h):
            s = jnp.dot(kc, qt_ref[h], preferred_element_type=jnp.float32) + bias_ref[0:size, :]
            s_ref[h, 0:size, :] = s
            mnew_ref[h] = jnp.maximum(m_ref[h], _reduce_keys(s, jnp.max))

        def weigh(h):
            m_new = mnew_ref[h]
            alpha = jnp.exp2(m_ref[h] - m_new)
            p = jnp.exp2(s_ref[h, 0:size, :] - m_new)
            acc_ref[h] = alpha * acc_ref[h] + jnp.dot(
                vt, p.astype(jnp.bfloat16), preferred_element_type=jnp.float32)
            m_ref[h] = m_new

        for h in range(N_HEADS):
            logits(h)
        for h in range(N_HEADS):
            weigh(h)

    def attend_full(c, _):
        attend_chunk(c, tk)
        return 0

    lax.fori_loop(0, n_full, attend_full, 0)

    @pl.when(has_tail)
    def _():
        attend_chunk(n_full, TAIL_KEYS)

    for h in range(N_HEADS):
        rows = slice(h * HEAD_DIM, (h + 1) * HEAD_DIM)
        out_ref[rows, :] = acc_ref[h, 0:HEAD_DIM, :] / acc_ref[h, HEAD_DIM:HEAD_DIM + 1, :]
    y = out_ref[...].T * _silu(ga_ref[...])
    y_ref[...] = y.astype(y_ref.dtype)


def _dsa_attention(qt, kv, vt, wi, ga, tri):
    batch, seq, _ = kv.shape
    row = lambda b, i: (b, i, 0)
    return pl.pallas_call(
        _attn_kernel,
        grid=(batch, seq // Q_TILE),
        in_specs=[
            pl.BlockSpec((None, N_HEADS + N_IDX_HEADS, HEAD_DIM, Q_TILE), lambda b, i: (b, 0, 0, i)),
            pl.BlockSpec((None, seq, 256), lambda b, i: (b, 0, 0)),
            pl.BlockSpec((None, seq // K_TILE, VT_ROWS, K_TILE), lambda b, i: (b, 0, 0, 0)),
            pl.BlockSpec((None, Q_TILE, 128), row),
            pl.BlockSpec((None, Q_TILE, D_ATTN), row),
            pl.BlockSpec((K_TILE, K_TILE), lambda b, i: (0, 0)),
        ],
        out_specs=pl.BlockSpec((None, Q_TILE, D_ATTN), row),
        out_shape=jax.ShapeDtypeStruct((batch, seq, D_ATTN), jnp.bfloat16),
        scratch_shapes=[
            pltpu.VMEM((seq, Q_TILE), jnp.float32),
            pltpu.VMEM((K_TILE, Q_TILE), jnp.float32),
            pltpu.VMEM((1, Q_TILE), jnp.float32),
            pltpu.VMEM((N_HEADS, K_TILE, Q_TILE), jnp.float32),
            pltpu.VMEM((N_HEADS, 1, Q_TILE), jnp.float32),
            pltpu.VMEM((N_HEADS, 1, Q_TILE), jnp.float32),
            pltpu.VMEM((N_HEADS, VT_ROWS, Q_TILE), jnp.float32),
            pltpu.VMEM((D_ATTN, Q_TILE), jnp.float32),
        ],
        compiler_params=pltpu.CompilerParams(
            dimension_semantics=("parallel", "parallel"), vmem_limit_bytes=VMEM_LIMIT),
        name="dsa_attention",
    )(qt, kv, vt, wi, ga, tri)


def _outproj_kernel(yr_ref, ya_ref, x_ref, mod_ref, wr_ref, wa_ref, g_ref, o_ref, *, final_norm):
    y = jnp.dot(yr_ref[...], wr_ref[...], preferred_element_type=jnp.float32)
    y = y + jnp.dot(ya_ref[...], wa_ref[...], preferred_element_type=jnp.float32)
    x_new = x_ref[...] + mod_ref[2:3, :] * y
    if final_norm:
        x_new = _rms(x_new, g_ref[...])
    o_ref[...] = x_new


def _out_projection(yr, ya, x, mod_l, w_r, w_a, final_g, final_norm):
    batch, seq, _ = x.shape
    row = lambda b, i: (b, i, 0)
    const = lambda b, i: (0, 0)
    return pl.pallas_call(
        functools.partial(_outproj_kernel, final_norm=final_norm),
        grid=(batch, seq // ROW_TILE),
        in_specs=[
            pl.BlockSpec((None, ROW_TILE, D_RNN), row),
            pl.BlockSpec((None, ROW_TILE, D_ATTN), row),
            pl.BlockSpec((None, ROW_TILE, D_MODEL), row),
            pl.BlockSpec((None, 3, D_MODEL), lambda b, i: (b, 0, 0)),
            pl.BlockSpec((D_RNN, D_MODEL), const),
            pl.BlockSpec((D_ATTN, D_MODEL), const),
            pl.BlockSpec((1, D_MODEL), const),
        ],
        out_specs=pl.BlockSpec((None, ROW_TILE, D_MODEL), row),
        out_shape=jax.ShapeDtypeStruct((batch, seq, D_MODEL), jnp.float32),
        compiler_params=pltpu.CompilerParams(
            dimension_semantics=("parallel", "parallel"), vmem_limit_bytes=VMEM_LIMIT),
        name="out_proj",
    )(yr, ya, x, mod_l, w_r, w_a, final_g.reshape(1, D_MODEL))


def _pad_w_in(w):
    return jnp.pad(w.astype(jnp.bfloat16), ((0, 0), (0, Z_COLS - D_IN)))


def _block_diag(w):
    n, c, d = w.shape
    eye = jnp.eye(n, dtype=w.dtype)
    return (eye[:, None, :, None] * w[:, :, None, :]).reshape(n * c, n * d)


def kernel(x, c, norm_g, ada_w, ada_b, w_in, conv_w, conv_b, lru_wx, lru_bx, lru_wa, lru_ba, lru_a, w_out, final_g):
    depth = w_in.shape[0]
    mod = _modulation(c, ada_w, ada_b)
    idx = jnp.arange(K_TILE)
    tri = (idx[None, :] < idx[:, None]).astype(jnp.bfloat16)
    for l in range(depth):
        w_pad = _pad_w_in(w_in[l])
        w_gates = jnp.concatenate([_block_diag(lru_wx[l]), _block_diag(lru_wa[l])], axis=-1).astype(jnp.bfloat16)
        b_gates = jnp.concatenate([lru_bx[l], lru_ba[l]]).reshape(1, 2 * D_RNN)
        w_o = w_out[l].astype(jnp.bfloat16)
        y_r, qt, ga, kv, wi, vt = _in_projection(x, mod[l], norm_g[l], w_pad, conv_w[l], conv_b[l],
                                                 w_gates, b_gates, lru_a[l])
        y_a = _dsa_attention(qt, kv, vt, wi, ga, tri)
        x = _out_projection(y_r, y_a, x, mod[l], w_o[:D_RNN], w_o[D_RNN:], final_g, l == depth - 1)
    return x
```

```python
import functools

import jax
import jax.numpy as jnp
from jax import lax
from jax.experimental import pallas as pl
from jax.experimental.pallas import tpu as pltpu

D_MODEL = 1024
D_RNN = 512
N_RNN_BLOCKS = 8
RNN_BLOCK = D_RNN // N_RNN_BLOCKS
CONV_WIDTH = 4
LRU_C = 8.0
N_HEADS = 8
HEAD_DIM = 64
D_ATTN = N_HEADS * HEAD_DIM
N_IDX_HEADS = 8
IDX_DIM = 64
MAX_TOPK = 256
EPS = 1e-6
LOG2_E = 1.4426950408889634

Z_RNN = 0
Z_Q = 1024
Z_KV = 1536
Z_GA = 1664
Z_QI = 2176
Z_KI = 2688
Z_COLS = 2816
D_IN = 2760
WI_LANE = 64

ROW_TILE = 512
Q_TILE = 256
K_TILE = 512
VT_ROWS = 80
MASKED = -1e30
COARSE_PROBES = 9
PROBES_PER_ROUND = 7
PLAIN_BISECT_ROUNDS = 5
PARTIAL_ROWS = 32
SWEEP_TILE = 256
TAIL_KEYS = K_TILE - Q_TILE
assert K_TILE == 2 * Q_TILE
VMEM_LIMIT = 48 * 1024 * 1024


def _silu(x):
    return x * jax.nn.sigmoid(x)


def _rms(x, g):
    return x * lax.rsqrt(jnp.mean(x * x, axis=-1, keepdims=True) + EPS) * g


def _reduce_keys(x, op):
    keys, queries = x.shape
    part = op(x.reshape(keys // PARTIAL_ROWS, PARTIAL_ROWS, queries), axis=0)
    return op(part, axis=0, keepdims=True)


def _mod_kernel(c_ref, w_ref, b_ref, o_ref):
    c_act = _silu(c_ref[...])
    o_ref[...] = jnp.dot(c_act, w_ref[...], precision=lax.Precision.HIGHEST,
                         preferred_element_type=jnp.float32) + b_ref[...]


def _modulation(c, ada_w, ada_b):
    depth = ada_w.shape[0]
    batch = c.shape[0]
    out = pl.pallas_call(
        _mod_kernel,
        grid=(depth, 3),
        in_specs=[
            pl.BlockSpec((batch, D_MODEL), lambda l, j: (0, 0)),
            pl.BlockSpec((None, D_MODEL, D_MODEL), lambda l, j: (l, 0, j)),
            pl.BlockSpec((None, None, 1, D_MODEL), lambda l, j: (l, j, 0, 0)),
        ],
        out_specs=pl.BlockSpec((None, None, batch, D_MODEL), lambda l, j: (l, j, 0, 0)),
        out_shape=jax.ShapeDtypeStruct((depth, 3, batch, D_MODEL), jnp.float32),
        compiler_params=pltpu.CompilerParams(vmem_limit_bytes=VMEM_LIMIT),
        name="adaln_mod",
    )(c, ada_w, ada_b.reshape(depth, 3, 1, D_MODEL))
    return out.transpose(0, 2, 1, 3)


def _inproj_kernel(x_ref, mod_ref, g_ref, w_ref, cw_ref, cb_ref, wg_ref, bg_ref, ap_ref,
                   yr_ref, qt_ref, ga_ref, kv_ref, wi_ref, vt_ref, xbuf_ref, h_ref):
    @pl.when(pl.program_id(1) == 0)
    def _():
        xbuf_ref[0:8, :] = jnp.zeros((8, D_RNN), jnp.float32)
        h_ref[...] = jnp.zeros((1, D_RNN), jnp.float32)

    x = x_ref[...]
    shift = mod_ref[0:1, :]
    scale = mod_ref[1:2, :]
    h = _rms(x, g_ref[...]) * (1.0 + scale) + shift
    h = h.astype(jnp.bfloat16)
    z_rnn = jnp.dot(h, w_ref[:, Z_RNN:Z_Q], preferred_element_type=jnp.float32)
    y_r = _rg_lru_tile(z_rnn[:, 0:D_RNN], z_rnn[:, D_RNN:2 * D_RNN],
                       cw_ref, cb_ref, wg_ref, bg_ref, ap_ref, xbuf_ref, h_ref)
    yr_ref[...] = y_r.astype(yr_ref.dtype)
    z = jnp.dot(h, w_ref[:, Z_Q:Z_COLS], preferred_element_type=jnp.float32)
    col = lambda start, width: z[:, start - Z_Q:start - Z_Q + width]
    for h in range(N_HEADS):
        q_h = col(Z_Q + h * HEAD_DIM, HEAD_DIM) * (HEAD_DIM ** -0.5 * LOG2_E)
        qt_ref[h] = q_h.T.astype(jnp.bfloat16)
    for h in range(N_IDX_HEADS):
        qt_ref[N_HEADS + h] = col(Z_QI + h * IDX_DIM, IDX_DIM).T.astype(jnp.bfloat16)
    ga_ref[...] = col(Z_GA, D_ATTN)
    kv_ref[...] = jnp.concatenate([col(Z_KV, 128), col(Z_KI, 128)], axis=-1).astype(jnp.bfloat16)
    wi_ref[...] = col(Z_KI, 128)
    pad_rows = lax.broadcasted_iota(jnp.int32, (VT_ROWS - HEAD_DIM, K_TILE), 0)
    for j in range(ROW_TILE // K_TILE):
        v_t = col(Z_KV + HEAD_DIM, HEAD_DIM)[j * K_TILE:(j + 1) * K_TILE, :].T
        vt_ref[j, 0:HEAD_DIM, :] = v_t.astype(jnp.bfloat16)
        vt_ref[j, HEAD_DIM:VT_ROWS, :] = jnp.where(pad_rows == 0, 1.0, 0.0).astype(jnp.bfloat16)


def _in_projection(x, mod_l, norm_g, w_pad, conv_w, conv_b, w_gates, b_gates, a_param):
    batch, seq, _ = x.shape
    row = lambda b, i: (b, i, 0)
    const = lambda b, i: (0, 0)
    return pl.pallas_call(
        _inproj_kernel,
        grid=(batch, seq // ROW_TILE),
        in_specs=[
            pl.BlockSpec((None, ROW_TILE, D_MODEL), row),
            pl.BlockSpec((None, 3, D_MODEL), lambda b, i: (b, 0, 0)),
            pl.BlockSpec((1, D_MODEL), const),
            pl.BlockSpec((D_MODEL, Z_COLS), const),
            pl.BlockSpec((CONV_WIDTH, D_RNN), const),
            pl.BlockSpec((1, D_RNN), const),
            pl.BlockSpec((D_RNN, 2 * D_RNN), const),
            pl.BlockSpec((1, 2 * D_RNN), const),
            pl.BlockSpec((1, D_RNN), const),
        ],
        out_specs=[
            pl.BlockSpec((None, ROW_TILE, D_RNN), row),
            pl.BlockSpec((None, N_HEADS + N_IDX_HEADS, HEAD_DIM, ROW_TILE), lambda b, i: (b, 0, 0, i)),
            pl.BlockSpec((None, ROW_TILE, 512), row),
            pl.BlockSpec((None, ROW_TILE, 256), row),
            pl.BlockSpec((None, ROW_TILE, 128), row),
            pl.BlockSpec((None, ROW_TILE // K_TILE, VT_ROWS, K_TILE), lambda b, i: (b, i, 0, 0)),
        ],
        out_shape=[
            jax.ShapeDtypeStruct((batch, seq, D_RNN), jnp.bfloat16),
            jax.ShapeDtypeStruct((batch, N_HEADS + N_IDX_HEADS, HEAD_DIM, seq), jnp.bfloat16),
            jax.ShapeDtypeStruct((batch, seq, 512), jnp.float32),
            jax.ShapeDtypeStruct((batch, seq, 256), jnp.bfloat16),
            jax.ShapeDtypeStruct((batch, seq, 128), jnp.float32),
            jax.ShapeDtypeStruct((batch, seq // K_TILE, VT_ROWS, K_TILE), jnp.bfloat16),
        ],
        scratch_shapes=[
            pltpu.VMEM((ROW_TILE + 8, D_RNN), jnp.float32),
            pltpu.VMEM((1, D_RNN), jnp.float32),
        ],
        compiler_params=pltpu.CompilerParams(
            dimension_semantics=("parallel", "arbitrary"), vmem_limit_bytes=VMEM_LIMIT),
        name="in_proj_rg_lru",
    )(x, mod_l, norm_g.reshape(1, D_MODEL), w_pad, conv_w, conv_b.reshape(1, D_RNN),
      w_gates, b_gates, a_param.reshape(1, D_RNN))


def _rg_lru_tile(xr, gr, cw_ref, cb_ref, wg_ref, bg_ref, ap_ref, xbuf_ref, h_ref):
    ts = ROW_TILE
    xbuf_ref[8:8 + ts, :] = xr
    xc = cw_ref[3:4, :] * xr + cb_ref[...]
    for k in range(CONV_WIDTH - 1):
        back = CONV_WIDTH - 1 - k
        xc = xc + cw_ref[k:k + 1, :] * xbuf_ref[8 - back:8 - back + ts, :]
    xbuf_ref[0:8, :] = xbuf_ref[ts:ts + 8, :]

    gates = jnp.dot(xc.astype(jnp.bfloat16), wg_ref[...], preferred_element_type=jnp.float32) + bg_ref[...]
    gate_x = jax.nn.sigmoid(gates[:, 0:D_RNN])
    gate_a = jax.nn.sigmoid(gates[:, D_RNN:2 * D_RNN])
    neg_ap = -ap_ref[...]
    softplus = jnp.maximum(neg_ap, 0.0) + jnp.log(1.0 + jnp.exp(-jnp.abs(neg_ap)))
    log_a = (-LRU_C) * gate_a * softplus
    a = jnp.exp(log_a)
    mult = jnp.sqrt(1.0 - a * a)
    u = mult * gate_x * xc

    groups = ts // 8
    a = a.reshape(groups, 8, D_RNN)
    u = u.reshape(groups, 8, D_RNN)
    sub = lax.broadcasted_iota(jnp.int32, (groups, 8, D_RNN), 1)
    for d in (1, 2, 4):
        keep = sub >= d
        u = jnp.where(keep, u + a * pltpu.roll(u, d, 1), u)
        a = jnp.where(keep, a * pltpu.roll(a, d, 1), a)
    carry = h_ref[...]
    hs = []
    for g in range(groups):
        h_g = u[g] + a[g] * carry
        hs.append(h_g)
        carry = h_g[7:8, :]
    h_ref[...] = carry
    return jnp.concatenate(hs, axis=0) * _silu(gr)


def _attn_kernel(qt_ref, kv_ref, vt_ref, wi_ref, ga_ref, tri_ref, y_ref,
                 sc_ref, sc16_ref, bias_ref, eqb_ref, s_ref, m_ref, mnew_ref, acc_ref, out_ref):
    tq, tk = Q_TILE, K_TILE
    topk = float(MAX_TOPK)
    qb = pl.program_id(1)
    n_keys = (qb + 1) * tq
    n_full = n_keys // tk
    has_tail = n_keys - n_full * tk > 0

    w_t = wi_ref[...].T[WI_LANE:WI_LANE + N_IDX_HEADS, :] * ((IDX_DIM ** -0.5) * (N_IDX_HEADS ** -0.5))

    def score_chunk(c, carry, size):
        rmax, rmin = carry
        k0 = pl.multiple_of(c * tk, tk)
        ki = kv_ref[pl.ds(k0, size), 128:128 + IDX_DIM]
        score = jnp.zeros((size, tq), jnp.float32)
        for h in range(N_IDX_HEADS):
            logits = jnp.dot(ki, qt_ref[N_HEADS + h], preferred_element_type=jnp.float32)
            score = score + jnp.maximum(logits, 0.0) * w_t[h:h + 1, :]
        key_pos = k0 + lax.broadcasted_iota(jnp.int32, (size, tq), 0)
        causal = key_pos <= qb * tq + lax.broadcasted_iota(jnp.int32, (size, tq), 1)
        masked = jnp.where(causal, score, -jnp.inf)
        sc_ref[pl.ds(k0, size), :] = masked
        sc16_ref[pl.ds(k0, size), :] = masked.astype(jnp.bfloat16)
        rmax = jnp.maximum(rmax, _reduce_keys(jnp.where(causal, score, -jnp.inf), jnp.max))
        rmin = jnp.minimum(rmin, _reduce_keys(jnp.where(causal, score, jnp.inf), jnp.min))
        return rmax, rmin

    extremes = lax.fori_loop(
        0, n_full, functools.partial(score_chunk, size=tk),
        (jnp.full((1, tq), -jnp.inf, jnp.float32), jnp.full((1, tq), jnp.inf, jnp.float32)))
    rmax, rmin = lax.cond(has_tail, lambda e: score_chunk(n_full, e, TAIL_KEYS), lambda e: e, extremes)

    n_sweep_tiles = ((qb + 1) * tq) // SWEEP_TILE

    def sweep(mid, want):
        kinds = {"gt": (jnp.sum, 0.0), "ge": (jnp.sum, 0.0), "above": (jnp.min, jnp.inf), "below": (jnp.max, -jnp.inf)}

        def body(c, carry):
            k0 = pl.multiple_of(c * SWEEP_TILE, SWEEP_TILE)
            s = sc_ref[pl.ds(k0, SWEEP_TILE), :].reshape(SWEEP_TILE // PARTIAL_ROWS, PARTIAL_ROWS, tq)
            gt = s > mid
            terms = {"gt": lambda: jnp.where(gt, 1.0, 0.0), "ge": lambda: jnp.where(s >= mid, 1.0, 0.0),
                     "above": lambda: jnp.where(gt, s, jnp.inf), "below": lambda: jnp.where(gt, -jnp.inf, s)}
            out = []
            for name, acc in zip(want, carry):
                op = kinds[name][0]
                part = op(terms[name](), axis=0)
                out.append(acc + part if op is jnp.sum else
                           (jnp.minimum(acc, part) if op is jnp.min else jnp.maximum(acc, part)))
            return tuple(out)

        init = tuple(jnp.full((PARTIAL_ROWS, tq), kinds[name][1], jnp.float32) for name in want)
        res = lax.fori_loop(0, n_sweep_tiles, body, init)
        return [kinds[name][0](r, axis=0, keepdims=True) for name, r in zip(want, res)]

    def update(state, mid, cnt, tie, new_lo, new_hi):
        lo, hi, chi, act = state
        on = act > 0.5
        fin = jnp.logical_and(on, jnp.logical_or(cnt == topk, tie))
        go_on = jnp.logical_and(on, jnp.logical_not(fin))
        up = jnp.logical_and(go_on, cnt > topk)
        dn = jnp.logical_and(go_on, cnt < topk)
        lo = jnp.where(fin, mid, jnp.where(up, new_lo, lo))
        hi = jnp.where(fin, mid, jnp.where(dn, new_hi, hi))
        chi = jnp.where(jnp.logical_or(fin, dn), cnt, chi)
        act = jnp.where(jnp.logical_and(go_on, lo < hi), 1.0, 0.0)
        return lo, hi, chi, act

    def midpoint(state):
        return 0.5 * state[0] + 0.5 * state[1]

    n_valid = (qb * tq + 1 + lax.broadcasted_iota(jnp.int32, (1, tq), 1)).astype(jnp.float32)
    few = n_valid <= topk
    state = (rmin,
             jnp.where(few, -jnp.inf, rmax),
             jnp.where(few, topk, 0.0),
             jnp.where(jnp.logical_or(few, rmin >= rmax), 0.0, 1.0))

    mid = jnp.minimum(jnp.maximum(0.0, state[0]), state[1])
    cnt, cge = sweep(mid, ("gt", "ge"))
    state = update(state, mid, cnt, jnp.logical_and(cnt < topk, cge >= topk), mid, mid)

    def coarse_count(t16):
        def body(c, acc):
            k0 = pl.multiple_of(c * SWEEP_TILE, SWEEP_TILE)
            s = sc16_ref[pl.ds(k0, SWEEP_TILE), :].reshape(SWEEP_TILE // PARTIAL_ROWS, PARTIAL_ROWS, tq)
            one, zero = jnp.ones((), jnp.bfloat16), jnp.zeros((), jnp.bfloat16)
            hits = jnp.where(s > t16, one, zero)
            parts = [hits[g] for g in range(SWEEP_TILE // PARTIAL_ROWS)]
            while len(parts) > 1:
                parts = [parts[i] + parts[i + 1] for i in range(0, len(parts), 2)]
            return acc + parts[0]
        acc = lax.fori_loop(0, n_sweep_tiles, body, jnp.zeros((PARTIAL_ROWS, tq), jnp.bfloat16))
        return jnp.sum(acc.astype(jnp.float32), axis=0, keepdims=True)

    lo, hi, chi, act = state
    for _ in range(COARSE_PROBES):
        t16 = (0.5 * lo + 0.5 * hi).astype(jnp.bfloat16)
        t = t16.astype(jnp.float32)
        cnt16 = coarse_count(t16)
        on = act > 0.5
        lo = jnp.where(jnp.logical_and(on, cnt16 >= topk), jnp.maximum(lo, t), lo)
        hi = jnp.where(jnp.logical_and(on, cnt16 < topk), jnp.minimum(hi, t + jnp.abs(t) * (2.0 ** -7) + 1e-30), hi)
    (cnt,) = sweep(hi, ("gt",))
    state = update((lo, hi, chi, act), hi, cnt, False, hi, hi)

    def pending(state):
        return jnp.where(jnp.logical_and(state[3] > 0.5, topk - state[2] != 1.0), 1.0, 0.0)

    def plain_cond(st):
        return jnp.logical_and(st[1] > 0.5, st[0] < PLAIN_BISECT_ROUNDS)

    def plain_body(st):
        state = st[2:]
        for _ in range(PROBES_PER_ROUND):
            mid = midpoint(state)
            (cnt,) = sweep(mid, ("gt",))
            state = update(state, mid, cnt, False, mid, mid)
        return (st[0] + 1, jnp.max(pending(state))) + state

    st = lax.while_loop(plain_cond, plain_body, (jnp.int32(0), jnp.max(pending(state))) + state)
    lo, hi, chi, act = st[2:]
    (largest_below,) = sweep(hi, ("below",))
    one_short = jnp.logical_and(act > 0.5, topk - chi == 1.0)
    state = (jnp.where(one_short, largest_below, lo), jnp.where(one_short, largest_below, hi), chi,
             jnp.where(one_short, 0.0, act))

    def snap_cond(st):
        return jnp.logical_and(st[1] > 0.5, st[0] < 4096)

    def snap_body(st):
        state = st[2:]
        mid = midpoint(state)
        mid = jnp.where(mid >= state[1], state[0], mid)
        cnt, above, below = sweep(mid, ("gt", "above", "below"))
        state = update(state, mid, cnt, False, above, below)
        return (st[0] + 1, jnp.max(state[3])) + state

    st = lax.while_loop(snap_cond, snap_body, (jnp.int32(0), jnp.max(state[3])) + state)
    thr = st[3]
    need = topk - st[4]
    any_tie = jnp.max(need) > 0.5

    m_ref[...] = jnp.full(m_ref.shape, MASKED, jnp.float32)
    acc_ref[...] = jnp.zeros(acc_ref.shape, jnp.float32)
    eqb_ref[...] = jnp.zeros(eqb_ref.shape, jnp.float32)

    def attend_chunk(c, size):
        k0 = pl.multiple_of(c * tk, tk)
        s_idx = sc_ref[pl.ds(k0, size), :]
        bias_ref[0:size, :] = jnp.where(s_idx > thr, 0.0, MASKED)

        @pl.when(any_tie)
        def _():
            eq = s_idx == thr
            eq_f = jnp.where(eq, 1.0, 0.0)
            rank = jnp.dot(tri_ref[0:size, 0:size], eq_f.astype(jnp.bfloat16),
                           preferred_element_type=jnp.float32) + eqb_ref[...]
            tie_bias = jnp.where(jnp.logical_and(eq, rank < need), 0.0, MASKED)
            bias_ref[0:size, :] = jnp.where(s_idx > thr, 0.0, tie_bias)
            eqb_ref[...] = eqb_ref[...] + _reduce_keys(eq_f, jnp.sum)

        kc = kv_ref[pl.ds(k0, size), 0:HEAD_DIM]
        vt = vt_ref[c, :, 0:size]

        def logits(h):
            s = jnp.dot(kc, qt_ref[h], preferred_element_type=jnp.float32) + bias_ref[0:size, :]
            s_ref[h, 0:size, :] = s
            mnew_ref[h] = jnp.maximum(m_ref[h], _reduce_keys(s, jnp.max))

        def weigh(h):
            m_new = mnew_ref[h]
            alpha = jnp.exp2(m_ref[h] - m_new)
            p = jnp.exp2(s_ref[h, 0:size, :] - m_new)
            acc_ref[h] = alpha * acc_ref[h] + jnp.dot(
                vt, p.astype(jnp.bfloat16), preferred_element_type=jnp.float32)
            m_ref[h] = m_new

        for h in range(N_HEADS):
            logits(h)
        for h in range(N_HEADS):
            weigh(h)

    def attend_full(c, _):
        attend_chunk(c, tk)
        return 0

    lax.fori_loop(0, n_full, attend_full, 0)

    @pl.when(has_tail)
    def _():
        attend_chunk(n_full, TAIL_KEYS)

    for h in range(N_HEADS):
        rows = slice(h * HEAD_DIM, (h + 1) * HEAD_DIM)
        out_ref[rows, :] = acc_ref[h, 0:HEAD_DIM, :] / acc_ref[h, HEAD_DIM:HEAD_DIM + 1, :]
    y = out_ref[...].T * _silu(ga_ref[...])
    y_ref[...] = y.astype(y_ref.dtype)


def _dsa_attention(qt, kv, vt, wi, ga, tri):
    batch, seq, _ = kv.shape
    row = lambda b, i: (b, i, 0)
    return pl.pallas_call(
        _attn_kernel,
        grid=(batch, seq // Q_TILE),
        in_specs=[
            pl.BlockSpec((None, N_HEADS + N_IDX_HEADS, HEAD_DIM, Q_TILE), lambda b, i: (b, 0, 0, i)),
            pl.BlockSpec((None, seq, 256), lambda b, i: (b, 0, 0)),
            pl.BlockSpec((None, seq // K_TILE, VT_ROWS, K_TILE), lambda b, i: (b, 0, 0, 0)),
            pl.BlockSpec((None, Q_TILE, 128), row),
            pl.BlockSpec((None, Q_TILE, D_ATTN), row),
            pl.BlockSpec((K_TILE, K_TILE), lambda b, i: (0, 0)),
        ],
        out_specs=pl.BlockSpec((None, Q_TILE, D_ATTN), row),
        out_shape=jax.ShapeDtypeStruct((batch, seq, D_ATTN), jnp.bfloat16),
        scratch_shapes=[
            pltpu.VMEM((seq, Q_TILE), jnp.float32),
            pltpu.VMEM((seq, Q_TILE), jnp.bfloat16),
            pltpu.VMEM((K_TILE, Q_TILE), jnp.float32),
            pltpu.VMEM((1, Q_TILE), jnp.float32),
            pltpu.VMEM((N_HEADS, K_TILE, Q_TILE), jnp.float32),
            pltpu.VMEM((N_HEADS, 1, Q_TILE), jnp.float32),
            pltpu.VMEM((N_HEADS, 1, Q_TILE), jnp.float32),
            pltpu.VMEM((N_HEADS, VT_ROWS, Q_TILE), jnp.float32),
            pltpu.VMEM((D_ATTN, Q_TILE), jnp.float32),
        ],
        compiler_params=pltpu.CompilerParams(
            dimension_semantics=("parallel", "parallel"), vmem_limit_bytes=VMEM_LIMIT),
        name="dsa_attention",
    )(qt, kv, vt, wi, ga, tri)


def _outproj_kernel(yr_ref, ya_ref, x_ref, mod_ref, wr_ref, wa_ref, g_ref, o_ref, *, final_norm):
    y = jnp.dot(yr_ref[...], wr_ref[...], preferred_element_type=jnp.float32)
    y = y + jnp.dot(ya_ref[...], wa_ref[...], preferred_element_type=jnp.float32)
    x_new = x_ref[...] + mod_ref[2:3, :] * y
    if final_norm:
        x_new = _rms(x_new, g_ref[...])
    o_ref[...] = x_new


def _out_projection(yr, ya, x, mod_l, w_r, w_a, final_g, final_norm):
    batch, seq, _ = x.shape
    row = lambda b, i: (b, i, 0)
    const = lambda b, i: (0, 0)
    return pl.pallas_call(
        functools.partial(_outproj_kernel, final_norm=final_norm),
        grid=(batch, seq // ROW_TILE),
        in_specs=[
            pl.BlockSpec((None, ROW_TILE, D_RNN), row),
            pl.BlockSpec((None, ROW_TILE, D_ATTN), row),
            pl.BlockSpec((None, ROW_TILE, D_MODEL), row),
            pl.BlockSpec((None, 3, D_MODEL), lambda b, i: (b, 0, 0)),
            pl.BlockSpec((D_RNN, D_MODEL), const),
            pl.BlockSpec((D_ATTN, D_MODEL), const),
            pl.BlockSpec((1, D_MODEL), const),
        ],
        out_specs=pl.BlockSpec((None, ROW_TILE, D_MODEL), row),
        out_shape=jax.ShapeDtypeStruct((batch, seq, D_MODEL), jnp.float32),
        compiler_params=pltpu.CompilerParams(
            dimension_semantics=("parallel", "parallel"), vmem_limit_bytes=VMEM_LIMIT),
        name="out_proj",
    )(yr, ya, x, mod_l, w_r, w_a, final_g.reshape(1, D_MODEL))


def _pad_w_in(w):
    return jnp.pad(w.astype(jnp.bfloat16), ((0, 0), (0, Z_COLS - D_IN)))


def _block_diag(w):
    n, c, d = w.shape
    eye = jnp.eye(n, dtype=w.dtype)
    return (eye[:, None, :, None] * w[:, :, None, :]).reshape(n * c, n * d)


def kernel(x, c, norm_g, ada_w, ada_b, w_in, conv_w, conv_b, lru_wx, lru_bx, lru_wa, lru_ba, lru_a, w_out, final_g):
    depth = w_in.shape[0]
    mod = _modulation(c, ada_w, ada_b)
    idx = jnp.arange(K_TILE)
    tri = (idx[None, :] < idx[:, None]).astype(jnp.bfloat16)
    for l in range(depth):
        w_pad = _pad_w_in(w_in[l])
        w_gates = jnp.concatenate([_block_diag(lru_wx[l]), _block_diag(lru_wa[l])], axis=-1).astype(jnp.bfloat16)
        b_gates = jnp.concatenate([lru_bx[l], lru_ba[l]]).reshape(1, 2 * D_RNN)
        w_o = w_out[l].astype(jnp.bfloat16)
        y_r, qt, ga, kv, wi, vt = _in_projection(x, mod[l], norm_g[l], w_pad, conv_w[l], conv_b[l],
                                                 w_gates, b_gates, lru_a[l])
        y_a = _dsa_attention(qt, kv, vt, wi, ga, tri)
        x = _out_projection(y_r, y_a, x, mod[l], w_o[:D_RNN], w_o[D_RNN:], final_g, l == depth - 1)
    return x
```

```python
import functools

import jax
import jax.numpy as jnp
from jax import lax
from jax.experimental import pallas as pl
from jax.experimental.pallas import tpu as pltpu

D_MODEL = 1024
D_RNN = 512
N_RNN_BLOCKS = 8
RNN_BLOCK = D_RNN // N_RNN_BLOCKS
CONV_WIDTH = 4
LRU_C = 8.0
N_HEADS = 8
HEAD_DIM = 64
D_ATTN = N_HEADS * HEAD_DIM
N_IDX_HEADS = 8
IDX_DIM = 64
MAX_TOPK = 256
EPS = 1e-6
LOG2_E = 1.4426950408889634

Z_RNN = 0
Z_Q = 1024
Z_KV = 1536
Z_GA = 1664
Z_QI = 2176
Z_KI = 2688
Z_COLS = 2816
D_IN = 2760
WI_LANE = 64

ROW_TILE = 512
Q_TILE = 256
K_TILE = 512
VT_ROWS = 80
MASKED = -1e30
COARSE_PROBES = 9
PROBES_PER_ROUND = 7
PLAIN_BISECT_ROUNDS = 5
PARTIAL_ROWS = 32
SWEEP_TILE = 256
TAIL_KEYS = K_TILE - Q_TILE
assert K_TILE == 2 * Q_TILE
VMEM_LIMIT = 48 * 1024 * 1024


def _silu(x):
    return x * jax.nn.sigmoid(x)


def _rms(x, g):
    return x * lax.rsqrt(jnp.mean(x * x, axis=-1, keepdims=True) + EPS) * g


def _reduce_keys(x, op):
    keys, queries = x.shape
    part = op(x.reshape(keys // PARTIAL_ROWS, PARTIAL_ROWS, queries), axis=0)
    return op(part, axis=0, keepdims=True)


def _mod_kernel(c_ref, w_ref, b_ref, o_ref):
    c_act = _silu(c_ref[...])
    o_ref[...] = jnp.dot(c_act, w_ref[...], precision=lax.Precision.HIGHEST,
                         preferred_element_type=jnp.float32) + b_ref[...]


def _modulation(c, ada_w, ada_b):
    depth = ada_w.shape[0]
    batch = c.shape[0]
    out = pl.pallas_call(
        _mod_kernel,
        grid=(depth, 3),
        in_specs=[
            pl.BlockSpec((batch, D_MODEL), lambda l, j: (0, 0)),
            pl.BlockSpec((None, D_MODEL, D_MODEL), lambda l, j: (l, 0, j)),
            pl.BlockSpec((None, None, 1, D_MODEL), lambda l, j: (l, j, 0, 0)),
        ],
        out_specs=pl.BlockSpec((None, None, batch, D_MODEL), lambda l, j: (l, j, 0, 0)),
        out_shape=jax.ShapeDtypeStruct((depth, 3, batch, D_MODEL), jnp.float32),
        compiler_params=pltpu.CompilerParams(vmem_limit_bytes=VMEM_LIMIT),
        name="adaln_mod",
    )(c, ada_w, ada_b.reshape(depth, 3, 1, D_MODEL))
    return out.transpose(0, 2, 1, 3)


def _inproj_kernel(x_ref, mod_ref, g_ref, w_ref, cw_ref, cb_ref, wg_ref, bg_ref, ap_ref,
                   yr_ref, qt_ref, ga_ref, kv_ref, wi_ref, vt_ref, xbuf_ref, h_ref):
    @pl.when(pl.program_id(1) == 0)
    def _():
        xbuf_ref[0:8, :] = jnp.zeros((8, D_RNN), jnp.float32)
        h_ref[...] = jnp.zeros((1, D_RNN), jnp.float32)

    x = x_ref[...]
    shift = mod_ref[0:1, :]
    scale = mod_ref[1:2, :]
    h = _rms(x, g_ref[...]) * (1.0 + scale) + shift
    h = h.astype(jnp.bfloat16)
    z_rnn = jnp.dot(h, w_ref[:, Z_RNN:Z_Q], preferred_element_type=jnp.float32)
    y_r = _rg_lru_tile(z_rnn[:, 0:D_RNN], z_rnn[:, D_RNN:2 * D_RNN],
                       cw_ref, cb_ref, wg_ref, bg_ref, ap_ref, xbuf_ref, h_ref)
    yr_ref[...] = y_r.astype(yr_ref.dtype)
    z = jnp.dot(h, w_ref[:, Z_Q:Z_COLS], preferred_element_type=jnp.float32)
    col = lambda start, width: z[:, start - Z_Q:start - Z_Q + width]
    for h in range(N_HEADS):
        q_h = col(Z_Q + h * HEAD_DIM, HEAD_DIM) * (HEAD_DIM ** -0.5 * LOG2_E)
        qt_ref[h] = q_h.T.astype(jnp.bfloat16)
    for h in range(N_IDX_HEADS):
        qt_ref[N_HEADS + h] = col(Z_QI + h * IDX_DIM, IDX_DIM).T.astype(jnp.bfloat16)
    ga_ref[...] = col(Z_GA, D_ATTN)
    kv_ref[...] = jnp.concatenate([col(Z_KV, 128), col(Z_KI, 128)], axis=-1).astype(jnp.bfloat16)
    wi_ref[...] = col(Z_KI, 128)
    pad_rows = lax.broadcasted_iota(jnp.int32, (VT_ROWS - HEAD_DIM, K_TILE), 0)
    for j in range(ROW_TILE // K_TILE):
        v_t = col(Z_KV + HEAD_DIM, HEAD_DIM)[j * K_TILE:(j + 1) * K_TILE, :].T
        vt_ref[j, 0:HEAD_DIM, :] = v_t.astype(jnp.bfloat16)
        vt_ref[j, HEAD_DIM:VT_ROWS, :] = jnp.where(pad_rows == 0, 1.0, 0.0).astype(jnp.bfloat16)


def _in_projection(x, mod_l, norm_g, w_pad, conv_w, conv_b, w_gates, b_gates, a_param):
    batch, seq, _ = x.shape
    row = lambda b, i: (b, i, 0)
    const = lambda b, i: (0, 0)
    return pl.pallas_call(
        _inproj_kernel,
        grid=(batch, seq // ROW_TILE),
        in_specs=[
            pl.BlockSpec((None, ROW_TILE, D_MODEL), row),
            pl.BlockSpec((None, 3, D_MODEL), lambda b, i: (b, 0, 0)),
            pl.BlockSpec((1, D_MODEL), const),
            pl.BlockSpec((D_MODEL, Z_COLS), const),
            pl.BlockSpec((CONV_WIDTH, D_RNN), const),
            pl.BlockSpec((1, D_RNN), const),
            pl.BlockSpec((D_RNN, 2 * D_RNN), const),
            pl.BlockSpec((1, 2 * D_RNN), const),
            pl.BlockSpec((1, D_RNN), const),
        ],
        out_specs=[
            pl.BlockSpec((None, ROW_TILE, D_RNN), row),
            pl.BlockSpec((None, N_HEADS + N_IDX_HEADS, HEAD_DIM, ROW_TILE), lambda b, i: (b, 0, 0, i)),
            pl.BlockSpec((None, ROW_TILE, 512), row),
            pl.BlockSpec((None, ROW_TILE, 256), row),
            pl.BlockSpec((None, ROW_TILE, 128), row),
            pl.BlockSpec((None, ROW_TILE // K_TILE, VT_ROWS, K_TILE), lambda b, i: (b, i, 0, 0)),
        ],
        out_shape=[
            jax.ShapeDtypeStruct((batch, seq, D_RNN), jnp.bfloat16),
            jax.ShapeDtypeStruct((batch, N_HEADS + N_IDX_HEADS, HEAD_DIM, seq), jnp.bfloat16),
            jax.ShapeDtypeStruct((batch, seq, 512), jnp.float32),
            jax.ShapeDtypeStruct((batch, seq, 256), jnp.bfloat16),
            jax.ShapeDtypeStruct((batch, seq, 128), jnp.float32),
            jax.ShapeDtypeStruct((batch, seq // K_TILE, VT_ROWS, K_TILE), jnp.bfloat16),
        ],
        scratch_shapes=[
            pltpu.VMEM((ROW_TILE + 8, D_RNN), jnp.float32),
            pltpu.VMEM((1, D_RNN), jnp.float32),
        ],
        compiler_params=pltpu.CompilerParams(
            dimension_semantics=("parallel", "arbitrary"), vmem_limit_bytes=VMEM_LIMIT),
        name="in_proj_rg_lru",
    )(x, mod_l, norm_g.reshape(1, D_MODEL), w_pad, conv_w, conv_b.reshape(1, D_RNN),
      w_gates, b_gates, a_param.reshape(1, D_RNN))


def _rg_lru_tile(xr, gr, cw_ref, cb_ref, wg_ref, bg_ref, ap_ref, xbuf_ref, h_ref):
    ts = ROW_TILE
    xbuf_ref[8:8 + ts, :] = xr
    xc = cw_ref[3:4, :] * xr + cb_ref[...]
    for k in range(CONV_WIDTH - 1):
        back = CONV_WIDTH - 1 - k
        xc = xc + cw_ref[k:k + 1, :] * xbuf_ref[8 - back:8 - back + ts, :]
    xbuf_ref[0:8, :] = xbuf_ref[ts:ts + 8, :]

    gates = jnp.dot(xc.astype(jnp.bfloat16), wg_ref[...], preferred_element_type=jnp.float32) + bg_ref[...]
    gate_x = jax.nn.sigmoid(gates[:, 0:D_RNN])
    gate_a = jax.nn.sigmoid(gates[:, D_RNN:2 * D_RNN])
    neg_ap = -ap_ref[...]
    softplus = jnp.maximum(neg_ap, 0.0) + jnp.log(1.0 + jnp.exp(-jnp.abs(neg_ap)))
    log_a = (-LRU_C) * gate_a * softplus
    a = jnp.exp(log_a)
    mult = jnp.sqrt(1.0 - a * a)
    u = mult * gate_x * xc

    groups = ts // 8
    a = a.reshape(groups, 8, D_RNN)
    u = u.reshape(groups, 8, D_RNN)
    sub = lax.broadcasted_iota(jnp.int32, (groups, 8, D_RNN), 1)
    for d in (1, 2, 4):
        keep = sub >= d
        u = jnp.where(keep, u + a * pltpu.roll(u, d, 1), u)
        a = jnp.where(keep, a * pltpu.roll(a, d, 1), a)
    carry = h_ref[...]
    hs = []
    for g in range(groups):
        h_g = u[g] + a[g] * carry
        hs.append(h_g)
        carry = h_g[7:8, :]
    h_ref[...] = carry
    return jnp.concatenate(hs, axis=0) * _silu(gr)


def _attn_kernel(qt_ref, kv_ref, vt_ref, wi_ref, ga_ref, tri_ref, y_ref,
                 sc_ref, sc16_ref, bias_ref, eqb_ref, s_ref, m_ref, mnew_ref, acc_ref, out_ref):
    tq, tk = Q_TILE, K_TILE
    topk = float(MAX_TOPK)
    qb = pl.program_id(1)
    n_keys = (qb + 1) * tq
    n_full = n_keys // tk
    has_tail = n_keys - n_full * tk > 0

    w_t = wi_ref[...].T[WI_LANE:WI_LANE + N_IDX_HEADS, :] * ((IDX_DIM ** -0.5) * (N_IDX_HEADS ** -0.5))

    def score_chunk(c, carry, size, diagonal):
        rmax, rmin = carry
        k0 = pl.multiple_of(c * tk, tk)
        ki = kv_ref[pl.ds(k0, size), 128:128 + IDX_DIM]
        score = jnp.zeros((size, tq), jnp.float32)
        for h in range(N_IDX_HEADS):
            logits = jnp.dot(ki, qt_ref[N_HEADS + h], preferred_element_type=jnp.float32)
            score = score + jnp.maximum(logits, 0.0) * w_t[h:h + 1, :]
        if diagonal:
            key_pos = k0 + lax.broadcasted_iota(jnp.int32, (size, tq), 0)
            causal = key_pos <= qb * tq + lax.broadcasted_iota(jnp.int32, (size, tq), 1)
            low, high = jnp.where(causal, score, -jnp.inf), jnp.where(causal, score, jnp.inf)
        else:
            low = high = score
        sc_ref[pl.ds(k0, size), :] = low
        sc16_ref[pl.ds(k0, size), :] = low.astype(jnp.bfloat16)
        return jnp.maximum(rmax, _reduce_keys(low, jnp.max)), jnp.minimum(rmin, _reduce_keys(high, jnp.min))

    n_before = jnp.where(has_tail, n_full, n_full - 1)
    extremes = lax.fori_loop(
        0, n_before, functools.partial(score_chunk, size=tk, diagonal=False),
        (jnp.full((1, tq), -jnp.inf, jnp.float32), jnp.full((1, tq), jnp.inf, jnp.float32)))
    rmax, rmin = lax.cond(has_tail,
                          lambda e: score_chunk(n_full, e, TAIL_KEYS, True),
                          lambda e: score_chunk(n_full - 1, e, tk, True), extremes)

    n_sweep_tiles = ((qb + 1) * tq) // SWEEP_TILE

    def sweep(mid, want):
        kinds = {"gt": (jnp.sum, 0.0), "ge": (jnp.sum, 0.0), "above": (jnp.min, jnp.inf), "below": (jnp.max, -jnp.inf)}

        def body(c, carry):
            k0 = pl.multiple_of(c * SWEEP_TILE, SWEEP_TILE)
            s = sc_ref[pl.ds(k0, SWEEP_TILE), :].reshape(SWEEP_TILE // PARTIAL_ROWS, PARTIAL_ROWS, tq)
            gt = s > mid
            terms = {"gt": lambda: jnp.where(gt, 1.0, 0.0), "ge": lambda: jnp.where(s >= mid, 1.0, 0.0),
                     "above": lambda: jnp.where(gt, s, jnp.inf), "below": lambda: jnp.where(gt, -jnp.inf, s)}
            out = []
            for name, acc in zip(want, carry):
                op = kinds[name][0]
                part = op(terms[name](), axis=0)
                out.append(acc + part if op is jnp.sum else
                           (jnp.minimum(acc, part) if op is jnp.min else jnp.maximum(acc, part)))
            return tuple(out)

        init = tuple(jnp.full((PARTIAL_ROWS, tq), kinds[name][1], jnp.float32) for name in want)
        res = lax.fori_loop(0, n_sweep_tiles, body, init)
        return [kinds[name][0](r, axis=0, keepdims=True) for name, r in zip(want, res)]

    def update(state, mid, cnt, tie, new_lo, new_hi):
        lo, hi, chi, act = state
        on = act > 0.5
        fin = jnp.logical_and(on, jnp.logical_or(cnt == topk, tie))
        go_on = jnp.logical_and(on, jnp.logical_not(fin))
        up = jnp.logical_and(go_on, cnt > topk)
        dn = jnp.logical_and(go_on, cnt < topk)
        lo = jnp.where(fin, mid, jnp.where(up, new_lo, lo))
        hi = jnp.where(fin, mid, jnp.where(dn, new_hi, hi))
        chi = jnp.where(jnp.logical_or(fin, dn), cnt, chi)
        act = jnp.where(jnp.logical_and(go_on, lo < hi), 1.0, 0.0)
        return lo, hi, chi, act

    def midpoint(state):
        return 0.5 * state[0] + 0.5 * state[1]

    n_valid = (qb * tq + 1 + lax.broadcasted_iota(jnp.int32, (1, tq), 1)).astype(jnp.float32)
    few = n_valid <= topk
    state = (rmin,
             jnp.where(few, -jnp.inf, rmax),
             jnp.where(few, topk, 0.0),
             jnp.where(jnp.logical_or(few, rmin >= rmax), 0.0, 1.0))

    def coarse_count(t16, also_ge=False):
        one, zero = jnp.ones((), jnp.bfloat16), jnp.zeros((), jnp.bfloat16)

        def tile_sum(hits):
            parts = [hits[g] for g in range(SWEEP_TILE // PARTIAL_ROWS)]
            while len(parts) > 1:
                parts = [parts[i] + parts[i + 1] for i in range(0, len(parts), 2)]
            return parts[0]

        def body(c, accs):
            k0 = pl.multiple_of(c * SWEEP_TILE, SWEEP_TILE)
            s = sc16_ref[pl.ds(k0, SWEEP_TILE), :].reshape(SWEEP_TILE // PARTIAL_ROWS, PARTIAL_ROWS, tq)
            out = [accs[0] + tile_sum(jnp.where(s > t16, one, zero))]
            if also_ge:
                out.append(accs[1] + tile_sum(jnp.where(s >= t16, one, zero)))
            return tuple(out)

        init = (jnp.zeros((PARTIAL_ROWS, tq), jnp.bfloat16),) * (2 if also_ge else 1)
        accs = lax.fori_loop(0, n_sweep_tiles, body, init)
        return [jnp.sum(a.astype(jnp.float32), axis=0, keepdims=True) for a in accs]

    lo, hi, chi, act = state
    on = act > 0.5
    pos, nonneg = coarse_count(jnp.zeros((1, tq), jnp.bfloat16), also_ge=True)
    at_zero = jnp.logical_and(on, jnp.logical_and(pos <= topk, nonneg >= topk))
    lo = jnp.where(at_zero, 0.0, jnp.where(jnp.logical_and(on, pos > topk), jnp.maximum(lo, 0.0), lo))
    hi = jnp.where(at_zero, 0.0, jnp.where(jnp.logical_and(on, nonneg < topk), jnp.minimum(hi, 0.0), hi))
    chi = jnp.where(at_zero, pos, chi)
    act = jnp.where(at_zero, 0.0, act)
    state = (lo, hi, chi, act)

    lo, hi, chi, act = state
    for _ in range(COARSE_PROBES):
        t16 = (0.5 * lo + 0.5 * hi).astype(jnp.bfloat16)
        t = t16.astype(jnp.float32)
        (cnt16,) = coarse_count(t16)
        on = act > 0.5
        lo = jnp.where(jnp.logical_and(on, cnt16 >= topk), jnp.maximum(lo, t), lo)
        hi = jnp.where(jnp.logical_and(on, cnt16 < topk), jnp.minimum(hi, t + jnp.abs(t) * (2.0 ** -7) + 1e-30), hi)
    (cnt,) = sweep(hi, ("gt",))
    state = update((lo, hi, chi, act), hi, cnt, False, hi, hi)

    def pending(state):
        return jnp.where(jnp.logical_and(state[3] > 0.5, topk - state[2] != 1.0), 1.0, 0.0)

    def plain_cond(st):
        return jnp.logical_and(st[1] > 0.5, st[0] < PLAIN_BISECT_ROUNDS)

    def plain_body(st):
        state = st[2:]
        for _ in range(PROBES_PER_ROUND):
            mid = midpoint(state)
            (cnt,) = sweep(mid, ("gt",))
            state = update(state, mid, cnt, False, mid, mid)
        return (st[0] + 1, jnp.max(pending(state))) + state

    st = lax.while_loop(plain_cond, plain_body, (jnp.int32(0), jnp.max(pending(state))) + state)
    lo, hi, chi, act = st[2:]
    (largest_below,) = sweep(hi, ("below",))
    one_short = jnp.logical_and(act > 0.5, topk - chi == 1.0)
    state = (jnp.where(one_short, largest_below, lo), jnp.where(one_short, largest_below, hi), chi,
             jnp.where(one_short, 0.0, act))

    def snap_cond(st):
        return jnp.logical_and(st[1] > 0.5, st[0] < 4096)

    def snap_body(st):
        state = st[2:]
        mid = midpoint(state)
        mid = jnp.where(mid >= state[1], state[0], mid)
        cnt, above, below = sweep(mid, ("gt", "above", "below"))
        state = update(state, mid, cnt, False, above, below)
        return (st[0] + 1, jnp.max(state[3])) + state

    st = lax.while_loop(snap_cond, snap_body, (jnp.int32(0), jnp.max(state[3])) + state)
    thr = st[3]
    need = topk - st[4]
    any_tie = jnp.max(need) > 0.5

    m_ref[...] = jnp.full(m_ref.shape, MASKED, jnp.float32)
    acc_ref[...] = jnp.zeros(acc_ref.shape, jnp.float32)
    eqb_ref[...] = jnp.zeros(eqb_ref.shape, jnp.float32)

    def attend_chunk(c, size):
        k0 = pl.multiple_of(c * tk, tk)
        s_idx = sc_ref[pl.ds(k0, size), :]
        bias_ref[0:size, :] = jnp.where(s_idx > thr, 0.0, MASKED)

        @pl.when(any_tie)
        def _():
            eq = s_idx == thr
            eq_f = jnp.where(eq, 1.0, 0.0)
            rank = jnp.dot(tri_ref[0:size, 0:size], eq_f.astype(jnp.bfloat16),
                           preferred_element_type=jnp.float32) + eqb_ref[...]
            tie_bias = jnp.where(jnp.logical_and(eq, rank < need), 0.0, MASKED)
            bias_ref[0:size, :] = jnp.where(s_idx > thr, 0.0, tie_bias)
            eqb_ref[...] = eqb_ref[...] + _reduce_keys(eq_f, jnp.sum)

        kc = kv_ref[pl.ds(k0, size), 0:HEAD_DIM]
        vt = vt_ref[c, :, 0:size]

        def logits(h):
            s = jnp.dot(kc, qt_ref[h], preferred_element_type=jnp.float32) + bias_ref[0:size, :]
            s_ref[h, 0:size, :] = s
            mnew_ref[h] = jnp.maximum(m_ref[h], _reduce_keys(s, jnp.max))

        def weigh(h):
            m_new = mnew_ref[h]
            alpha = jnp.exp2(m_ref[h] - m_new)
            p = jnp.exp2(s_ref[h, 0:size, :] - m_new)
            acc_ref[h] = alpha * acc_ref[h] + jnp.dot(
                vt, p.astype(jnp.bfloat16), preferred_element_type=jnp.float32)
            m_ref[h] = m_new

        for h in range(N_HEADS):
            logits(h)
        for h in range(N_HEADS):
            weigh(h)

    def attend_full(c, _):
        attend_chunk(c, tk)
        return 0

    lax.fori_loop(0, n_full, attend_full, 0)

    @pl.when(has_tail)
    def _():
        attend_chunk(n_full, TAIL_KEYS)

    for h in range(N_HEADS):
        rows = slice(h * HEAD_DIM, (h + 1) * HEAD_DIM)
        out_ref[rows, :] = acc_ref[h, 0:HEAD_DIM, :] / acc_ref[h, HEAD_DIM:HEAD_DIM + 1, :]
    y = out_ref[...].T * _silu(ga_ref[...])
    y_ref[...] = y.astype(y_ref.dtype)


def _dsa_attention(qt, kv, vt, wi, ga, tri):
    batch, seq, _ = kv.shape
    row = lambda b, i: (b, i, 0)
    return pl.pallas_call(
        _attn_kernel,
        grid=(batch, seq // Q_TILE),
        in_specs=[
            pl.BlockSpec((None, N_HEADS + N_IDX_HEADS, HEAD_DIM, Q_TILE), lambda b, i: (b, 0, 0, i)),
            pl.BlockSpec((None, seq, 256), lambda b, i: (b, 0, 0)),
            pl.BlockSpec((None, seq // K_TILE, VT_ROWS, K_TILE), lambda b, i: (b, 0, 0, 0)),
            pl.BlockSpec((None, Q_TILE, 128), row),
            pl.BlockSpec((None, Q_TILE, D_ATTN), row),
            pl.BlockSpec((K_TILE, K_TILE), lambda b, i: (0, 0)),
        ],
        out_specs=pl.BlockSpec((None, Q_TILE, D_ATTN), row),
        out_shape=jax.ShapeDtypeStruct((batch, seq, D_ATTN), jnp.bfloat16),
        scratch_shapes=[
            pltpu.VMEM((seq, Q_TILE), jnp.float32),
            pltpu.VMEM((seq, Q_TILE), jnp.bfloat16),
            pltpu.VMEM((K_TILE, Q_TILE), jnp.float32),
            pltpu.VMEM((1, Q_TILE), jnp.float32),
            pltpu.VMEM((N_HEADS, K_TILE, Q_TILE), jnp.float32),
            pltpu.VMEM((N_HEADS, 1, Q_TILE), jnp.float32),
            pltpu.VMEM((N_HEADS, 1, Q_TILE), jnp.float32),
            pltpu.VMEM((N_HEADS, VT_ROWS, Q_TILE), jnp.float32),
            pltpu.VMEM((D_ATTN, Q_TILE), jnp.float32),
        ],
        compiler_params=pltpu.CompilerParams(
            dimension_semantics=("parallel", "parallel"), vmem_limit_bytes=VMEM_LIMIT),
        name="dsa_attention",
    )(qt, kv, vt, wi, ga, tri)


def _outproj_kernel(yr_ref, ya_ref, x_ref, mod_ref, wr_ref, wa_ref, g_ref, o_ref, *, final_norm):
    y = jnp.dot(yr_ref[...], wr_ref[...], preferred_element_type=jnp.float32)
    y = y + jnp.dot(ya_ref[...], wa_ref[...], preferred_element_type=jnp.float32)
    x_new = x_ref[...] + mod_ref[2:3, :] * y
    if final_norm:
        x_new = _rms(x_new, g_ref[...])
    o_ref[...] = x_new


def _out_projection(yr, ya, x, mod_l, w_r, w_a, final_g, final_norm):
    batch, seq, _ = x.shape
    row = lambda b, i: (b, i, 0)
    const = lambda b, i: (0, 0)
    return pl.pallas_call(
        functools.partial(_outproj_kernel, final_norm=final_norm),
        grid=(batch, seq // ROW_TILE),
        in_specs=[
            pl.BlockSpec((None, ROW_TILE, D_RNN), row),
            pl.BlockSpec((None, ROW_TILE, D_ATTN), row),
            pl.BlockSpec((None, ROW_TILE, D_MODEL), row),
            pl.BlockSpec((None, 3, D_MODEL), lambda b, i: (b, 0, 0)),
            pl.BlockSpec((D_RNN, D_MODEL), const),
            pl.BlockSpec((D_ATTN, D_MODEL), const),
            pl.BlockSpec((1, D_MODEL), const),
        ],
        out_specs=pl.BlockSpec((None, ROW_TILE, D_MODEL), row),
        out_shape=jax.ShapeDtypeStruct((batch, seq, D_MODEL), jnp.float32),
        compiler_params=pltpu.CompilerParams(
            dimension_semantics=("parallel", "parallel"), vmem_limit_bytes=VMEM_LIMIT),
        name="out_proj",
    )(yr, ya, x, mod_l, w_r, w_a, final_g.reshape(1, D_MODEL))


def _pad_w_in(w):
    return jnp.pad(w.astype(jnp.bfloat16), ((0, 0), (0, Z_COLS - D_IN)))


def _block_diag(w):
    n, c, d = w.shape
    eye = jnp.eye(n, dtype=w.dtype)
    return (eye[:, None, :, None] * w[:, :, None, :]).reshape(n * c, n * d)


def kernel(x, c, norm_g, ada_w, ada_b, w_in, conv_w, conv_b, lru_wx, lru_bx, lru_wa, lru_ba, lru_a, w_out, final_g):
    depth = w_in.shape[0]
    mod = _modulation(c, ada_w, ada_b)
    idx = jnp.arange(K_TILE)
    tri = (idx[None, :] < idx[:, None]).astype(jnp.bfloat16)
    for l in range(depth):
        w_pad = _pad_w_in(w_in[l])
        w_gates = jnp.concatenate([_block_diag(lru_wx[l]), _block_diag(lru_wa[l])], axis=-1).astype(jnp.bfloat16)
        b_gates = jnp.concatenate([lru_bx[l], lru_ba[l]]).reshape(1, 2 * D_RNN)
        w_o = w_out[l].astype(jnp.bfloat16)
        y_r, qt, ga, kv, wi, vt = _in_projection(x, mod[l], norm_g[l], w_pad, conv_w[l], conv_b[l],
                                                 w_gates, b_gates, lru_a[l])
        y_a = _dsa_attention(qt, kv, vt, wi, ga, tri)
        x = _out_projection(y_r, y_a, x, mod[l], w_o[:D_RNN], w_o[D_RNN:], final_g, l == depth - 1)
    return x
```

```python
import functools

import jax
import jax.numpy as jnp
from jax import lax
from jax.experimental import pallas as pl
from jax.experimental.pallas import tpu as pltpu

D_MODEL = 1024
D_RNN = 512
N_RNN_BLOCKS = 8
RNN_BLOCK = D_RNN // N_RNN_BLOCKS
CONV_WIDTH = 4
LRU_C = 8.0
N_HEADS = 8
HEAD_DIM = 64
D_ATTN = N_HEADS * HEAD_DIM
N_IDX_HEADS = 8
IDX_DIM = 64
MAX_TOPK = 256
EPS = 1e-6
LOG2_E = 1.4426950408889634

Z_RNN = 0
Z_Q = 1024
Z_KV = 1536
Z_GA = 1664
Z_QI = 2176
Z_KI = 2688
Z_COLS = 2816
D_IN = 2760
WI_LANE = 64

ROW_TILE = 512
OUT_ROW_TILE = 1024
Q_TILE = 256
K_TILE = 512
VT_ROWS = 80
MASKED = -1e30
COARSE_PROBES = 9
PROBES_PER_ROUND = 7
PLAIN_BISECT_ROUNDS = 5
PARTIAL_ROWS = 32
SWEEP_TILE = 256
TAIL_KEYS = K_TILE - Q_TILE
assert K_TILE == 2 * Q_TILE
VMEM_LIMIT = 48 * 1024 * 1024


def _silu(x):
    return x * jax.nn.sigmoid(x)


def _rms(x, g):
    return x * lax.rsqrt(jnp.mean(x * x, axis=-1, keepdims=True) + EPS) * g


def _reduce_keys(x, op):
    keys, queries = x.shape
    part = op(x.reshape(keys // PARTIAL_ROWS, PARTIAL_ROWS, queries), axis=0)
    return op(part, axis=0, keepdims=True)


def _mod_kernel(c_ref, w_ref, b_ref, o_ref):
    c_act = _silu(c_ref[...])
    o_ref[...] = jnp.dot(c_act, w_ref[...], precision=lax.Precision.HIGHEST,
                         preferred_element_type=jnp.float32) + b_ref[...]


def _modulation(c, ada_w, ada_b):
    depth = ada_w.shape[0]
    batch = c.shape[0]
    out = pl.pallas_call(
        _mod_kernel,
        grid=(depth, 3),
        in_specs=[
            pl.BlockSpec((batch, D_MODEL), lambda l, j: (0, 0)),
            pl.BlockSpec((None, D_MODEL, D_MODEL), lambda l, j: (l, 0, j)),
            pl.BlockSpec((None, None, 1, D_MODEL), lambda l, j: (l, j, 0, 0)),
        ],
        out_specs=pl.BlockSpec((None, None, batch, D_MODEL), lambda l, j: (l, j, 0, 0)),
        out_shape=jax.ShapeDtypeStruct((depth, 3, batch, D_MODEL), jnp.float32),
        compiler_params=pltpu.CompilerParams(vmem_limit_bytes=VMEM_LIMIT),
        name="adaln_mod",
    )(c, ada_w, ada_b.reshape(depth, 3, 1, D_MODEL))
    return out.transpose(0, 2, 1, 3)


def _inproj_kernel(x_ref, mod_ref, g_ref, w_ref, cw_ref, cb_ref, wg_ref, bg_ref, ap_ref,
                   yr_ref, qt_ref, ga_ref, kv_ref, wi_ref, vt_ref, xbuf_ref, h_ref):
    @pl.when(pl.program_id(1) == 0)
    def _():
        xbuf_ref[0:8, :] = jnp.zeros((8, D_RNN), jnp.float32)
        h_ref[...] = jnp.zeros((1, D_RNN), jnp.float32)

    x = x_ref[...]
    shift = mod_ref[0:1, :]
    scale = mod_ref[1:2, :]
    h = _rms(x, g_ref[...]) * (1.0 + scale) + shift
    h = h.astype(jnp.bfloat16)
    z_rnn = jnp.dot(h, w_ref[:, Z_RNN:Z_Q], preferred_element_type=jnp.float32)
    y_r = _rg_lru_tile(z_rnn[:, 0:D_RNN], z_rnn[:, D_RNN:2 * D_RNN],
                       cw_ref, cb_ref, wg_ref, bg_ref, ap_ref, xbuf_ref, h_ref)
    yr_ref[...] = y_r.astype(yr_ref.dtype)
    z = jnp.dot(h, w_ref[:, Z_Q:Z_COLS], preferred_element_type=jnp.float32)
    col = lambda start, width: z[:, start - Z_Q:start - Z_Q + width]
    for h in range(N_HEADS):
        q_h = col(Z_Q + h * HEAD_DIM, HEAD_DIM) * (HEAD_DIM ** -0.5 * LOG2_E)
        _store_query_tiles(qt_ref, h, q_h.T.astype(jnp.bfloat16))
    for h in range(N_IDX_HEADS):
        _store_query_tiles(qt_ref, N_HEADS + h, col(Z_QI + h * IDX_DIM, IDX_DIM).T.astype(jnp.bfloat16))
    ga_ref[...] = col(Z_GA, D_ATTN)
    kv_ref[...] = jnp.concatenate([col(Z_KV, 128), col(Z_KI, 128)], axis=-1).astype(jnp.bfloat16)
    wi_ref[...] = col(Z_KI, 128)
    pad_rows = lax.broadcasted_iota(jnp.int32, (VT_ROWS - HEAD_DIM, K_TILE), 0)
    for j in range(ROW_TILE // K_TILE):
        v_t = col(Z_KV + HEAD_DIM, HEAD_DIM)[j * K_TILE:(j + 1) * K_TILE, :].T
        vt_ref[j, 0:HEAD_DIM, :] = v_t.astype(jnp.bfloat16)
        vt_ref[j, HEAD_DIM:VT_ROWS, :] = jnp.where(pad_rows == 0, 1.0, 0.0).astype(jnp.bfloat16)


def _store_query_tiles(qt_ref, head, q_t):
    for j in range(ROW_TILE // Q_TILE):
        qt_ref[j, head] = q_t[:, j * Q_TILE:(j + 1) * Q_TILE]


def _in_projection(x, mod_l, norm_g, w_pad, conv_w, conv_b, w_gates, b_gates, a_param):
    batch, seq, _ = x.shape
    row = lambda b, i: (b, i, 0)
    const = lambda b, i: (0, 0)
    return pl.pallas_call(
        _inproj_kernel,
        grid=(batch, seq // ROW_TILE),
        in_specs=[
            pl.BlockSpec((None, ROW_TILE, D_MODEL), row),
            pl.BlockSpec((None, 3, D_MODEL), lambda b, i: (b, 0, 0)),
            pl.BlockSpec((1, D_MODEL), const),
            pl.BlockSpec((D_MODEL, Z_COLS), const),
            pl.BlockSpec((CONV_WIDTH, D_RNN), const),
            pl.BlockSpec((1, D_RNN), const),
            pl.BlockSpec((D_RNN, 2 * D_RNN), const),
            pl.BlockSpec((1, 2 * D_RNN), const),
            pl.BlockSpec((1, D_RNN), const),
        ],
        out_specs=[
            pl.BlockSpec((None, ROW_TILE, D_RNN), row),
            pl.BlockSpec((None, ROW_TILE // Q_TILE, N_HEADS + N_IDX_HEADS, HEAD_DIM, Q_TILE),
                         lambda b, i: (b, i, 0, 0, 0)),
            pl.BlockSpec((None, ROW_TILE, 512), row),
            pl.BlockSpec((None, ROW_TILE, 256), row),
            pl.BlockSpec((None, ROW_TILE, 128), row),
            pl.BlockSpec((None, ROW_TILE // K_TILE, VT_ROWS, K_TILE), lambda b, i: (b, i, 0, 0)),
        ],
        out_shape=[
            jax.ShapeDtypeStruct((batch, seq, D_RNN), jnp.bfloat16),
            jax.ShapeDtypeStruct((batch, seq // Q_TILE, N_HEADS + N_IDX_HEADS, HEAD_DIM, Q_TILE), jnp.bfloat16),
            jax.ShapeDtypeStruct((batch, seq, 512), jnp.float32),
            jax.ShapeDtypeStruct((batch, seq, 256), jnp.bfloat16),
            jax.ShapeDtypeStruct((batch, seq, 128), jnp.float32),
            jax.ShapeDtypeStruct((batch, seq // K_TILE, VT_ROWS, K_TILE), jnp.bfloat16),
        ],
        scratch_shapes=[
            pltpu.VMEM((ROW_TILE + 8, D_RNN), jnp.float32),
            pltpu.VMEM((1, D_RNN), jnp.float32),
        ],
        compiler_params=pltpu.CompilerParams(
            dimension_semantics=("parallel", "arbitrary"), vmem_limit_bytes=VMEM_LIMIT),
        name="in_proj_rg_lru",
    )(x, mod_l, norm_g.reshape(1, D_MODEL), w_pad, conv_w, conv_b.reshape(1, D_RNN),
      w_gates, b_gates, a_param.reshape(1, D_RNN))


def _rg_lru_tile(xr, gr, cw_ref, cb_ref, wg_ref, bg_ref, ap_ref, xbuf_ref, h_ref):
    ts = ROW_TILE
    xbuf_ref[8:8 + ts, :] = xr
    xc = cw_ref[3:4, :] * xr + cb_ref[...]
    for k in range(CONV_WIDTH - 1):
        back = CONV_WIDTH - 1 - k
        xc = xc + cw_ref[k:k + 1, :] * xbuf_ref[8 - back:8 - back + ts, :]
    xbuf_ref[0:8, :] = xbuf_ref[ts:ts + 8, :]

    gates = jnp.dot(xc.astype(jnp.bfloat16), wg_ref[...], preferred_element_type=jnp.float32) + bg_ref[...]
    gate_x = jax.nn.sigmoid(gates[:, 0:D_RNN])
    gate_a = jax.nn.sigmoid(gates[:, D_RNN:2 * D_RNN])
    neg_ap = -ap_ref[...]
    softplus = jnp.maximum(neg_ap, 0.0) + jnp.log(1.0 + jnp.exp(-jnp.abs(neg_ap)))
    log_a = (-LRU_C) * gate_a * softplus
    a = jnp.exp(log_a)
    mult = jnp.sqrt(1.0 - a * a)
    u = mult * gate_x * xc

    groups = ts // 8
    a = a.reshape(groups, 8, D_RNN)
    u = u.reshape(groups, 8, D_RNN)
    sub = lax.broadcasted_iota(jnp.int32, (groups, 8, D_RNN), 1)
    for d in (1, 2, 4):
        keep = sub >= d
        u = jnp.where(keep, u + a * pltpu.roll(u, d, 1), u)
        a = jnp.where(keep, a * pltpu.roll(a, d, 1), a)
    carry = h_ref[...]
    hs = []
    for g in range(groups):
        h_g = u[g] + a[g] * carry
        hs.append(h_g)
        carry = h_g[7:8, :]
    h_ref[...] = carry
    return jnp.concatenate(hs, axis=0) * _silu(gr)


def _attn_kernel(qt_ref, kv_ref, vt_ref, wi_ref, ga_ref, tri_ref, y_ref,
                 sc_ref, sc16_ref, bias_ref, eqb_ref, s_ref, m_ref, mnew_ref, acc_ref, out_ref):
    tq, tk = Q_TILE, K_TILE
    topk = float(MAX_TOPK)
    qb = pl.program_id(1)
    n_keys = (qb + 1) * tq
    n_full = n_keys // tk
    has_tail = n_keys - n_full * tk > 0

    w_t = wi_ref[...].T[WI_LANE:WI_LANE + N_IDX_HEADS, :] * ((IDX_DIM ** -0.5) * (N_IDX_HEADS ** -0.5))

    def score_chunk(c, carry, size, diagonal):
        rmax, rmin = carry
        k0 = pl.multiple_of(c * tk, tk)
        ki = kv_ref[pl.ds(k0, size), 128:128 + IDX_DIM]
        score = jnp.zeros((size, tq), jnp.float32)
        for h in range(N_IDX_HEADS):
            logits = jnp.dot(ki, qt_ref[N_HEADS + h], preferred_element_type=jnp.float32)
            score = score + jnp.maximum(logits, 0.0) * w_t[h:h + 1, :]
        if diagonal:
            key_pos = k0 + lax.broadcasted_iota(jnp.int32, (size, tq), 0)
            causal = key_pos <= qb * tq + lax.broadcasted_iota(jnp.int32, (size, tq), 1)
            low, high = jnp.where(causal, score, -jnp.inf), jnp.where(causal, score, jnp.inf)
        else:
            low = high = score
        sc_ref[pl.ds(k0, size), :] = low
        sc16_ref[pl.ds(k0, size), :] = low.astype(jnp.bfloat16)
        return jnp.maximum(rmax, _reduce_keys(low, jnp.max)), jnp.minimum(rmin, _reduce_keys(high, jnp.min))

    n_before = jnp.where(has_tail, n_full, n_full - 1)
    extremes = lax.fori_loop(
        0, n_before, functools.partial(score_chunk, size=tk, diagonal=False),
        (jnp.full((1, tq), -jnp.inf, jnp.float32), jnp.full((1, tq), jnp.inf, jnp.float32)))
    rmax, rmin = lax.cond(has_tail,
                          lambda e: score_chunk(n_full, e, TAIL_KEYS, True),
                          lambda e: score_chunk(n_full - 1, e, tk, True), extremes)

    n_sweep_tiles = ((qb + 1) * tq) // SWEEP_TILE

    def sweep(mid, want):
        kinds = {"gt": (jnp.sum, 0.0), "ge": (jnp.sum, 0.0), "above": (jnp.min, jnp.inf), "below": (jnp.max, -jnp.inf)}

        def body(c, carry):
            k0 = pl.multiple_of(c * SWEEP_TILE, SWEEP_TILE)
            s = sc_ref[pl.ds(k0, SWEEP_TILE), :].reshape(SWEEP_TILE // PARTIAL_ROWS, PARTIAL_ROWS, tq)
            gt = s > mid
            terms = {"gt": lambda: jnp.where(gt, 1.0, 0.0), "ge": lambda: jnp.where(s >= mid, 1.0, 0.0),
                     "above": lambda: jnp.where(gt, s, jnp.inf), "below": lambda: jnp.where(gt, -jnp.inf, s)}
            out = []
            for name, acc in zip(want, carry):
                op = kinds[name][0]
                part = op(terms[name](), axis=0)
                out.append(acc + part if op is jnp.sum else
                           (jnp.minimum(acc, part) if op is jnp.min else jnp.maximum(acc, part)))
            return tuple(out)

        init = tuple(jnp.full((PARTIAL_ROWS, tq), kinds[name][1], jnp.float32) for name in want)
        res = lax.fori_loop(0, n_sweep_tiles, body, init)
        return [kinds[name][0](r, axis=0, keepdims=True) for name, r in zip(want, res)]

    def update(state, mid, cnt, tie, new_lo, new_hi):
        lo, hi, chi, act = state
        on = act > 0.5
        fin = jnp.logical_and(on, jnp.logical_or(cnt == topk, tie))
        go_on = jnp.logical_and(on, jnp.logical_not(fin))
        up = jnp.logical_and(go_on, cnt > topk)
        dn = jnp.logical_and(go_on, cnt < topk)
        lo = jnp.where(fin, mid, jnp.where(up, new_lo, lo))
        hi = jnp.where(fin, mid, jnp.where(dn, new_hi, hi))
        chi = jnp.where(jnp.logical_or(fin, dn), cnt, chi)
        act = jnp.where(jnp.logical_and(go_on, lo < hi), 1.0, 0.0)
        return lo, hi, chi, act

    def midpoint(state):
        return 0.5 * state[0] + 0.5 * state[1]

    n_valid = (qb * tq + 1 + lax.broadcasted_iota(jnp.int32, (1, tq), 1)).astype(jnp.float32)
    few = n_valid <= topk
    state = (rmin,
             jnp.where(few, -jnp.inf, rmax),
             jnp.where(few, topk, 0.0),
             jnp.where(jnp.logical_or(few, rmin >= rmax), 0.0, 1.0))

    def coarse_count(t16, also_ge=False):
        one, zero = jnp.ones((), jnp.bfloat16), jnp.zeros((), jnp.bfloat16)

        def tile_sum(hits):
            parts = [hits[g] for g in range(SWEEP_TILE // PARTIAL_ROWS)]
            while len(parts) > 1:
                parts = [parts[i] + parts[i + 1] for i in range(0, len(parts), 2)]
            return parts[0]

        def body(c, accs):
            k0 = pl.multiple_of(c * SWEEP_TILE, SWEEP_TILE)
            s = sc16_ref[pl.ds(k0, SWEEP_TILE), :].reshape(SWEEP_TILE // PARTIAL_ROWS, PARTIAL_ROWS, tq)
            out = [accs[0] + tile_sum(jnp.where(s > t16, one, zero))]
            if also_ge:
                out.append(accs[1] + tile_sum(jnp.where(s >= t16, one, zero)))
            return tuple(out)

        init = (jnp.zeros((PARTIAL_ROWS, tq), jnp.bfloat16),) * (2 if also_ge else 1)
        accs = lax.fori_loop(0, n_sweep_tiles, body, init)
        return [jnp.sum(a.astype(jnp.float32), axis=0, keepdims=True) for a in accs]

    lo, hi, chi, act = state
    on = act > 0.5
    pos, nonneg = coarse_count(jnp.zeros((1, tq), jnp.bfloat16), also_ge=True)
    at_zero = jnp.logical_and(on, jnp.logical_and(pos <= topk, nonneg >= topk))
    lo = jnp.where(at_zero, 0.0, jnp.where(jnp.logical_and(on, pos > topk), jnp.maximum(lo, 0.0), lo))
    hi = jnp.where(at_zero, 0.0, jnp.where(jnp.logical_and(on, nonneg < topk), jnp.minimum(hi, 0.0), hi))
    chi = jnp.where(at_zero, pos, chi)
    act = jnp.where(at_zero, 0.0, act)
    state = (lo, hi, chi, act)

    lo, hi, chi, act = state
    for _ in range(COARSE_PROBES):
        t16 = (0.5 * lo + 0.5 * hi).astype(jnp.bfloat16)
        t = t16.astype(jnp.float32)
        (cnt16,) = coarse_count(t16)
        on = act > 0.5
        lo = jnp.where(jnp.logical_and(on, cnt16 >= topk), jnp.maximum(lo, t), lo)
        hi = jnp.where(jnp.logical_and(on, cnt16 < topk), jnp.minimum(hi, t + jnp.abs(t) * (2.0 ** -7) + 1e-30), hi)
    (cnt,) = sweep(hi, ("gt",))
    state = update((lo, hi, chi, act), hi, cnt, False, hi, hi)

    def pending(state):
        return jnp.where(jnp.logical_and(state[3] > 0.5, topk - state[2] != 1.0), 1.0, 0.0)

    def plain_cond(st):
        return jnp.logical_and(st[1] > 0.5, st[0] < PLAIN_BISECT_ROUNDS)

    def plain_body(st):
        state = st[2:]
        for _ in range(PROBES_PER_ROUND):
            mid = midpoint(state)
            (cnt,) = sweep(mid, ("gt",))
            state = update(state, mid, cnt, False, mid, mid)
        return (st[0] + 1, jnp.max(pending(state))) + state

    st = lax.while_loop(plain_cond, plain_body, (jnp.int32(0), jnp.max(pending(state))) + state)
    lo, hi, chi, act = st[2:]
    (largest_below,) = sweep(hi, ("below",))
    one_short = jnp.logical_and(act > 0.5, topk - chi == 1.0)
    state = (jnp.where(one_short, largest_below, lo), jnp.where(one_short, largest_below, hi), chi,
             jnp.where(one_short, 0.0, act))

    def snap_cond(st):
        return jnp.logical_and(st[1] > 0.5, st[0] < 4096)

    def snap_body(st):
        state = st[2:]
        mid = midpoint(state)
        mid = jnp.where(mid >= state[1], state[0], mid)
        cnt, above, below = sweep(mid, ("gt", "above", "below"))
        state = update(state, mid, cnt, False, above, below)
        return (st[0] + 1, jnp.max(state[3])) + state

    st = lax.while_loop(snap_cond, snap_body, (jnp.int32(0), jnp.max(state[3])) + state)
    thr = st[3]
    need = topk - st[4]
    any_tie = jnp.max(need) > 0.5

    m_ref[...] = jnp.full(m_ref.shape, MASKED, jnp.float32)
    acc_ref[...] = jnp.zeros(acc_ref.shape, jnp.float32)
    eqb_ref[...] = jnp.zeros(eqb_ref.shape, jnp.float32)

    def attend_chunk(c, size):
        k0 = pl.multiple_of(c * tk, tk)
        s_idx = sc_ref[pl.ds(k0, size), :]
        bias_ref[0:size, :] = jnp.where(s_idx > thr, 0.0, MASKED)

        @pl.when(any_tie)
        def _():
            eq = s_idx == thr
            eq_f = jnp.where(eq, 1.0, 0.0)
            rank = jnp.dot(tri_ref[0:size, 0:size], eq_f.astype(jnp.bfloat16),
                           preferred_element_type=jnp.float32) + eqb_ref[...]
            tie_bias = jnp.where(jnp.logical_and(eq, rank < need), 0.0, MASKED)
            bias_ref[0:size, :] = jnp.where(s_idx > thr, 0.0, tie_bias)
            eqb_ref[...] = eqb_ref[...] + _reduce_keys(eq_f, jnp.sum)

        kc = kv_ref[pl.ds(k0, size), 0:HEAD_DIM]
        vt = vt_ref[c, :, 0:size]

        def logits(h):
            s = jnp.dot(kc, qt_ref[h], preferred_element_type=jnp.float32) + bias_ref[0:size, :]
            s_ref[h, 0:size, :] = s
            mnew_ref[h] = jnp.maximum(m_ref[h], _reduce_keys(s, jnp.max))

        def weigh(h):
            m_new = mnew_ref[h]
            alpha = jnp.exp2(m_ref[h] - m_new)
            p = jnp.exp2(s_ref[h, 0:size, :] - m_new)
            acc_ref[h] = alpha * acc_ref[h] + jnp.dot(
                vt, p.astype(jnp.bfloat16), preferred_element_type=jnp.float32)
            m_ref[h] = m_new

        for h in range(N_HEADS):
            logits(h)
        for h in range(N_HEADS):
            weigh(h)

    def attend_full(c, _):
        attend_chunk(c, tk)
        return 0

    lax.fori_loop(0, n_full, attend_full, 0)

    @pl.when(has_tail)
    def _():
        attend_chunk(n_full, TAIL_KEYS)

    for h in range(N_HEADS):
        rows = slice(h * HEAD_DIM, (h + 1) * HEAD_DIM)
        out_ref[rows, :] = acc_ref[h, 0:HEAD_DIM, :] / acc_ref[h, HEAD_DIM:HEAD_DIM + 1, :]
    y = out_ref[...].T * _silu(ga_ref[...])
    y_ref[...] = y.astype(y_ref.dtype)


def _dsa_attention(qt, kv, vt, wi, ga, tri):
    batch, seq, _ = kv.shape
    row = lambda b, i: (b, i, 0)
    return pl.pallas_call(
        _attn_kernel,
        grid=(batch, seq // Q_TILE),
        in_specs=[
            pl.BlockSpec((None, None, N_HEADS + N_IDX_HEADS, HEAD_DIM, Q_TILE), lambda b, i: (b, i, 0, 0, 0)),
            pl.BlockSpec((None, seq, 256), lambda b, i: (b, 0, 0)),
            pl.BlockSpec((None, seq // K_TILE, VT_ROWS, K_TILE), lambda b, i: (b, 0, 0, 0)),
            pl.BlockSpec((None, Q_TILE, 128), row),
            pl.BlockSpec((None, Q_TILE, D_ATTN), row),
            pl.BlockSpec((K_TILE, K_TILE), lambda b, i: (0, 0)),
        ],
        out_specs=pl.BlockSpec((None, Q_TILE, D_ATTN), row),
        out_shape=jax.ShapeDtypeStruct((batch, seq, D_ATTN), jnp.bfloat16),
        scratch_shapes=[
            pltpu.VMEM((seq, Q_TILE), jnp.float32),
            pltpu.VMEM((seq, Q_TILE), jnp.bfloat16),
            pltpu.VMEM((K_TILE, Q_TILE), jnp.float32),
            pltpu.VMEM((1, Q_TILE), jnp.float32),
            pltpu.VMEM((N_HEADS, K_TILE, Q_TILE), jnp.float32),
            pltpu.VMEM((N_HEADS, 1, Q_TILE), jnp.float32),
            pltpu.VMEM((N_HEADS, 1, Q_TILE), jnp.float32),
            pltpu.VMEM((N_HEADS, VT_ROWS, Q_TILE), jnp.float32),
            pltpu.VMEM((D_ATTN, Q_TILE), jnp.float32),
        ],
        compiler_params=pltpu.CompilerParams(
            dimension_semantics=("parallel", "parallel"), vmem_limit_bytes=VMEM_LIMIT),
        name="dsa_attention",
    )(qt, kv, vt, wi, ga, tri)


def _outproj_kernel(yr_ref, ya_ref, x_ref, mod_ref, wr_ref, wa_ref, g_ref, o_ref, *, final_norm):
    y = jnp.dot(yr_ref[...], wr_ref[...], preferred_element_type=jnp.float32)
    y = y + jnp.dot(ya_ref[...], wa_ref[...], preferred_element_type=jnp.float32)
    x_new = x_ref[...] + mod_ref[2:3, :] * y
    if final_norm:
        x_new = _rms(x_new, g_ref[...])
    o_ref[...] = x_new


def _out_projection(yr, ya, x, mod_l, w_r, w_a, final_g, final_norm):
    batch, seq, _ = x.shape
    row = lambda b, i: (b, i, 0)
    const = lambda b, i: (0, 0)
    return pl.pallas_call(
        functools.partial(_outproj_kernel, final_norm=final_norm),
        grid=(batch, seq // OUT_ROW_TILE),
        in_specs=[
            pl.BlockSpec((None, OUT_ROW_TILE, D_RNN), row),
            pl.BlockSpec((None, OUT_ROW_TILE, D_ATTN), row),
            pl.BlockSpec((None, OUT_ROW_TILE, D_MODEL), row),
            pl.BlockSpec((None, 3, D_MODEL), lambda b, i: (b, 0, 0)),
            pl.BlockSpec((D_RNN, D_MODEL), const),
            pl.BlockSpec((D_ATTN, D_MODEL), const),
            pl.BlockSpec((1, D_MODEL), const),
        ],
        out_specs=pl.BlockSpec((None, OUT_ROW_TILE, D_MODEL), row),
        out_shape=jax.ShapeDtypeStruct((batch, seq, D_MODEL), jnp.float32),
        compiler_params=pltpu.CompilerParams(
            dimension_semantics=("parallel", "parallel"), vmem_limit_bytes=VMEM_LIMIT),
        name="out_proj",
    )(yr, ya, x, mod_l, w_r, w_a, final_g.reshape(1, D_MODEL))


def _pad_w_in(w):
    return jnp.pad(w.astype(jnp.bfloat16), ((0, 0), (0, Z_COLS - D_IN)))


def _block_diag(w):
    n, c, d = w.shape
    eye = jnp.eye(n, dtype=w.dtype)
    return (eye[:, None, :, None] * w[:, :, None, :]).reshape(n * c, n * d)


def kernel(x, c, norm_g, ada_w, ada_b, w_in, conv_w, conv_b, lru_wx, lru_bx, lru_wa, lru_ba, lru_a, w_out, final_g):
    depth = w_in.shape[0]
    mod = _modulation(c, ada_w, ada_b)
    idx = jnp.arange(K_TILE)
    tri = (idx[None, :] < idx[:, None]).astype(jnp.bfloat16)
    for l in range(depth):
        w_pad = _pad_w_in(w_in[l])
        w_gates = jnp.concatenate([_block_diag(lru_wx[l]), _block_diag(lru_wa[l])], axis=-1).astype(jnp.bfloat16)
        b_gates = jnp.concatenate([lru_bx[l], lru_ba[l]]).reshape(1, 2 * D_RNN)
        w_o = w_out[l].astype(jnp.bfloat16)
        y_r, qt, ga, kv, wi, vt = _in_projection(x, mod[l], norm_g[l], w_pad, conv_w[l], conv_b[l],
                                                 w_gates, b_gates, lru_a[l])
        y_a = _dsa_attention(qt, kv, vt, wi, ga, tri)
        x = _out_projection(y_r, y_a, x, mod[l], w_o[:D_RNN], w_o[D_RNN:], final_g, l == depth - 1)
    return x
```

```python
import functools

import jax
import jax.numpy as jnp
from jax import lax
from jax.experimental import pallas as pl
from jax.experimental.pallas import tpu as pltpu

D_MODEL = 1024
D_RNN = 512
N_RNN_BLOCKS = 8
RNN_BLOCK = D_RNN // N_RNN_BLOCKS
CONV_WIDTH = 4
LRU_C = 8.0
N_HEADS = 8
HEAD_DIM = 64
D_ATTN = N_HEADS * HEAD_DIM
N_IDX_HEADS = 8
IDX_DIM = 64
MAX_TOPK = 256
EPS = 1e-6
LOG2_E = 1.4426950408889634

Z_RNN = 0
Z_Q = 1024
Z_KV = 1536
Z_GA = 1664
Z_QI = 2176
Z_KI = 2688
Z_COLS = 2816
D_IN = 2760
WI_LANE = 64

ROW_TILE = 512
OUT_ROW_TILE = 1024
LRU_STEPS = ROW_TILE // 8
Q_TILE = 256
K_TILE = 512
VT_ROWS = 80
MASKED = -1e30
COARSE_PROBES = 9
PROBES_PER_ROUND = 7
PLAIN_BISECT_ROUNDS = 5
PARTIAL_ROWS = 32
SWEEP_TILE = 256
TAIL_KEYS = K_TILE - Q_TILE
assert K_TILE == 2 * Q_TILE
VMEM_LIMIT = 48 * 1024 * 1024


def _silu(x):
    return x * jax.nn.sigmoid(x)


def _rms(x, g):
    return x * lax.rsqrt(jnp.mean(x * x, axis=-1, keepdims=True) + EPS) * g


def _reduce_keys(x, op):
    keys, queries = x.shape
    part = op(x.reshape(keys // PARTIAL_ROWS, PARTIAL_ROWS, queries), axis=0)
    return op(part, axis=0, keepdims=True)


def _mod_kernel(c_ref, w_ref, b_ref, o_ref):
    c_act = _silu(c_ref[...])
    o_ref[...] = jnp.dot(c_act, w_ref[...], precision=lax.Precision.HIGHEST,
                         preferred_element_type=jnp.float32) + b_ref[...]


def _modulation(c, ada_w, ada_b):
    depth = ada_w.shape[0]
    batch = c.shape[0]
    out = pl.pallas_call(
        _mod_kernel,
        grid=(depth, 3),
        in_specs=[
            pl.BlockSpec((batch, D_MODEL), lambda l, j: (0, 0)),
            pl.BlockSpec((None, D_MODEL, D_MODEL), lambda l, j: (l, 0, j)),
            pl.BlockSpec((None, None, 1, D_MODEL), lambda l, j: (l, j, 0, 0)),
        ],
        out_specs=pl.BlockSpec((None, None, batch, D_MODEL), lambda l, j: (l, j, 0, 0)),
        out_shape=jax.ShapeDtypeStruct((depth, 3, batch, D_MODEL), jnp.float32),
        compiler_params=pltpu.CompilerParams(vmem_limit_bytes=VMEM_LIMIT),
        name="adaln_mod",
    )(c, ada_w, ada_b.reshape(depth, 3, 1, D_MODEL))
    return out.transpose(0, 2, 1, 3)


def _inproj_kernel(x_ref, mod_ref, g_ref, w_ref, cw_ref, cb_ref, wg_ref, bg_ref, ap_ref,
                   yr_ref, qt_ref, ga_ref, kv_ref, wi_ref, vt_ref, perm_ref, tail_ref, h_ref):
    @pl.when(pl.program_id(1) == 0)
    def _():
        tail_ref[...] = jnp.zeros((8, D_RNN), jnp.float32)
        h_ref[...] = jnp.zeros((1, D_RNN), jnp.float32)

    x = x_ref[...]
    shift = mod_ref[0:1, :]
    scale = mod_ref[1:2, :]
    h = _rms(x, g_ref[...]) * (1.0 + scale) + shift
    h = h.astype(jnp.bfloat16)
    z_rnn = jnp.dot(h, w_ref[:, Z_RNN:Z_Q], preferred_element_type=jnp.float32)
    y_r = _rg_lru_tile(z_rnn[:, 0:D_RNN], z_rnn[:, D_RNN:2 * D_RNN],
                       cw_ref, cb_ref, wg_ref, bg_ref, ap_ref, perm_ref, tail_ref, h_ref)
    yr_ref[...] = y_r.astype(yr_ref.dtype)
    z = jnp.dot(h, w_ref[:, Z_Q:Z_COLS], preferred_element_type=jnp.float32)
    col = lambda start, width: z[:, start - Z_Q:start - Z_Q + width]
    for h in range(N_HEADS):
        q_h = col(Z_Q + h * HEAD_DIM, HEAD_DIM) * (HEAD_DIM ** -0.5 * LOG2_E)
        _store_query_tiles(qt_ref, h, q_h.T.astype(jnp.bfloat16))
    for h in range(N_IDX_HEADS):
        _store_query_tiles(qt_ref, N_HEADS + h, col(Z_QI + h * IDX_DIM, IDX_DIM).T.astype(jnp.bfloat16))
    ga_ref[...] = col(Z_GA, D_ATTN)
    kv_ref[...] = jnp.concatenate([col(Z_KV, 128), col(Z_KI, 128)], axis=-1).astype(jnp.bfloat16)
    wi_ref[...] = col(Z_KI, 128)
    pad_rows = lax.broadcasted_iota(jnp.int32, (VT_ROWS - HEAD_DIM, K_TILE), 0)
    for j in range(ROW_TILE // K_TILE):
        v_t = col(Z_KV + HEAD_DIM, HEAD_DIM)[j * K_TILE:(j + 1) * K_TILE, :].T
        vt_ref[j, 0:HEAD_DIM, :] = v_t.astype(jnp.bfloat16)
        vt_ref[j, HEAD_DIM:VT_ROWS, :] = jnp.where(pad_rows == 0, 1.0, 0.0).astype(jnp.bfloat16)


def _store_query_tiles(qt_ref, head, q_t):
    for j in range(ROW_TILE // Q_TILE):
        qt_ref[j, head] = q_t[:, j * Q_TILE:(j + 1) * Q_TILE]


def _in_projection(x, mod_l, norm_g, w_pad, conv_w, conv_b, w_gates, b_gates, a_param):
    batch, seq, _ = x.shape
    row = lambda b, i: (b, i, 0)
    const = lambda b, i: (0, 0)
    return pl.pallas_call(
        _inproj_kernel,
        grid=(batch, seq // ROW_TILE),
        in_specs=[
            pl.BlockSpec((None, ROW_TILE, D_MODEL), row),
            pl.BlockSpec((None, 3, D_MODEL), lambda b, i: (b, 0, 0)),
            pl.BlockSpec((1, D_MODEL), const),
            pl.BlockSpec((D_MODEL, Z_COLS), const),
            pl.BlockSpec((CONV_WIDTH, D_RNN), const),
            pl.BlockSpec((1, D_RNN), const),
            pl.BlockSpec((D_RNN, 2 * D_RNN), const),
            pl.BlockSpec((1, 2 * D_RNN), const),
            pl.BlockSpec((1, D_RNN), const),
        ],
        out_specs=[
            pl.BlockSpec((None, ROW_TILE, D_RNN), row),
            pl.BlockSpec((None, ROW_TILE // Q_TILE, N_HEADS + N_IDX_HEADS, HEAD_DIM, Q_TILE),
                         lambda b, i: (b, i, 0, 0, 0)),
            pl.BlockSpec((None, ROW_TILE, 512), row),
            pl.BlockSpec((None, ROW_TILE, 256), row),
            pl.BlockSpec((None, ROW_TILE, 128), row),
            pl.BlockSpec((None, ROW_TILE // K_TILE, VT_ROWS, K_TILE), lambda b, i: (b, i, 0, 0)),
        ],
        out_shape=[
            jax.ShapeDtypeStruct((batch, seq, D_RNN), jnp.bfloat16),
            jax.ShapeDtypeStruct((batch, seq // Q_TILE, N_HEADS + N_IDX_HEADS, HEAD_DIM, Q_TILE), jnp.bfloat16),
            jax.ShapeDtypeStruct((batch, seq, 512), jnp.float32),
            jax.ShapeDtypeStruct((batch, seq, 256), jnp.bfloat16),
            jax.ShapeDtypeStruct((batch, seq, 128), jnp.float32),
            jax.ShapeDtypeStruct((batch, seq // K_TILE, VT_ROWS, K_TILE), jnp.bfloat16),
        ],
        scratch_shapes=[
            pltpu.VMEM((D_RNN // 128, ROW_TILE, 128), jnp.float32),
            pltpu.VMEM((8, D_RNN), jnp.float32),
            pltpu.VMEM((1, D_RNN), jnp.float32),
        ],
        compiler_params=pltpu.CompilerParams(
            dimension_semantics=("parallel", "arbitrary"), vmem_limit_bytes=VMEM_LIMIT),
        name="in_proj_rg_lru",
    )(x, mod_l, norm_g.reshape(1, D_MODEL), w_pad, conv_w, conv_b.reshape(1, D_RNN),
      w_gates, b_gates, a_param.reshape(1, D_RNN))


def _rg_lru_tile(xr, gr, cw_ref, cb_ref, wg_ref, bg_ref, ap_ref, perm_ref, tail_ref, h_ref):
    ts, steps, slabs = ROW_TILE, LRU_STEPS, D_RNN // 128
    lanes = lambda k: slice(k * 128, (k + 1) * 128)

    for s in range(8):
        for k in range(slabs):
            perm_ref[k, pl.ds(s, steps, stride=8), :] = xr[s * steps:(s + 1) * steps, lanes(k)]
    x = jnp.concatenate([perm_ref[k] for k in range(slabs)], axis=-1).reshape(steps, 8, D_RNN)

    first = lax.broadcasted_iota(jnp.int32, (8, D_RNN), 0) == 0
    before = [jnp.where(first, tail_ref[8 - m:9 - m, :], pltpu.roll(x[steps - m], 1, 0))
              for m in range(CONV_WIDTH - 1, 0, -1)]
    x_ext = jnp.concatenate([jnp.stack(before), x], axis=0)
    xc = cb_ref[...] + cw_ref[CONV_WIDTH - 1:CONV_WIDTH, :] * x
    for k in range(CONV_WIDTH - 1):
        xc = xc + cw_ref[k:k + 1, :] * x_ext[k:k + steps]
    tail_ref[...] = xr[ts - 8:ts, :]

    gates = jnp.dot(xc.reshape(ts, D_RNN).astype(jnp.bfloat16), wg_ref[...],
                    preferred_element_type=jnp.float32) + bg_ref[...]
    gates = gates.reshape(steps, 8, 2 * D_RNN)
    gate_x = jax.nn.sigmoid(gates[:, :, 0:D_RNN])
    gate_a = jax.nn.sigmoid(gates[:, :, D_RNN:2 * D_RNN])
    neg_ap = -ap_ref[...]
    softplus = jnp.maximum(neg_ap, 0.0) + jnp.log(1.0 + jnp.exp(-jnp.abs(neg_ap)))
    log_a = (-LRU_C) * gate_a * softplus
    a = jnp.exp(log_a)
    gap = 1.0 - a * a
    mult = jnp.where(gap > 0.0, gap * lax.rsqrt(gap), 0.0)
    u = mult * gate_x * xc

    h_run, a_run = u[0], a[0]
    h_loc, a_cum = [h_run], [a_run]
    for j in range(1, steps):
        h_run = a[j] * h_run + u[j]
        a_run = a[j] * a_run
        h_loc.append(h_run)
        a_cum.append(a_run)
    state = h_ref[...]
    entering = []
    for s in range(8):
        entering.append(state)
        state = h_run[s:s + 1, :] + a_run[s:s + 1, :] * state
    h_ref[...] = state
    entering = jnp.concatenate(entering, axis=0)
    h = jnp.stack([h_loc[j] + a_cum[j] * entering for j in range(steps)]).reshape(ts, D_RNN)

    for k in range(slabs):
        perm_ref[k] = h[:, lanes(k)]
    h = jnp.concatenate(
        [jnp.concatenate([perm_ref[k, pl.ds(s, steps, stride=8), :] for k in range(slabs)], axis=-1)
         for s in range(8)], axis=0)
    return h * _silu(gr)


def _attn_kernel(qt_ref, kv_ref, vt_ref, wi_ref, ga_ref, tri_ref, y_ref,
                 sc_ref, sc16_ref, bias_ref, eqb_ref, s_ref, m_ref, mnew_ref, acc_ref, out_ref):
    tq, tk = Q_TILE, K_TILE
    topk = float(MAX_TOPK)
    qb = pl.program_id(1)
    n_keys = (qb + 1) * tq
    n_full = n_keys // tk
    has_tail = n_keys - n_full * tk > 0

    w_t = wi_ref[...].T[WI_LANE:WI_LANE + N_IDX_HEADS, :] * ((IDX_DIM ** -0.5) * (N_IDX_HEADS ** -0.5))

    def score_chunk(c, carry, size, diagonal):
        rmax, rmin = carry
        k0 = pl.multiple_of(c * tk, tk)
        ki = kv_ref[pl.ds(k0, size), 128:128 + IDX_DIM]
        score = jnp.zeros((size, tq), jnp.float32)
        for h in range(N_IDX_HEADS):
            logits = jnp.dot(ki, qt_ref[N_HEADS + h], preferred_element_type=jnp.float32)
            score = score + jnp.maximum(logits, 0.0) * w_t[h:h + 1, :]
        if diagonal:
            key_pos = k0 + lax.broadcasted_iota(jnp.int32, (size, tq), 0)
            causal = key_pos <= qb * tq + lax.broadcasted_iota(jnp.int32, (size, tq), 1)
            low, high = jnp.where(causal, score, -jnp.inf), jnp.where(causal, score, jnp.inf)
        else:
            low = high = score
        sc_ref[pl.ds(k0, size), :] = low
        sc16_ref[pl.ds(k0, size), :] = low.astype(jnp.bfloat16)
        return jnp.maximum(rmax, _reduce_keys(low, jnp.max)), jnp.minimum(rmin, _reduce_keys(high, jnp.min))

    n_before = jnp.where(has_tail, n_full, n_full - 1)
    extremes = lax.fori_loop(
        0, n_before, functools.partial(score_chunk, size=tk, diagonal=False),
        (jnp.full((1, tq), -jnp.inf, jnp.float32), jnp.full((1, tq), jnp.inf, jnp.float32)))
    rmax, rmin = lax.cond(has_tail,
                          lambda e: score_chunk(n_full, e, TAIL_KEYS, True),
                          lambda e: score_chunk(n_full - 1, e, tk, True), extremes)

    n_sweep_tiles = ((qb + 1) * tq) // SWEEP_TILE

    def sweep(mid, want):
        kinds = {"gt": (jnp.sum, 0.0), "ge": (jnp.sum, 0.0), "above": (jnp.min, jnp.inf), "below": (jnp.max, -jnp.inf)}

        def body(c, carry):
            k0 = pl.multiple_of(c * SWEEP_TILE, SWEEP_TILE)
            s = sc_ref[pl.ds(k0, SWEEP_TILE), :].reshape(SWEEP_TILE // PARTIAL_ROWS, PARTIAL_ROWS, tq)
            gt = s > mid
            terms = {"gt": lambda: jnp.where(gt, 1.0, 0.0), "ge": lambda: jnp.where(s >= mid, 1.0, 0.0),
                     "above": lambda: jnp.where(gt, s, jnp.inf), "below": lambda: jnp.where(gt, -jnp.inf, s)}
            out = []
            for name, acc in zip(want, carry):
                op = kinds[name][0]
                part = op(terms[name](), axis=0)
                out.append(acc + part if op is jnp.sum else
                           (jnp.minimum(acc, part) if op is jnp.min else jnp.maximum(acc, part)))
            return tuple(out)

        init = tuple(jnp.full((PARTIAL_ROWS, tq), kinds[name][1], jnp.float32) for name in want)
        res = lax.fori_loop(0, n_sweep_tiles, body, init)
        return [kinds[name][0](r, axis=0, keepdims=True) for name, r in zip(want, res)]

    def update(state, mid, cnt, tie, new_lo, new_hi):
        lo, hi, chi, act = state
        on = act > 0.5
        fin = jnp.logical_and(on, jnp.logical_or(cnt == topk, tie))
        go_on = jnp.logical_and(on, jnp.logical_not(fin))
        up = jnp.logical_and(go_on, cnt > topk)
        dn = jnp.logical_and(go_on, cnt < topk)
        lo = jnp.where(fin, mid, jnp.where(up, new_lo, lo))
        hi = jnp.where(fin, mid, jnp.where(dn, new_hi, hi))
        chi = jnp.where(jnp.logical_or(fin, dn), cnt, chi)
        act = jnp.where(jnp.logical_and(go_on, lo < hi), 1.0, 0.0)
        return lo, hi, chi, act

    def midpoint(state):
        return 0.5 * state[0] + 0.5 * state[1]

    n_valid = (qb * tq + 1 + lax.broadcasted_iota(jnp.int32, (1, tq), 1)).astype(jnp.float32)
    few = n_valid <= topk
    state = (rmin,
             jnp.where(few, -jnp.inf, rmax),
             jnp.where(few, topk, 0.0),
             jnp.where(jnp.logical_or(few, rmin >= rmax), 0.0, 1.0))

    def coarse_count(t16, also_ge=False):
        one, zero = jnp.ones((), jnp.bfloat16), jnp.zeros((), jnp.bfloat16)

        def tile_sum(hits):
            parts = [hits[g] for g in range(SWEEP_TILE // PARTIAL_ROWS)]
            while len(parts) > 1:
                parts = [parts[i] + parts[i + 1] for i in range(0, len(parts), 2)]
            return parts[0]

        def body(c, accs):
            k0 = pl.multiple_of(c * SWEEP_TILE, SWEEP_TILE)
            s = sc16_ref[pl.ds(k0, SWEEP_TILE), :].reshape(SWEEP_TILE // PARTIAL_ROWS, PARTIAL_ROWS, tq)
            out = [accs[0] + tile_sum(jnp.where(s > t16, one, zero))]
            if also_ge:
                out.append(accs[1] + tile_sum(jnp.where(s >= t16, one, zero)))
            return tuple(out)

        init = (jnp.zeros((PARTIAL_ROWS, tq), jnp.bfloat16),) * (2 if also_ge else 1)
        accs = lax.fori_loop(0, n_sweep_tiles, body, init)
        return [jnp.sum(a.astype(jnp.float32), axis=0, keepdims=True) for a in accs]

    lo, hi, chi, act = state
    on = act > 0.5
    pos, nonneg = coarse_count(jnp.zeros((1, tq), jnp.bfloat16), also_ge=True)
    at_zero = jnp.logical_and(on, jnp.logical_and(pos <= topk, nonneg >= topk))
    lo = jnp.where(at_zero, 0.0, jnp.where(jnp.logical_and(on, pos > topk), jnp.maximum(lo, 0.0), lo))
    hi = jnp.where(at_zero, 0.0, jnp.where(jnp.logical_and(on, nonneg < topk), jnp.minimum(hi, 0.0), hi))
    chi = jnp.where(at_zero, pos, chi)
    act = jnp.where(at_zero, 0.0, act)
    state = (lo, hi, chi, act)

    lo, hi, chi, act = state
    for _ in range(COARSE_PROBES):
        t16 = (0.5 * lo + 0.5 * hi).astype(jnp.bfloat16)
        t = t16.astype(jnp.float32)
        (cnt16,) = coarse_count(t16)
        on = act > 0.5
        lo = jnp.where(jnp.logical_and(on, cnt16 >= topk), jnp.maximum(lo, t), lo)
        hi = jnp.where(jnp.logical_and(on, cnt16 < topk), jnp.minimum(hi, t + jnp.abs(t) * (2.0 ** -7) + 1e-30), hi)
    (cnt,) = sweep(hi, ("gt",))
    state = update((lo, hi, chi, act), hi, cnt, False, hi, hi)

    def pending(state):
        return jnp.where(jnp.logical_and(state[3] > 0.5, topk - state[2] != 1.0), 1.0, 0.0)

    def plain_cond(st):
        return jnp.logical_and(st[1] > 0.5, st[0] < PLAIN_BISECT_ROUNDS)

    def plain_body(st):
        state = st[2:]
        for _ in range(PROBES_PER_ROUND):
            mid = midpoint(state)
            (cnt,) = sweep(mid, ("gt",))
            state = update(state, mid, cnt, False, mid, mid)
        return (st[0] + 1, jnp.max(pending(state))) + state

    st = lax.while_loop(plain_cond, plain_body, (jnp.int32(0), jnp.max(pending(state))) + state)
    lo, hi, chi, act = st[2:]
    (largest_below,) = sweep(hi, ("below",))
    one_short = jnp.logical_and(act > 0.5, topk - chi == 1.0)
    state = (jnp.where(one_short, largest_below, lo), jnp.where(one_short, largest_below, hi), chi,
             jnp.where(one_short, 0.0, act))

    def snap_cond(st):
        return jnp.logical_and(st[1] > 0.5, st[0] < 4096)

    def snap_body(st):
        state = st[2:]
        mid = midpoint(state)
        mid = jnp.where(mid >= state[1], state[0], mid)
        cnt, above, below = sweep(mid, ("gt", "above", "below"))
        state = update(state, mid, cnt, False, above, below)
        return (st[0] + 1, jnp.max(state[3])) + state

    st = lax.while_loop(snap_cond, snap_body, (jnp.int32(0), jnp.max(state[3])) + state)
    thr = st[3]
    need = topk - st[4]
    any_tie = jnp.max(need) > 0.5

    m_ref[...] = jnp.full(m_ref.shape, MASKED, jnp.float32)
    acc_ref[...] = jnp.zeros(acc_ref.shape, jnp.float32)
    eqb_ref[...] = jnp.zeros(eqb_ref.shape, jnp.float32)

    def attend_chunk(c, size):
        k0 = pl.multiple_of(c * tk, tk)
        s_idx = sc_ref[pl.ds(k0, size), :]
        bias_ref[0:size, :] = jnp.where(s_idx > thr, 0.0, MASKED)

        @pl.when(any_tie)
        def _():
            eq = s_idx == thr
            eq_f = jnp.where(eq, 1.0, 0.0)
            rank = jnp.dot(tri_ref[0:size, 0:size], eq_f.astype(jnp.bfloat16),
                           preferred_element_type=jnp.float32) + eqb_ref[...]
            tie_bias = jnp.where(jnp.logical_and(eq, rank < need), 0.0, MASKED)
            bias_ref[0:size, :] = jnp.where(s_idx > thr, 0.0, tie_bias)
            eqb_ref[...] = eqb_ref[...] + _reduce_keys(eq_f, jnp.sum)

        kc = kv_ref[pl.ds(k0, size), 0:HEAD_DIM]
        vt = vt_ref[c, :, 0:size]

        def logits(h):
            s = jnp.dot(kc, qt_ref[h], preferred_element_type=jnp.float32) + bias_ref[0:size, :]
            s_ref[h, 0:size, :] = s
            mnew_ref[h] = jnp.maximum(m_ref[h], _reduce_keys(s, jnp.max))

        def weigh(h):
            m_new = mnew_ref[h]
            alpha = jnp.exp2(m_ref[h] - m_new)
            p = jnp.exp2(s_ref[h, 0:size, :] - m_new)
            acc_ref[h] = alpha * acc_ref[h] + jnp.dot(
                vt, p.astype(jnp.bfloat16), preferred_element_type=jnp.float32)
            m_ref[h] = m_new

        for h in range(N_HEADS):
            logits(h)
        for h in range(N_HEADS):
            weigh(h)

    def attend_full(c, _):
        attend_chunk(c, tk)
        return 0

    lax.fori_loop(0, n_full, attend_full, 0)

    @pl.when(has_tail)
    def _():
        attend_chunk(n_full, TAIL_KEYS)

    for h in range(N_HEADS):
        rows = slice(h * HEAD_DIM, (h + 1) * HEAD_DIM)
        out_ref[rows, :] = acc_ref[h, 0:HEAD_DIM, :] / acc_ref[h, HEAD_DIM:HEAD_DIM + 1, :]
    y = out_ref[...].T * _silu(ga_ref[...])
    y_ref[...] = y.astype(y_ref.dtype)


def _dsa_attention(qt, kv, vt, wi, ga, tri):
    batch, seq, _ = kv.shape
    row = lambda b, i: (b, i, 0)
    return pl.pallas_call(
        _attn_kernel,
        grid=(batch, seq // Q_TILE),
        in_specs=[
            pl.BlockSpec((None, None, N_HEADS + N_IDX_HEADS, HEAD_DIM, Q_TILE), lambda b, i: (b, i, 0, 0, 0)),
            pl.BlockSpec((None, seq, 256), lambda b, i: (b, 0, 0)),
            pl.BlockSpec((None, seq // K_TILE, VT_ROWS, K_TILE), lambda b, i: (b, 0, 0, 0)),
            pl.BlockSpec((None, Q_TILE, 128), row),
            pl.BlockSpec((None, Q_TILE, D_ATTN), row),
            pl.BlockSpec((K_TILE, K_TILE), lambda b, i: (0, 0)),
        ],
        out_specs=pl.BlockSpec((None, Q_TILE, D_ATTN), row),
        out_shape=jax.ShapeDtypeStruct((batch, seq, D_ATTN), jnp.bfloat16),
        scratch_shapes=[
            pltpu.VMEM((seq, Q_TILE), jnp.float32),
            pltpu.VMEM((seq, Q_TILE), jnp.bfloat16),
            pltpu.VMEM((K_TILE, Q_TILE), jnp.float32),
            pltpu.VMEM((1, Q_TILE), jnp.float32),
            pltpu.VMEM((N_HEADS, K_TILE, Q_TILE), jnp.float32),
            pltpu.VMEM((N_HEADS, 1, Q_TILE), jnp.float32),
            pltpu.VMEM((N_HEADS, 1, Q_TILE), jnp.float32),
            pltpu.VMEM((N_HEADS, VT_ROWS, Q_TILE), jnp.float32),
            pltpu.VMEM((D_ATTN, Q_TILE), jnp.float32),
        ],
        compiler_params=pltpu.CompilerParams(
            dimension_semantics=("parallel", "parallel"), vmem_limit_bytes=VMEM_LIMIT),
        name="dsa_attention",
    )(qt, kv, vt, wi, ga, tri)


def _outproj_kernel(yr_ref, ya_ref, x_ref, mod_ref, wr_ref, wa_ref, g_ref, o_ref, *, final_norm):
    y = jnp.dot(yr_ref[...], wr_ref[...], preferred_element_type=jnp.float32)
    y = y + jnp.dot(ya_ref[...], wa_ref[...], preferred_element_type=jnp.float32)
    x_new = x_ref[...] + mod_ref[2:3, :] * y
    if final_norm:
        x_new = _rms(x_new, g_ref[...])
    o_ref[...] = x_new


def _out_projection(yr, ya, x, mod_l, w_r, w_a, final_g, final_norm):
    batch, seq, _ = x.shape
    row = lambda b, i: (b, i, 0)
    const = lambda b, i: (0, 0)
    return pl.pallas_call(
        functools.partial(_outproj_kernel, final_norm=final_norm),
        grid=(batch, seq // OUT_ROW_TILE),
        in_specs=[
            pl.BlockSpec((None, OUT_ROW_TILE, D_RNN), row),
            pl.BlockSpec((None, OUT_ROW_TILE, D_ATTN), row),
            pl.BlockSpec((None, OUT_ROW_TILE, D_MODEL), row),
            pl.BlockSpec((None, 3, D_MODEL), lambda b, i: (b, 0, 0)),
            pl.BlockSpec((D_RNN, D_MODEL), const),
            pl.BlockSpec((D_ATTN, D_MODEL), const),
            pl.BlockSpec((1, D_MODEL), const),
        ],
        out_specs=pl.BlockSpec((None, OUT_ROW_TILE, D_MODEL), row),
        out_shape=jax.ShapeDtypeStruct((batch, seq, D_MODEL), jnp.float32),
        compiler_params=pltpu.CompilerParams(
            dimension_semantics=("parallel", "parallel"), vmem_limit_bytes=VMEM_LIMIT),
        name="out_proj",
    )(yr, ya, x, mod_l, w_r, w_a, final_g.reshape(1, D_MODEL))


def _pad_w_in(w):
    return jnp.pad(w.astype(jnp.bfloat16), ((0, 0), (0, Z_COLS - D_IN)))


def _block_diag(w):
    n, c, d = w.shape
    eye = jnp.eye(n, dtype=w.dtype)
    return (eye[:, None, :, None] * w[:, :, None, :]).reshape(n * c, n * d)


def kernel(x, c, norm_g, ada_w, ada_b, w_in, conv_w, conv_b, lru_wx, lru_bx, lru_wa, lru_ba, lru_a, w_out, final_g):
    depth = w_in.shape[0]
    mod = _modulation(c, ada_w, ada_b)
    idx = jnp.arange(K_TILE)
    tri = (idx[None, :] < idx[:, None]).astype(jnp.bfloat16)
    for l in range(depth):
        w_pad = _pad_w_in(w_in[l])
        w_gates = jnp.concatenate([_block_diag(lru_wx[l]), _block_diag(lru_wa[l])], axis=-1).astype(jnp.bfloat16)
        b_gates = jnp.concatenate([lru_bx[l], lru_ba[l]]).reshape(1, 2 * D_RNN)
        w_o = w_out[l].astype(jnp.bfloat16)
        y_r, qt, ga, kv, wi, vt = _in_projection(x, mod[l], norm_g[l], w_pad, conv_w[l], conv_b[l],
                                                 w_gates, b_gates, lru_a[l])
        y_a = _dsa_attention(qt, kv, vt, wi, ga, tri)
        x = _out_projection(y_r, y_a, x, mod[l], w_o[:D_RNN], w_o[D_RNN:], final_g, l == depth - 1)
    return x
```

```python
import functools

import jax
import jax.numpy as jnp
from jax import lax
from jax.experimental import pallas as pl
from jax.experimental.pallas import tpu as pltpu

D_MODEL = 1024
D_RNN = 512
N_RNN_BLOCKS = 8
RNN_BLOCK = D_RNN // N_RNN_BLOCKS
CONV_WIDTH = 4
LRU_C = 8.0
N_HEADS = 8
HEAD_DIM = 64
D_ATTN = N_HEADS * HEAD_DIM
N_IDX_HEADS = 8
IDX_DIM = 64
MAX_TOPK = 256
EPS = 1e-6
LOG2_E = 1.4426950408889634

Z_RNN = 0
Z_Q = 1024
Z_KV = 1536
Z_GA = 1664
Z_QI = 2176
Z_KI = 2688
Z_COLS = 2816
D_IN = 2760
WI_LANE = 64

ROW_TILE = 512
OUT_ROW_TILE = 1024
LRU_STEPS = ROW_TILE // 8
Q_TILE = 256
K_TILE = 512
VT_ROWS = 80
MASKED = -1e30
COARSE_PROBES = 9
PROBES_PER_ROUND = 7
PLAIN_BISECT_ROUNDS = 5
PARTIAL_ROWS = 32
SWEEP_TILE = 256
TAIL_KEYS = K_TILE - Q_TILE
assert K_TILE == 2 * Q_TILE
VMEM_LIMIT = 48 * 1024 * 1024


def _sigmoid(x):
    return 0.5 * jnp.tanh(0.5 * x) + 0.5


def _silu(x):
    return x * _sigmoid(x)


def _rms(x, g):
    return x * lax.rsqrt(jnp.mean(x * x, axis=-1, keepdims=True) + EPS) * g


def _reduce_keys(x, op):
    keys, queries = x.shape
    part = op(x.reshape(keys // PARTIAL_ROWS, PARTIAL_ROWS, queries), axis=0)
    return op(part, axis=0, keepdims=True)


def _mod_kernel(c_ref, w_ref, b_ref, o_ref):
    c_act = _silu(c_ref[...])
    o_ref[...] = jnp.dot(c_act, w_ref[...], precision=lax.Precision.HIGHEST,
                         preferred_element_type=jnp.float32) + b_ref[...]


def _modulation(c, ada_w, ada_b):
    depth = ada_w.shape[0]
    batch = c.shape[0]
    out = pl.pallas_call(
        _mod_kernel,
        grid=(depth, 3),
        in_specs=[
            pl.BlockSpec((batch, D_MODEL), lambda l, j: (0, 0)),
            pl.BlockSpec((None, D_MODEL, D_MODEL), lambda l, j: (l, 0, j)),
            pl.BlockSpec((None, None, 1, D_MODEL), lambda l, j: (l, j, 0, 0)),
        ],
        out_specs=pl.BlockSpec((None, None, batch, D_MODEL), lambda l, j: (l, j, 0, 0)),
        out_shape=jax.ShapeDtypeStruct((depth, 3, batch, D_MODEL), jnp.float32),
        compiler_params=pltpu.CompilerParams(vmem_limit_bytes=VMEM_LIMIT),
        name="adaln_mod",
    )(c, ada_w, ada_b.reshape(depth, 3, 1, D_MODEL))
    return out.transpose(0, 2, 1, 3)


def _inproj_kernel(x_ref, mod_ref, g_ref, w_ref, cw_ref, cb_ref, wg_ref, bg_ref, ap_ref,
                   yr_ref, qt_ref, ga_ref, kv_ref, wi_ref, vt_ref, perm_ref, tail_ref, h_ref):
    @pl.when(pl.program_id(1) == 0)
    def _():
        tail_ref[...] = jnp.zeros((8, D_RNN), jnp.float32)
        h_ref[...] = jnp.zeros((1, D_RNN), jnp.float32)

    x = x_ref[...]
    shift = mod_ref[0:1, :]
    scale = mod_ref[1:2, :]
    h = _rms(x, g_ref[...]) * (1.0 + scale) + shift
    h = h.astype(jnp.bfloat16)
    z_rnn = jnp.dot(h, w_ref[:, Z_RNN:Z_Q], preferred_element_type=jnp.float32)
    y_r = _rg_lru_tile(z_rnn[:, 0:D_RNN], z_rnn[:, D_RNN:2 * D_RNN],
                       cw_ref, cb_ref, wg_ref, bg_ref, ap_ref, perm_ref, tail_ref, h_ref)
    yr_ref[...] = y_r.astype(yr_ref.dtype)
    z = jnp.dot(h, w_ref[:, Z_Q:Z_COLS], preferred_element_type=jnp.float32)
    col = lambda start, width: z[:, start - Z_Q:start - Z_Q + width]
    for h in range(N_HEADS):
        q_h = col(Z_Q + h * HEAD_DIM, HEAD_DIM) * (HEAD_DIM ** -0.5 * LOG2_E)
        _store_query_tiles(qt_ref, h, q_h.T.astype(jnp.bfloat16))
    for h in range(N_IDX_HEADS):
        _store_query_tiles(qt_ref, N_HEADS + h, col(Z_QI + h * IDX_DIM, IDX_DIM).T.astype(jnp.bfloat16))
    ga_ref[...] = col(Z_GA, D_ATTN)
    kv_ref[...] = jnp.concatenate([col(Z_KV, 128), col(Z_KI, 128)], axis=-1).astype(jnp.bfloat16)
    wi_ref[...] = col(Z_KI, 128)
    pad_rows = lax.broadcasted_iota(jnp.int32, (VT_ROWS - HEAD_DIM, K_TILE), 0)
    for j in range(ROW_TILE // K_TILE):
        v_t = col(Z_KV + HEAD_DIM, HEAD_DIM)[j * K_TILE:(j + 1) * K_TILE, :].T
        vt_ref[j, 0:HEAD_DIM, :] = v_t.astype(jnp.bfloat16)
        vt_ref[j, HEAD_DIM:VT_ROWS, :] = jnp.where(pad_rows == 0, 1.0, 0.0).astype(jnp.bfloat16)


def _store_query_tiles(qt_ref, head, q_t):
    for j in range(ROW_TILE // Q_TILE):
        qt_ref[j, head] = q_t[:, j * Q_TILE:(j + 1) * Q_TILE]


def _in_projection(x, mod_l, norm_g, w_pad, conv_w, conv_b, w_gates, b_gates, a_param):
    batch, seq, _ = x.shape
    row = lambda b, i: (b, i, 0)
    const = lambda b, i: (0, 0)
    return pl.pallas_call(
        _inproj_kernel,
        grid=(batch, seq // ROW_TILE),
        in_specs=[
            pl.BlockSpec((None, ROW_TILE, D_MODEL), row),
            pl.BlockSpec((None, 3, D_MODEL), lambda b, i: (b, 0, 0)),
            pl.BlockSpec((1, D_MODEL), const),
            pl.BlockSpec((D_MODEL, Z_COLS), const),
            pl.BlockSpec((CONV_WIDTH, D_RNN), const),
            pl.BlockSpec((1, D_RNN), const),
            pl.BlockSpec((D_RNN, 2 * D_RNN), const),
            pl.BlockSpec((1, 2 * D_RNN), const),
            pl.BlockSpec((1, D_RNN), const),
        ],
        out_specs=[
            pl.BlockSpec((None, ROW_TILE, D_RNN), row),
            pl.BlockSpec((None, ROW_TILE // Q_TILE, N_HEADS + N_IDX_HEADS, HEAD_DIM, Q_TILE),
                         lambda b, i: (b, i, 0, 0, 0)),
            pl.BlockSpec((None, ROW_TILE, 512), row),
            pl.BlockSpec((None, ROW_TILE, 256), row),
            pl.BlockSpec((None, ROW_TILE, 128), row),
            pl.BlockSpec((None, ROW_TILE // K_TILE, VT_ROWS, K_TILE), lambda b, i: (b, i, 0, 0)),
        ],
        out_shape=[
            jax.ShapeDtypeStruct((batch, seq, D_RNN), jnp.bfloat16),
            jax.ShapeDtypeStruct((batch, seq // Q_TILE, N_HEADS + N_IDX_HEADS, HEAD_DIM, Q_TILE), jnp.bfloat16),
            jax.ShapeDtypeStruct((batch, seq, 512), jnp.float32),
            jax.ShapeDtypeStruct((batch, seq, 256), jnp.bfloat16),
            jax.ShapeDtypeStruct((batch, seq, 128), jnp.float32),
            jax.ShapeDtypeStruct((batch, seq // K_TILE, VT_ROWS, K_TILE), jnp.bfloat16),
        ],
        scratch_shapes=[
            pltpu.VMEM((D_RNN // 128, ROW_TILE, 128), jnp.float32),
            pltpu.VMEM((8, D_RNN), jnp.float32),
            pltpu.VMEM((1, D_RNN), jnp.float32),
        ],
        compiler_params=pltpu.CompilerParams(
            dimension_semantics=("parallel", "arbitrary"), vmem_limit_bytes=VMEM_LIMIT),
        name="in_proj_rg_lru",
    )(x, mod_l, norm_g.reshape(1, D_MODEL), w_pad, conv_w, conv_b.reshape(1, D_RNN),
      w_gates, b_gates, a_param.reshape(1, D_RNN))


def _rg_lru_tile(xr, gr, cw_ref, cb_ref, wg_ref, bg_ref, ap_ref, perm_ref, tail_ref, h_ref):
    ts, steps, slabs = ROW_TILE, LRU_STEPS, D_RNN // 128
    lanes = lambda k: slice(k * 128, (k + 1) * 128)

    for s in range(8):
        for k in range(slabs):
            perm_ref[k, pl.ds(s, steps, stride=8), :] = xr[s * steps:(s + 1) * steps, lanes(k)]
    x = jnp.concatenate([perm_ref[k] for k in range(slabs)], axis=-1).reshape(steps, 8, D_RNN)

    first = lax.broadcasted_iota(jnp.int32, (8, D_RNN), 0) == 0
    before = [jnp.where(first, tail_ref[8 - m:9 - m, :], pltpu.roll(x[steps - m], 1, 0))
              for m in range(CONV_WIDTH - 1, 0, -1)]
    x_ext = jnp.concatenate([jnp.stack(before), x], axis=0)
    xc = cb_ref[...] + cw_ref[CONV_WIDTH - 1:CONV_WIDTH, :] * x
    for k in range(CONV_WIDTH - 1):
        xc = xc + cw_ref[k:k + 1, :] * x_ext[k:k + steps]
    tail_ref[...] = xr[ts - 8:ts, :]

    gates = jnp.dot(xc.reshape(ts, D_RNN).astype(jnp.bfloat16), wg_ref[...],
                    preferred_element_type=jnp.float32) + bg_ref[...]
    gates = gates.reshape(steps, 8, 2 * D_RNN)
    gate_x = _sigmoid(gates[:, :, 0:D_RNN])
    gate_a = _sigmoid(gates[:, :, D_RNN:2 * D_RNN])
    neg_ap = -ap_ref[...]
    softplus = jnp.maximum(neg_ap, 0.0) + jnp.log(1.0 + jnp.exp(-jnp.abs(neg_ap)))
    log_a = (-LRU_C) * gate_a * softplus
    a = jnp.exp(log_a)
    gap = 1.0 - a * a
    mult = jnp.where(gap > 0.0, gap * lax.rsqrt(gap), 0.0)
    u = mult * gate_x * xc

    h_run, a_run = u[0], a[0]
    h_loc, a_cum = [h_run], [a_run]
    for j in range(1, steps):
        h_run = a[j] * h_run + u[j]
        a_run = a[j] * a_run
        h_loc.append(h_run)
        a_cum.append(a_run)
    state = h_ref[...]
    entering = []
    for s in range(8):
        entering.append(state)
        state = h_run[s:s + 1, :] + a_run[s:s + 1, :] * state
    h_ref[...] = state
    entering = jnp.concatenate(entering, axis=0)
    h = jnp.stack([h_loc[j] + a_cum[j] * entering for j in range(steps)]).reshape(ts, D_RNN)

    for k in range(slabs):
        perm_ref[k] = h[:, lanes(k)]
    h = jnp.concatenate(
        [jnp.concatenate([perm_ref[k, pl.ds(s, steps, stride=8), :] for k in range(slabs)], axis=-1)
         for s in range(8)], axis=0)
    return h * _silu(gr)


def _attn_kernel(qt_ref, kv_ref, vt_ref, wi_ref, ga_ref, tri_ref, y_ref,
                 sc_ref, sc16_ref, bias_ref, eqb_ref, s_ref, m_ref, mnew_ref, acc_ref, out_ref):
    tq, tk = Q_TILE, K_TILE
    topk = float(MAX_TOPK)
    qb = pl.program_id(1)
    n_keys = (qb + 1) * tq
    n_full = n_keys // tk
    has_tail = n_keys - n_full * tk > 0

    w_t = wi_ref[...].T[WI_LANE:WI_LANE + N_IDX_HEADS, :] * ((IDX_DIM ** -0.5) * (N_IDX_HEADS ** -0.5))

    def score_chunk(c, carry, size, diagonal):
        rmax, rmin = carry
        k0 = pl.multiple_of(c * tk, tk)
        ki = kv_ref[pl.ds(k0, size), 128:128 + IDX_DIM]
        score = jnp.zeros((size, tq), jnp.float32)
        for h in range(N_IDX_HEADS):
            logits = jnp.dot(ki, qt_ref[N_HEADS + h], preferred_element_type=jnp.float32)
            score = score + jnp.maximum(logits, 0.0) * w_t[h:h + 1, :]
        if diagonal:
            key_pos = k0 + lax.broadcasted_iota(jnp.int32, (size, tq), 0)
            causal = key_pos <= qb * tq + lax.broadcasted_iota(jnp.int32, (size, tq), 1)
            low, high = jnp.where(causal, score, -jnp.inf), jnp.where(causal, score, jnp.inf)
        else:
            low = high = score
        sc_ref[pl.ds(k0, size), :] = low
        sc16_ref[pl.ds(k0, size), :] = low.astype(jnp.bfloat16)
        return jnp.maximum(rmax, _reduce_keys(low, jnp.max)), jnp.minimum(rmin, _reduce_keys(high, jnp.min))

    n_before = jnp.where(has_tail, n_full, n_full - 1)
    extremes = lax.fori_loop(
        0, n_before, functools.partial(score_chunk, size=tk, diagonal=False),
        (jnp.full((1, tq), -jnp.inf, jnp.float32), jnp.full((1, tq), jnp.inf, jnp.float32)))
    rmax, rmin = lax.cond(has_tail,
                          lambda e: score_chunk(n_full, e, TAIL_KEYS, True),
                          lambda e: score_chunk(n_full - 1, e, tk, True), extremes)

    n_sweep_tiles = ((qb + 1) * tq) // SWEEP_TILE

    def sweep(mid, want):
        kinds = {"gt": (jnp.sum, 0.0), "ge": (jnp.sum, 0.0), "above": (jnp.min, jnp.inf), "below": (jnp.max, -jnp.inf)}

        def body(c, carry):
            k0 = pl.multiple_of(c * SWEEP_TILE, SWEEP_TILE)
            s = sc_ref[pl.ds(k0, SWEEP_TILE), :].reshape(SWEEP_TILE // PARTIAL_ROWS, PARTIAL_ROWS, tq)
            gt = s > mid
            terms = {"gt": lambda: jnp.where(gt, 1.0, 0.0), "ge": lambda: jnp.where(s >= mid, 1.0, 0.0),
                     "above": lambda: jnp.where(gt, s, jnp.inf), "below": lambda: jnp.where(gt, -jnp.inf, s)}
            out = []
            for name, acc in zip(want, carry):
                op = kinds[name][0]
                part = op(terms[name](), axis=0)
                out.append(acc + part if op is jnp.sum else
                           (jnp.minimum(acc, part) if op is jnp.min else jnp.maximum(acc, part)))
            return tuple(out)

        init = tuple(jnp.full((PARTIAL_ROWS, tq), kinds[name][1], jnp.float32) for name in want)
        res = lax.fori_loop(0, n_sweep_tiles, body, init)
        return [kinds[name][0](r, axis=0, keepdims=True) for name, r in zip(want, res)]

    def update(state, mid, cnt, tie, new_lo, new_hi):
        lo, hi, chi, act = state
        on = act > 0.5
        fin = jnp.logical_and(on, jnp.logical_or(cnt == topk, tie))
        go_on = jnp.logical_and(on, jnp.logical_not(fin))
        up = jnp.logical_and(go_on, cnt > topk)
        dn = jnp.logical_and(go_on, cnt < topk)
        lo = jnp.where(fin, mid, jnp.where(up, new_lo, lo))
        hi = jnp.where(fin, mid, jnp.where(dn, new_hi, hi))
        chi = jnp.where(jnp.logical_or(fin, dn), cnt, chi)
        act = jnp.where(jnp.logical_and(go_on, lo < hi), 1.0, 0.0)
        return lo, hi, chi, act

    def midpoint(state):
        return 0.5 * state[0] + 0.5 * state[1]

    n_valid = (qb * tq + 1 + lax.broadcasted_iota(jnp.int32, (1, tq), 1)).astype(jnp.float32)
    few = n_valid <= topk
    state = (rmin,
             jnp.where(few, -jnp.inf, rmax),
             jnp.where(few, topk, 0.0),
             jnp.where(jnp.logical_or(few, rmin >= rmax), 0.0, 1.0))

    def coarse_count(t16, also_ge=False):
        one, zero = jnp.ones((), jnp.bfloat16), jnp.zeros((), jnp.bfloat16)

        def tile_sum(hits):
            parts = [hits[g] for g in range(SWEEP_TILE // PARTIAL_ROWS)]
            while len(parts) > 1:
                parts = [parts[i] + parts[i + 1] for i in range(0, len(parts), 2)]
            return parts[0]

        def body(c, accs):
            k0 = pl.multiple_of(c * SWEEP_TILE, SWEEP_TILE)
            s = sc16_ref[pl.ds(k0, SWEEP_TILE), :].reshape(SWEEP_TILE // PARTIAL_ROWS, PARTIAL_ROWS, tq)
            out = [accs[0] + tile_sum(jnp.where(s > t16, one, zero))]
            if also_ge:
                out.append(accs[1] + tile_sum(jnp.where(s >= t16, one, zero)))
            return tuple(out)

        init = (jnp.zeros((PARTIAL_ROWS, tq), jnp.bfloat16),) * (2 if also_ge else 1)
        accs = lax.fori_loop(0, n_sweep_tiles, body, init)
        return [jnp.sum(a.astype(jnp.float32), axis=0, keepdims=True) for a in accs]

    lo, hi, chi, act = state
    on = act > 0.5
    pos, nonneg = coarse_count(jnp.zeros((1, tq), jnp.bfloat16), also_ge=True)
    at_zero = jnp.logical_and(on, jnp.logical_and(pos <= topk, nonneg >= topk))
    lo = jnp.where(at_zero, 0.0, jnp.where(jnp.logical_and(on, pos > topk), jnp.maximum(lo, 0.0), lo))
    hi = jnp.where(at_zero, 0.0, jnp.where(jnp.logical_and(on, nonneg < topk), jnp.minimum(hi, 0.0), hi))
    chi = jnp.where(at_zero, pos, chi)
    act = jnp.where(at_zero, 0.0, act)

    for _ in range(COARSE_PROBES):
        t16 = (0.5 * lo + 0.5 * hi).astype(jnp.bfloat16)
        t = t16.astype(jnp.float32)
        (cnt16,) = coarse_count(t16)
        on = act > 0.5
        lo = jnp.where(jnp.logical_and(on, cnt16 >= topk), jnp.maximum(lo, t), lo)
        hi = jnp.where(jnp.logical_and(on, cnt16 < topk), jnp.minimum(hi, t + jnp.abs(t) * (2.0 ** -7) + 1e-30), hi)
    (cnt,) = sweep(hi, ("gt",))
    state = update((lo, hi, chi, act), hi, cnt, False, hi, hi)

    def pending(state):
        return jnp.where(jnp.logical_and(state[3] > 0.5, topk - state[2] != 1.0), 1.0, 0.0)

    def plain_cond(st):
        return jnp.logical_and(st[1] > 0.5, st[0] < PLAIN_BISECT_ROUNDS)

    def plain_body(st):
        state = st[2:]
        for _ in range(PROBES_PER_ROUND):
            mid = midpoint(state)
            (cnt,) = sweep(mid, ("gt",))
            state = update(state, mid, cnt, False, mid, mid)
        return (st[0] + 1, jnp.max(pending(state))) + state

    st = lax.while_loop(plain_cond, plain_body, (jnp.int32(0), jnp.max(pending(state))) + state)
    lo, hi, chi, act = st[2:]
    (largest_below,) = sweep(hi, ("below",))
    one_short = jnp.logical_and(act > 0.5, topk - chi == 1.0)
    state = (jnp.where(one_short, largest_below, lo), jnp.where(one_short, largest_below, hi), chi,
             jnp.where(one_short, 0.0, act))

    def snap_cond(st):
        return jnp.logical_and(st[1] > 0.5, st[0] < 4096)

    def snap_body(st):
        state = st[2:]
        mid = midpoint(state)
        mid = jnp.where(mid >= state[1], state[0], mid)
        cnt, above, below = sweep(mid, ("gt", "above", "below"))
        state = update(state, mid, cnt, False, above, below)
        return (st[0] + 1, jnp.max(state[3])) + state

    st = lax.while_loop(snap_cond, snap_body, (jnp.int32(0), jnp.max(state[3])) + state)
    thr = st[3]
    need = topk - st[4]
    any_tie = jnp.max(need) > 0.5

    m_ref[...] = jnp.full(m_ref.shape, MASKED, jnp.float32)
    acc_ref[...] = jnp.zeros(acc_ref.shape, jnp.float32)
    eqb_ref[...] = jnp.zeros(eqb_ref.shape, jnp.float32)

    def attend_chunk(c, size):
        k0 = pl.multiple_of(c * tk, tk)
        s_idx = sc_ref[pl.ds(k0, size), :]
        bias_ref[0:size, :] = jnp.where(s_idx > thr, 0.0, MASKED)

        @pl.when(any_tie)
        def _():
            eq = s_idx == thr
            eq_f = jnp.where(eq, 1.0, 0.0)
            rank = jnp.dot(tri_ref[0:size, 0:size], eq_f.astype(jnp.bfloat16),
                           preferred_element_type=jnp.float32) + eqb_ref[...]
            tie_bias = jnp.where(jnp.logical_and(eq, rank < need), 0.0, MASKED)
            bias_ref[0:size, :] = jnp.where(s_idx > thr, 0.0, tie_bias)
            eqb_ref[...] = eqb_ref[...] + _reduce_keys(eq_f, jnp.sum)

        kc = kv_ref[pl.ds(k0, size), 0:HEAD_DIM]
        vt = vt_ref[c, :, 0:size]

        def logits(h):
            s = jnp.dot(kc, qt_ref[h], preferred_element_type=jnp.float32) + bias_ref[0:size, :]
            s_ref[h, 0:size, :] = s
            mnew_ref[h] = jnp.maximum(m_ref[h], _reduce_keys(s, jnp.max))

        def weigh(h):
            m_new = mnew_ref[h]
            alpha = jnp.exp2(m_ref[h] - m_new)
            p = jnp.exp2(s_ref[h, 0:size, :] - m_new)
            acc_ref[h] = alpha * acc_ref[h] + jnp.dot(
                vt, p.astype(jnp.bfloat16), preferred_element_type=jnp.float32)
            m_ref[h] = m_new

        for h in range(N_HEADS):
            logits(h)
        for h in range(N_HEADS):
            weigh(h)

    def attend_full(c, _):
        attend_chunk(c, tk)
        return 0

    lax.fori_loop(0, n_full, attend_full, 0)

    @pl.when(has_tail)
    def _():
        attend_chunk(n_full, TAIL_KEYS)

    for h in range(N_HEADS):
        rows = slice(h * HEAD_DIM, (h + 1) * HEAD_DIM)
        out_ref[rows, :] = acc_ref[h, 0:HEAD_DIM, :] / acc_ref[h, HEAD_DIM:HEAD_DIM + 1, :]
    y = out_ref[...].T * _silu(ga_ref[...])
    y_ref[...] = y.astype(y_ref.dtype)


def _dsa_attention(qt, kv, vt, wi, ga, tri):
    batch, seq, _ = kv.shape
    assert seq // PARTIAL_ROWS <= 256, "bf16 hit counters are exact only up to 256 per slot"
    row = lambda b, i: (b, i, 0)
    return pl.pallas_call(
        _attn_kernel,
        grid=(batch, seq // Q_TILE),
        in_specs=[
            pl.BlockSpec((None, None, N_HEADS + N_IDX_HEADS, HEAD_DIM, Q_TILE), lambda b, i: (b, i, 0, 0, 0)),
            pl.BlockSpec((None, seq, 256), lambda b, i: (b, 0, 0)),
            pl.BlockSpec((None, seq // K_TILE, VT_ROWS, K_TILE), lambda b, i: (b, 0, 0, 0)),
            pl.BlockSpec((None, Q_TILE, 128), row),
            pl.BlockSpec((None, Q_TILE, D_ATTN), row),
            pl.BlockSpec((K_TILE, K_TILE), lambda b, i: (0, 0)),
        ],
        out_specs=pl.BlockSpec((None, Q_TILE, D_ATTN), row),
        out_shape=jax.ShapeDtypeStruct((batch, seq, D_ATTN), jnp.bfloat16),
        scratch_shapes=[
            pltpu.VMEM((seq, Q_TILE), jnp.float32),
            pltpu.VMEM((seq, Q_TILE), jnp.bfloat16),
            pltpu.VMEM((K_TILE, Q_TILE), jnp.float32),
            pltpu.VMEM((1, Q_TILE), jnp.float32),
            pltpu.VMEM((N_HEADS, K_TILE, Q_TILE), jnp.float32),
            pltpu.VMEM((N_HEADS, 1, Q_TILE), jnp.float32),
            pltpu.VMEM((N_HEADS, 1, Q_TILE), jnp.float32),
            pltpu.VMEM((N_HEADS, VT_ROWS, Q_TILE), jnp.float32),
            pltpu.VMEM((D_ATTN, Q_TILE), jnp.float32),
        ],
        compiler_params=pltpu.CompilerParams(
            dimension_semantics=("parallel", "parallel"), vmem_limit_bytes=VMEM_LIMIT),
        name="dsa_attention",
    )(qt, kv, vt, wi, ga, tri)


def _outproj_kernel(yr_ref, ya_ref, x_ref, mod_ref, wr_ref, wa_ref, g_ref, o_ref, *, final_norm):
    y = jnp.dot(yr_ref[...], wr_ref[...], preferred_element_type=jnp.float32)
    y = y + jnp.dot(ya_ref[...], wa_ref[...], preferred_element_type=jnp.float32)
    x_new = x_ref[...] + mod_ref[2:3, :] * y
    if final_norm:
        x_new = _rms(x_new, g_ref[...])
    o_ref[...] = x_new


def _out_projection(yr, ya, x, mod_l, w_r, w_a, final_g, final_norm):
    batch, seq, _ = x.shape
    row = lambda b, i: (b, i, 0)
    const = lambda b, i: (0, 0)
    return pl.pallas_call(
        functools.partial(_outproj_kernel, final_norm=final_norm),
        grid=(batch, seq // OUT_ROW_TILE),
        in_specs=[
            pl.BlockSpec((None, OUT_ROW_TILE, D_RNN), row),
            pl.BlockSpec((None, OUT_ROW_TILE, D_ATTN), row),
            pl.BlockSpec((None, OUT_ROW_TILE, D_MODEL), row),
            pl.BlockSpec((None, 3, D_MODEL), lambda b, i: (b, 0, 0)),
            pl.BlockSpec((D_RNN, D_MODEL), const),
            pl.BlockSpec((D_ATTN, D_MODEL), const),
            pl.BlockSpec((1, D_MODEL), const),
        ],
        out_specs=pl.BlockSpec((None, OUT_ROW_TILE, D_MODEL), row),
        out_shape=jax.ShapeDtypeStruct((batch, seq, D_MODEL), jnp.float32),
        compiler_params=pltpu.CompilerParams(
            dimension_semantics=("parallel", "parallel"), vmem_limit_bytes=VMEM_LIMIT),
        name="out_proj",
    )(yr, ya, x, mod_l, w_r, w_a, final_g.reshape(1, D_MODEL))


def _pad_w_in(w):
    return jnp.pad(w.astype(jnp.bfloat16), ((0, 0), (0, 0), (0, Z_COLS - D_IN)))


def _block_diag(w):
    n, c, d = w.shape
    eye = jnp.eye(n, dtype=w.dtype)
    return (eye[:, None, :, None] * w[:, :, None, :]).reshape(n * c, n * d)


def kernel(x, c, norm_g, ada_w, ada_b, w_in, conv_w, conv_b, lru_wx, lru_bx, lru_wa, lru_ba, lru_a, w_out, final_g):
    depth = w_in.shape[0]
    mod = _modulation(c, ada_w, ada_b)
    idx = jnp.arange(K_TILE)
    tri = (idx[None, :] < idx[:, None]).astype(jnp.bfloat16)
    w_pad = _pad_w_in(w_in)
    for l in range(depth):
        w_gates = jnp.concatenate([_block_diag(lru_wx[l]), _block_diag(lru_wa[l])], axis=-1).astype(jnp.bfloat16)
        b_gates = jnp.concatenate([lru_bx[l], lru_ba[l]]).reshape(1, 2 * D_RNN)
        w_o = w_out[l].astype(jnp.bfloat16)
        y_r, qt, ga, kv, wi, vt = _in_projection(x, mod[l], norm_g[l], w_pad[l], conv_w[l], conv_b[l],
                                                 w_gates, b_gates, lru_a[l])
        y_a = _dsa_attention(qt, kv, vt, wi, ga, tri)
        x = _out_projection(y_r, y_a, x, mod[l], w_o[:D_RNN], w_o[D_RNN:], final_g, l == depth - 1)
    return x
```

```python
import functools

import jax
import jax.numpy as jnp
from jax import lax
from jax.experimental import pallas as pl
from jax.experimental.pallas import tpu as pltpu

D_MODEL = 1024
D_RNN = 512
N_RNN_BLOCKS = 8
RNN_BLOCK = D_RNN // N_RNN_BLOCKS
CONV_WIDTH = 4
LRU_C = 8.0
N_HEADS = 8
HEAD_DIM = 64
D_ATTN = N_HEADS * HEAD_DIM
N_IDX_HEADS = 8
IDX_DIM = 64
MAX_TOPK = 256
EPS = 1e-6
LOG2_E = 1.4426950408889634

Z_RNN = 0
Z_Q = 1024
Z_KV = 1536
Z_GA = 1664
Z_QI = 2176
Z_KI = 2688
Z_COLS = 2816
D_IN = 2760
WI_LANE = 64

ROW_TILE = 512
OUT_ROW_TILE = 1024
LRU_STEPS = ROW_TILE // 8
Q_TILE = 256
K_TILE = 512
VT_ROWS = 80
MASKED = -1e30
COARSE_PROBES = 9
PROBES_PER_ROUND = 7
PLAIN_BISECT_ROUNDS = 5
PARTIAL_ROWS = 32
SWEEP_TILE = 256
TAIL_KEYS = K_TILE - Q_TILE
assert K_TILE == 2 * Q_TILE
VMEM_LIMIT = 48 * 1024 * 1024


def _sigmoid(x):
    return 0.5 * jnp.tanh(0.5 * x) + 0.5


def _silu(x):
    return x * _sigmoid(x)


def _rms(x, g):
    return x * lax.rsqrt(jnp.mean(x * x, axis=-1, keepdims=True) + EPS) * g


def _reduce_keys(x, op):
    keys, queries = x.shape
    part = op(x.reshape(keys // PARTIAL_ROWS, PARTIAL_ROWS, queries), axis=0)
    return op(part, axis=0, keepdims=True)


def _mod_kernel(c_ref, w_ref, b_ref, o_ref):
    c_act = _silu(c_ref[...])
    o_ref[...] = jnp.dot(c_act, w_ref[...], precision=lax.Precision.HIGHEST,
                         preferred_element_type=jnp.float32) + b_ref[...]


def _modulation(c, ada_w, ada_b):
    depth = ada_w.shape[0]
    batch = c.shape[0]
    out = pl.pallas_call(
        _mod_kernel,
        grid=(depth, 3),
        in_specs=[
            pl.BlockSpec((batch, D_MODEL), lambda l, j: (0, 0)),
            pl.BlockSpec((None, D_MODEL, D_MODEL), lambda l, j: (l, 0, j)),
            pl.BlockSpec((None, None, 1, D_MODEL), lambda l, j: (l, j, 0, 0)),
        ],
        out_specs=pl.BlockSpec((None, None, batch, D_MODEL), lambda l, j: (l, j, 0, 0)),
        out_shape=jax.ShapeDtypeStruct((depth, 3, batch, D_MODEL), jnp.float32),
        compiler_params=pltpu.CompilerParams(vmem_limit_bytes=VMEM_LIMIT),
        name="adaln_mod",
    )(c, ada_w, ada_b.reshape(depth, 3, 1, D_MODEL))
    return out.transpose(0, 2, 1, 3)


def _inproj_kernel(x_ref, mod_ref, g_ref, w_ref, cw_ref, cb_ref, wg_ref, bg_ref, ap_ref,
                   yr_ref, qt_ref, ga_ref, kv_ref, wi_ref, vt_ref, perm_ref, tail_ref, h_ref):
    @pl.when(pl.program_id(1) == 0)
    def _():
        tail_ref[...] = jnp.zeros((8, D_RNN), jnp.float32)
        h_ref[...] = jnp.zeros((1, D_RNN), jnp.float32)

    x = x_ref[...]
    shift = mod_ref[0:1, :]
    scale = mod_ref[1:2, :]
    h = _rms(x, g_ref[...]) * (1.0 + scale) + shift
    h = h.astype(jnp.bfloat16)
    z_rnn = jnp.dot(h, w_ref[:, Z_RNN:Z_Q], preferred_element_type=jnp.float32)
    y_r = _rg_lru_tile(z_rnn[:, 0:D_RNN], z_rnn[:, D_RNN:2 * D_RNN],
                       cw_ref, cb_ref, wg_ref, bg_ref, ap_ref, perm_ref, tail_ref, h_ref)
    yr_ref[...] = y_r.astype(yr_ref.dtype)
    z = jnp.dot(h, w_ref[:, Z_Q:Z_COLS], preferred_element_type=jnp.float32)
    col = lambda start, width: z[:, start - Z_Q:start - Z_Q + width]
    for h in range(N_HEADS):
        q_h = col(Z_Q + h * HEAD_DIM, HEAD_DIM) * (HEAD_DIM ** -0.5 * LOG2_E)
        _store_query_tiles(qt_ref, h, q_h.T.astype(jnp.bfloat16))
    for h in range(N_IDX_HEADS):
        _store_query_tiles(qt_ref, N_HEADS + h, col(Z_QI + h * IDX_DIM, IDX_DIM).T.astype(jnp.bfloat16))
    ga_ref[...] = col(Z_GA, D_ATTN)
    kv_ref[...] = jnp.concatenate([col(Z_KV, 128), col(Z_KI, 128)], axis=-1).astype(jnp.bfloat16)
    wi_ref[...] = col(Z_KI, 128)
    pad_rows = lax.broadcasted_iota(jnp.int32, (VT_ROWS - HEAD_DIM, K_TILE), 0)
    for j in range(ROW_TILE // K_TILE):
        v_t = col(Z_KV + HEAD_DIM, HEAD_DIM)[j * K_TILE:(j + 1) * K_TILE, :].T
        vt_ref[j, 0:HEAD_DIM, :] = v_t.astype(jnp.bfloat16)
        vt_ref[j, HEAD_DIM:VT_ROWS, :] = jnp.where(pad_rows == 0, 1.0, 0.0).astype(jnp.bfloat16)


def _store_query_tiles(qt_ref, head, q_t):
    for j in range(ROW_TILE // Q_TILE):
        qt_ref[j, head] = q_t[:, j * Q_TILE:(j + 1) * Q_TILE]


def _in_projection(x, mod_l, norm_g, w_pad, conv_w, conv_b, w_gates, b_gates, a_param):
    batch, seq, _ = x.shape
    row = lambda b, i: (b, i, 0)
    const = lambda b, i: (0, 0)
    return pl.pallas_call(
        _inproj_kernel,
        grid=(batch, seq // ROW_TILE),
        in_specs=[
            pl.BlockSpec((None, ROW_TILE, D_MODEL), row),
            pl.BlockSpec((None, 3, D_MODEL), lambda b, i: (b, 0, 0)),
            pl.BlockSpec((1, D_MODEL), const),
            pl.BlockSpec((D_MODEL, Z_COLS), const),
            pl.BlockSpec((CONV_WIDTH, D_RNN), const),
            pl.BlockSpec((1, D_RNN), const),
            pl.BlockSpec((D_RNN, 2 * D_RNN), const),
            pl.BlockSpec((1, 2 * D_RNN), const),
            pl.BlockSpec((1, D_RNN), const),
        ],
        out_specs=[
            pl.BlockSpec((None, ROW_TILE, D_RNN), row),
            pl.BlockSpec((None, ROW_TILE // Q_TILE, N_HEADS + N_IDX_HEADS, HEAD_DIM, Q_TILE),
                         lambda b, i: (b, i, 0, 0, 0)),
            pl.BlockSpec((None, ROW_TILE, 512), row),
            pl.BlockSpec((None, ROW_TILE, 256), row),
            pl.BlockSpec((None, ROW_TILE, 128), row),
            pl.BlockSpec((None, ROW_TILE // K_TILE, VT_ROWS, K_TILE), lambda b, i: (b, i, 0, 0)),
        ],
        out_shape=[
            jax.ShapeDtypeStruct((batch, seq, D_RNN), jnp.bfloat16),
            jax.ShapeDtypeStruct((batch, seq // Q_TILE, N_HEADS + N_IDX_HEADS, HEAD_DIM, Q_TILE), jnp.bfloat16),
            jax.ShapeDtypeStruct((batch, seq, 512), jnp.float32),
            jax.ShapeDtypeStruct((batch, seq, 256), jnp.bfloat16),
            jax.ShapeDtypeStruct((batch, seq, 128), jnp.float32),
            jax.ShapeDtypeStruct((batch, seq // K_TILE, VT_ROWS, K_TILE), jnp.bfloat16),
        ],
        scratch_shapes=[
            pltpu.VMEM((D_RNN // 128, ROW_TILE, 128), jnp.float32),
            pltpu.VMEM((8, D_RNN), jnp.float32),
            pltpu.VMEM((1, D_RNN), jnp.float32),
        ],
        compiler_params=pltpu.CompilerParams(
            dimension_semantics=("parallel", "arbitrary"), vmem_limit_bytes=VMEM_LIMIT),
        name="in_proj_rg_lru",
    )(x, mod_l, norm_g.reshape(1, D_MODEL), w_pad, conv_w, conv_b.reshape(1, D_RNN),
      w_gates, b_gates, a_param.reshape(1, D_RNN))


def _rg_lru_tile(xr, gr, cw_ref, cb_ref, wg_ref, bg_ref, ap_ref, perm_ref, tail_ref, h_ref):
    ts, steps, slabs = ROW_TILE, LRU_STEPS, D_RNN // 128
    lanes = lambda k: slice(k * 128, (k + 1) * 128)

    for s in range(8):
        for k in range(slabs):
            perm_ref[k, pl.ds(s, steps, stride=8), :] = xr[s * steps:(s + 1) * steps, lanes(k)]
    x = jnp.concatenate([perm_ref[k] for k in range(slabs)], axis=-1).reshape(steps, 8, D_RNN)

    first = lax.broadcasted_iota(jnp.int32, (8, D_RNN), 0) == 0
    before = [jnp.where(first, tail_ref[8 - m:9 - m, :], pltpu.roll(x[steps - m], 1, 0))
              for m in range(CONV_WIDTH - 1, 0, -1)]
    x_ext = jnp.concatenate([jnp.stack(before), x], axis=0)
    xc = cb_ref[...] + cw_ref[CONV_WIDTH - 1:CONV_WIDTH, :] * x
    for k in range(CONV_WIDTH - 1):
        xc = xc + cw_ref[k:k + 1, :] * x_ext[k:k + steps]
    tail_ref[...] = xr[ts - 8:ts, :]

    gates = jnp.dot(xc.reshape(ts, D_RNN).astype(jnp.bfloat16), wg_ref[...],
                    preferred_element_type=jnp.float32) + bg_ref[...]
    gates = gates.reshape(steps, 8, 2 * D_RNN)
    gate_x = _sigmoid(gates[:, :, 0:D_RNN])
    gate_a = _sigmoid(gates[:, :, D_RNN:2 * D_RNN])
    neg_ap = -ap_ref[...]
    softplus = jnp.maximum(neg_ap, 0.0) + jnp.log(1.0 + jnp.exp(-jnp.abs(neg_ap)))
    log_a = (-LRU_C) * gate_a * softplus
    a = jnp.exp(log_a)
    gap = 1.0 - a * a
    mult = jnp.where(gap > 0.0, gap * lax.rsqrt(gap), 0.0)
    u = mult * gate_x * xc

    h_run, a_run = u[0], a[0]
    h_loc, a_cum = [h_run], [a_run]
    for j in range(1, steps):
        h_run = a[j] * h_run + u[j]
        a_run = a[j] * a_run
        h_loc.append(h_run)
        a_cum.append(a_run)
    state = h_ref[...]
    entering = []
    for s in range(8):
        entering.append(state)
        state = h_run[s:s + 1, :] + a_run[s:s + 1, :] * state
    h_ref[...] = state
    entering = jnp.concatenate(entering, axis=0)
    h = jnp.stack([h_loc[j] + a_cum[j] * entering for j in range(steps)]).reshape(ts, D_RNN)

    for k in range(slabs):
        perm_ref[k] = h[:, lanes(k)]
    h = jnp.concatenate(
        [jnp.concatenate([perm_ref[k, pl.ds(s, steps, stride=8), :] for k in range(slabs)], axis=-1)
         for s in range(8)], axis=0)
    return h * _silu(gr)


def _attn_kernel(qt_ref, kv_ref, vt_ref, wi_ref, ga_ref, tri_ref, y_ref,
                 sc_ref, sc16_ref, bias_ref, eqb_ref, s_ref, m_ref, mnew_ref, acc_ref, out_ref):
    tq, tk = Q_TILE, K_TILE
    topk = float(MAX_TOPK)
    qb = pl.program_id(1)
    n_keys = (qb + 1) * tq
    n_full = n_keys // tk
    has_tail = n_keys - n_full * tk > 0

    w_t = wi_ref[...].T[WI_LANE:WI_LANE + N_IDX_HEADS, :] * ((IDX_DIM ** -0.5) * (N_IDX_HEADS ** -0.5))

    def score_chunk(c, carry, size, diagonal):
        rmax, rmin = carry
        k0 = pl.multiple_of(c * tk, tk)
        ki = kv_ref[pl.ds(k0, size), 128:128 + IDX_DIM]
        score = jnp.zeros((size, tq), jnp.float32)
        for h in range(N_IDX_HEADS):
            logits = jnp.dot(ki, qt_ref[N_HEADS + h], preferred_element_type=jnp.float32)
            score = score + jnp.maximum(logits, 0.0) * w_t[h:h + 1, :]
        if diagonal:
            key_pos = k0 + lax.broadcasted_iota(jnp.int32, (size, tq), 0)
            causal = key_pos <= qb * tq + lax.broadcasted_iota(jnp.int32, (size, tq), 1)
            low, high = jnp.where(causal, score, -jnp.inf), jnp.where(causal, score, jnp.inf)
        else:
            low = high = score
        sc_ref[pl.ds(k0, size), :] = low
        sc16_ref[pl.ds(k0, size), :] = low.astype(jnp.bfloat16)
        return jnp.maximum(rmax, _reduce_keys(low, jnp.max)), jnp.minimum(rmin, _reduce_keys(high, jnp.min))

    n_before = jnp.where(has_tail, n_full, n_full - 1)
    extremes = lax.fori_loop(
        0, n_before, functools.partial(score_chunk, size=tk, diagonal=False),
        (jnp.full((1, tq), -jnp.inf, jnp.float32), jnp.full((1, tq), jnp.inf, jnp.float32)))
    rmax, rmin = lax.cond(has_tail,
                          lambda e: score_chunk(n_full, e, TAIL_KEYS, True),
                          lambda e: score_chunk(n_full - 1, e, tk, True), extremes)

    n_sweep_tiles = ((qb + 1) * tq) // SWEEP_TILE

    def sweep(mid, want):
        kinds = {"gt": (jnp.sum, 0.0), "ge": (jnp.sum, 0.0), "above": (jnp.min, jnp.inf), "below": (jnp.max, -jnp.inf)}

        def body(c, carry):
            k0 = pl.multiple_of(c * SWEEP_TILE, SWEEP_TILE)
            s = sc_ref[pl.ds(k0, SWEEP_TILE), :].reshape(SWEEP_TILE // PARTIAL_ROWS, PARTIAL_ROWS, tq)
            gt = s > mid
            terms = {"gt": lambda: jnp.where(gt, 1.0, 0.0), "ge": lambda: jnp.where(s >= mid, 1.0, 0.0),
                     "above": lambda: jnp.where(gt, s, jnp.inf), "below": lambda: jnp.where(gt, -jnp.inf, s)}
            out = []
            for name, acc in zip(want, carry):
                op = kinds[name][0]
                part = op(terms[name](), axis=0)
                out.append(acc + part if op is jnp.sum else
                           (jnp.minimum(acc, part) if op is jnp.min else jnp.maximum(acc, part)))
            return tuple(out)

        init = tuple(jnp.full((PARTIAL_ROWS, tq), kinds[name][1], jnp.float32) for name in want)
        res = lax.fori_loop(0, n_sweep_tiles, body, init)
        return [kinds[name][0](r, axis=0, keepdims=True) for name, r in zip(want, res)]

    def update(state, mid, cnt, tie, new_lo, new_hi):
        lo, hi, chi, act = state
        on = act > 0.5
        fin = jnp.logical_and(on, jnp.logical_or(cnt == topk, tie))
        go_on = jnp.logical_and(on, jnp.logical_not(fin))
        up = jnp.logical_and(go_on, cnt > topk)
        dn = jnp.logical_and(go_on, cnt < topk)
        lo = jnp.where(fin, mid, jnp.where(up, new_lo, lo))
        hi = jnp.where(fin, mid, jnp.where(dn, new_hi, hi))
        chi = jnp.where(jnp.logical_or(fin, dn), cnt, chi)
        act = jnp.where(jnp.logical_and(go_on, lo < hi), 1.0, 0.0)
        return lo, hi, chi, act

    def midpoint(state):
        return 0.5 * state[0] + 0.5 * state[1]

    n_valid = (qb * tq + 1 + lax.broadcasted_iota(jnp.int32, (1, tq), 1)).astype(jnp.float32)
    few = n_valid <= topk
    state = (rmin,
             jnp.where(few, -jnp.inf, rmax),
             jnp.where(few, topk, 0.0),
             jnp.where(jnp.logical_or(few, rmin >= rmax), 0.0, 1.0))

    def coarse_count(t16, also_ge=False):
        one, zero = jnp.ones((), jnp.bfloat16), jnp.zeros((), jnp.bfloat16)

        def tile_sum(hits):
            parts = [hits[g] for g in range(SWEEP_TILE // PARTIAL_ROWS)]
            while len(parts) > 1:
                parts = [parts[i] + parts[i + 1] for i in range(0, len(parts), 2)]
            return parts[0]

        def body(c, accs):
            k0 = pl.multiple_of(c * SWEEP_TILE, SWEEP_TILE)
            s = sc16_ref[pl.ds(k0, SWEEP_TILE), :].reshape(SWEEP_TILE // PARTIAL_ROWS, PARTIAL_ROWS, tq)
            out = [accs[0] + tile_sum(jnp.where(s > t16, one, zero))]
            if also_ge:
                out.append(accs[1] + tile_sum(jnp.where(s >= t16, one, zero)))
            return tuple(out)

        init = (jnp.zeros((PARTIAL_ROWS, tq), jnp.bfloat16),) * (2 if also_ge else 1)
        accs = lax.fori_loop(0, n_sweep_tiles, body, init)
        return [jnp.sum(a.astype(jnp.float32), axis=0, keepdims=True) for a in accs]

    lo, hi, chi, act = state
    on = act > 0.5
    pos, nonneg = coarse_count(jnp.zeros((1, tq), jnp.bfloat16), also_ge=True)
    at_zero = jnp.logical_and(on, jnp.logical_and(pos <= topk, nonneg >= topk))
    lo = jnp.where(at_zero, 0.0, jnp.where(jnp.logical_and(on, pos > topk), jnp.maximum(lo, 0.0), lo))
    hi = jnp.where(at_zero, 0.0, jnp.where(jnp.logical_and(on, nonneg < topk), jnp.minimum(hi, 0.0), hi))
    chi = jnp.where(at_zero, pos, chi)
    act = jnp.where(at_zero, 0.0, act)

    for _ in range(COARSE_PROBES):
        t16 = (0.5 * lo + 0.5 * hi).astype(jnp.bfloat16)
        t = t16.astype(jnp.float32)
        (cnt16,) = coarse_count(t16)
        on = act > 0.5
        lo = jnp.where(jnp.logical_and(on, cnt16 >= topk), jnp.maximum(lo, t), lo)
        hi = jnp.where(jnp.logical_and(on, cnt16 < topk), jnp.minimum(hi, t + jnp.abs(t) * (2.0 ** -7) + 1e-30), hi)
    (cnt,) = sweep(hi, ("gt",))
    state = update((lo, hi, chi, act), hi, cnt, False, hi, hi)

    def pending(state):
        return jnp.where(jnp.logical_and(state[3] > 0.5, topk - state[2] != 1.0), 1.0, 0.0)

    def plain_cond(st):
        return jnp.logical_and(st[1] > 0.5, st[0] < PLAIN_BISECT_ROUNDS)

    def plain_body(st):
        state = st[2:]
        for _ in range(PROBES_PER_ROUND):
            mid = midpoint(state)
            (cnt,) = sweep(mid, ("gt",))
            state = update(state, mid, cnt, False, mid, mid)
        return (st[0] + 1, jnp.max(pending(state))) + state

    st = lax.while_loop(plain_cond, plain_body, (jnp.int32(0), jnp.max(pending(state))) + state)
    lo, hi, chi, act = st[2:]
    def two_largest_below(bound):
        groups = SWEEP_TILE // PARTIAL_ROWS

        def merge(a1, a2, b1, b2):
            return jnp.maximum(a1, b1), jnp.maximum(jnp.minimum(a1, b1), jnp.maximum(a2, b2))

        def body(c, carry):
            m1, m2 = carry
            k0 = pl.multiple_of(c * SWEEP_TILE, SWEEP_TILE)
            s = sc_ref[pl.ds(k0, SWEEP_TILE), :].reshape(groups, PARTIAL_ROWS, tq)
            x = jnp.where(s > bound, -jnp.inf, s)
            for g in range(groups):
                m1, m2 = jnp.maximum(m1, x[g]), jnp.maximum(m2, jnp.minimum(m1, x[g]))
            return m1, m2

        low = jnp.full((PARTIAL_ROWS, tq), -jnp.inf, jnp.float32)
        m1, m2 = lax.fori_loop(0, n_sweep_tiles, body, (low, low))
        rows = PARTIAL_ROWS
        while rows > 1:
            rows //= 2
            m1, m2 = merge(m1[:rows], m2[:rows], m1[rows:], m2[rows:])
        return m1, m2

    largest, second = two_largest_below(hi)
    one_short = jnp.logical_and(act > 0.5, topk - chi == 1.0)
    separated = jnp.logical_and(one_short, second < largest)
    new_thr = jnp.where(separated, second, largest)
    state = (jnp.where(one_short, new_thr, lo), jnp.where(one_short, new_thr, hi),
             jnp.where(separated, topk, chi), jnp.where(one_short, 0.0, act))

    def snap_cond(st):
        return jnp.logical_and(st[1] > 0.5, st[0] < 4096)

    def snap_body(st):
        state = st[2:]
        mid = midpoint(state)
        mid = jnp.where(mid >= state[1], state[0], mid)
        cnt, above, below = sweep(mid, ("gt", "above", "below"))
        state = update(state, mid, cnt, False, above, below)
        return (st[0] + 1, jnp.max(state[3])) + state

    st = lax.while_loop(snap_cond, snap_body, (jnp.int32(0), st[1]) + state)
    thr = st[3]
    need = topk - st[4]
    any_tie = jnp.max(need) > 0.5

    m_ref[...] = jnp.full(m_ref.shape, MASKED, jnp.float32)
    acc_ref[...] = jnp.zeros(acc_ref.shape, jnp.float32)
    eqb_ref[...] = jnp.zeros(eqb_ref.shape, jnp.float32)

    def attend_chunk(c, size):
        k0 = pl.multiple_of(c * tk, tk)
        s_idx = sc_ref[pl.ds(k0, size), :]
        bias_ref[0:size, :] = jnp.where(s_idx > thr, 0.0, MASKED)

        @pl.when(any_tie)
        def _():
            eq = s_idx == thr
            eq_f = jnp.where(eq, 1.0, 0.0)
            rank = jnp.dot(tri_ref[0:size, 0:size], eq_f.astype(jnp.bfloat16),
                           preferred_element_type=jnp.float32) + eqb_ref[...]
            tie_bias = jnp.where(jnp.logical_and(eq, rank < need), 0.0, MASKED)
            bias_ref[0:size, :] = jnp.where(s_idx > thr, 0.0, tie_bias)
            eqb_ref[...] = eqb_ref[...] + _reduce_keys(eq_f, jnp.sum)

        kc = kv_ref[pl.ds(k0, size), 0:HEAD_DIM]
        vt = vt_ref[c, :, 0:size]

        def logits(h):
            s = jnp.dot(kc, qt_ref[h], preferred_element_type=jnp.float32) + bias_ref[0:size, :]
            s_ref[h, 0:size, :] = s
            mnew_ref[h] = jnp.maximum(m_ref[h], _reduce_keys(s, jnp.max))

        def weigh(h):
            m_new = mnew_ref[h]
            alpha = jnp.exp2(m_ref[h] - m_new)
            p = jnp.exp2(s_ref[h, 0:size, :] - m_new)
            acc_ref[h] = alpha * acc_ref[h] + jnp.dot(
                vt, p.astype(jnp.bfloat16), preferred_element_type=jnp.float32)
            m_ref[h] = m_new

        for h in range(N_HEADS):
            logits(h)
        for h in range(N_HEADS):
            weigh(h)

    def attend_full(c, _):
        attend_chunk(c, tk)
        return 0

    lax.fori_loop(0, n_full, attend_full, 0)

    @pl.when(has_tail)
    def _():
        attend_chunk(n_full, TAIL_KEYS)

    for h in range(N_HEADS):
        rows = slice(h * HEAD_DIM, (h + 1) * HEAD_DIM)
        out_ref[rows, :] = acc_ref[h, 0:HEAD_DIM, :] / acc_ref[h, HEAD_DIM:HEAD_DIM + 1, :]
    y = out_ref[...].T * _silu(ga_ref[...])
    y_ref[...] = y.astype(y_ref.dtype)


def _dsa_attention(qt, kv, vt, wi, ga, tri):
    batch, seq, _ = kv.shape
    assert seq // PARTIAL_ROWS <= 256, "bf16 hit counters are exact only up to 256 per slot"
    row = lambda b, i: (b, i, 0)
    return pl.pallas_call(
        _attn_kernel,
        grid=(batch, seq // Q_TILE),
        in_specs=[
            pl.BlockSpec((None, None, N_HEADS + N_IDX_HEADS, HEAD_DIM, Q_TILE), lambda b, i: (b, i, 0, 0, 0)),
            pl.BlockSpec((None, seq, 256), lambda b, i: (b, 0, 0)),
            pl.BlockSpec((None, seq // K_TILE, VT_ROWS, K_TILE), lambda b, i: (b, 0, 0, 0)),
            pl.BlockSpec((None, Q_TILE, 128), row),
            pl.BlockSpec((None, Q_TILE, D_ATTN), row),
            pl.BlockSpec((K_TILE, K_TILE), lambda b, i: (0, 0)),
        ],
        out_specs=pl.BlockSpec((None, Q_TILE, D_ATTN), row),
        out_shape=jax.ShapeDtypeStruct((batch, seq, D_ATTN), jnp.bfloat16),
        scratch_shapes=[
            pltpu.VMEM((seq, Q_TILE), jnp.float32),
            pltpu.VMEM((seq, Q_TILE), jnp.bfloat16),
            pltpu.VMEM((K_TILE, Q_TILE), jnp.float32),
            pltpu.VMEM((1, Q_TILE), jnp.float32),
            pltpu.VMEM((N_HEADS, K_TILE, Q_TILE), jnp.float32),
            pltpu.VMEM((N_HEADS, 1, Q_TILE), jnp.float32),
            pltpu.VMEM((N_HEADS, 1, Q_TILE), jnp.float32),
            pltpu.VMEM((N_HEADS, VT_ROWS, Q_TILE), jnp.float32),
            pltpu.VMEM((D_ATTN, Q_TILE), jnp.float32),
        ],
        compiler_params=pltpu.CompilerParams(
            dimension_semantics=("parallel", "parallel"), vmem_limit_bytes=VMEM_LIMIT),
        name="dsa_attention",
    )(qt, kv, vt, wi, ga, tri)


def _outproj_kernel(yr_ref, ya_ref, x_ref, mod_ref, wr_ref, wa_ref, g_ref, o_ref, *, final_norm):
    y = jnp.dot(yr_ref[...], wr_ref[...], preferred_element_type=jnp.float32)
    y = y + jnp.dot(ya_ref[...], wa_ref[...], preferred_element_type=jnp.float32)
    x_new = x_ref[...] + mod_ref[2:3, :] * y
    if final_norm:
        x_new = _rms(x_new, g_ref[...])
    o_ref[...] = x_new


def _out_projection(yr, ya, x, mod_l, w_r, w_a, final_g, final_norm):
    batch, seq, _ = x.shape
    row = lambda b, i: (b, i, 0)
    const = lambda b, i: (0, 0)
    return pl.pallas_call(
        functools.partial(_outproj_kernel, final_norm=final_norm),
        grid=(batch, seq // OUT_ROW_TILE),
        in_specs=[
            pl.BlockSpec((None, OUT_ROW_TILE, D_RNN), row),
            pl.BlockSpec((None, OUT_ROW_TILE, D_ATTN), row),
            pl.BlockSpec((None, OUT_ROW_TILE, D_MODEL), row),
            pl.BlockSpec((None, 3, D_MODEL), lambda b, i: (b, 0, 0)),
            pl.BlockSpec((D_RNN, D_MODEL), const),
            pl.BlockSpec((D_ATTN, D_MODEL), const),
            pl.BlockSpec((1, D_MODEL), const),
        ],
        out_specs=pl.BlockSpec((None, OUT_ROW_TILE, D_MODEL), row),
        out_shape=jax.ShapeDtypeStruct((batch, seq, D_MODEL), jnp.float32),
        compiler_params=pltpu.CompilerParams(
            dimension_semantics=("parallel", "parallel"), vmem_limit_bytes=VMEM_LIMIT),
        name="out_proj",
    )(yr, ya, x, mod_l, w_r, w_a, final_g.reshape(1, D_MODEL))


def _pad_w_in(w):
    return jnp.pad(w.astype(jnp.bfloat16), ((0, 0), (0, 0), (0, Z_COLS - D_IN)))


def _block_diag(w):
    n, c, d = w.shape
    eye = jnp.eye(n, dtype=w.dtype)
    return (eye[:, None, :, None] * w[:, :, None, :]).reshape(n * c, n * d)


def kernel(x, c, norm_g, ada_w, ada_b, w_in, conv_w, conv_b, lru_wx, lru_bx, lru_wa, lru_ba, lru_a, w_out, final_g):
    depth = w_in.shape[0]
    mod = _modulation(c, ada_w, ada_b)
    idx = jnp.arange(K_TILE)
    tri = (idx[None, :] < idx[:, None]).astype(jnp.bfloat16)
    w_pad = _pad_w_in(w_in)
    for l in range(depth):
        w_gates = jnp.concatenate([_block_diag(lru_wx[l]), _block_diag(lru_wa[l])], axis=-1).astype(jnp.bfloat16)
        b_gates = jnp.concatenate([lru_bx[l], lru_ba[l]]).reshape(1, 2 * D_RNN)
        w_o = w_out[l].astype(jnp.bfloat16)
        y_r, qt, ga, kv, wi, vt = _in_projection(x, mod[l], norm_g[l], w_pad[l], conv_w[l], conv_b[l],
                                                 w_gates, b_gates, lru_a[l])
        y_a = _dsa_attention(qt, kv, vt, wi, ga, tri)
        x = _out_projection(y_r, y_a, x, mod[l], w_o[:D_RNN], w_o[D_RNN:], final_g, l == depth - 1)
    return x
```

```python
import functools

import jax
import jax.numpy as jnp
from jax import lax
from jax.experimental import pallas as pl
from jax.experimental.pallas import tpu as pltpu

D_MODEL = 1024
D_RNN = 512
N_RNN_BLOCKS = 8
RNN_BLOCK = D_RNN // N_RNN_BLOCKS
CONV_WIDTH = 4
LRU_C = 8.0
N_HEADS = 8
HEAD_DIM = 64
D_ATTN = N_HEADS * HEAD_DIM
N_IDX_HEADS = 8
IDX_DIM = 64
MAX_TOPK = 256
EPS = 1e-6
LOG2_E = 1.4426950408889634

Z_RNN = 0
Z_Q = 1024
Z_KV = 1536
Z_GA = 1664
Z_QI = 2176
Z_KI = 2688
Z_COLS = 2816
D_IN = 2760
WI_LANE = 64

ROW_TILE = 512
OUT_ROW_TILE = 1024
LRU_STEPS = ROW_TILE // 8
Q_TILE = 256
K_TILE = 512
VT_ROWS = 80
MASKED = -1e30
COARSE_PROBES = 9
PROBES_PER_ROUND = 7
PLAIN_BISECT_ROUNDS = 5
PARTIAL_ROWS = 32
SWEEP_TILE = 256
TAIL_KEYS = K_TILE - Q_TILE
assert K_TILE == 2 * Q_TILE
VMEM_LIMIT = 48 * 1024 * 1024


def _sigmoid(x):
    return 0.5 * jnp.tanh(0.5 * x) + 0.5


def _silu(x):
    return x * _sigmoid(x)


def _rms(x, g):
    return x * lax.rsqrt(jnp.mean(x * x, axis=-1, keepdims=True) + EPS) * g


def _reduce_keys(x, op):
    keys, queries = x.shape
    part = op(x.reshape(keys // PARTIAL_ROWS, PARTIAL_ROWS, queries), axis=0)
    return op(part, axis=0, keepdims=True)


def _mod_kernel(c_ref, w_ref, b_ref, o_ref):
    c_act = _silu(c_ref[...])
    o_ref[...] = jnp.dot(c_act, w_ref[...], precision=lax.Precision.HIGHEST,
                         preferred_element_type=jnp.float32) + b_ref[...]


def _modulation(c, ada_w, ada_b):
    depth = ada_w.shape[0]
    batch = c.shape[0]
    out = pl.pallas_call(
        _mod_kernel,
        grid=(depth, 3),
        in_specs=[
            pl.BlockSpec((batch, D_MODEL), lambda l, j: (0, 0)),
            pl.BlockSpec((None, D_MODEL, D_MODEL), lambda l, j: (l, 0, j)),
            pl.BlockSpec((None, None, 1, D_MODEL), lambda l, j: (l, j, 0, 0)),
        ],
        out_specs=pl.BlockSpec((None, None, batch, D_MODEL), lambda l, j: (l, j, 0, 0)),
        out_shape=jax.ShapeDtypeStruct((depth, 3, batch, D_MODEL), jnp.float32),
        compiler_params=pltpu.CompilerParams(vmem_limit_bytes=VMEM_LIMIT),
        name="adaln_mod",
    )(c, ada_w, ada_b.reshape(depth, 3, 1, D_MODEL))
    return out.transpose(0, 2, 1, 3)


def _inproj_kernel(x_ref, mod_ref, g_ref, w_ref, cw_ref, cb_ref, wg_ref, bg_ref, ap_ref,
                   yr_ref, qt_ref, ga_ref, kv_ref, wi_ref, vt_ref, perm_ref, tail_ref, h_ref):
    @pl.when(pl.program_id(1) == 0)
    def _():
        tail_ref[...] = jnp.zeros((8, D_RNN), jnp.float32)
        h_ref[...] = jnp.zeros((1, D_RNN), jnp.float32)

    x = x_ref[...]
    shift = mod_ref[0:1, :]
    scale = mod_ref[1:2, :]
    h = _rms(x, g_ref[...]) * (1.0 + scale) + shift
    h = h.astype(jnp.bfloat16)
    z_rnn = jnp.dot(h, w_ref[:, Z_RNN:Z_Q], preferred_element_type=jnp.float32)
    y_r = _rg_lru_tile(z_rnn[:, 0:D_RNN], z_rnn[:, D_RNN:2 * D_RNN],
                       cw_ref, cb_ref, wg_ref, bg_ref, ap_ref, perm_ref, tail_ref, h_ref)
    yr_ref[...] = y_r.astype(yr_ref.dtype)
    z = jnp.dot(h, w_ref[:, Z_Q:Z_COLS], preferred_element_type=jnp.float32)
    col = lambda start, width: z[:, start - Z_Q:start - Z_Q + width]
    for h in range(N_HEADS):
        q_h = col(Z_Q + h * HEAD_DIM, HEAD_DIM) * (HEAD_DIM ** -0.5 * LOG2_E)
        _store_query_tiles(qt_ref, h, q_h.T.astype(jnp.bfloat16))
    for h in range(N_IDX_HEADS):
        _store_query_tiles(qt_ref, N_HEADS + h, col(Z_QI + h * IDX_DIM, IDX_DIM).T.astype(jnp.bfloat16))
    ga_ref[...] = col(Z_GA, D_ATTN)
    kv_ref[...] = jnp.concatenate([col(Z_KV, 128), col(Z_KI, 128)], axis=-1).astype(jnp.bfloat16)
    wi_ref[...] = col(Z_KI, 128)
    pad_rows = lax.broadcasted_iota(jnp.int32, (VT_ROWS - HEAD_DIM, K_TILE), 0)
    for j in range(ROW_TILE // K_TILE):
        v_t = col(Z_KV + HEAD_DIM, HEAD_DIM)[j * K_TILE:(j + 1) * K_TILE, :].T
        vt_ref[j, 0:HEAD_DIM, :] = v_t.astype(jnp.bfloat16)
        vt_ref[j, HEAD_DIM:VT_ROWS, :] = jnp.where(pad_rows == 0, 1.0, 0.0).astype(jnp.bfloat16)


def _store_query_tiles(qt_ref, head, q_t):
    for j in range(ROW_TILE // Q_TILE):
        qt_ref[j, head] = q_t[:, j * Q_TILE:(j + 1) * Q_TILE]


def _in_projection(x, mod_l, norm_g, w_pad, conv_w, conv_b, w_gates, b_gates, a_param):
    batch, seq, _ = x.shape
    row = lambda b, i: (b, i, 0)
    const = lambda b, i: (0, 0)
    return pl.pallas_call(
        _inproj_kernel,
        grid=(batch, seq // ROW_TILE),
        in_specs=[
            pl.BlockSpec((None, ROW_TILE, D_MODEL), row),
            pl.BlockSpec((None, 3, D_MODEL), lambda b, i: (b, 0, 0)),
            pl.BlockSpec((1, D_MODEL), const),
            pl.BlockSpec((D_MODEL, Z_COLS), const),
            pl.BlockSpec((CONV_WIDTH, D_RNN), const),
            pl.BlockSpec((1, D_RNN), const),
            pl.BlockSpec((D_RNN, 2 * D_RNN), const),
            pl.BlockSpec((1, 2 * D_RNN), const),
            pl.BlockSpec((1, D_RNN), const),
        ],
        out_specs=[
            pl.BlockSpec((None, ROW_TILE, D_RNN), row),
            pl.BlockSpec((None, ROW_TILE // Q_TILE, N_HEADS + N_IDX_HEADS, HEAD_DIM, Q_TILE),
                         lambda b, i: (b, i, 0, 0, 0)),
            pl.BlockSpec((None, ROW_TILE, 512), row),
            pl.BlockSpec((None, ROW_TILE, 256), row),
            pl.BlockSpec((None, ROW_TILE, 128), row),
            pl.BlockSpec((None, ROW_TILE // K_TILE, VT_ROWS, K_TILE), lambda b, i: (b, i, 0, 0)),
        ],
        out_shape=[
            jax.ShapeDtypeStruct((batch, seq, D_RNN), jnp.bfloat16),
            jax.ShapeDtypeStruct((batch, seq // Q_TILE, N_HEADS + N_IDX_HEADS, HEAD_DIM, Q_TILE), jnp.bfloat16),
            jax.ShapeDtypeStruct((batch, seq, 512), jnp.float32),
            jax.ShapeDtypeStruct((batch, seq, 256), jnp.bfloat16),
            jax.ShapeDtypeStruct((batch, seq, 128), jnp.float32),
            jax.ShapeDtypeStruct((batch, seq // K_TILE, VT_ROWS, K_TILE), jnp.bfloat16),
        ],
        scratch_shapes=[
            pltpu.VMEM((D_RNN // 128, ROW_TILE, 128), jnp.float32),
            pltpu.VMEM((8, D_RNN), jnp.float32),
            pltpu.VMEM((1, D_RNN), jnp.float32),
        ],
        compiler_params=pltpu.CompilerParams(
            dimension_semantics=("parallel", "arbitrary"), vmem_limit_bytes=VMEM_LIMIT),
        name="in_proj_rg_lru",
    )(x, mod_l, norm_g.reshape(1, D_MODEL), w_pad, conv_w, conv_b.reshape(1, D_RNN),
      w_gates, b_gates, a_param.reshape(1, D_RNN))


def _rg_lru_tile(xr, gr, cw_ref, cb_ref, wg_ref, bg_ref, ap_ref, perm_ref, tail_ref, h_ref):
    ts, steps, slabs = ROW_TILE, LRU_STEPS, D_RNN // 128
    lanes = lambda k: slice(k * 128, (k + 1) * 128)

    for s in range(8):
        for k in range(slabs):
            perm_ref[k, pl.ds(s, steps, stride=8), :] = xr[s * steps:(s + 1) * steps, lanes(k)]
    x = jnp.concatenate([perm_ref[k] for k in range(slabs)], axis=-1).reshape(steps, 8, D_RNN)

    first = lax.broadcasted_iota(jnp.int32, (8, D_RNN), 0) == 0
    before = [jnp.where(first, tail_ref[8 - m:9 - m, :], pltpu.roll(x[steps - m], 1, 0))
              for m in range(CONV_WIDTH - 1, 0, -1)]
    x_ext = jnp.concatenate([jnp.stack(before), x], axis=0)
    xc = cb_ref[...] + cw_ref[CONV_WIDTH - 1:CONV_WIDTH, :] * x
    for k in range(CONV_WIDTH - 1):
        xc = xc + cw_ref[k:k + 1, :] * x_ext[k:k + steps]
    tail_ref[...] = xr[ts - 8:ts, :]

    gates = jnp.dot(xc.reshape(ts, D_RNN).astype(jnp.bfloat16), wg_ref[...],
                    preferred_element_type=jnp.float32) + bg_ref[...]
    gates = gates.reshape(steps, 8, 2 * D_RNN)
    gate_x = _sigmoid(gates[:, :, 0:D_RNN])
    gate_a = _sigmoid(gates[:, :, D_RNN:2 * D_RNN])
    neg_ap = -ap_ref[...]
    softplus = jnp.maximum(neg_ap, 0.0) + jnp.log(1.0 + jnp.exp(-jnp.abs(neg_ap)))
    log_a = (-LRU_C) * gate_a * softplus
    a = jnp.exp(log_a)
    gap = 1.0 - a * a
    mult = jnp.where(gap > 0.0, gap * lax.rsqrt(gap), 0.0)
    u = mult * gate_x * xc

    h_run, a_run = u[0], a[0]
    h_loc, a_cum = [h_run], [a_run]
    for j in range(1, steps):
        h_run = a[j] * h_run + u[j]
        a_run = a[j] * a_run
        h_loc.append(h_run)
        a_cum.append(a_run)
    state = h_ref[...]
    entering = []
    for s in range(8):
        entering.append(state)
        state = h_run[s:s + 1, :] + a_run[s:s + 1, :] * state
    h_ref[...] = state
    entering = jnp.concatenate(entering, axis=0)
    h = jnp.stack([h_loc[j] + a_cum[j] * entering for j in range(steps)]).reshape(ts, D_RNN)

    for k in range(slabs):
        perm_ref[k] = h[:, lanes(k)]
    h = jnp.concatenate(
        [jnp.concatenate([perm_ref[k, pl.ds(s, steps, stride=8), :] for k in range(slabs)], axis=-1)
         for s in range(8)], axis=0)
    return h * _silu(gr)


def _attn_kernel(qt_ref, kv_ref, vt_ref, wi_ref, ga_ref, tri_ref, y_ref,
                 sc_ref, sc16_ref, bias_ref, eqb_ref, s_ref, m_ref, mnew_ref, acc_ref, out_ref):
    tq, tk = Q_TILE, K_TILE
    topk = float(MAX_TOPK)
    qb = pl.program_id(1)
    n_keys = (qb + 1) * tq
    n_full = n_keys // tk
    has_tail = n_keys - n_full * tk > 0

    w_t = wi_ref[...].T[WI_LANE:WI_LANE + N_IDX_HEADS, :] * ((IDX_DIM ** -0.5) * (N_IDX_HEADS ** -0.5))

    def score_chunk(c, carry, size, diagonal):
        rmax, rmin = carry
        k0 = pl.multiple_of(c * tk, tk)
        ki = kv_ref[pl.ds(k0, size), 128:128 + IDX_DIM]
        score = jnp.zeros((size, tq), jnp.float32)
        for h in range(N_IDX_HEADS):
            logits = jnp.dot(ki, qt_ref[N_HEADS + h], preferred_element_type=jnp.float32)
            score = score + jnp.maximum(logits, 0.0) * w_t[h:h + 1, :]
        if diagonal:
            key_pos = k0 + lax.broadcasted_iota(jnp.int32, (size, tq), 0)
            causal = key_pos <= qb * tq + lax.broadcasted_iota(jnp.int32, (size, tq), 1)
            low, high = jnp.where(causal, score, -jnp.inf), jnp.where(causal, score, jnp.inf)
        else:
            low = high = score
        sc_ref[pl.ds(k0, size), :] = low
        sc16_ref[pl.ds(k0, size), :] = low.astype(jnp.bfloat16)
        return jnp.maximum(rmax, _reduce_keys(low, jnp.max)), jnp.minimum(rmin, _reduce_keys(high, jnp.min))

    n_before = jnp.where(has_tail, n_full, n_full - 1)
    extremes = lax.fori_loop(
        0, n_before, functools.partial(score_chunk, size=tk, diagonal=False),
        (jnp.full((1, tq), -jnp.inf, jnp.float32), jnp.full((1, tq), jnp.inf, jnp.float32)))

    def last_chunk(e):
        return lax.cond(has_tail,
                        lambda e: score_chunk(n_full, e, TAIL_KEYS, True),
                        lambda e: score_chunk(n_full - 1, e, tk, True), e)

    def no_scores(e):
        key_pos = lax.broadcasted_iota(jnp.int32, (tq, tq), 0)
        causal = key_pos <= lax.broadcasted_iota(jnp.int32, (tq, tq), 1)
        sc_ref[0:tq, :] = jnp.where(causal, 0.0, -jnp.inf)
        return jnp.zeros((1, tq), jnp.float32), jnp.zeros((1, tq), jnp.float32)

    all_selected = n_keys <= MAX_TOPK
    rmax, rmin = lax.cond(all_selected, no_scores, last_chunk, extremes)

    n_sweep_tiles = jnp.where(all_selected, 0, n_keys // SWEEP_TILE)

    def sweep(mid, want):
        kinds = {"gt": (jnp.sum, 0.0), "ge": (jnp.sum, 0.0), "above": (jnp.min, jnp.inf), "below": (jnp.max, -jnp.inf)}

        def body(c, carry):
            k0 = pl.multiple_of(c * SWEEP_TILE, SWEEP_TILE)
            s = sc_ref[pl.ds(k0, SWEEP_TILE), :].reshape(SWEEP_TILE // PARTIAL_ROWS, PARTIAL_ROWS, tq)
            gt = s > mid
            terms = {"gt": lambda: jnp.where(gt, 1.0, 0.0), "ge": lambda: jnp.where(s >= mid, 1.0, 0.0),
                     "above": lambda: jnp.where(gt, s, jnp.inf), "below": lambda: jnp.where(gt, -jnp.inf, s)}
            out = []
            for name, acc in zip(want, carry):
                op = kinds[name][0]
                part = op(terms[name](), axis=0)
                out.append(acc + part if op is jnp.sum else
                           (jnp.minimum(acc, part) if op is jnp.min else jnp.maximum(acc, part)))
            return tuple(out)

        init = tuple(jnp.full((PARTIAL_ROWS, tq), kinds[name][1], jnp.float32) for name in want)
        res = lax.fori_loop(0, n_sweep_tiles, body, init)
        return [kinds[name][0](r, axis=0, keepdims=True) for name, r in zip(want, res)]

    def update(state, mid, cnt, tie, new_lo, new_hi):
        lo, hi, chi, act = state
        on = act > 0.5
        fin = jnp.logical_and(on, jnp.logical_or(cnt == topk, tie))
        go_on = jnp.logical_and(on, jnp.logical_not(fin))
        up = jnp.logical_and(go_on, cnt > topk)
        dn = jnp.logical_and(go_on, cnt < topk)
        lo = jnp.where(fin, mid, jnp.where(up, new_lo, lo))
        hi = jnp.where(fin, mid, jnp.where(dn, new_hi, hi))
        chi = jnp.where(jnp.logical_or(fin, dn), cnt, chi)
        act = jnp.where(jnp.logical_and(go_on, lo < hi), 1.0, 0.0)
        return lo, hi, chi, act

    def midpoint(state):
        return 0.5 * state[0] + 0.5 * state[1]

    n_valid = (qb * tq + 1 + lax.broadcasted_iota(jnp.int32, (1, tq), 1)).astype(jnp.float32)
    few = n_valid <= topk
    state = (rmin,
             jnp.where(few, -jnp.inf, rmax),
             jnp.where(few, topk, 0.0),
             jnp.where(jnp.logical_or(few, rmin >= rmax), 0.0, 1.0))

    def coarse_count(t16, also_ge=False):
        one, zero = jnp.ones((), jnp.bfloat16), jnp.zeros((), jnp.bfloat16)

        def tile_sum(hits):
            parts = [hits[g] for g in range(SWEEP_TILE // PARTIAL_ROWS)]
            while len(parts) > 1:
                parts = [parts[i] + parts[i + 1] for i in range(0, len(parts), 2)]
            return parts[0]

        def body(c, accs):
            k0 = pl.multiple_of(c * SWEEP_TILE, SWEEP_TILE)
            s = sc16_ref[pl.ds(k0, SWEEP_TILE), :].reshape(SWEEP_TILE // PARTIAL_ROWS, PARTIAL_ROWS, tq)
            out = [accs[0] + tile_sum(jnp.where(s > t16, one, zero))]
            if also_ge:
                out.append(accs[1] + tile_sum(jnp.where(s >= t16, one, zero)))
            return tuple(out)

        init = (jnp.zeros((PARTIAL_ROWS, tq), jnp.bfloat16),) * (2 if also_ge else 1)
        accs = lax.fori_loop(0, n_sweep_tiles, body, init)
        return [jnp.sum(a.astype(jnp.float32), axis=0, keepdims=True) for a in accs]

    lo, hi, chi, act = state
    on = act > 0.5
    pos, nonneg = coarse_count(jnp.zeros((1, tq), jnp.bfloat16), also_ge=True)
    at_zero = jnp.logical_and(on, jnp.logical_and(pos <= topk, nonneg >= topk))
    lo = jnp.where(at_zero, 0.0, jnp.where(jnp.logical_and(on, pos > topk), jnp.maximum(lo, 0.0), lo))
    hi = jnp.where(at_zero, 0.0, jnp.where(jnp.logical_and(on, nonneg < topk), jnp.minimum(hi, 0.0), hi))
    chi = jnp.where(at_zero, pos, chi)
    act = jnp.where(at_zero, 0.0, act)

    for _ in range(COARSE_PROBES):
        t16 = (0.5 * lo + 0.5 * hi).astype(jnp.bfloat16)
        t = t16.astype(jnp.float32)
        (cnt16,) = coarse_count(t16)
        on = act > 0.5
        lo = jnp.where(jnp.logical_and(on, cnt16 >= topk), jnp.maximum(lo, t), lo)
        hi = jnp.where(jnp.logical_and(on, cnt16 < topk), jnp.minimum(hi, t + jnp.abs(t) * (2.0 ** -7) + 1e-30), hi)
    (cnt,) = sweep(hi, ("gt",))
    state = update((lo, hi, chi, act), hi, cnt, False, hi, hi)

    def pending(state):
        return jnp.where(jnp.logical_and(state[3] > 0.5, topk - state[2] != 1.0), 1.0, 0.0)

    def plain_cond(st):
        return jnp.logical_and(st[1] > 0.5, st[0] < PLAIN_BISECT_ROUNDS)

    def plain_body(st):
        state = st[2:]
        for _ in range(PROBES_PER_ROUND):
            mid = midpoint(state)
            (cnt,) = sweep(mid, ("gt",))
            state = update(state, mid, cnt, False, mid, mid)
        return (st[0] + 1, jnp.max(pending(state))) + state

    st = lax.while_loop(plain_cond, plain_body, (jnp.int32(0), jnp.max(pending(state))) + state)
    lo, hi, chi, act = st[2:]
    def two_largest_below(bound):
        groups = SWEEP_TILE // PARTIAL_ROWS

        def merge(a1, a2, b1, b2):
            return jnp.maximum(a1, b1), jnp.maximum(jnp.minimum(a1, b1), jnp.maximum(a2, b2))

        def body(c, carry):
            m1, m2 = carry
            k0 = pl.multiple_of(c * SWEEP_TILE, SWEEP_TILE)
            s = sc_ref[pl.ds(k0, SWEEP_TILE), :].reshape(groups, PARTIAL_ROWS, tq)
            x = jnp.where(s > bound, -jnp.inf, s)
            for g in range(groups):
                m1, m2 = jnp.maximum(m1, x[g]), jnp.maximum(m2, jnp.minimum(m1, x[g]))
            return m1, m2

        low = jnp.full((PARTIAL_ROWS, tq), -jnp.inf, jnp.float32)
        m1, m2 = lax.fori_loop(0, n_sweep_tiles, body, (low, low))
        rows = PARTIAL_ROWS
        while rows > 1:
            rows //= 2
            m1, m2 = merge(m1[:rows], m2[:rows], m1[rows:], m2[rows:])
        return m1, m2

    largest, second = two_largest_below(hi)
    one_short = jnp.logical_and(act > 0.5, topk - chi == 1.0)
    separated = jnp.logical_and(one_short, second < largest)
    new_thr = jnp.where(separated, second, largest)
    state = (jnp.where(one_short, new_thr, lo), jnp.where(one_short, new_thr, hi),
             jnp.where(separated, topk, chi), jnp.where(one_short, 0.0, act))

    def snap_cond(st):
        return jnp.logical_and(st[1] > 0.5, st[0] < 4096)

    def snap_body(st):
        state = st[2:]
        mid = midpoint(state)
        mid = jnp.where(mid >= state[1], state[0], mid)
        cnt, above, below = sweep(mid, ("gt", "above", "below"))
        state = update(state, mid, cnt, False, above, below)
        return (st[0] + 1, jnp.max(state[3])) + state

    st = lax.while_loop(snap_cond, snap_body, (jnp.int32(0), st[1]) + state)
    thr = st[3]
    need = topk - st[4]
    any_tie = jnp.max(need) > 0.5

    m_ref[...] = jnp.full(m_ref.shape, MASKED, jnp.float32)
    acc_ref[...] = jnp.zeros(acc_ref.shape, jnp.float32)
    eqb_ref[...] = jnp.zeros(eqb_ref.shape, jnp.float32)

    def attend_chunk(c, size):
        k0 = pl.multiple_of(c * tk, tk)
        s_idx = sc_ref[pl.ds(k0, size), :]
        bias_ref[0:size, :] = jnp.where(s_idx > thr, 0.0, MASKED)

        @pl.when(any_tie)
        def _():
            eq = s_idx == thr
            eq_f = jnp.where(eq, 1.0, 0.0)
            rank = jnp.dot(tri_ref[0:size, 0:size], eq_f.astype(jnp.bfloat16),
                           preferred_element_type=jnp.float32) + eqb_ref[...]
            tie_bias = jnp.where(jnp.logical_and(eq, rank < need), 0.0, MASKED)
            bias_ref[0:size, :] = jnp.where(s_idx > thr, 0.0, tie_bias)
            eqb_ref[...] = eqb_ref[...] + _reduce_keys(eq_f, jnp.sum)

        kc = kv_ref[pl.ds(k0, size), 0:HEAD_DIM]
        vt = vt_ref[c, :, 0:size]

        def logits(h):
            s = jnp.dot(kc, qt_ref[h], preferred_element_type=jnp.float32) + bias_ref[0:size, :]
            s_ref[h, 0:size, :] = s
            mnew_ref[h] = jnp.maximum(m_ref[h], _reduce_keys(s, jnp.max))

        def weigh(h):
            m_new = mnew_ref[h]
            alpha = jnp.exp2(m_ref[h] - m_new)
            p = jnp.exp2(s_ref[h, 0:size, :] - m_new)
            acc_ref[h] = alpha * acc_ref[h] + jnp.dot(
                vt, p.astype(jnp.bfloat16), preferred_element_type=jnp.float32)
            m_ref[h] = m_new

        for h in range(N_HEADS):
            logits(h)
        for h in range(N_HEADS):
            weigh(h)

    def attend_full(c, _):
        attend_chunk(c, tk)
        return 0

    lax.fori_loop(0, n_full, attend_full, 0)

    @pl.when(has_tail)
    def _():
        attend_chunk(n_full, TAIL_KEYS)

    for h in range(N_HEADS):
        rows = slice(h * HEAD_DIM, (h + 1) * HEAD_DIM)
        out_ref[rows, :] = acc_ref[h, 0:HEAD_DIM, :] / acc_ref[h, HEAD_DIM:HEAD_DIM + 1, :]
    y = out_ref[...].T * _silu(ga_ref[...])
    y_ref[...] = y.astype(y_ref.dtype)


def _dsa_attention(qt, kv, vt, wi, ga, tri):
    batch, seq, _ = kv.shape
    assert seq // PARTIAL_ROWS <= 256, "bf16 hit counters are exact only up to 256 per slot"
    row = lambda b, i: (b, i, 0)
    return pl.pallas_call(
        _attn_kernel,
        grid=(batch, seq // Q_TILE),
        in_specs=[
            pl.BlockSpec((None, None, N_HEADS + N_IDX_HEADS, HEAD_DIM, Q_TILE), lambda b, i: (b, i, 0, 0, 0)),
            pl.BlockSpec((None, seq, 256), lambda b, i: (b, 0, 0)),
            pl.BlockSpec((None, seq // K_TILE, VT_ROWS, K_TILE), lambda b, i: (b, 0, 0, 0)),
            pl.BlockSpec((None, Q_TILE, 128), row),
            pl.BlockSpec((None, Q_TILE, D_ATTN), row),
            pl.BlockSpec((K_TILE, K_TILE), lambda b, i: (0, 0)),
        ],
        out_specs=pl.BlockSpec((None, Q_TILE, D_ATTN), row),
        out_shape=jax.ShapeDtypeStruct((batch, seq, D_ATTN), jnp.bfloat16),
        scratch_shapes=[
            pltpu.VMEM((seq, Q_TILE), jnp.float32),
            pltpu.VMEM((seq, Q_TILE), jnp.bfloat16),
            pltpu.VMEM((K_TILE, Q_TILE), jnp.float32),
            pltpu.VMEM((1, Q_TILE), jnp.float32),
            pltpu.VMEM((N_HEADS, K_TILE, Q_TILE), jnp.float32),
            pltpu.VMEM((N_HEADS, 1, Q_TILE), jnp.float32),
            pltpu.VMEM((N_HEADS, 1, Q_TILE), jnp.float32),
            pltpu.VMEM((N_HEADS, VT_ROWS, Q_TILE), jnp.float32),
            pltpu.VMEM((D_ATTN, Q_TILE), jnp.float32),
        ],
        compiler_params=pltpu.CompilerParams(
            dimension_semantics=("parallel", "parallel"), vmem_limit_bytes=VMEM_LIMIT),
        name="dsa_attention",
    )(qt, kv, vt, wi, ga, tri)


def _outproj_kernel(yr_ref, ya_ref, x_ref, mod_ref, wr_ref, wa_ref, g_ref, o_ref, *, final_norm):
    y = jnp.dot(yr_ref[...], wr_ref[...], preferred_element_type=jnp.float32)
    y = y + jnp.dot(ya_ref[...], wa_ref[...], preferred_element_type=jnp.float32)
    x_new = x_ref[...] + mod_ref[2:3, :] * y
    if final_norm:
        x_new = _rms(x_new, g_ref[...])
    o_ref[...] = x_new


def _out_projection(yr, ya, x, mod_l, w_r, w_a, final_g, final_norm):
    batch, seq, _ = x.shape
    row = lambda b, i: (b, i, 0)
    const = lambda b, i: (0, 0)
    return pl.pallas_call(
        functools.partial(_outproj_kernel, final_norm=final_norm),
        grid=(batch, seq // OUT_ROW_TILE),
        in_specs=[
            pl.BlockSpec((None, OUT_ROW_TILE, D_RNN), row),
            pl.BlockSpec((None, OUT_ROW_TILE, D_ATTN), row),
            pl.BlockSpec((None, OUT_ROW_TILE, D_MODEL), row),
            pl.BlockSpec((None, 3, D_MODEL), lambda b, i: (b, 0, 0)),
            pl.BlockSpec((D_RNN, D_MODEL), const),
            pl.BlockSpec((D_ATTN, D_MODEL), const),
            pl.BlockSpec((1, D_MODEL), const),
        ],
        out_specs=pl.BlockSpec((None, OUT_ROW_TILE, D_MODEL), row),
        out_shape=jax.ShapeDtypeStruct((batch, seq, D_MODEL), jnp.float32),
        compiler_params=pltpu.CompilerParams(
            dimension_semantics=("parallel", "parallel"), vmem_limit_bytes=VMEM_LIMIT),
        name="out_proj",
    )(yr, ya, x, mod_l, w_r, w_a, final_g.reshape(1, D_MODEL))


def _pad_w_in(w):
    return jnp.pad(w.astype(jnp.bfloat16), ((0, 0), (0, 0), (0, Z_COLS - D_IN)))


def _block_diag(w):
    n, c, d = w.shape
    eye = jnp.eye(n, dtype=w.dtype)
    return (eye[:, None, :, None] * w[:, :, None, :]).reshape(n * c, n * d)


def kernel(x, c, norm_g, ada_w, ada_b, w_in, conv_w, conv_b, lru_wx, lru_bx, lru_wa, lru_ba, lru_a, w_out, final_g):
    depth = w_in.shape[0]
    mod = _modulation(c, ada_w, ada_b)
    idx = jnp.arange(K_TILE)
    tri = (idx[None, :] < idx[:, None]).astype(jnp.bfloat16)
    w_pad = _pad_w_in(w_in)
    for l in range(depth):
        w_gates = jnp.concatenate([_block_diag(lru_wx[l]), _block_diag(lru_wa[l])], axis=-1).astype(jnp.bfloat16)
        b_gates = jnp.concatenate([lru_bx[l], lru_ba[l]]).reshape(1, 2 * D_RNN)
        w_o = w_out[l].astype(jnp.bfloat16)
        y_r, qt, ga, kv, wi, vt = _in_projection(x, mod[l], norm_g[l], w_pad[l], conv_w[l], conv_b[l],
                                                 w_gates, b_gates, lru_a[l])
        y_a = _dsa_attention(qt, kv, vt, wi, ga, tri)
        x = _out_projection(y_r, y_a, x, mod[l], w_o[:D_RNN], w_o[D_RNN:], final_g, l == depth - 1)
    return x
```

```python
import functools

import jax
import jax.numpy as jnp
from jax import lax
from jax.experimental import pallas as pl
from jax.experimental.pallas import tpu as pltpu

D_MODEL = 1024
D_RNN = 512
N_RNN_BLOCKS = 8
RNN_BLOCK = D_RNN // N_RNN_BLOCKS
CONV_WIDTH = 4
LRU_C = 8.0
N_HEADS = 8
HEAD_DIM = 64
D_ATTN = N_HEADS * HEAD_DIM
N_IDX_HEADS = 8
IDX_DIM = 64
MAX_TOPK = 256
EPS = 1e-6
LOG2_E = 1.4426950408889634

Z_RNN = 0
Z_Q = 1024
Z_KV = 1536
Z_GA = 1664
Z_QI = 2176
Z_KI = 2688
Z_COLS = 2816
D_IN = 2760
WI_LANE = 64

ROW_TILE = 512
OUT_ROW_TILE = 1024
LRU_STEPS = ROW_TILE // 8
Q_TILE = 256
K_TILE = 512
VT_ROWS = 80
MASKED = -1e30
COARSE_PROBES = 9
PROBES_PER_ROUND = 7
PLAIN_BISECT_ROUNDS = 5
PARTIAL_ROWS = 32
SWEEP_TILE = 256
TAIL_KEYS = K_TILE - Q_TILE
assert K_TILE == 2 * Q_TILE
VMEM_LIMIT = 48 * 1024 * 1024


def _sigmoid(x):
    return 0.5 * jnp.tanh(0.5 * x) + 0.5


def _silu(x):
    return x * _sigmoid(x)


def _rms(x, g):
    return x * lax.rsqrt(jnp.mean(x * x, axis=-1, keepdims=True) + EPS) * g


def _reduce_keys(x, op):
    keys, queries = x.shape
    part = op(x.reshape(keys // PARTIAL_ROWS, PARTIAL_ROWS, queries), axis=0)
    return op(part, axis=0, keepdims=True)


def _mod_kernel(c_ref, w_ref, b_ref, o_ref):
    c_act = _silu(c_ref[...])
    o_ref[...] = jnp.dot(c_act, w_ref[...], precision=lax.Precision.HIGHEST,
                         preferred_element_type=jnp.float32) + b_ref[...]


def _modulation(c, ada_w, ada_b):
    depth = ada_w.shape[0]
    batch = c.shape[0]
    out = pl.pallas_call(
        _mod_kernel,
        grid=(depth, 3),
        in_specs=[
            pl.BlockSpec((batch, D_MODEL), lambda l, j: (0, 0)),
            pl.BlockSpec((None, D_MODEL, D_MODEL), lambda l, j: (l, 0, j)),
            pl.BlockSpec((None, None, 1, D_MODEL), lambda l, j: (l, j, 0, 0)),
        ],
        out_specs=pl.BlockSpec((None, None, batch, D_MODEL), lambda l, j: (l, j, 0, 0)),
        out_shape=jax.ShapeDtypeStruct((depth, 3, batch, D_MODEL), jnp.float32),
        compiler_params=pltpu.CompilerParams(vmem_limit_bytes=VMEM_LIMIT),
        name="adaln_mod",
    )(c, ada_w, ada_b.reshape(depth, 3, 1, D_MODEL))
    return out.transpose(0, 2, 1, 3)


def _inproj_kernel(x_ref, mod_ref, g_ref, w_ref, cw_ref, cb_ref, wg_ref, bg_ref, ap_ref,
                   yr_ref, qt_ref, ga_ref, kv_ref, wi_ref, vt_ref, perm_ref, tail_ref, h_ref):
    @pl.when(pl.program_id(1) == 0)
    def _():
        tail_ref[...] = jnp.zeros((8, D_RNN), jnp.float32)
        h_ref[...] = jnp.zeros((1, D_RNN), jnp.float32)

    x = x_ref[...]
    shift = mod_ref[0:1, :]
    scale = mod_ref[1:2, :]
    h = _rms(x, g_ref[...]) * (1.0 + scale) + shift
    h = h.astype(jnp.bfloat16)
    z_rnn = jnp.dot(h, w_ref[:, Z_RNN:Z_Q], preferred_element_type=jnp.float32)
    y_r = _rg_lru_tile(z_rnn[:, 0:D_RNN], z_rnn[:, D_RNN:2 * D_RNN],
                       cw_ref, cb_ref, wg_ref, bg_ref, ap_ref, perm_ref, tail_ref, h_ref)
    yr_ref[...] = y_r.astype(yr_ref.dtype)
    z = jnp.dot(h, w_ref[:, Z_Q:Z_COLS], preferred_element_type=jnp.float32)
    col = lambda start, width: z[:, start - Z_Q:start - Z_Q + width]
    for h in range(N_HEADS):
        q_h = col(Z_Q + h * HEAD_DIM, HEAD_DIM) * (HEAD_DIM ** -0.5 * LOG2_E)
        _store_query_tiles(qt_ref, h, q_h.T.astype(jnp.bfloat16))
    for h in range(N_IDX_HEADS):
        _store_query_tiles(qt_ref, N_HEADS + h, col(Z_QI + h * IDX_DIM, IDX_DIM).T.astype(jnp.bfloat16))
    ga_ref[...] = col(Z_GA, D_ATTN)
    kv_ref[...] = jnp.concatenate([col(Z_KV, 128), col(Z_KI, 128)], axis=-1).astype(jnp.bfloat16)
    wi_ref[...] = col(Z_KI, 128)
    pad_rows = lax.broadcasted_iota(jnp.int32, (VT_ROWS - HEAD_DIM, K_TILE), 0)
    for j in range(ROW_TILE // K_TILE):
        v_t = col(Z_KV + HEAD_DIM, HEAD_DIM)[j * K_TILE:(j + 1) * K_TILE, :].T
        vt_ref[j, 0:HEAD_DIM, :] = v_t.astype(jnp.bfloat16)
        vt_ref[j, HEAD_DIM:VT_ROWS, :] = jnp.where(pad_rows == 0, 1.0, 0.0).astype(jnp.bfloat16)


def _store_query_tiles(qt_ref, head, q_t):
    for j in range(ROW_TILE // Q_TILE):
        qt_ref[j, head] = q_t[:, j * Q_TILE:(j + 1) * Q_TILE]


def _in_projection(x, mod_l, norm_g, w_pad, conv_w, conv_b, w_gates, b_gates, a_param):
    batch, seq, _ = x.shape
    row = lambda b, i: (b, i, 0)
    const = lambda b, i: (0, 0)
    return pl.pallas_call(
        _inproj_kernel,
        grid=(batch, seq // ROW_TILE),
        in_specs=[
            pl.BlockSpec((None, ROW_TILE, D_MODEL), row),
            pl.BlockSpec((None, 3, D_MODEL), lambda b, i: (b, 0, 0)),
            pl.BlockSpec((1, D_MODEL), const),
            pl.BlockSpec((D_MODEL, Z_COLS), const),
            pl.BlockSpec((CONV_WIDTH, D_RNN), const),
            pl.BlockSpec((1, D_RNN), const),
            pl.BlockSpec((D_RNN // 128, 128, 256), lambda b, i: (0, 0, 0)),
            pl.BlockSpec((1, 2 * D_RNN), const),
            pl.BlockSpec((1, D_RNN), const),
        ],
        out_specs=[
            pl.BlockSpec((None, ROW_TILE, D_RNN), row),
            pl.BlockSpec((None, ROW_TILE // Q_TILE, N_HEADS + N_IDX_HEADS, HEAD_DIM, Q_TILE),
                         lambda b, i: (b, i, 0, 0, 0)),
            pl.BlockSpec((None, ROW_TILE, 512), row),
            pl.BlockSpec((None, ROW_TILE, 256), row),
            pl.BlockSpec((None, ROW_TILE, 128), row),
            pl.BlockSpec((None, ROW_TILE // K_TILE, VT_ROWS, K_TILE), lambda b, i: (b, i, 0, 0)),
        ],
        out_shape=[
            jax.ShapeDtypeStruct((batch, seq, D_RNN), jnp.bfloat16),
            jax.ShapeDtypeStruct((batch, seq // Q_TILE, N_HEADS + N_IDX_HEADS, HEAD_DIM, Q_TILE), jnp.bfloat16),
            jax.ShapeDtypeStruct((batch, seq, 512), jnp.float32),
            jax.ShapeDtypeStruct((batch, seq, 256), jnp.bfloat16),
            jax.ShapeDtypeStruct((batch, seq, 128), jnp.float32),
            jax.ShapeDtypeStruct((batch, seq // K_TILE, VT_ROWS, K_TILE), jnp.bfloat16),
        ],
        scratch_shapes=[
            pltpu.VMEM((D_RNN // 128, ROW_TILE, 128), jnp.float32),
            pltpu.VMEM((8, D_RNN), jnp.float32),
            pltpu.VMEM((1, D_RNN), jnp.float32),
        ],
        compiler_params=pltpu.CompilerParams(
            dimension_semantics=("parallel", "arbitrary"), vmem_limit_bytes=VMEM_LIMIT),
        name="in_proj_rg_lru",
    )(x, mod_l, norm_g.reshape(1, D_MODEL), w_pad, conv_w, conv_b.reshape(1, D_RNN),
      w_gates, b_gates, a_param.reshape(1, D_RNN))


def _rg_lru_tile(xr, gr, cw_ref, cb_ref, wg_ref, bg_ref, ap_ref, perm_ref, tail_ref, h_ref):
    ts, steps, slabs = ROW_TILE, LRU_STEPS, D_RNN // 128
    lanes = lambda k: slice(k * 128, (k + 1) * 128)

    for s in range(8):
        for k in range(slabs):
            perm_ref[k, pl.ds(s, steps, stride=8), :] = xr[s * steps:(s + 1) * steps, lanes(k)]
    x = jnp.concatenate([perm_ref[k] for k in range(slabs)], axis=-1).reshape(steps, 8, D_RNN)

    first = lax.broadcasted_iota(jnp.int32, (8, D_RNN), 0) == 0
    before = [jnp.where(first, tail_ref[8 - m:9 - m, :], pltpu.roll(x[steps - m], 1, 0))
              for m in range(CONV_WIDTH - 1, 0, -1)]
    x_ext = jnp.concatenate([jnp.stack(before), x], axis=0)
    xc = cb_ref[...] + cw_ref[CONV_WIDTH - 1:CONV_WIDTH, :] * x
    for k in range(CONV_WIDTH - 1):
        xc = xc + cw_ref[k:k + 1, :] * x_ext[k:k + steps]
    tail_ref[...] = xr[ts - 8:ts, :]

    xc_bf = xc.reshape(ts, D_RNN).astype(jnp.bfloat16)
    slab_gates = [jnp.dot(xc_bf[:, lanes(k)], wg_ref[k], preferred_element_type=jnp.float32) for k in range(slabs)]
    gates_x = jnp.concatenate([g[:, 0:128] for g in slab_gates], axis=-1) + bg_ref[:, 0:D_RNN]
    gates_a = jnp.concatenate([g[:, 128:256] for g in slab_gates], axis=-1) + bg_ref[:, D_RNN:2 * D_RNN]
    gate_x = _sigmoid(gates_x.reshape(steps, 8, D_RNN))
    gate_a = _sigmoid(gates_a.reshape(steps, 8, D_RNN))
    neg_ap = -ap_ref[...]
    softplus = jnp.maximum(neg_ap, 0.0) + jnp.log(1.0 + jnp.exp(-jnp.abs(neg_ap)))
    log_a = (-LRU_C) * gate_a * softplus
    a = jnp.exp(log_a)
    gap = 1.0 - a * a
    mult = jnp.where(gap > 0.0, gap * lax.rsqrt(gap), 0.0)
    u = mult * gate_x * xc

    h_run, a_run = u[0], a[0]
    h_loc, a_cum = [h_run], [a_run]
    for j in range(1, steps):
        h_run = a[j] * h_run + u[j]
        a_run = a[j] * a_run
        h_loc.append(h_run)
        a_cum.append(a_run)
    state = h_ref[...]
    entering = []
    for s in range(8):
        entering.append(state)
        state = h_run[s:s + 1, :] + a_run[s:s + 1, :] * state
    h_ref[...] = state
    entering = jnp.concatenate(entering, axis=0)
    h = jnp.stack([h_loc[j] + a_cum[j] * entering for j in range(steps)]).reshape(ts, D_RNN)

    for k in range(slabs):
        perm_ref[k] = h[:, lanes(k)]
    h = jnp.concatenate(
        [jnp.concatenate([perm_ref[k, pl.ds(s, steps, stride=8), :] for k in range(slabs)], axis=-1)
         for s in range(8)], axis=0)
    return h * _silu(gr)


def _attn_kernel(qt_ref, kv_ref, vt_ref, wi_ref, ga_ref, tri_ref, y_ref,
                 sc_ref, sc16_ref, bias_ref, eqb_ref, s_ref, m_ref, mnew_ref, acc_ref, out_ref):
    tq, tk = Q_TILE, K_TILE
    topk = float(MAX_TOPK)
    qb = pl.program_id(1)
    n_keys = (qb + 1) * tq
    n_full = n_keys // tk
    has_tail = n_keys - n_full * tk > 0

    w_t = wi_ref[...].T[WI_LANE:WI_LANE + N_IDX_HEADS, :] * ((IDX_DIM ** -0.5) * (N_IDX_HEADS ** -0.5))

    def score_chunk(c, carry, size, diagonal):
        rmax, rmin = carry
        k0 = pl.multiple_of(c * tk, tk)
        ki = kv_ref[pl.ds(k0, size), 128:128 + IDX_DIM]
        score = jnp.zeros((size, tq), jnp.float32)
        for h in range(N_IDX_HEADS):
            logits = jnp.dot(ki, qt_ref[N_HEADS + h], preferred_element_type=jnp.float32)
            score = score + jnp.maximum(logits, 0.0) * w_t[h:h + 1, :]
        if diagonal:
            key_pos = k0 + lax.broadcasted_iota(jnp.int32, (size, tq), 0)
            causal = key_pos <= qb * tq + lax.broadcasted_iota(jnp.int32, (size, tq), 1)
            low, high = jnp.where(causal, score, -jnp.inf), jnp.where(causal, score, jnp.inf)
        else:
            low = high = score
        sc_ref[pl.ds(k0, size), :] = low
        sc16_ref[pl.ds(k0, size), :] = low.astype(jnp.bfloat16)
        return jnp.maximum(rmax, _reduce_keys(low, jnp.max)), jnp.minimum(rmin, _reduce_keys(high, jnp.min))

    n_before = jnp.where(has_tail, n_full, n_full - 1)
    extremes = lax.fori_loop(
        0, n_before, functools.partial(score_chunk, size=tk, diagonal=False),
        (jnp.full((1, tq), -jnp.inf, jnp.float32), jnp.full((1, tq), jnp.inf, jnp.float32)))

    def last_chunk(e):
        return lax.cond(has_tail,
                        lambda e: score_chunk(n_full, e, TAIL_KEYS, True),
                        lambda e: score_chunk(n_full - 1, e, tk, True), e)

    def no_scores(e):
        key_pos = lax.broadcasted_iota(jnp.int32, (tq, tq), 0)
        causal = key_pos <= lax.broadcasted_iota(jnp.int32, (tq, tq), 1)
        sc_ref[0:tq, :] = jnp.where(causal, 0.0, -jnp.inf)
        return jnp.zeros((1, tq), jnp.float32), jnp.zeros((1, tq), jnp.float32)

    all_selected = n_keys <= MAX_TOPK
    rmax, rmin = lax.cond(all_selected, no_scores, last_chunk, extremes)

    n_sweep_tiles = jnp.where(all_selected, 0, n_keys // SWEEP_TILE)

    def sweep(mid, want):
        kinds = {"gt": (jnp.sum, 0.0), "ge": (jnp.sum, 0.0), "above": (jnp.min, jnp.inf), "below": (jnp.max, -jnp.inf)}

        def body(c, carry):
            k0 = pl.multiple_of(c * SWEEP_TILE, SWEEP_TILE)
            s = sc_ref[pl.ds(k0, SWEEP_TILE), :].reshape(SWEEP_TILE // PARTIAL_ROWS, PARTIAL_ROWS, tq)
            gt = s > mid
            terms = {"gt": lambda: jnp.where(gt, 1.0, 0.0), "ge": lambda: jnp.where(s >= mid, 1.0, 0.0),
                     "above": lambda: jnp.where(gt, s, jnp.inf), "below": lambda: jnp.where(gt, -jnp.inf, s)}
            out = []
            for name, acc in zip(want, carry):
                op = kinds[name][0]
                part = op(terms[name](), axis=0)
                out.append(acc + part if op is jnp.sum else
                           (jnp.minimum(acc, part) if op is jnp.min else jnp.maximum(acc, part)))
            return tuple(out)

        init = tuple(jnp.full((PARTIAL_ROWS, tq), kinds[name][1], jnp.float32) for name in want)
        res = lax.fori_loop(0, n_sweep_tiles, body, init)
        return [kinds[name][0](r, axis=0, keepdims=True) for name, r in zip(want, res)]

    def update(state, mid, cnt, tie, new_lo, new_hi):
        lo, hi, chi, act = state
        on = act > 0.5
        fin = jnp.logical_and(on, jnp.logical_or(cnt == topk, tie))
        go_on = jnp.logical_and(on, jnp.logical_not(fin))
        up = jnp.logical_and(go_on, cnt > topk)
        dn = jnp.logical_and(go_on, cnt < topk)
        lo = jnp.where(fin, mid, jnp.where(up, new_lo, lo))
        hi = jnp.where(fin, mid, jnp.where(dn, new_hi, hi))
        chi = jnp.where(jnp.logical_or(fin, dn), cnt, chi)
        act = jnp.where(jnp.logical_and(go_on, lo < hi), 1.0, 0.0)
        return lo, hi, chi, act

    def midpoint(state):
        return 0.5 * state[0] + 0.5 * state[1]

    n_valid = (qb * tq + 1 + lax.broadcasted_iota(jnp.int32, (1, tq), 1)).astype(jnp.float32)
    few = n_valid <= topk
    state = (rmin,
             jnp.where(few, -jnp.inf, rmax),
             jnp.where(few, topk, 0.0),
             jnp.where(jnp.logical_or(few, rmin >= rmax), 0.0, 1.0))

    def coarse_count(t16, also_ge=False):
        one, zero = jnp.ones((), jnp.bfloat16), jnp.zeros((), jnp.bfloat16)

        def tile_sum(hits):
            parts = [hits[g] for g in range(SWEEP_TILE // PARTIAL_ROWS)]
            while len(parts) > 1:
                parts = [parts[i] + parts[i + 1] for i in range(0, len(parts), 2)]
            return parts[0]

        def body(c, accs):
            k0 = pl.multiple_of(c * SWEEP_TILE, SWEEP_TILE)
            s = sc16_ref[pl.ds(k0, SWEEP_TILE), :].reshape(SWEEP_TILE // PARTIAL_ROWS, PARTIAL_ROWS, tq)
            out = [accs[0] + tile_sum(jnp.where(s > t16, one, zero))]
            if also_ge:
                out.append(accs[1] + tile_sum(jnp.where(s >= t16, one, zero)))
            return tuple(out)

        init = (jnp.zeros((PARTIAL_ROWS, tq), jnp.bfloat16),) * (2 if also_ge else 1)
        accs = lax.fori_loop(0, n_sweep_tiles, body, init)
        return [jnp.sum(a.astype(jnp.float32), axis=0, keepdims=True) for a in accs]

    lo, hi, chi, act = state
    on = act > 0.5
    pos, nonneg = coarse_count(jnp.zeros((1, tq), jnp.bfloat16), also_ge=True)
    at_zero = jnp.logical_and(on, jnp.logical_and(pos <= topk, nonneg >= topk))
    lo = jnp.where(at_zero, 0.0, jnp.where(jnp.logical_and(on, pos > topk), jnp.maximum(lo, 0.0), lo))
    hi = jnp.where(at_zero, 0.0, jnp.where(jnp.logical_and(on, nonneg < topk), jnp.minimum(hi, 0.0), hi))
    chi = jnp.where(at_zero, pos, chi)
    act = jnp.where(at_zero, 0.0, act)

    for _ in range(COARSE_PROBES):
        t16 = (0.5 * lo + 0.5 * hi).astype(jnp.bfloat16)
        t = t16.astype(jnp.float32)
        (cnt16,) = coarse_count(t16)
        on = act > 0.5
        lo = jnp.where(jnp.logical_and(on, cnt16 >= topk), jnp.maximum(lo, t), lo)
        hi = jnp.where(jnp.logical_and(on, cnt16 < topk), jnp.minimum(hi, t + jnp.abs(t) * (2.0 ** -7) + 1e-30), hi)
    (cnt,) = sweep(hi, ("gt",))
    state = update((lo, hi, chi, act), hi, cnt, False, hi, hi)

    def pending(state):
        return jnp.where(jnp.logical_and(state[3] > 0.5, topk - state[2] != 1.0), 1.0, 0.0)

    def plain_cond(st):
        return jnp.logical_and(st[1] > 0.5, st[0] < PLAIN_BISECT_ROUNDS)

    def plain_body(st):
        state = st[2:]
        for _ in range(PROBES_PER_ROUND):
            mid = midpoint(state)
            (cnt,) = sweep(mid, ("gt",))
            state = update(state, mid, cnt, False, mid, mid)
        return (st[0] + 1, jnp.max(pending(state))) + state

    st = lax.while_loop(plain_cond, plain_body, (jnp.int32(0), jnp.max(pending(state))) + state)
    lo, hi, chi, act = st[2:]
    def two_largest_below(bound):
        groups = SWEEP_TILE // PARTIAL_ROWS

        def merge(a1, a2, b1, b2):
            return jnp.maximum(a1, b1), jnp.maximum(jnp.minimum(a1, b1), jnp.maximum(a2, b2))

        def body(c, carry):
            m1, m2 = carry
            k0 = pl.multiple_of(c * SWEEP_TILE, SWEEP_TILE)
            s = sc_ref[pl.ds(k0, SWEEP_TILE), :].reshape(groups, PARTIAL_ROWS, tq)
            x = jnp.where(s > bound, -jnp.inf, s)
            for g in range(groups):
                m1, m2 = jnp.maximum(m1, x[g]), jnp.maximum(m2, jnp.minimum(m1, x[g]))
            return m1, m2

        low = jnp.full((PARTIAL_ROWS, tq), -jnp.inf, jnp.float32)
        m1, m2 = lax.fori_loop(0, n_sweep_tiles, body, (low, low))
        rows = PARTIAL_ROWS
        while rows > 1:
            rows //= 2
            m1, m2 = merge(m1[:rows], m2[:rows], m1[rows:], m2[rows:])
        return m1, m2

    largest, second = two_largest_below(hi)
    one_short = jnp.logical_and(act > 0.5, topk - chi == 1.0)
    separated = jnp.logical_and(one_short, second < largest)
    new_thr = jnp.where(separated, second, largest)
    state = (jnp.where(one_short, new_thr, lo), jnp.where(one_short, new_thr, hi),
             jnp.where(separated, topk, chi), jnp.where(one_short, 0.0, act))

    def snap_cond(st):
        return jnp.logical_and(st[1] > 0.5, st[0] < 4096)

    def snap_body(st):
        state = st[2:]
        mid = midpoint(state)
        mid = jnp.where(mid >= state[1], state[0], mid)
        cnt, above, below = sweep(mid, ("gt", "above", "below"))
        state = update(state, mid, cnt, False, above, below)
        return (st[0] + 1, jnp.max(state[3])) + state

    st = lax.while_loop(snap_cond, snap_body, (jnp.int32(0), st[1]) + state)
    thr = st[3]
    need = topk - st[4]
    any_tie = jnp.max(need) > 0.5

    m_ref[...] = jnp.full(m_ref.shape, MASKED, jnp.float32)
    acc_ref[...] = jnp.zeros(acc_ref.shape, jnp.float32)
    eqb_ref[...] = jnp.zeros(eqb_ref.shape, jnp.float32)

    def attend_chunk(c, size):
        k0 = pl.multiple_of(c * tk, tk)
        s_idx = sc_ref[pl.ds(k0, size), :]
        bias_ref[0:size, :] = jnp.where(s_idx > thr, 0.0, MASKED)

        @pl.when(any_tie)
        def _():
            eq = s_idx == thr
            eq_f = jnp.where(eq, 1.0, 0.0)
            rank = jnp.dot(tri_ref[0:size, 0:size], eq_f.astype(jnp.bfloat16),
                           preferred_element_type=jnp.float32) + eqb_ref[...]
            tie_bias = jnp.where(jnp.logical_and(eq, rank < need), 0.0, MASKED)
            bias_ref[0:size, :] = jnp.where(s_idx > thr, 0.0, tie_bias)
            eqb_ref[...] = eqb_ref[...] + _reduce_keys(eq_f, jnp.sum)

        kc = kv_ref[pl.ds(k0, size), 0:HEAD_DIM]
        vt = vt_ref[c, :, 0:size]

        def logits(h):
            s = jnp.dot(kc, qt_ref[h], preferred_element_type=jnp.float32) + bias_ref[0:size, :]
            s_ref[h, 0:size, :] = s
            mnew_ref[h] = jnp.maximum(m_ref[h], _reduce_keys(s, jnp.max))

        def weigh(h):
            m_new = mnew_ref[h]
            alpha = jnp.exp2(m_ref[h] - m_new)
            p = jnp.exp2(s_ref[h, 0:size, :] - m_new)
            acc_ref[h] = alpha * acc_ref[h] + jnp.dot(
                vt, p.astype(jnp.bfloat16), preferred_element_type=jnp.float32)
            m_ref[h] = m_new

        for h in range(N_HEADS):
            logits(h)
        for h in range(N_HEADS):
            weigh(h)

    def attend_full(c, _):
        attend_chunk(c, tk)
        return 0

    lax.fori_loop(0, n_full, attend_full, 0)

    @pl.when(has_tail)
    def _():
        attend_chunk(n_full, TAIL_KEYS)

    for h in range(N_HEADS):
        rows = slice(h * HEAD_DIM, (h + 1) * HEAD_DIM)
        out_ref[rows, :] = acc_ref[h, 0:HEAD_DIM, :] / acc_ref[h, HEAD_DIM:HEAD_DIM + 1, :]
    y = out_ref[...].T * _silu(ga_ref[...])
    y_ref[...] = y.astype(y_ref.dtype)


def _dsa_attention(qt, kv, vt, wi, ga, tri):
    batch, seq, _ = kv.shape
    assert seq // PARTIAL_ROWS <= 256, "bf16 hit counters are exact only up to 256 per slot"
    row = lambda b, i: (b, i, 0)
    return pl.pallas_call(
        _attn_kernel,
        grid=(batch, seq // Q_TILE),
        in_specs=[
            pl.BlockSpec((None, None, N_HEADS + N_IDX_HEADS, HEAD_DIM, Q_TILE), lambda b, i: (b, i, 0, 0, 0)),
            pl.BlockSpec((None, seq, 256), lambda b, i: (b, 0, 0)),
            pl.BlockSpec((None, seq // K_TILE, VT_ROWS, K_TILE), lambda b, i: (b, 0, 0, 0)),
            pl.BlockSpec((None, Q_TILE, 128), row),
            pl.BlockSpec((None, Q_TILE, D_ATTN), row),
            pl.BlockSpec((K_TILE, K_TILE), lambda b, i: (0, 0)),
        ],
        out_specs=pl.BlockSpec((None, Q_TILE, D_ATTN), row),
        out_shape=jax.ShapeDtypeStruct((batch, seq, D_ATTN), jnp.bfloat16),
        scratch_shapes=[
            pltpu.VMEM((seq, Q_TILE), jnp.float32),
            pltpu.VMEM((seq, Q_TILE), jnp.bfloat16),
            pltpu.VMEM((K_TILE, Q_TILE), jnp.float32),
            pltpu.VMEM((1, Q_TILE), jnp.float32),
            pltpu.VMEM((N_HEADS, K_TILE, Q_TILE), jnp.float32),
            pltpu.VMEM((N_HEADS, 1, Q_TILE), jnp.float32),
            pltpu.VMEM((N_HEADS, 1, Q_TILE), jnp.float32),
            pltpu.VMEM((N_HEADS, VT_ROWS, Q_TILE), jnp.float32),
            pltpu.VMEM((D_ATTN, Q_TILE), jnp.float32),
        ],
        compiler_params=pltpu.CompilerParams(
            dimension_semantics=("parallel", "parallel"), vmem_limit_bytes=VMEM_LIMIT),
        name="dsa_attention",
    )(qt, kv, vt, wi, ga, tri)


def _outproj_kernel(yr_ref, ya_ref, x_ref, mod_ref, wr_ref, wa_ref, g_ref, o_ref, *, final_norm):
    y = jnp.dot(yr_ref[...], wr_ref[...], preferred_element_type=jnp.float32)
    y = y + jnp.dot(ya_ref[...], wa_ref[...], preferred_element_type=jnp.float32)
    x_new = x_ref[...] + mod_ref[2:3, :] * y
    if final_norm:
        x_new = _rms(x_new, g_ref[...])
    o_ref[...] = x_new


def _out_projection(yr, ya, x, mod_l, w_r, w_a, final_g, final_norm):
    batch, seq, _ = x.shape
    row = lambda b, i: (b, i, 0)
    const = lambda b, i: (0, 0)
    return pl.pallas_call(
        functools.partial(_outproj_kernel, final_norm=final_norm),
        grid=(batch, seq // OUT_ROW_TILE),
        in_specs=[
            pl.BlockSpec((None, OUT_ROW_TILE, D_RNN), row),
            pl.BlockSpec((None, OUT_ROW_TILE, D_ATTN), row),
            pl.BlockSpec((None, OUT_ROW_TILE, D_MODEL), row),
            pl.BlockSpec((None, 3, D_MODEL), lambda b, i: (b, 0, 0)),
            pl.BlockSpec((D_RNN, D_MODEL), const),
            pl.BlockSpec((D_ATTN, D_MODEL), const),
            pl.BlockSpec((1, D_MODEL), const),
        ],
        out_specs=pl.BlockSpec((None, OUT_ROW_TILE, D_MODEL), row),
        out_shape=jax.ShapeDtypeStruct((batch, seq, D_MODEL), jnp.float32),
        compiler_params=pltpu.CompilerParams(
            dimension_semantics=("parallel", "parallel"), vmem_limit_bytes=VMEM_LIMIT),
        name="out_proj",
    )(yr, ya, x, mod_l, w_r, w_a, final_g.reshape(1, D_MODEL))


def _pad_w_in(w):
    return jnp.pad(w.astype(jnp.bfloat16), ((0, 0), (0, 0), (0, Z_COLS - D_IN)))


def _block_diag(w):
    n, c, d = w.shape
    eye = jnp.eye(n, dtype=w.dtype)
    return (eye[:, None, :, None] * w[:, :, None, :]).reshape(n * c, n * d)


def _gate_slabs(w_x, w_a):
    per_slab = 128 // RNN_BLOCK
    slabs = [jnp.concatenate([_block_diag(w_x[k:k + per_slab]), _block_diag(w_a[k:k + per_slab])], axis=-1)
             for k in range(0, N_RNN_BLOCKS, per_slab)]
    return jnp.stack(slabs).astype(jnp.bfloat16)


def kernel(x, c, norm_g, ada_w, ada_b, w_in, conv_w, conv_b, lru_wx, lru_bx, lru_wa, lru_ba, lru_a, w_out, final_g):
    depth = w_in.shape[0]
    mod = _modulation(c, ada_w, ada_b)
    idx = jnp.arange(K_TILE)
    tri = (idx[None, :] < idx[:, None]).astype(jnp.bfloat16)
    w_pad = _pad_w_in(w_in)
    for l in range(depth):
        w_gates = _gate_slabs(lru_wx[l], lru_wa[l])
        b_gates = jnp.concatenate([lru_bx[l], lru_ba[l]]).reshape(1, 2 * D_RNN)
        w_o = w_out[l].astype(jnp.bfloat16)
        y_r, qt, ga, kv, wi, vt = _in_projection(x, mod[l], norm_g[l], w_pad[l], conv_w[l], conv_b[l],
                                                 w_gates, b_gates, lru_a[l])
        y_a = _dsa_attention(qt, kv, vt, wi, ga, tri)
        x = _out_projection(y_r, y_a, x, mod[l], w_o[:D_RNN], w_o[D_RNN:], final_g, l == depth - 1)
    return x
```

```python
import functools

import jax
import jax.numpy as jnp
from jax import lax
from jax.experimental import pallas as pl
from jax.experimental.pallas import tpu as pltpu

D_MODEL = 1024
D_RNN = 512
N_RNN_BLOCKS = 8
RNN_BLOCK = D_RNN // N_RNN_BLOCKS
CONV_WIDTH = 4
LRU_C = 8.0
N_HEADS = 8
HEAD_DIM = 64
D_ATTN = N_HEADS * HEAD_DIM
N_IDX_HEADS = 8
IDX_DIM = 64
MAX_TOPK = 256
EPS = 1e-6
LOG2_E = 1.4426950408889634

Z_RNN = 0
Z_Q = 1024
Z_KV = 1536
Z_GA = 1664
Z_QI = 2176
Z_KI = 2688
Z_COLS = 2816
D_IN = 2760
WI_LANE = 64
KI_LANE = 128
BF16_STEP = 2.0 ** -7

ROW_TILE = 512
OUT_ROW_TILE = 1024
LRU_STEPS = ROW_TILE // 8
Q_TILE = 256
K_TILE = 512
VT_ROWS = 80
MASKED = -1e30
COARSE_PROBES = 9
PROBES_PER_ROUND = 7
PLAIN_BISECT_ROUNDS = 5
PARTIAL_ROWS = 32
SWEEP_TILE = 256
TAIL_KEYS = K_TILE - Q_TILE
assert K_TILE == 2 * Q_TILE
VMEM_LIMIT = 48 * 1024 * 1024


def _sigmoid(x):
    return 0.5 * jnp.tanh(0.5 * x) + 0.5


def _silu(x):
    return x * _sigmoid(x)


def _rms(x, g):
    return x * lax.rsqrt(jnp.mean(x * x, axis=-1, keepdims=True) + EPS) * g


def _reduce_keys(x, op):
    keys, queries = x.shape
    part = op(x.reshape(keys // PARTIAL_ROWS, PARTIAL_ROWS, queries), axis=0)
    return op(part, axis=0, keepdims=True)


def _mod_kernel(c_ref, w_ref, b_ref, o_ref):
    c_act = _silu(c_ref[...])
    o_ref[...] = jnp.dot(c_act, w_ref[...], precision=lax.Precision.HIGHEST,
                         preferred_element_type=jnp.float32) + b_ref[...]


def _modulation(c, ada_w, ada_b):
    depth = ada_w.shape[0]
    batch = c.shape[0]
    out = pl.pallas_call(
        _mod_kernel,
        grid=(depth, 3),
        in_specs=[
            pl.BlockSpec((batch, D_MODEL), lambda l, j: (0, 0)),
            pl.BlockSpec((None, D_MODEL, D_MODEL), lambda l, j: (l, 0, j)),
            pl.BlockSpec((None, None, 1, D_MODEL), lambda l, j: (l, j, 0, 0)),
        ],
        out_specs=pl.BlockSpec((None, None, batch, D_MODEL), lambda l, j: (l, j, 0, 0)),
        out_shape=jax.ShapeDtypeStruct((depth, 3, batch, D_MODEL), jnp.float32),
        compiler_params=pltpu.CompilerParams(vmem_limit_bytes=VMEM_LIMIT),
        name="adaln_mod",
    )(c, ada_w, ada_b.reshape(depth, 3, 1, D_MODEL))
    return out.transpose(0, 2, 1, 3)


def _inproj_kernel(x_ref, mod_ref, g_ref, w_ref, cw_ref, cb_ref, wg_ref, bg_ref, ap_ref,
                   yr_ref, qt_ref, ga_ref, kv_ref, wi_ref, vt_ref, perm_ref, tail_ref, h_ref):
    @pl.when(pl.program_id(1) == 0)
    def _():
        tail_ref[...] = jnp.zeros((8, D_RNN), jnp.float32)
        h_ref[...] = jnp.zeros((1, D_RNN), jnp.float32)

    x = x_ref[...]
    shift = mod_ref[0:1, :]
    scale = mod_ref[1:2, :]
    h = _rms(x, g_ref[...] * (1.0 + scale)) + shift
    h = h.astype(jnp.bfloat16)
    z_rnn = jnp.dot(h, w_ref[:, Z_RNN:Z_Q], preferred_element_type=jnp.float32)
    y_r = _rg_lru_tile(z_rnn[:, 0:D_RNN], z_rnn[:, D_RNN:2 * D_RNN],
                       cw_ref, cb_ref, wg_ref, bg_ref, ap_ref, perm_ref, tail_ref, h_ref)
    yr_ref[...] = y_r.astype(yr_ref.dtype)
    z = jnp.dot(h, w_ref[:, Z_Q:Z_COLS], preferred_element_type=jnp.float32)
    col = lambda start, width: z[:, start - Z_Q:start - Z_Q + width]
    for h in range(N_HEADS):
        q_h = col(Z_Q + h * HEAD_DIM, HEAD_DIM) * (HEAD_DIM ** -0.5 * LOG2_E)
        _store_query_tiles(qt_ref, h, q_h.T.astype(jnp.bfloat16))
    for h in range(N_IDX_HEADS):
        _store_query_tiles(qt_ref, N_HEADS + h, col(Z_QI + h * IDX_DIM, IDX_DIM).T.astype(jnp.bfloat16))
    ga_ref[...] = col(Z_GA, D_ATTN)
    kv_ref[...] = jnp.concatenate([col(Z_KV, 128), col(Z_KI, 128)], axis=-1).astype(jnp.bfloat16)
    wi_ref[...] = col(Z_KI, 128)
    pad_rows = lax.broadcasted_iota(jnp.int32, (VT_ROWS - HEAD_DIM, K_TILE), 0)
    for j in range(ROW_TILE // K_TILE):
        v_t = col(Z_KV + HEAD_DIM, HEAD_DIM)[j * K_TILE:(j + 1) * K_TILE, :].T
        vt_ref[j, 0:HEAD_DIM, :] = v_t.astype(jnp.bfloat16)
        vt_ref[j, HEAD_DIM:VT_ROWS, :] = jnp.where(pad_rows == 0, 1.0, 0.0).astype(jnp.bfloat16)


def _store_query_tiles(qt_ref, head, q_t):
    for j in range(ROW_TILE // Q_TILE):
        qt_ref[j, head] = q_t[:, j * Q_TILE:(j + 1) * Q_TILE]


def _in_projection(x, mod_l, norm_g, w_pad, conv_w, conv_b, w_gates, b_gates, a_param):
    batch, seq, _ = x.shape
    row = lambda b, i: (b, i, 0)
    const = lambda b, i: (0, 0)
    return pl.pallas_call(
        _inproj_kernel,
        grid=(batch, seq // ROW_TILE),
        in_specs=[
            pl.BlockSpec((None, ROW_TILE, D_MODEL), row),
            pl.BlockSpec((None, 3, D_MODEL), lambda b, i: (b, 0, 0)),
            pl.BlockSpec((1, D_MODEL), const),
            pl.BlockSpec((D_MODEL, Z_COLS), const),
            pl.BlockSpec((CONV_WIDTH, D_RNN), const),
            pl.BlockSpec((1, D_RNN), const),
            pl.BlockSpec((D_RNN // 128, 128, 256), lambda b, i: (0, 0, 0)),
            pl.BlockSpec((1, 2 * D_RNN), const),
            pl.BlockSpec((1, D_RNN), const),
        ],
        out_specs=[
            pl.BlockSpec((None, ROW_TILE, D_RNN), row),
            pl.BlockSpec((None, ROW_TILE // Q_TILE, N_HEADS + N_IDX_HEADS, HEAD_DIM, Q_TILE),
                         lambda b, i: (b, i, 0, 0, 0)),
            pl.BlockSpec((None, ROW_TILE, 512), row),
            pl.BlockSpec((None, ROW_TILE, 256), row),
            pl.BlockSpec((None, ROW_TILE, 128), row),
            pl.BlockSpec((None, ROW_TILE // K_TILE, VT_ROWS, K_TILE), lambda b, i: (b, i, 0, 0)),
        ],
        out_shape=[
            jax.ShapeDtypeStruct((batch, seq, D_RNN), jnp.bfloat16),
            jax.ShapeDtypeStruct((batch, seq // Q_TILE, N_HEADS + N_IDX_HEADS, HEAD_DIM, Q_TILE), jnp.bfloat16),
            jax.ShapeDtypeStruct((batch, seq, 512), jnp.float32),
            jax.ShapeDtypeStruct((batch, seq, 256), jnp.bfloat16),
            jax.ShapeDtypeStruct((batch, seq, 128), jnp.float32),
            jax.ShapeDtypeStruct((batch, seq // K_TILE, VT_ROWS, K_TILE), jnp.bfloat16),
        ],
        scratch_shapes=[
            pltpu.VMEM((D_RNN // 128, ROW_TILE, 128), jnp.float32),
            pltpu.VMEM((8, D_RNN), jnp.float32),
            pltpu.VMEM((1, D_RNN), jnp.float32),
        ],
        compiler_params=pltpu.CompilerParams(
            dimension_semantics=("parallel", "arbitrary"), vmem_limit_bytes=VMEM_LIMIT),
        name="in_proj_rg_lru",
    )(x, mod_l, norm_g.reshape(1, D_MODEL), w_pad, conv_w, conv_b.reshape(1, D_RNN),
      w_gates, b_gates, a_param.reshape(1, D_RNN))


def _rg_lru_tile(xr, gr, cw_ref, cb_ref, wg_ref, bg_ref, ap_ref, perm_ref, tail_ref, h_ref):
    ts, steps, slabs = ROW_TILE, LRU_STEPS, D_RNN // 128
    lanes = lambda k: slice(k * 128, (k + 1) * 128)

    for s in range(8):
        for k in range(slabs):
            perm_ref[k, pl.ds(s, steps, stride=8), :] = xr[s * steps:(s + 1) * steps, lanes(k)]
    x = jnp.concatenate([perm_ref[k] for k in range(slabs)], axis=-1).reshape(steps, 8, D_RNN)

    first = lax.broadcasted_iota(jnp.int32, (8, D_RNN), 0) == 0
    before = [jnp.where(first, tail_ref[8 - m:9 - m, :], pltpu.roll(x[steps - m], 1, 0))
              for m in range(CONV_WIDTH - 1, 0, -1)]
    x_ext = jnp.concatenate([jnp.stack(before), x], axis=0)
    xc = cb_ref[...] + cw_ref[CONV_WIDTH - 1:CONV_WIDTH, :] * x
    for k in range(CONV_WIDTH - 1):
        xc = xc + cw_ref[k:k + 1, :] * x_ext[k:k + steps]
    tail_ref[...] = xr[ts - 8:ts, :]

    xc_bf = xc.reshape(ts, D_RNN).astype(jnp.bfloat16)
    slab_gates = [jnp.dot(xc_bf[:, lanes(k)], wg_ref[k], preferred_element_type=jnp.float32) for k in range(slabs)]
    gates_x = jnp.concatenate([g[:, 0:128] for g in slab_gates], axis=-1) + bg_ref[:, 0:D_RNN]
    gates_a = jnp.concatenate([g[:, 128:256] for g in slab_gates], axis=-1) + bg_ref[:, D_RNN:2 * D_RNN]
    gate_x = _sigmoid(gates_x.reshape(steps, 8, D_RNN))
    gate_a = _sigmoid(gates_a.reshape(steps, 8, D_RNN))
    neg_ap = -ap_ref[...]
    softplus = jnp.maximum(neg_ap, 0.0) + jnp.log(1.0 + jnp.exp(-jnp.abs(neg_ap)))
    log_a = (-LRU_C) * gate_a * softplus
    a = jnp.exp(log_a)
    gap = 1.0 - a * a
    mult = jnp.where(gap > 0.0, gap * lax.rsqrt(gap), 0.0)
    u = mult * gate_x * xc

    h_run, a_run = u[0], a[0]
    h_loc, a_cum = [h_run], [a_run]
    for j in range(1, steps):
        h_run = a[j] * h_run + u[j]
        a_run = a[j] * a_run
        h_loc.append(h_run)
        a_cum.append(a_run)
    state = h_ref[...]
    entering = []
    for s in range(8):
        entering.append(state)
        state = h_run[s:s + 1, :] + a_run[s:s + 1, :] * state
    h_ref[...] = state
    entering = jnp.concatenate(entering, axis=0)
    h = jnp.stack([h_loc[j] + a_cum[j] * entering for j in range(steps)]).reshape(ts, D_RNN)

    for k in range(slabs):
        perm_ref[k] = h[:, lanes(k)]
    h = jnp.concatenate(
        [jnp.concatenate([perm_ref[k, pl.ds(s, steps, stride=8), :] for k in range(slabs)], axis=-1)
         for s in range(8)], axis=0)
    return h * _silu(gr)


def _attn_kernel(qt_ref, kv_ref, vt_ref, wi_ref, ga_ref, tri_ref, y_ref,
                 sc_ref, sc16_ref, bias_ref, eqb_ref, s_ref, m_ref, mnew_ref, acc_ref, out_ref):
    tq, tk = Q_TILE, K_TILE
    topk = float(MAX_TOPK)
    qb = pl.program_id(1)
    n_keys = (qb + 1) * tq
    n_full = n_keys // tk
    has_tail = n_keys - n_full * tk > 0

    w_t = wi_ref[...].T[WI_LANE:WI_LANE + N_IDX_HEADS, :] * ((IDX_DIM ** -0.5) * (N_IDX_HEADS ** -0.5))

    def score_chunk(c, carry, size, diagonal):
        rmax, rmin = carry
        k0 = pl.multiple_of(c * tk, tk)
        ki = kv_ref[pl.ds(k0, size), KI_LANE:KI_LANE + IDX_DIM]
        score = jnp.zeros((size, tq), jnp.float32)
        for h in range(N_IDX_HEADS):
            logits = jnp.dot(ki, qt_ref[N_HEADS + h], preferred_element_type=jnp.float32)
            score = score + jnp.maximum(logits, 0.0) * w_t[h:h + 1, :]
        if diagonal:
            key_pos = k0 + lax.broadcasted_iota(jnp.int32, (size, tq), 0)
            causal = key_pos <= qb * tq + lax.broadcasted_iota(jnp.int32, (size, tq), 1)
            low, high = jnp.where(causal, score, -jnp.inf), jnp.where(causal, score, jnp.inf)
        else:
            low = high = score
        sc_ref[pl.ds(k0, size), :] = low
        sc16_ref[pl.ds(k0, size), :] = low.astype(jnp.bfloat16)
        return jnp.maximum(rmax, _reduce_keys(low, jnp.max)), jnp.minimum(rmin, _reduce_keys(high, jnp.min))

    n_before = jnp.where(has_tail, n_full, n_full - 1)
    extremes = lax.fori_loop(
        0, n_before, functools.partial(score_chunk, size=tk, diagonal=False),
        (jnp.full((1, tq), -jnp.inf, jnp.float32), jnp.full((1, tq), jnp.inf, jnp.float32)))

    def last_chunk(e):
        return lax.cond(has_tail,
                        lambda e: score_chunk(n_full, e, TAIL_KEYS, True),
                        lambda e: score_chunk(n_full - 1, e, tk, True), e)

    def no_scores(e):
        key_pos = lax.broadcasted_iota(jnp.int32, (tq, tq), 0)
        causal = key_pos <= lax.broadcasted_iota(jnp.int32, (tq, tq), 1)
        sc_ref[0:tq, :] = jnp.where(causal, 0.0, -jnp.inf)
        return jnp.zeros((1, tq), jnp.float32), jnp.zeros((1, tq), jnp.float32)

    all_selected = n_keys <= MAX_TOPK
    rmax, rmin = lax.cond(all_selected, no_scores, last_chunk, extremes)

    n_sweep_tiles = jnp.where(all_selected, 0, n_keys // SWEEP_TILE)

    def sweep(mid, want):
        kinds = {"gt": (jnp.sum, 0.0), "ge": (jnp.sum, 0.0), "above": (jnp.min, jnp.inf), "below": (jnp.max, -jnp.inf)}

        def body(c, carry):
            k0 = pl.multiple_of(c * SWEEP_TILE, SWEEP_TILE)
            s = sc_ref[pl.ds(k0, SWEEP_TILE), :].reshape(SWEEP_TILE // PARTIAL_ROWS, PARTIAL_ROWS, tq)
            gt = s > mid
            terms = {"gt": lambda: jnp.where(gt, 1.0, 0.0), "ge": lambda: jnp.where(s >= mid, 1.0, 0.0),
                     "above": lambda: jnp.where(gt, s, jnp.inf), "below": lambda: jnp.where(gt, -jnp.inf, s)}
            out = []
            for name, acc in zip(want, carry):
                op = kinds[name][0]
                part = op(terms[name](), axis=0)
                out.append(acc + part if op is jnp.sum else
                           (jnp.minimum(acc, part) if op is jnp.min else jnp.maximum(acc, part)))
            return tuple(out)

        init = tuple(jnp.full((PARTIAL_ROWS, tq), kinds[name][1], jnp.float32) for name in want)
        res = lax.fori_loop(0, n_sweep_tiles, body, init)
        return [kinds[name][0](r, axis=0, keepdims=True) for name, r in zip(want, res)]

    def update(state, mid, cnt, tie, new_lo, new_hi):
        lo, hi, chi, act = state
        on = act > 0.5
        fin = jnp.logical_and(on, jnp.logical_or(cnt == topk, tie))
        go_on = jnp.logical_and(on, jnp.logical_not(fin))
        up = jnp.logical_and(go_on, cnt > topk)
        dn = jnp.logical_and(go_on, cnt < topk)
        lo = jnp.where(fin, mid, jnp.where(up, new_lo, lo))
        hi = jnp.where(fin, mid, jnp.where(dn, new_hi, hi))
        chi = jnp.where(jnp.logical_or(fin, dn), cnt, chi)
        act = jnp.where(jnp.logical_and(go_on, lo < hi), 1.0, 0.0)
        return lo, hi, chi, act

    def midpoint(state):
        return 0.5 * state[0] + 0.5 * state[1]

    n_valid = (qb * tq + 1 + lax.broadcasted_iota(jnp.int32, (1, tq), 1)).astype(jnp.float32)
    few = n_valid <= topk
    state = (rmin,
             jnp.where(few, -jnp.inf, rmax),
             jnp.where(few, topk, 0.0),
             jnp.where(jnp.logical_or(few, rmin >= rmax), 0.0, 1.0))

    def coarse_count(t16, also_ge=False):
        one, zero = jnp.ones((), jnp.bfloat16), jnp.zeros((), jnp.bfloat16)

        def tile_sum(hits):
            parts = [hits[g] for g in range(SWEEP_TILE // PARTIAL_ROWS)]
            while len(parts) > 1:
                parts = [parts[i] + parts[i + 1] for i in range(0, len(parts), 2)]
            return parts[0]

        def body(c, accs):
            k0 = pl.multiple_of(c * SWEEP_TILE, SWEEP_TILE)
            s = sc16_ref[pl.ds(k0, SWEEP_TILE), :].reshape(SWEEP_TILE // PARTIAL_ROWS, PARTIAL_ROWS, tq)
            out = [accs[0] + tile_sum(jnp.where(s > t16, one, zero))]
            if also_ge:
                out.append(accs[1] + tile_sum(jnp.where(s >= t16, one, zero)))
            return tuple(out)

        init = (jnp.zeros((PARTIAL_ROWS, tq), jnp.bfloat16),) * (2 if also_ge else 1)
        accs = lax.fori_loop(0, n_sweep_tiles, body, init)
        return [jnp.sum(a.astype(jnp.float32), axis=0, keepdims=True) for a in accs]

    lo, hi, chi, act = state
    on = act > 0.5
    pos, nonneg = coarse_count(jnp.zeros((1, tq), jnp.bfloat16), also_ge=True)
    at_zero = jnp.logical_and(on, jnp.logical_and(pos <= topk, nonneg >= topk))
    lo = jnp.where(at_zero, 0.0, jnp.where(jnp.logical_and(on, pos > topk), jnp.maximum(lo, 0.0), lo))
    hi = jnp.where(at_zero, 0.0, jnp.where(jnp.logical_and(on, nonneg < topk), jnp.minimum(hi, 0.0), hi))
    chi = jnp.where(at_zero, pos, chi)
    act = jnp.where(at_zero, 0.0, act)

    for _ in range(COARSE_PROBES):
        t16 = (0.5 * lo + 0.5 * hi).astype(jnp.bfloat16)
        t = t16.astype(jnp.float32)
        (cnt16,) = coarse_count(t16)
        on = act > 0.5
        lo = jnp.where(jnp.logical_and(on, cnt16 >= topk), jnp.maximum(lo, t), lo)
        hi = jnp.where(jnp.logical_and(on, cnt16 < topk), jnp.minimum(hi, t + jnp.abs(t) * BF16_STEP + 1e-30), hi)
    (cnt,) = sweep(hi, ("gt",))
    state = update((lo, hi, chi, act), hi, cnt, False, hi, hi)

    def pending(state):
        return jnp.where(jnp.logical_and(state[3] > 0.5, topk - state[2] != 1.0), 1.0, 0.0)

    def plain_cond(st):
        return jnp.logical_and(st[1] > 0.5, st[0] < PLAIN_BISECT_ROUNDS)

    def plain_body(st):
        state = st[2:]
        for _ in range(PROBES_PER_ROUND):
            mid = midpoint(state)
            (cnt,) = sweep(mid, ("gt",))
            state = update(state, mid, cnt, False, mid, mid)
        return (st[0] + 1, jnp.max(pending(state))) + state

    st = lax.while_loop(plain_cond, plain_body, (jnp.int32(0), jnp.max(pending(state))) + state)
    lo, hi, chi, act = st[2:]
    def two_largest_below(bound):
        groups = SWEEP_TILE // PARTIAL_ROWS

        def merge(a1, a2, b1, b2):
            return jnp.maximum(a1, b1), jnp.maximum(jnp.minimum(a1, b1), jnp.maximum(a2, b2))

        def body(c, carry):
            m1, m2 = carry
            k0 = pl.multiple_of(c * SWEEP_TILE, SWEEP_TILE)
            s = sc_ref[pl.ds(k0, SWEEP_TILE), :].reshape(groups, PARTIAL_ROWS, tq)
            x = jnp.where(s > bound, -jnp.inf, s)
            for g in range(groups):
                m1, m2 = jnp.maximum(m1, x[g]), jnp.maximum(m2, jnp.minimum(m1, x[g]))
            return m1, m2

        low = jnp.full((PARTIAL_ROWS, tq), -jnp.inf, jnp.float32)
        m1, m2 = lax.fori_loop(0, n_sweep_tiles, body, (low, low))
        rows = PARTIAL_ROWS
        while rows > 1:
            rows //= 2
            m1, m2 = merge(m1[:rows], m2[:rows], m1[rows:], m2[rows:])
        return m1, m2

    largest, second = two_largest_below(hi)
    one_short = jnp.logical_and(act > 0.5, topk - chi == 1.0)
    separated = jnp.logical_and(one_short, second < largest)
    new_thr = jnp.where(separated, second, largest)
    state = (jnp.where(one_short, new_thr, lo), jnp.where(one_short, new_thr, hi),
             jnp.where(separated, topk, chi), jnp.where(one_short, 0.0, act))

    def snap_cond(st):
        return jnp.logical_and(st[1] > 0.5, st[0] < 4096)

    def snap_body(st):
        state = st[2:]
        mid = midpoint(state)
        mid = jnp.where(mid >= state[1], state[0], mid)
        cnt, above, below = sweep(mid, ("gt", "above", "below"))
        state = update(state, mid, cnt, False, above, below)
        return (st[0] + 1, jnp.max(state[3])) + state

    st = lax.while_loop(snap_cond, snap_body, (jnp.int32(0), st[1]) + state)
    thr = st[3]
    need = topk - st[4]
    any_tie = jnp.max(need) > 0.5

    m_ref[...] = jnp.full(m_ref.shape, MASKED, jnp.float32)
    acc_ref[...] = jnp.zeros(acc_ref.shape, jnp.float32)
    eqb_ref[...] = jnp.zeros(eqb_ref.shape, jnp.float32)

    def attend_chunk(c, size):
        k0 = pl.multiple_of(c * tk, tk)
        s_idx = sc_ref[pl.ds(k0, size), :]
        bias_ref[0:size, :] = jnp.where(s_idx > thr, 0.0, MASKED)

        @pl.when(any_tie)
        def _():
            eq = s_idx == thr
            eq_f = jnp.where(eq, 1.0, 0.0)
            rank = jnp.dot(tri_ref[0:size, 0:size], eq_f.astype(jnp.bfloat16),
                           preferred_element_type=jnp.float32) + eqb_ref[...]
            tie_bias = jnp.where(jnp.logical_and(eq, rank < need), 0.0, MASKED)
            bias_ref[0:size, :] = jnp.where(s_idx > thr, 0.0, tie_bias)
            eqb_ref[...] = eqb_ref[...] + _reduce_keys(eq_f, jnp.sum)

        kc = kv_ref[pl.ds(k0, size), 0:HEAD_DIM]
        vt = vt_ref[c, :, 0:size]

        def logits(h):
            s = jnp.dot(kc, qt_ref[h], preferred_element_type=jnp.float32) + bias_ref[0:size, :]
            s_ref[h, 0:size, :] = s
            mnew_ref[h] = jnp.maximum(m_ref[h], _reduce_keys(s, jnp.max))

        def weigh(h):
            m_new = mnew_ref[h]
            alpha = jnp.exp2(m_ref[h] - m_new)
            p = jnp.exp2(s_ref[h, 0:size, :] - m_new)
            acc_ref[h] = alpha * acc_ref[h] + jnp.dot(
                vt, p.astype(jnp.bfloat16), preferred_element_type=jnp.float32)
            m_ref[h] = m_new

        for h in range(N_HEADS):
            logits(h)
        for h in range(N_HEADS):
            weigh(h)

    def attend_full(c, _):
        attend_chunk(c, tk)
        return 0

    lax.fori_loop(0, n_full, attend_full, 0)

    @pl.when(has_tail)
    def _():
        attend_chunk(n_full, TAIL_KEYS)

    for h in range(N_HEADS):
        rows = slice(h * HEAD_DIM, (h + 1) * HEAD_DIM)
        out_ref[rows, :] = acc_ref[h, 0:HEAD_DIM, :] * (1.0 / acc_ref[h, HEAD_DIM:HEAD_DIM + 1, :])
    y = out_ref[...].T * _silu(ga_ref[...])
    y_ref[...] = y.astype(y_ref.dtype)


def _dsa_attention(qt, kv, vt, wi, ga, tri):
    batch, seq, _ = kv.shape
    assert seq // PARTIAL_ROWS <= 256, "bf16 hit counters are exact only up to 256 per slot"
    row = lambda b, i: (b, i, 0)
    return pl.pallas_call(
        _attn_kernel,
        grid=(batch, seq // Q_TILE),
        in_specs=[
            pl.BlockSpec((None, None, N_HEADS + N_IDX_HEADS, HEAD_DIM, Q_TILE), lambda b, i: (b, i, 0, 0, 0)),
            pl.BlockSpec((None, seq, 256), lambda b, i: (b, 0, 0)),
            pl.BlockSpec((None, seq // K_TILE, VT_ROWS, K_TILE), lambda b, i: (b, 0, 0, 0)),
            pl.BlockSpec((None, Q_TILE, 128), row),
            pl.BlockSpec((None, Q_TILE, D_ATTN), row),
            pl.BlockSpec((K_TILE, K_TILE), lambda b, i: (0, 0)),
        ],
        out_specs=pl.BlockSpec((None, Q_TILE, D_ATTN), row),
        out_shape=jax.ShapeDtypeStruct((batch, seq, D_ATTN), jnp.bfloat16),
        scratch_shapes=[
            pltpu.VMEM((seq, Q_TILE), jnp.float32),
            pltpu.VMEM((seq, Q_TILE), jnp.bfloat16),
            pltpu.VMEM((K_TILE, Q_TILE), jnp.float32),
            pltpu.VMEM((1, Q_TILE), jnp.float32),
            pltpu.VMEM((N_HEADS, K_TILE, Q_TILE), jnp.float32),
            pltpu.VMEM((N_HEADS, 1, Q_TILE), jnp.float32),
            pltpu.VMEM((N_HEADS, 1, Q_TILE), jnp.float32),
            pltpu.VMEM((N_HEADS, VT_ROWS, Q_TILE), jnp.float32),
            pltpu.VMEM((D_ATTN, Q_TILE), jnp.float32),
        ],
        compiler_params=pltpu.CompilerParams(
            dimension_semantics=("parallel", "parallel"), vmem_limit_bytes=VMEM_LIMIT),
        name="dsa_attention",
    )(qt, kv, vt, wi, ga, tri)


def _outproj_kernel(yr_ref, ya_ref, x_ref, mod_ref, wr_ref, wa_ref, g_ref, o_ref, *, final_norm):
    y = jnp.dot(yr_ref[...], wr_ref[...], preferred_element_type=jnp.float32)
    y = y + jnp.dot(ya_ref[...], wa_ref[...], preferred_element_type=jnp.float32)
    x_new = x_ref[...] + mod_ref[2:3, :] * y
    if final_norm:
        x_new = _rms(x_new, g_ref[...])
    o_ref[...] = x_new


def _out_projection(yr, ya, x, mod_l, w_r, w_a, final_g, final_norm):
    batch, seq, _ = x.shape
    row = lambda b, i: (b, i, 0)
    const = lambda b, i: (0, 0)
    return pl.pallas_call(
        functools.partial(_outproj_kernel, final_norm=final_norm),
        grid=(batch, seq // OUT_ROW_TILE),
        in_specs=[
            pl.BlockSpec((None, OUT_ROW_TILE, D_RNN), row),
            pl.BlockSpec((None, OUT_ROW_TILE, D_ATTN), row),
            pl.BlockSpec((None, OUT_ROW_TILE, D_MODEL), row),
            pl.BlockSpec((None, 3, D_MODEL), lambda b, i: (b, 0, 0)),
            pl.BlockSpec((D_RNN, D_MODEL), const),
            pl.BlockSpec((D_ATTN, D_MODEL), const),
            pl.BlockSpec((1, D_MODEL), const),
        ],
        out_specs=pl.BlockSpec((None, OUT_ROW_TILE, D_MODEL), row),
        out_shape=jax.ShapeDtypeStruct((batch, seq, D_MODEL), jnp.float32),
        compiler_params=pltpu.CompilerParams(
            dimension_semantics=("parallel", "parallel"), vmem_limit_bytes=VMEM_LIMIT),
        name="out_proj",
    )(yr, ya, x, mod_l, w_r, w_a, final_g.reshape(1, D_MODEL))


def _pad_w_in(w):
    return jnp.pad(w.astype(jnp.bfloat16), ((0, 0), (0, 0), (0, Z_COLS - D_IN)))


def _block_diag(w):
    n, c, d = w.shape
    eye = jnp.eye(n, dtype=w.dtype)
    return (eye[:, None, :, None] * w[:, :, None, :]).reshape(n * c, n * d)


def _gate_slabs(w_x, w_a):
    per_slab = 128 // RNN_BLOCK
    slabs = [jnp.concatenate([_block_diag(w_x[k:k + per_slab]), _block_diag(w_a[k:k + per_slab])], axis=-1)
             for k in range(0, N_RNN_BLOCKS, per_slab)]
    return jnp.stack(slabs).astype(jnp.bfloat16)


def kernel(x, c, norm_g, ada_w, ada_b, w_in, conv_w, conv_b, lru_wx, lru_bx, lru_wa, lru_ba, lru_a, w_out, final_g):
    depth = w_in.shape[0]
    mod = _modulation(c, ada_w, ada_b)
    idx = jnp.arange(K_TILE)
    tri = (idx[None, :] < idx[:, None]).astype(jnp.bfloat16)
    w_pad = _pad_w_in(w_in)
    for l in range(depth):
        w_gates = _gate_slabs(lru_wx[l], lru_wa[l])
        b_gates = jnp.concatenate([lru_bx[l], lru_ba[l]]).reshape(1, 2 * D_RNN)
        w_o = w_out[l].astype(jnp.bfloat16)
        y_r, qt, ga, kv, wi, vt = _in_projection(x, mod[l], norm_g[l], w_pad[l], conv_w[l], conv_b[l],
                                                 w_gates, b_gates, lru_a[l])
        y_a = _dsa_attention(qt, kv, vt, wi, ga, tri)
        x = _out_projection(y_r, y_a, x, mod[l], w_o[:D_RNN], w_o[D_RNN:], final_g, l == depth - 1)
    return x
```

```python
import functools

import jax
import jax.numpy as jnp
from jax import lax
from jax.experimental import pallas as pl
from jax.experimental.pallas import tpu as pltpu

D_MODEL = 1024
D_RNN = 512
N_RNN_BLOCKS = 8
RNN_BLOCK = D_RNN // N_RNN_BLOCKS
CONV_WIDTH = 4
LRU_C = 8.0
N_HEADS = 8
HEAD_DIM = 64
D_ATTN = N_HEADS * HEAD_DIM
N_IDX_HEADS = 8
IDX_DIM = 64
MAX_TOPK = 256
EPS = 1e-6
LOG2_E = 1.4426950408889634

Z_RNN = 0
Z_Q = 1024
Z_KV = 1536
Z_GA = 1664
Z_QI = 2176
Z_KI = 2688
Z_COLS = 2816
D_IN = 2760
WI_LANE = 64
KI_LANE = 128
BF16_STEP = 2.0 ** -7

ROW_TILE = 512
OUT_ROW_TILE = 1024
LRU_STEPS = ROW_TILE // 8
Q_TILE = 256
K_TILE = 512
VT_ROWS = 80
MASKED = -1e30
COARSE_PROBES = 9
PROBES_PER_ROUND = 7
PLAIN_BISECT_ROUNDS = 5
PARTIAL_ROWS = 32
SWEEP_TILE = 256
TAIL_KEYS = K_TILE - Q_TILE
assert K_TILE == 2 * Q_TILE
VMEM_LIMIT = 48 * 1024 * 1024


def _sigmoid(x):
    return 0.5 * jnp.tanh(0.5 * x) + 0.5


def _silu(x):
    return x * _sigmoid(x)


def _rms(x, g):
    return x * lax.rsqrt(jnp.mean(x * x, axis=-1, keepdims=True) + EPS) * g


def _reduce_keys(x, op):
    keys, queries = x.shape
    part = op(x.reshape(keys // PARTIAL_ROWS, PARTIAL_ROWS, queries), axis=0)
    return op(part, axis=0, keepdims=True)


def _mod_kernel(c_ref, w_ref, b_ref, o_ref):
    c_act = _silu(c_ref[...])
    o_ref[...] = jnp.dot(c_act, w_ref[...], precision=lax.Precision.HIGHEST,
                         preferred_element_type=jnp.float32) + b_ref[...]


def _modulation(c, ada_w, ada_b):
    depth = ada_w.shape[0]
    batch = c.shape[0]
    out = pl.pallas_call(
        _mod_kernel,
        grid=(depth, 3),
        in_specs=[
            pl.BlockSpec((batch, D_MODEL), lambda l, j: (0, 0)),
            pl.BlockSpec((None, D_MODEL, D_MODEL), lambda l, j: (l, 0, j)),
            pl.BlockSpec((None, None, 1, D_MODEL), lambda l, j: (l, j, 0, 0)),
        ],
        out_specs=pl.BlockSpec((None, None, batch, D_MODEL), lambda l, j: (l, j, 0, 0)),
        out_shape=jax.ShapeDtypeStruct((depth, 3, batch, D_MODEL), jnp.float32),
        compiler_params=pltpu.CompilerParams(vmem_limit_bytes=VMEM_LIMIT),
        name="adaln_mod",
    )(c, ada_w, ada_b.reshape(depth, 3, 1, D_MODEL))
    return out.transpose(0, 2, 1, 3)


def _inproj_kernel(x_ref, mod_ref, g_ref, w_ref, cw_ref, cb_ref, wg_ref, bg_ref, ap_ref,
                   yr_ref, qt_ref, ga_ref, kv_ref, wi_ref, vt_ref, perm_ref, tail_ref, h_ref):
    @pl.when(pl.program_id(1) == 0)
    def _():
        tail_ref[...] = jnp.zeros((8, D_RNN), jnp.float32)
        h_ref[...] = jnp.zeros((1, D_RNN), jnp.float32)

    x = x_ref[...]
    shift = mod_ref[0:1, :]
    scale = mod_ref[1:2, :]
    h = _rms(x, g_ref[...]) * (1.0 + scale) + shift
    h = h.astype(jnp.bfloat16)
    z_rnn = jnp.dot(h, w_ref[:, Z_RNN:Z_Q], preferred_element_type=jnp.float32)
    y_r = _rg_lru_tile(z_rnn[:, 0:D_RNN], z_rnn[:, D_RNN:2 * D_RNN],
                       cw_ref, cb_ref, wg_ref, bg_ref, ap_ref, perm_ref, tail_ref, h_ref)
    yr_ref[...] = y_r.astype(yr_ref.dtype)
    z = jnp.dot(h, w_ref[:, Z_Q:Z_COLS], preferred_element_type=jnp.float32)
    col = lambda start, width: z[:, start - Z_Q:start - Z_Q + width]
    for h in range(N_HEADS):
        q_h = col(Z_Q + h * HEAD_DIM, HEAD_DIM) * (HEAD_DIM ** -0.5 * LOG2_E)
        _store_query_tiles(qt_ref, h, q_h.T.astype(jnp.bfloat16))
    for h in range(N_IDX_HEADS):
        _store_query_tiles(qt_ref, N_HEADS + h, col(Z_QI + h * IDX_DIM, IDX_DIM).T.astype(jnp.bfloat16))
    ga_ref[...] = col(Z_GA, D_ATTN)
    kv_ref[...] = jnp.concatenate([col(Z_KV, 128), col(Z_KI, 128)], axis=-1).astype(jnp.bfloat16)
    wi_ref[...] = col(Z_KI, 128)
    pad_rows = lax.broadcasted_iota(jnp.int32, (VT_ROWS - HEAD_DIM, K_TILE), 0)
    for j in range(ROW_TILE // K_TILE):
        v_t = col(Z_KV + HEAD_DIM, HEAD_DIM)[j * K_TILE:(j + 1) * K_TILE, :].T
        vt_ref[j, 0:HEAD_DIM, :] = v_t.astype(jnp.bfloat16)
        vt_ref[j, HEAD_DIM:VT_ROWS, :] = jnp.where(pad_rows == 0, 1.0, 0.0).astype(jnp.bfloat16)


def _store_query_tiles(qt_ref, head, q_t):
    for j in range(ROW_TILE // Q_TILE):
        qt_ref[j, head] = q_t[:, j * Q_TILE:(j + 1) * Q_TILE]


def _in_projection(x, mod_l, norm_g, w_pad, conv_w, conv_b, w_gates, b_gates, a_param):
    batch, seq, _ = x.shape
    row = lambda b, i: (b, i, 0)
    const = lambda b, i: (0, 0)
    return pl.pallas_call(
        _inproj_kernel,
        grid=(batch, seq // ROW_TILE),
        in_specs=[
            pl.BlockSpec((None, ROW_TILE, D_MODEL), row),
            pl.BlockSpec((None, 3, D_MODEL), lambda b, i: (b, 0, 0)),
            pl.BlockSpec((1, D_MODEL), const),
            pl.BlockSpec((D_MODEL, Z_COLS), const),
            pl.BlockSpec((CONV_WIDTH, D_RNN), const),
            pl.BlockSpec((1, D_RNN), const),
            pl.BlockSpec((D_RNN // 128, 128, 256), lambda b, i: (0, 0, 0)),
            pl.BlockSpec((1, 2 * D_RNN), const),
            pl.BlockSpec((1, D_RNN), const),
        ],
        out_specs=[
            pl.BlockSpec((None, ROW_TILE, D_RNN), row),
            pl.BlockSpec((None, ROW_TILE // Q_TILE, N_HEADS + N_IDX_HEADS, HEAD_DIM, Q_TILE),
                         lambda b, i: (b, i, 0, 0, 0)),
            pl.BlockSpec((None, ROW_TILE, 512), row),
            pl.BlockSpec((None, ROW_TILE, 256), row),
            pl.BlockSpec((None, ROW_TILE, 128), row),
            pl.BlockSpec((None, ROW_TILE // K_TILE, VT_ROWS, K_TILE), lambda b, i: (b, i, 0, 0)),
        ],
        out_shape=[
            jax.ShapeDtypeStruct((batch, seq, D_RNN), jnp.bfloat16),
            jax.ShapeDtypeStruct((batch, seq // Q_TILE, N_HEADS + N_IDX_HEADS, HEAD_DIM, Q_TILE), jnp.bfloat16),
            jax.ShapeDtypeStruct((batch, seq, 512), jnp.float32),
            jax.ShapeDtypeStruct((batch, seq, 256), jnp.bfloat16),
            jax.ShapeDtypeStruct((batch, seq, 128), jnp.float32),
            jax.ShapeDtypeStruct((batch, seq // K_TILE, VT_ROWS, K_TILE), jnp.bfloat16),
        ],
        scratch_shapes=[
            pltpu.VMEM((D_RNN // 128, ROW_TILE, 128), jnp.float32),
            pltpu.VMEM((8, D_RNN), jnp.float32),
            pltpu.VMEM((1, D_RNN), jnp.float32),
        ],
        compiler_params=pltpu.CompilerParams(
            dimension_semantics=("parallel", "arbitrary"), vmem_limit_bytes=VMEM_LIMIT),
        name="in_proj_rg_lru",
    )(x, mod_l, norm_g.reshape(1, D_MODEL), w_pad, conv_w, conv_b.reshape(1, D_RNN),
      w_gates, b_gates, a_param.reshape(1, D_RNN))


def _rg_lru_tile(xr, gr, cw_ref, cb_ref, wg_ref, bg_ref, ap_ref, perm_ref, tail_ref, h_ref):
    ts, steps, slabs = ROW_TILE, LRU_STEPS, D_RNN // 128
    lanes = lambda k: slice(k * 128, (k + 1) * 128)

    for s in range(8):
        for k in range(slabs):
            perm_ref[k, pl.ds(s, steps, stride=8), :] = xr[s * steps:(s + 1) * steps, lanes(k)]
    x = jnp.concatenate([perm_ref[k] for k in range(slabs)], axis=-1).reshape(steps, 8, D_RNN)

    first = lax.broadcasted_iota(jnp.int32, (8, D_RNN), 0) == 0
    before = [jnp.where(first, tail_ref[8 - m:9 - m, :], pltpu.roll(x[steps - m], 1, 0))
              for m in range(CONV_WIDTH - 1, 0, -1)]
    x_ext = jnp.concatenate([jnp.stack(before), x], axis=0)
    xc = cb_ref[...] + cw_ref[CONV_WIDTH - 1:CONV_WIDTH, :] * x
    for k in range(CONV_WIDTH - 1):
        xc = xc + cw_ref[k:k + 1, :] * x_ext[k:k + steps]
    tail_ref[...] = xr[ts - 8:ts, :]

    xc_bf = xc.reshape(ts, D_RNN).astype(jnp.bfloat16)
    slab_gates = [jnp.dot(xc_bf[:, lanes(k)], wg_ref[k], preferred_element_type=jnp.float32) for k in range(slabs)]
    gates_x = jnp.concatenate([g[:, 0:128] for g in slab_gates], axis=-1) + bg_ref[:, 0:D_RNN]
    gates_a = jnp.concatenate([g[:, 128:256] for g in slab_gates], axis=-1) + bg_ref[:, D_RNN:2 * D_RNN]
    gate_x = _sigmoid(gates_x.reshape(steps, 8, D_RNN))
    gate_a = _sigmoid(gates_a.reshape(steps, 8, D_RNN))
    neg_ap = -ap_ref[...]
    softplus = jnp.maximum(neg_ap, 0.0) + jnp.log(1.0 + jnp.exp(-jnp.abs(neg_ap)))
    log_a = (-LRU_C) * gate_a * softplus
    a = jnp.exp(log_a)
    gap = 1.0 - a * a
    mult = jnp.where(gap > 0.0, gap * lax.rsqrt(gap), 0.0)
    u = mult * gate_x * xc

    h_run, a_run = u[0], a[0]
    h_loc, a_cum = [h_run], [a_run]
    for j in range(1, steps):
        h_run = a[j] * h_run + u[j]
        a_run = a[j] * a_run
        h_loc.append(h_run)
        a_cum.append(a_run)
    state = h_ref[...]
    entering = []
    for s in range(8):
        entering.append(state)
        state = h_run[s:s + 1, :] + a_run[s:s + 1, :] * state
    h_ref[...] = state
    entering = jnp.concatenate(entering, axis=0)
    h = jnp.stack([h_loc[j] + a_cum[j] * entering for j in range(steps)]).reshape(ts, D_RNN)

    for k in range(slabs):
        perm_ref[k] = h[:, lanes(k)]
    h = jnp.concatenate(
        [jnp.concatenate([perm_ref[k, pl.ds(s, steps, stride=8), :] for k in range(slabs)], axis=-1)
         for s in range(8)], axis=0)
    return h * _silu(gr)


def _attn_kernel(qt_ref, kv_ref, vt_ref, wi_ref, ga_ref, tri_ref, y_ref,
                 sc_ref, sc16_ref, bias_ref, eqb_ref, s_ref, m_ref, mnew_ref, acc_ref, out_ref):
    tq, tk = Q_TILE, K_TILE
    topk = float(MAX_TOPK)
    qb = pl.program_id(1)
    n_keys = (qb + 1) * tq
    n_full = n_keys // tk
    has_tail = n_keys - n_full * tk > 0

    w_t = wi_ref[...].T[WI_LANE:WI_LANE + N_IDX_HEADS, :] * ((IDX_DIM ** -0.5) * (N_IDX_HEADS ** -0.5))

    def score_chunk(c, carry, size, diagonal):
        rmax, rmin = carry
        k0 = pl.multiple_of(c * tk, tk)
        ki = kv_ref[pl.ds(k0, size), KI_LANE:KI_LANE + IDX_DIM]
        score = jnp.zeros((size, tq), jnp.float32)
        for h in range(N_IDX_HEADS):
            logits = jnp.dot(ki, qt_ref[N_HEADS + h], preferred_element_type=jnp.float32)
            score = score + jnp.maximum(logits, 0.0) * w_t[h:h + 1, :]
        if diagonal:
            key_pos = k0 + lax.broadcasted_iota(jnp.int32, (size, tq), 0)
            causal = key_pos <= qb * tq + lax.broadcasted_iota(jnp.int32, (size, tq), 1)
            low, high = jnp.where(causal, score, -jnp.inf), jnp.where(causal, score, jnp.inf)
        else:
            low = high = score
        sc_ref[pl.ds(k0, size), :] = low
        sc16_ref[pl.ds(k0, size), :] = low.astype(jnp.bfloat16)
        return jnp.maximum(rmax, _reduce_keys(low, jnp.max)), jnp.minimum(rmin, _reduce_keys(high, jnp.min))

    n_before = jnp.where(has_tail, n_full, n_full - 1)
    extremes = lax.fori_loop(
        0, n_before, functools.partial(score_chunk, size=tk, diagonal=False),
        (jnp.full((1, tq), -jnp.inf, jnp.float32), jnp.full((1, tq), jnp.inf, jnp.float32)))

    def last_chunk(e):
        return lax.cond(has_tail,
                        lambda e: score_chunk(n_full, e, TAIL_KEYS, True),
                        lambda e: score_chunk(n_full - 1, e, tk, True), e)

    def no_scores(e):
        key_pos = lax.broadcasted_iota(jnp.int32, (tq, tq), 0)
        causal = key_pos <= lax.broadcasted_iota(jnp.int32, (tq, tq), 1)
        sc_ref[0:tq, :] = jnp.where(causal, 0.0, -jnp.inf)
        return jnp.zeros((1, tq), jnp.float32), jnp.zeros((1, tq), jnp.float32)

    all_selected = n_keys <= MAX_TOPK
    rmax, rmin = lax.cond(all_selected, no_scores, last_chunk, extremes)

    n_sweep_tiles = jnp.where(all_selected, 0, n_keys // SWEEP_TILE)

    def sweep(mid, want):
        kinds = {"gt": (jnp.sum, 0.0), "ge": (jnp.sum, 0.0), "above": (jnp.min, jnp.inf), "below": (jnp.max, -jnp.inf)}

        def body(c, carry):
            k0 = pl.multiple_of(c * SWEEP_TILE, SWEEP_TILE)
            s = sc_ref[pl.ds(k0, SWEEP_TILE), :].reshape(SWEEP_TILE // PARTIAL_ROWS, PARTIAL_ROWS, tq)
            gt = s > mid
            terms = {"gt": lambda: jnp.where(gt, 1.0, 0.0), "ge": lambda: jnp.where(s >= mid, 1.0, 0.0),
                     "above": lambda: jnp.where(gt, s, jnp.inf), "below": lambda: jnp.where(gt, -jnp.inf, s)}
            out = []
            for name, acc in zip(want, carry):
                op = kinds[name][0]
                part = op(terms[name](), axis=0)
                out.append(acc + part if op is jnp.sum else
                           (jnp.minimum(acc, part) if op is jnp.min else jnp.maximum(acc, part)))
            return tuple(out)

        init = tuple(jnp.full((PARTIAL_ROWS, tq), kinds[name][1], jnp.float32) for name in want)
        res = lax.fori_loop(0, n_sweep_tiles, body, init)
        return [kinds[name][0](r, axis=0, keepdims=True) for name, r in zip(want, res)]

    def update(state, mid, cnt, tie, new_lo, new_hi):
        lo, hi, chi, act = state
        on = act > 0.5
        fin = jnp.logical_and(on, jnp.logical_or(cnt == topk, tie))
        go_on = jnp.logical_and(on, jnp.logical_not(fin))
        up = jnp.logical_and(go_on, cnt > topk)
        dn = jnp.logical_and(go_on, cnt < topk)
        lo = jnp.where(fin, mid, jnp.where(up, new_lo, lo))
        hi = jnp.where(fin, mid, jnp.where(dn, new_hi, hi))
        chi = jnp.where(jnp.logical_or(fin, dn), cnt, chi)
        act = jnp.where(jnp.logical_and(go_on, lo < hi), 1.0, 0.0)
        return lo, hi, chi, act

    def midpoint(state):
        return 0.5 * state[0] + 0.5 * state[1]

    n_valid = (qb * tq + 1 + lax.broadcasted_iota(jnp.int32, (1, tq), 1)).astype(jnp.float32)
    few = n_valid <= topk
    state = (rmin,
             jnp.where(few, -jnp.inf, rmax),
             jnp.where(few, topk, 0.0),
             jnp.where(jnp.logical_or(few, rmin >= rmax), 0.0, 1.0))

    def coarse_count(t16, also_ge=False):
        one, zero = jnp.ones((), jnp.bfloat16), jnp.zeros((), jnp.bfloat16)

        def tile_sum(hits):
            parts = [hits[g] for g in range(SWEEP_TILE // PARTIAL_ROWS)]
            while len(parts) > 1:
                parts = [parts[i] + parts[i + 1] for i in range(0, len(parts), 2)]
            return parts[0]

        def body(c, accs):
            k0 = pl.multiple_of(c * SWEEP_TILE, SWEEP_TILE)
            s = sc16_ref[pl.ds(k0, SWEEP_TILE), :].reshape(SWEEP_TILE // PARTIAL_ROWS, PARTIAL_ROWS, tq)
            out = [accs[0] + tile_sum(jnp.where(s > t16, one, zero))]
            if also_ge:
                out.append(accs[1] + tile_sum(jnp.where(s >= t16, one, zero)))
            return tuple(out)

        init = (jnp.zeros((PARTIAL_ROWS, tq), jnp.bfloat16),) * (2 if also_ge else 1)
        accs = lax.fori_loop(0, n_sweep_tiles, body, init)
        return [jnp.sum(a.astype(jnp.float32), axis=0, keepdims=True) for a in accs]

    lo, hi, chi, act = state
    on = act > 0.5
    pos, nonneg = coarse_count(jnp.zeros((1, tq), jnp.bfloat16), also_ge=True)
    at_zero = jnp.logical_and(on, jnp.logical_and(pos <= topk, nonneg >= topk))
    lo = jnp.where(at_zero, 0.0, jnp.where(jnp.logical_and(on, pos > topk), jnp.maximum(lo, 0.0), lo))
    hi = jnp.where(at_zero, 0.0, jnp.where(jnp.logical_and(on, nonneg < topk), jnp.minimum(hi, 0.0), hi))
    chi = jnp.where(at_zero, pos, chi)
    act = jnp.where(at_zero, 0.0, act)

    for _ in range(COARSE_PROBES):
        t16 = (0.5 * lo + 0.5 * hi).astype(jnp.bfloat16)
        t = t16.astype(jnp.float32)
        (cnt16,) = coarse_count(t16)
        on = act > 0.5
        lo = jnp.where(jnp.logical_and(on, cnt16 >= topk), jnp.maximum(lo, t), lo)
        hi = jnp.where(jnp.logical_and(on, cnt16 < topk), jnp.minimum(hi, t + jnp.abs(t) * BF16_STEP + 1e-30), hi)
    (cnt,) = sweep(hi, ("gt",))
    state = update((lo, hi, chi, act), hi, cnt, False, hi, hi)

    def pending(state):
        return jnp.where(jnp.logical_and(state[3] > 0.5, topk - state[2] != 1.0), 1.0, 0.0)

    def plain_cond(st):
        return jnp.logical_and(st[1] > 0.5, st[0] < PLAIN_BISECT_ROUNDS)

    def plain_body(st):
        state = st[2:]
        for _ in range(PROBES_PER_ROUND):
            mid = midpoint(state)
            (cnt,) = sweep(mid, ("gt",))
            state = update(state, mid, cnt, False, mid, mid)
        return (st[0] + 1, jnp.max(pending(state))) + state

    st = lax.while_loop(plain_cond, plain_body, (jnp.int32(0), jnp.max(pending(state))) + state)
    lo, hi, chi, act = st[2:]
    def two_largest_below(bound):
        groups = SWEEP_TILE // PARTIAL_ROWS

        def merge(a1, a2, b1, b2):
            return jnp.maximum(a1, b1), jnp.maximum(jnp.minimum(a1, b1), jnp.maximum(a2, b2))

        def body(c, carry):
            m1, m2 = carry
            k0 = pl.multiple_of(c * SWEEP_TILE, SWEEP_TILE)
            s = sc_ref[pl.ds(k0, SWEEP_TILE), :].reshape(groups, PARTIAL_ROWS, tq)
            x = jnp.where(s > bound, -jnp.inf, s)
            for g in range(groups):
                m1, m2 = jnp.maximum(m1, x[g]), jnp.maximum(m2, jnp.minimum(m1, x[g]))
            return m1, m2

        low = jnp.full((PARTIAL_ROWS, tq), -jnp.inf, jnp.float32)
        m1, m2 = lax.fori_loop(0, n_sweep_tiles, body, (low, low))
        rows = PARTIAL_ROWS
        while rows > 1:
            rows //= 2
            m1, m2 = merge(m1[:rows], m2[:rows], m1[rows:], m2[rows:])
        return m1, m2

    largest, second = two_largest_below(hi)
    one_short = jnp.logical_and(act > 0.5, topk - chi == 1.0)
    separated = jnp.logical_and(one_short, second < largest)
    new_thr = jnp.where(separated, second, largest)
    state = (jnp.where(one_short, new_thr, lo), jnp.where(one_short, new_thr, hi),
             jnp.where(separated, topk, chi), jnp.where(one_short, 0.0, act))

    def snap_cond(st):
        return jnp.logical_and(st[1] > 0.5, st[0] < 4096)

    def snap_body(st):
        state = st[2:]
        mid = midpoint(state)
        mid = jnp.where(mid >= state[1], state[0], mid)
        cnt, above, below = sweep(mid, ("gt", "above", "below"))
        state = update(state, mid, cnt, False, above, below)
        return (st[0] + 1, jnp.max(state[3])) + state

    st = lax.while_loop(snap_cond, snap_body, (jnp.int32(0), st[1]) + state)
    thr = st[3]
    need = topk - st[4]
    any_tie = jnp.max(need) > 0.5

    m_ref[...] = jnp.full(m_ref.shape, MASKED, jnp.float32)
    acc_ref[...] = jnp.zeros(acc_ref.shape, jnp.float32)
    eqb_ref[...] = jnp.zeros(eqb_ref.shape, jnp.float32)

    def attend_chunk(c, size):
        k0 = pl.multiple_of(c * tk, tk)
        s_idx = sc_ref[pl.ds(k0, size), :]
        bias_ref[0:size, :] = jnp.where(s_idx > thr, 0.0, MASKED)

        @pl.when(any_tie)
        def _():
            eq = s_idx == thr
            eq_f = jnp.where(eq, 1.0, 0.0)
            rank = jnp.dot(tri_ref[0:size, 0:size], eq_f.astype(jnp.bfloat16),
                           preferred_element_type=jnp.float32) + eqb_ref[...]
            tie_bias = jnp.where(jnp.logical_and(eq, rank < need), 0.0, MASKED)
            bias_ref[0:size, :] = jnp.where(s_idx > thr, 0.0, tie_bias)
            eqb_ref[...] = eqb_ref[...] + _reduce_keys(eq_f, jnp.sum)

        kc = kv_ref[pl.ds(k0, size), 0:HEAD_DIM]
        vt = vt_ref[c, :, 0:size]

        def logits(h):
            s = jnp.dot(kc, qt_ref[h], preferred_element_type=jnp.float32) + bias_ref[0:size, :]
            s_ref[h, 0:size, :] = s
            mnew_ref[h] = jnp.maximum(m_ref[h], _reduce_keys(s, jnp.max))

        def weigh(h):
            m_new = mnew_ref[h]
            alpha = jnp.exp2(m_ref[h] - m_new)
            p = jnp.exp2(s_ref[h, 0:size, :] - m_new)
            acc_ref[h] = alpha * acc_ref[h] + jnp.dot(
                vt, p.astype(jnp.bfloat16), preferred_element_type=jnp.float32)
            m_ref[h] = m_new

        for h in range(N_HEADS):
            logits(h)
        for h in range(N_HEADS):
            weigh(h)

    def attend_full(c, _):
        attend_chunk(c, tk)
        return 0

    lax.fori_loop(0, n_full, attend_full, 0)

    @pl.when(has_tail)
    def _():
        attend_chunk(n_full, TAIL_KEYS)

    for h in range(N_HEADS):
        rows = slice(h * HEAD_DIM, (h + 1) * HEAD_DIM)
        out_ref[rows, :] = acc_ref[h, 0:HEAD_DIM, :] / acc_ref[h, HEAD_DIM:HEAD_DIM + 1, :]
    y = out_ref[...].T * _silu(ga_ref[...])
    y_ref[...] = y.astype(y_ref.dtype)


def _dsa_attention(qt, kv, vt, wi, ga, tri):
    batch, seq, _ = kv.shape
    assert seq // PARTIAL_ROWS <= 256, "bf16 hit counters are exact only up to 256 per slot"
    row = lambda b, i: (b, i, 0)
    return pl.pallas_call(
        _attn_kernel,
        grid=(batch, seq // Q_TILE),
        in_specs=[
            pl.BlockSpec((None, None, N_HEADS + N_IDX_HEADS, HEAD_DIM, Q_TILE), lambda b, i: (b, i, 0, 0, 0)),
            pl.BlockSpec((None, seq, 256), lambda b, i: (b, 0, 0)),
            pl.BlockSpec((None, seq // K_TILE, VT_ROWS, K_TILE), lambda b, i: (b, 0, 0, 0)),
            pl.BlockSpec((None, Q_TILE, 128), row),
            pl.BlockSpec((None, Q_TILE, D_ATTN), row),
            pl.BlockSpec((K_TILE, K_TILE), lambda b, i: (0, 0)),
        ],
        out_specs=pl.BlockSpec((None, Q_TILE, D_ATTN), row),
        out_shape=jax.ShapeDtypeStruct((batch, seq, D_ATTN), jnp.bfloat16),
        scratch_shapes=[
            pltpu.VMEM((seq, Q_TILE), jnp.float32),
            pltpu.VMEM((seq, Q_TILE), jnp.bfloat16),
            pltpu.VMEM((K_TILE, Q_TILE), jnp.float32),
            pltpu.VMEM((1, Q_TILE), jnp.float32),
            pltpu.VMEM((N_HEADS, K_TILE, Q_TILE), jnp.float32),
            pltpu.VMEM((N_HEADS, 1, Q_TILE), jnp.float32),
            pltpu.VMEM((N_HEADS, 1, Q_TILE), jnp.float32),
            pltpu.VMEM((N_HEADS, VT_ROWS, Q_TILE), jnp.float32),
            pltpu.VMEM((D_ATTN, Q_TILE), jnp.float32),
        ],
        compiler_params=pltpu.CompilerParams(
            dimension_semantics=("parallel", "parallel"), vmem_limit_bytes=VMEM_LIMIT),
        name="dsa_attention",
    )(qt, kv, vt, wi, ga, tri)


def _outproj_kernel(yr_ref, ya_ref, x_ref, mod_ref, wr_ref, wa_ref, g_ref, o_ref, *, final_norm):
    y = jnp.dot(yr_ref[...], wr_ref[...], preferred_element_type=jnp.float32)
    y = y + jnp.dot(ya_ref[...], wa_ref[...], preferred_element_type=jnp.float32)
    x_new = x_ref[...] + mod_ref[2:3, :] * y
    if final_norm:
        x_new = _rms(x_new, g_ref[...])
    o_ref[...] = x_new


def _out_projection(yr, ya, x, mod_l, w_r, w_a, final_g, final_norm):
    batch, seq, _ = x.shape
    row = lambda b, i: (b, i, 0)
    const = lambda b, i: (0, 0)
    return pl.pallas_call(
        functools.partial(_outproj_kernel, final_norm=final_norm),
        grid=(batch, seq // OUT_ROW_TILE),
        in_specs=[
            pl.BlockSpec((None, OUT_ROW_TILE, D_RNN), row),
            pl.BlockSpec((None, OUT_ROW_TILE, D_ATTN), row),
            pl.BlockSpec((None, OUT_ROW_TILE, D_MODEL), row),
            pl.BlockSpec((None, 3, D_MODEL), lambda b, i: (b, 0, 0)),
            pl.BlockSpec((D_RNN, D_MODEL), const),
            pl.BlockSpec((D_ATTN, D_MODEL), const),
            pl.BlockSpec((1, D_MODEL), const),
        ],
        out_specs=pl.BlockSpec((None, OUT_ROW_TILE, D_MODEL), row),
        out_shape=jax.ShapeDtypeStruct((batch, seq, D_MODEL), jnp.float32),
        compiler_params=pltpu.CompilerParams(
            dimension_semantics=("parallel", "parallel"), vmem_limit_bytes=VMEM_LIMIT),
        name="out_proj",
    )(yr, ya, x, mod_l, w_r, w_a, final_g.reshape(1, D_MODEL))


def _pad_w_in(w):
    return jnp.pad(w.astype(jnp.bfloat16), ((0, 0), (0, 0), (0, Z_COLS - D_IN)))


def _block_diag(w):
    n, c, d = w.shape
    eye = jnp.eye(n, dtype=w.dtype)
    return (eye[:, None, :, None] * w[:, :, None, :]).reshape(n * c, n * d)


def _gate_slabs(w_x, w_a):
    per_slab = 128 // RNN_BLOCK
    slabs = [jnp.concatenate([_block_diag(w_x[k:k + per_slab]), _block_diag(w_a[k:k + per_slab])], axis=-1)
             for k in range(0, N_RNN_BLOCKS, per_slab)]
    return jnp.stack(slabs).astype(jnp.bfloat16)


def kernel(x, c, norm_g, ada_w, ada_b, w_in, conv_w, conv_b, lru_wx, lru_bx, lru_wa, lru_ba, lru_a, w_out, final_g):
    depth = w_in.shape[0]
    mod = _modulation(c, ada_w, ada_b)
    idx = jnp.arange(K_TILE)
    tri = (idx[None, :] < idx[:, None]).astype(jnp.bfloat16)
    w_pad = _pad_w_in(w_in)
    for l in range(depth):
        w_gates = _gate_slabs(lru_wx[l], lru_wa[l])
        b_gates = jnp.concatenate([lru_bx[l], lru_ba[l]]).reshape(1, 2 * D_RNN)
        w_o = w_out[l].astype(jnp.bfloat16)
        y_r, qt, ga, kv, wi, vt = _in_projection(x, mod[l], norm_g[l], w_pad[l], conv_w[l], conv_b[l],
                                                 w_gates, b_gates, lru_a[l])
        y_a = _dsa_attention(qt, kv, vt, wi, ga, tri)
        x = _out_projection(y_r, y_a, x, mod[l], w_o[:D_RNN], w_o[D_RNN:], final_g, l == depth - 1)
    return x
```

```python
import functools

import jax
import jax.numpy as jnp
from jax import lax
from jax.experimental import pallas as pl
from jax.experimental.pallas import tpu as pltpu

D_MODEL = 1024
D_RNN = 512
N_RNN_BLOCKS = 8
RNN_BLOCK = D_RNN // N_RNN_BLOCKS
CONV_WIDTH = 4
LRU_C = 8.0
N_HEADS = 8
HEAD_DIM = 64
D_ATTN = N_HEADS * HEAD_DIM
N_IDX_HEADS = 8
IDX_DIM = 64
MAX_TOPK = 256
EPS = 1e-6
LOG2_E = 1.4426950408889634

Z_RNN = 0
Z_Q = 1024
Z_KV = 1536
Z_GA = 1664
Z_QI = 2176
Z_KI = 2688
Z_COLS = 2816
D_IN = 2760
WI_LANE = 64
KI_LANE = 128
BF16_STEP = 2.0 ** -7

ROW_TILE = 512
OUT_ROW_TILE = 1024
LRU_STEPS = ROW_TILE // 8
Q_TILE = 256
K_TILE = 512
VT_ROWS = 80
MASKED = -1e30
COARSE_PROBES = 9
PROBES_PER_ROUND = 7
PLAIN_BISECT_ROUNDS = 5
PARTIAL_ROWS = 32
SWEEP_TILE = 256
TAIL_KEYS = K_TILE - Q_TILE
assert K_TILE == 2 * Q_TILE
VMEM_LIMIT = 48 * 1024 * 1024


def _sigmoid(x):
    return 0.5 * jnp.tanh(0.5 * x) + 0.5


def _silu(x):
    return x * _sigmoid(x)


def _rms(x, g):
    return x * lax.rsqrt(jnp.mean(x * x, axis=-1, keepdims=True) + EPS) * g


def _reduce_keys(x, op):
    keys, queries = x.shape
    part = op(x.reshape(keys // PARTIAL_ROWS, PARTIAL_ROWS, queries), axis=0)
    return op(part, axis=0, keepdims=True)


def _mod_kernel(c_ref, w_ref, b_ref, o_ref):
    c_act = _silu(c_ref[...])
    o_ref[...] = jnp.dot(c_act, w_ref[...], precision=lax.Precision.HIGHEST,
                         preferred_element_type=jnp.float32) + b_ref[...]


def _modulation(c, ada_w, ada_b):
    depth = ada_w.shape[0]
    batch = c.shape[0]
    out = pl.pallas_call(
        _mod_kernel,
        grid=(depth, 3),
        in_specs=[
            pl.BlockSpec((batch, D_MODEL), lambda l, j: (0, 0)),
            pl.BlockSpec((None, D_MODEL, D_MODEL), lambda l, j: (l, 0, j)),
            pl.BlockSpec((None, None, 1, D_MODEL), lambda l, j: (l, j, 0, 0)),
        ],
        out_specs=pl.BlockSpec((None, None, batch, D_MODEL), lambda l, j: (l, j, 0, 0)),
        out_shape=jax.ShapeDtypeStruct((depth, 3, batch, D_MODEL), jnp.float32),
        compiler_params=pltpu.CompilerParams(vmem_limit_bytes=VMEM_LIMIT),
        name="adaln_mod",
    )(c, ada_w, ada_b.reshape(depth, 3, 1, D_MODEL))
    return out.transpose(0, 2, 1, 3)


def _inproj_kernel(x_ref, mod_ref, g_ref, w_ref, cw_ref, cb_ref, wg_ref, bg_ref, ap_ref,
                   yr_ref, qt_ref, ga_ref, kv_ref, wi_ref, vt_ref, perm_ref, tail_ref, h_ref):
    @pl.when(pl.program_id(1) == 0)
    def _():
        tail_ref[...] = jnp.zeros((8, D_RNN), jnp.float32)
        h_ref[...] = jnp.zeros((1, D_RNN), jnp.float32)

    x = x_ref[...]
    shift = mod_ref[0:1, :]
    scale = mod_ref[1:2, :]
    h = _rms(x, g_ref[...]) * (1.0 + scale) + shift
    h = h.astype(jnp.bfloat16)
    z_rnn = jnp.dot(h, w_ref[:, Z_RNN:Z_Q], preferred_element_type=jnp.float32)
    y_r = _rg_lru_tile(z_rnn[:, 0:D_RNN], z_rnn[:, D_RNN:2 * D_RNN],
                       cw_ref, cb_ref, wg_ref, bg_ref, ap_ref, perm_ref, tail_ref, h_ref)
    yr_ref[...] = y_r.astype(yr_ref.dtype)
    z = jnp.dot(h, w_ref[:, Z_Q:Z_COLS], preferred_element_type=jnp.float32)
    col = lambda start, width: z[:, start - Z_Q:start - Z_Q + width]
    for h in range(N_HEADS):
        q_h = col(Z_Q + h * HEAD_DIM, HEAD_DIM) * (HEAD_DIM ** -0.5 * LOG2_E)
        _store_query_tiles(qt_ref, h, q_h.T.astype(jnp.bfloat16))
    for h in range(N_IDX_HEADS):
        _store_query_tiles(qt_ref, N_HEADS + h, col(Z_QI + h * IDX_DIM, IDX_DIM).T.astype(jnp.bfloat16))
    ga_ref[...] = col(Z_GA, D_ATTN)
    kv_ref[...] = jnp.concatenate([col(Z_KV, 128), col(Z_KI, 128)], axis=-1).astype(jnp.bfloat16)
    wi_ref[...] = col(Z_KI, 128)
    pad_rows = lax.broadcasted_iota(jnp.int32, (VT_ROWS - HEAD_DIM, K_TILE), 0)
    for j in range(ROW_TILE // K_TILE):
        v_t = col(Z_KV + HEAD_DIM, HEAD_DIM)[j * K_TILE:(j + 1) * K_TILE, :].T
        vt_ref[j, 0:HEAD_DIM, :] = v_t.astype(jnp.bfloat16)
        vt_ref[j, HEAD_DIM:VT_ROWS, :] = jnp.where(pad_rows == 0, 1.0, 0.0).astype(jnp.bfloat16)


def _store_query_tiles(qt_ref, head, q_t):
    for j in range(ROW_TILE // Q_TILE):
        qt_ref[j, head] = q_t[:, j * Q_TILE:(j + 1) * Q_TILE]


def _in_projection(x, mod_l, norm_g, w_pad, conv_w, conv_b, w_gates, b_gates, a_param):
    batch, seq, _ = x.shape
    row = lambda b, i: (b, i, 0)
    const = lambda b, i: (0, 0)
    return pl.pallas_call(
        _inproj_kernel,
        grid=(batch, seq // ROW_TILE),
        in_specs=[
            pl.BlockSpec((None, ROW_TILE, D_MODEL), row),
            pl.BlockSpec((None, 3, D_MODEL), lambda b, i: (b, 0, 0)),
            pl.BlockSpec((1, D_MODEL), const),
            pl.BlockSpec((D_MODEL, Z_COLS), const),
            pl.BlockSpec((CONV_WIDTH, D_RNN), const),
            pl.BlockSpec((1, D_RNN), const),
            pl.BlockSpec((D_RNN // 128, 128, 256), lambda b, i: (0, 0, 0)),
            pl.BlockSpec((1, 2 * D_RNN), const),
            pl.BlockSpec((1, D_RNN), const),
        ],
        out_specs=[
            pl.BlockSpec((None, ROW_TILE, D_RNN), row),
            pl.BlockSpec((None, ROW_TILE // Q_TILE, N_HEADS + N_IDX_HEADS, HEAD_DIM, Q_TILE),
                         lambda b, i: (b, i, 0, 0, 0)),
            pl.BlockSpec((None, ROW_TILE, 512), row),
            pl.BlockSpec((None, ROW_TILE, 256), row),
            pl.BlockSpec((None, ROW_TILE, 128), row),
            pl.BlockSpec((None, ROW_TILE // K_TILE, VT_ROWS, K_TILE), lambda b, i: (b, i, 0, 0)),
        ],
        out_shape=[
            jax.ShapeDtypeStruct((batch, seq, D_RNN), jnp.bfloat16),
            jax.ShapeDtypeStruct((batch, seq // Q_TILE, N_HEADS + N_IDX_HEADS, HEAD_DIM, Q_TILE), jnp.bfloat16),
            jax.ShapeDtypeStruct((batch, seq, 512), jnp.float32),
            jax.ShapeDtypeStruct((batch, seq, 256), jnp.bfloat16),
            jax.ShapeDtypeStruct((batch, seq, 128), jnp.float32),
            jax.ShapeDtypeStruct((batch, seq // K_TILE, VT_ROWS, K_TILE), jnp.bfloat16),
        ],
        scratch_shapes=[
            pltpu.VMEM((D_RNN // 128, ROW_TILE, 128), jnp.float32),
            pltpu.VMEM((8, D_RNN), jnp.float32),
            pltpu.VMEM((1, D_RNN), jnp.float32),
        ],
        compiler_params=pltpu.CompilerParams(
            dimension_semantics=("parallel", "arbitrary"), vmem_limit_bytes=VMEM_LIMIT),
        name="in_proj_rg_lru",
    )(x, mod_l, norm_g.reshape(1, D_MODEL), w_pad, conv_w, conv_b.reshape(1, D_RNN),
      w_gates, b_gates, a_param.reshape(1, D_RNN))


def _rg_lru_tile(xr, gr, cw_ref, cb_ref, wg_ref, bg_ref, ap_ref, perm_ref, tail_ref, h_ref):
    ts, steps, slabs = ROW_TILE, LRU_STEPS, D_RNN // 128
    lanes = lambda k: slice(k * 128, (k + 1) * 128)

    for s in range(8):
        for k in range(slabs):
            perm_ref[k, pl.ds(s, steps, stride=8), :] = xr[s * steps:(s + 1) * steps, lanes(k)]
    x = jnp.concatenate([perm_ref[k] for k in range(slabs)], axis=-1).reshape(steps, 8, D_RNN)

    first = lax.broadcasted_iota(jnp.int32, (8, D_RNN), 0) == 0
    before = [jnp.where(first, tail_ref[8 - m:9 - m, :], pltpu.roll(x[steps - m], 1, 0))
              for m in range(CONV_WIDTH - 1, 0, -1)]
    x_ext = jnp.concatenate([jnp.stack(before), x], axis=0)
    xc = cb_ref[...] + cw_ref[CONV_WIDTH - 1:CONV_WIDTH, :] * x
    for k in range(CONV_WIDTH - 1):
        xc = xc + cw_ref[k:k + 1, :] * x_ext[k:k + steps]
    tail_ref[...] = xr[ts - 8:ts, :]

    xc_bf = xc.reshape(ts, D_RNN).astype(jnp.bfloat16)
    slab_gates = [jnp.dot(xc_bf[:, lanes(k)], wg_ref[k], preferred_element_type=jnp.float32) for k in range(slabs)]
    gates_x = jnp.concatenate([g[:, 0:128] for g in slab_gates], axis=-1) + bg_ref[:, 0:D_RNN]
    gates_a = jnp.concatenate([g[:, 128:256] for g in slab_gates], axis=-1) + bg_ref[:, D_RNN:2 * D_RNN]
    gate_x = _sigmoid(gates_x.reshape(steps, 8, D_RNN))
    gate_a = _sigmoid(gates_a.reshape(steps, 8, D_RNN))
    neg_ap = -ap_ref[...]
    softplus = jnp.maximum(neg_ap, 0.0) + jnp.log(1.0 + jnp.exp(-jnp.abs(neg_ap)))
    log_a = (-LRU_C) * gate_a * softplus
    a = jnp.exp(log_a)
    gap = 1.0 - a * a
    mult = jnp.where(gap > 0.0, gap * lax.rsqrt(gap), 0.0)
    u = mult * gate_x * xc

    h_run, a_run = u[0], a[0]
    h_loc, a_cum = [h_run], [a_run]
    for j in range(1, steps):
        h_run = a[j] * h_run + u[j]
        a_run = a[j] * a_run
        h_loc.append(h_run)
        a_cum.append(a_run)
    state = h_ref[...]
    entering = []
    for s in range(8):
        entering.append(state)
        state = h_run[s:s + 1, :] + a_run[s:s + 1, :] * state
    h_ref[...] = state
    entering = jnp.concatenate(entering, axis=0)
    h = jnp.stack([h_loc[j] + a_cum[j] * entering for j in range(steps)]).reshape(ts, D_RNN)

    for k in range(slabs):
        perm_ref[k] = h[:, lanes(k)]
    h = jnp.concatenate(
        [jnp.concatenate([perm_ref[k, pl.ds(s, steps, stride=8), :] for k in range(slabs)], axis=-1)
         for s in range(8)], axis=0)
    return h * _silu(gr)


def _attn_kernel(qt_ref, kv_ref, vt_ref, wi_ref, ga_ref, tri_ref, y_ref,
                 sc_ref, sc16_ref, bias_ref, eqb_ref, s_ref, m_ref, mnew_ref, acc_ref, out_ref):
    tq, tk = Q_TILE, K_TILE
    topk = float(MAX_TOPK)
    qb = pl.program_id(1)
    n_keys = (qb + 1) * tq
    n_full = n_keys // tk
    has_tail = n_keys - n_full * tk > 0

    w_t = wi_ref[...].T[WI_LANE:WI_LANE + N_IDX_HEADS, :] * ((IDX_DIM ** -0.5) * (N_IDX_HEADS ** -0.5))

    def score_chunk(c, carry, size, diagonal):
        rmax, rmin = carry
        k0 = pl.multiple_of(c * tk, tk)
        ki = kv_ref[pl.ds(k0, size), KI_LANE:KI_LANE + IDX_DIM]
        score = jnp.zeros((size, tq), jnp.float32)
        for h in range(N_IDX_HEADS):
            logits = jnp.dot(ki, qt_ref[N_HEADS + h], preferred_element_type=jnp.float32)
            score = score + jnp.maximum(logits, 0.0) * w_t[h:h + 1, :]
        if diagonal:
            key_pos = k0 + lax.broadcasted_iota(jnp.int32, (size, tq), 0)
            causal = key_pos <= qb * tq + lax.broadcasted_iota(jnp.int32, (size, tq), 1)
            low, high = jnp.where(causal, score, -jnp.inf), jnp.where(causal, score, jnp.inf)
        else:
            low = high = score
        sc_ref[pl.ds(k0, size), :] = low
        sc16_ref[pl.ds(k0, size), :] = low.astype(jnp.bfloat16)
        return jnp.maximum(rmax, _reduce_keys(low, jnp.max)), jnp.minimum(rmin, _reduce_keys(high, jnp.min))

    n_before = jnp.where(has_tail, n_full, n_full - 1)
    extremes = lax.fori_loop(
        0, n_before, functools.partial(score_chunk, size=tk, diagonal=False),
        (jnp.full((1, tq), -jnp.inf, jnp.float32), jnp.full((1, tq), jnp.inf, jnp.float32)))

    def last_chunk(e):
        return lax.cond(has_tail,
                        lambda e: score_chunk(n_full, e, TAIL_KEYS, True),
                        lambda e: score_chunk(n_full - 1, e, tk, True), e)

    def no_scores(e):
        key_pos = lax.broadcasted_iota(jnp.int32, (tq, tq), 0)
        causal = key_pos <= lax.broadcasted_iota(jnp.int32, (tq, tq), 1)
        sc_ref[0:tq, :] = jnp.where(causal, 0.0, -jnp.inf)
        return jnp.zeros((1, tq), jnp.float32), jnp.zeros((1, tq), jnp.float32)

    all_selected = n_keys <= MAX_TOPK
    rmax, rmin = lax.cond(all_selected, no_scores, last_chunk, extremes)

    n_sweep_tiles = jnp.where(all_selected, 0, n_keys // SWEEP_TILE)

    def sweep(mid, want):
        kinds = {"gt": (jnp.sum, 0.0), "ge": (jnp.sum, 0.0), "above": (jnp.min, jnp.inf), "below": (jnp.max, -jnp.inf)}

        def body(c, carry):
            k0 = pl.multiple_of(c * SWEEP_TILE, SWEEP_TILE)
            s = sc_ref[pl.ds(k0, SWEEP_TILE), :].reshape(SWEEP_TILE // PARTIAL_ROWS, PARTIAL_ROWS, tq)
            gt = s > mid
            terms = {"gt": lambda: jnp.where(gt, 1.0, 0.0), "ge": lambda: jnp.where(s >= mid, 1.0, 0.0),
                     "above": lambda: jnp.where(gt, s, jnp.inf), "below": lambda: jnp.where(gt, -jnp.inf, s)}
            out = []
            for name, acc in zip(want, carry):
                op = kinds[name][0]
                part = op(terms[name](), axis=0)
                out.append(acc + part if op is jnp.sum else
                           (jnp.minimum(acc, part) if op is jnp.min else jnp.maximum(acc, part)))
            return tuple(out)

        init = tuple(jnp.full((PARTIAL_ROWS, tq), kinds[name][1], jnp.float32) for name in want)
        res = lax.fori_loop(0, n_sweep_tiles, body, init)
        return [kinds[name][0](r, axis=0, keepdims=True) for name, r in zip(want, res)]

    def update(state, mid, cnt, tie, new_lo, new_hi):
        lo, hi, chi, act = state
        on = act > 0.5
        fin = jnp.logical_and(on, jnp.logical_or(cnt == topk, tie))
        go_on = jnp.logical_and(on, jnp.logical_not(fin))
        up = jnp.logical_and(go_on, cnt > topk)
        dn = jnp.logical_and(go_on, cnt < topk)
        lo = jnp.where(fin, mid, jnp.where(up, new_lo, lo))
        hi = jnp.where(fin, mid, jnp.where(dn, new_hi, hi))
        chi = jnp.where(jnp.logical_or(fin, dn), cnt, chi)
        act = jnp.where(jnp.logical_and(go_on, lo < hi), 1.0, 0.0)
        return lo, hi, chi, act

    def midpoint(state):
        return 0.5 * state[0] + 0.5 * state[1]

    n_valid = (qb * tq + 1 + lax.broadcasted_iota(jnp.int32, (1, tq), 1)).astype(jnp.float32)
    few = n_valid <= topk
    state = (rmin,
             jnp.where(few, -jnp.inf, rmax),
             jnp.where(few, topk, 0.0),
             jnp.where(jnp.logical_or(few, rmin >= rmax), 0.0, 1.0))

    def coarse_count(t16, also_ge=False):
        one, zero = jnp.ones((), jnp.bfloat16), jnp.zeros((), jnp.bfloat16)

        def tile_sum(hits):
            parts = [hits[g] for g in range(SWEEP_TILE // PARTIAL_ROWS)]
            while len(parts) > 1:
                parts = [parts[i] + parts[i + 1] for i in range(0, len(parts), 2)]
            return parts[0]

        def body(c, accs):
            k0 = pl.multiple_of(c * SWEEP_TILE, SWEEP_TILE)
            s = sc16_ref[pl.ds(k0, SWEEP_TILE), :].reshape(SWEEP_TILE // PARTIAL_ROWS, PARTIAL_ROWS, tq)
            out = [accs[0] + tile_sum(jnp.where(s > t16, one, zero))]
            if also_ge:
                out.append(accs[1] + tile_sum(jnp.where(s >= t16, one, zero)))
            return tuple(out)

        init = (jnp.zeros((PARTIAL_ROWS, tq), jnp.bfloat16),) * (2 if also_ge else 1)
        accs = lax.fori_loop(0, n_sweep_tiles, body, init)
        return [jnp.sum(a.astype(jnp.float32), axis=0, keepdims=True) for a in accs]

    lo, hi, chi, act = state
    on = act > 0.5
    pos, nonneg = coarse_count(jnp.zeros((1, tq), jnp.bfloat16), also_ge=True)
    at_zero = jnp.logical_and(on, jnp.logical_and(pos <= topk, nonneg >= topk))
    lo = jnp.where(at_zero, 0.0, jnp.where(jnp.logical_and(on, pos > topk), jnp.maximum(lo, 0.0), lo))
    hi = jnp.where(at_zero, 0.0, jnp.where(jnp.logical_and(on, nonneg < topk), jnp.minimum(hi, 0.0), hi))
    chi = jnp.where(at_zero, pos, chi)
    act = jnp.where(at_zero, 0.0, act)

    for _ in range(COARSE_PROBES):
        t16 = (0.5 * lo + 0.5 * hi).astype(jnp.bfloat16)
        t = t16.astype(jnp.float32)
        (cnt16,) = coarse_count(t16)
        on = act > 0.5
        lo = jnp.where(jnp.logical_and(on, cnt16 >= topk), jnp.maximum(lo, t), lo)
        hi = jnp.where(jnp.logical_and(on, cnt16 < topk), jnp.minimum(hi, t + jnp.abs(t) * BF16_STEP + 1e-30), hi)
    (cnt,) = sweep(hi, ("gt",))
    state = update((lo, hi, chi, act), hi, cnt, False, hi, hi)

    def two_largest_below(bound):
        groups = SWEEP_TILE // PARTIAL_ROWS

        def merge(a1, a2, b1, b2):
            return jnp.maximum(a1, b1), jnp.maximum(jnp.minimum(a1, b1), jnp.maximum(a2, b2))

        def body(c, carry):
            m1, m2 = carry
            k0 = pl.multiple_of(c * SWEEP_TILE, SWEEP_TILE)
            s = sc_ref[pl.ds(k0, SWEEP_TILE), :].reshape(groups, PARTIAL_ROWS, tq)
            x = jnp.where(s > bound, -jnp.inf, s)
            for g in range(groups):
                m1, m2 = jnp.maximum(m1, x[g]), jnp.maximum(m2, jnp.minimum(m1, x[g]))
            return m1, m2

        low = jnp.full((PARTIAL_ROWS, tq), -jnp.inf, jnp.float32)
        m1, m2 = lax.fori_loop(0, n_sweep_tiles, body, (low, low))
        rows = PARTIAL_ROWS
        while rows > 1:
            rows //= 2
            m1, m2 = merge(m1[:rows], m2[:rows], m1[rows:], m2[rows:])
        return m1, m2

    def probe_round(state):
        for _ in range(PROBES_PER_ROUND):
            mid = midpoint(state)
            (cnt,) = sweep(mid, ("gt",))
            state = update(state, mid, cnt, False, mid, mid)
        lo, hi, chi, act = state
        largest, second = two_largest_below(hi)
        one_short = jnp.logical_and(act > 0.5, topk - chi == 1.0)
        separated = jnp.logical_and(one_short, second < largest)
        new_thr = jnp.where(separated, second, largest)
        return (jnp.where(one_short, new_thr, lo), jnp.where(one_short, new_thr, hi),
                jnp.where(separated, topk, chi), jnp.where(one_short, 0.0, act))

    def flags(state):
        return jnp.max(state[3]), jnp.max(topk - state[2])

    state = probe_round(state)

    def more_cond(st):
        return jnp.logical_and(st[1] > 0.5, st[0] < PLAIN_BISECT_ROUNDS)

    def more_body(st):
        state = probe_round(st[3:])
        return (st[0] + 1,) + flags(state) + state

    st = lax.while_loop(more_cond, more_body, (jnp.int32(1),) + flags(state) + state)

    def snap_cond(st):
        return jnp.logical_and(st[1] > 0.5, st[0] < 4096)

    def snap_body(st):
        state = st[3:]
        mid = midpoint(state)
        mid = jnp.where(mid >= state[1], state[0], mid)
        cnt, above, below = sweep(mid, ("gt", "above", "below"))
        state = update(state, mid, cnt, False, above, below)
        return (st[0] + 1,) + flags(state) + state

    st = lax.while_loop(snap_cond, snap_body, (jnp.int32(0),) + st[1:])
    thr = st[4]
    need = topk - st[5]
    any_tie = st[2] > 0.5

    m_ref[...] = jnp.full(m_ref.shape, MASKED, jnp.float32)
    acc_ref[...] = jnp.zeros(acc_ref.shape, jnp.float32)
    eqb_ref[...] = jnp.zeros(eqb_ref.shape, jnp.float32)

    def attend_chunk(c, size):
        k0 = pl.multiple_of(c * tk, tk)
        s_idx = sc_ref[pl.ds(k0, size), :]
        bias_ref[0:size, :] = jnp.where(s_idx > thr, 0.0, MASKED)

        @pl.when(any_tie)
        def _():
            eq = s_idx == thr
            eq_f = jnp.where(eq, 1.0, 0.0)
            rank = jnp.dot(tri_ref[0:size, 0:size], eq_f.astype(jnp.bfloat16),
                           preferred_element_type=jnp.float32) + eqb_ref[...]
            tie_bias = jnp.where(jnp.logical_and(eq, rank < need), 0.0, MASKED)
            bias_ref[0:size, :] = jnp.where(s_idx > thr, 0.0, tie_bias)
            eqb_ref[...] = eqb_ref[...] + _reduce_keys(eq_f, jnp.sum)

        kc = kv_ref[pl.ds(k0, size), 0:HEAD_DIM]
        vt = vt_ref[c, :, 0:size]

        def logits(h):
            s = jnp.dot(kc, qt_ref[h], preferred_element_type=jnp.float32) + bias_ref[0:size, :]
            s_ref[h, 0:size, :] = s
            mnew_ref[h] = jnp.maximum(m_ref[h], _reduce_keys(s, jnp.max))

        def weigh(h):
            m_new = mnew_ref[h]
            alpha = jnp.exp2(m_ref[h] - m_new)
            p = jnp.exp2(s_ref[h, 0:size, :] - m_new)
            acc_ref[h] = alpha * acc_ref[h] + jnp.dot(
                vt, p.astype(jnp.bfloat16), preferred_element_type=jnp.float32)
            m_ref[h] = m_new

        for h in range(N_HEADS):
            logits(h)
        for h in range(N_HEADS):
            weigh(h)

    def attend_full(c, _):
        attend_chunk(c, tk)
        return 0

    lax.fori_loop(0, n_full, attend_full, 0)

    @pl.when(has_tail)
    def _():
        attend_chunk(n_full, TAIL_KEYS)

    for h in range(N_HEADS):
        rows = slice(h * HEAD_DIM, (h + 1) * HEAD_DIM)
        out_ref[rows, :] = acc_ref[h, 0:HEAD_DIM, :] / acc_ref[h, HEAD_DIM:HEAD_DIM + 1, :]
    y = out_ref[...].T * _silu(ga_ref[...])
    y_ref[...] = y.astype(y_ref.dtype)


def _dsa_attention(qt, kv, vt, wi, ga, tri):
    batch, seq, _ = kv.shape
    assert seq // PARTIAL_ROWS <= 256, "bf16 hit counters are exact only up to 256 per slot"
    row = lambda b, i: (b, i, 0)
    return pl.pallas_call(
        _attn_kernel,
        grid=(batch, seq // Q_TILE),
        in_specs=[
            pl.BlockSpec((None, None, N_HEADS + N_IDX_HEADS, HEAD_DIM, Q_TILE), lambda b, i: (b, i, 0, 0, 0)),
            pl.BlockSpec((None, seq, 256), lambda b, i: (b, 0, 0)),
            pl.BlockSpec((None, seq // K_TILE, VT_ROWS, K_TILE), lambda b, i: (b, 0, 0, 0)),
            pl.BlockSpec((None, Q_TILE, 128), row),
            pl.BlockSpec((None, Q_TILE, D_ATTN), row),
            pl.BlockSpec((K_TILE, K_TILE), lambda b, i: (0, 0)),
        ],
        out_specs=pl.BlockSpec((None, Q_TILE, D_ATTN), row),
        out_shape=jax.ShapeDtypeStruct((batch, seq, D_ATTN), jnp.bfloat16),
        scratch_shapes=[
            pltpu.VMEM((seq, Q_TILE), jnp.float32),
            pltpu.VMEM((seq, Q_TILE), jnp.bfloat16),
            pltpu.VMEM((K_TILE, Q_TILE), jnp.float32),
            pltpu.VMEM((1, Q_TILE), jnp.float32),
            pltpu.VMEM((N_HEADS, K_TILE, Q_TILE), jnp.float32),
            pltpu.VMEM((N_HEADS, 1, Q_TILE), jnp.float32),
            pltpu.VMEM((N_HEADS, 1, Q_TILE), jnp.float32),
            pltpu.VMEM((N_HEADS, VT_ROWS, Q_TILE), jnp.float32),
            pltpu.VMEM((D_ATTN, Q_TILE), jnp.float32),
        ],
        compiler_params=pltpu.CompilerParams(
            dimension_semantics=("parallel", "parallel"), vmem_limit_bytes=VMEM_LIMIT),
        name="dsa_attention",
    )(qt, kv, vt, wi, ga, tri)


def _outproj_kernel(yr_ref, ya_ref, x_ref, mod_ref, wr_ref, wa_ref, g_ref, o_ref, *, final_norm):
    y = jnp.dot(yr_ref[...], wr_ref[...], preferred_element_type=jnp.float32)
    y = y + jnp.dot(ya_ref[...], wa_ref[...], preferred_element_type=jnp.float32)
    x_new = x_ref[...] + mod_ref[2:3, :] * y
    if final_norm:
        x_new = _rms(x_new, g_ref[...])
    o_ref[...] = x_new


def _out_projection(yr, ya, x, mod_l, w_r, w_a, final_g, final_norm):
    batch, seq, _ = x.shape
    row = lambda b, i: (b, i, 0)
    const = lambda b, i: (0, 0)
    return pl.pallas_call(
        functools.partial(_outproj_kernel, final_norm=final_norm),
        grid=(batch, seq // OUT_ROW_TILE),
        in_specs=[
            pl.BlockSpec((None, OUT_ROW_TILE, D_RNN), row),
            pl.BlockSpec((None, OUT_ROW_TILE, D_ATTN), row),
            pl.BlockSpec((None, OUT_ROW_TILE, D_MODEL), row),
            pl.BlockSpec((None, 3, D_MODEL), lambda b, i: (b, 0, 0)),
            pl.BlockSpec((D_RNN, D_MODEL), const),
            pl.BlockSpec((D_ATTN, D_MODEL), const),
            pl.BlockSpec((1, D_MODEL), const),
        ],
        out_specs=pl.BlockSpec((None, OUT_ROW_TILE, D_MODEL), row),
        out_shape=jax.ShapeDtypeStruct((batch, seq, D_MODEL), jnp.float32),
        compiler_params=pltpu.CompilerParams(
            dimension_semantics=("parallel", "parallel"), vmem_limit_bytes=VMEM_LIMIT),
        name="out_proj",
    )(yr, ya, x, mod_l, w_r, w_a, final_g.reshape(1, D_MODEL))


def _pad_w_in(w):
    return jnp.pad(w.astype(jnp.bfloat16), ((0, 0), (0, 0), (0, Z_COLS - D_IN)))


def _block_diag(w):
    n, c, d = w.shape
    eye = jnp.eye(n, dtype=w.dtype)
    return (eye[:, None, :, None] * w[:, :, None, :]).reshape(n * c, n * d)


def _gate_slabs(w_x, w_a):
    per_slab = 128 // RNN_BLOCK
    slabs = [jnp.concatenate([_block_diag(w_x[k:k + per_slab]), _block_diag(w_a[k:k + per_slab])], axis=-1)
             for k in range(0, N_RNN_BLOCKS, per_slab)]
    return jnp.stack(slabs).astype(jnp.bfloat16)


def kernel(x, c, norm_g, ada_w, ada_b, w_in, conv_w, conv_b, lru_wx, lru_bx, lru_wa, lru_ba, lru_a, w_out, final_g):
    depth = w_in.shape[0]
    mod = _modulation(c, ada_w, ada_b)
    idx = jnp.arange(K_TILE)
    tri = (idx[None, :] < idx[:, None]).astype(jnp.bfloat16)
    w_pad = _pad_w_in(w_in)
    for l in range(depth):
        w_gates = _gate_slabs(lru_wx[l], lru_wa[l])
        b_gates = jnp.concatenate([lru_bx[l], lru_ba[l]]).reshape(1, 2 * D_RNN)
        w_o = w_out[l].astype(jnp.bfloat16)
        y_r, qt, ga, kv, wi, vt = _in_projection(x, mod[l], norm_g[l], w_pad[l], conv_w[l], conv_b[l],
                                                 w_gates, b_gates, lru_a[l])
        y_a = _dsa_attention(qt, kv, vt, wi, ga, tri)
        x = _out_projection(y_r, y_a, x, mod[l], w_o[:D_RNN], w_o[D_RNN:], final_g, l == depth - 1)
    return x
```

```python
import functools

import jax
import jax.numpy as jnp
from jax import lax
from jax.experimental import pallas as pl
from jax.experimental.pallas import tpu as pltpu

D_MODEL = 1024
D_RNN = 512
N_RNN_BLOCKS = 8
RNN_BLOCK = D_RNN // N_RNN_BLOCKS
CONV_WIDTH = 4
LRU_C = 8.0
N_HEADS = 8
HEAD_DIM = 64
D_ATTN = N_HEADS * HEAD_DIM
N_IDX_HEADS = 8
IDX_DIM = 64
MAX_TOPK = 256
EPS = 1e-6
LOG2_E = 1.4426950408889634

Z_RNN = 0
Z_Q = 1024
Z_KV = 1536
Z_GA = 1664
Z_QI = 2176
Z_KI = 2688
Z_COLS = 2816
D_IN = 2760
WI_LANE = 64
KI_LANE = 128
BF16_STEP = 2.0 ** -7

ROW_TILE = 512
OUT_ROW_TILE = 1024
LRU_STEPS = ROW_TILE // 8
Q_TILE = 256
K_TILE = 512
VT_ROWS = 80
MASKED = -1e30
COARSE_PROBES = 9
PROBES_PER_ROUND = 7
LATER_ROUND_PROBES = 3
PLAIN_BISECT_ROUNDS = 5
PARTIAL_ROWS = 32
SWEEP_TILE = 256
TAIL_KEYS = K_TILE - Q_TILE
assert K_TILE == 2 * Q_TILE
assert MAX_TOPK < 2 * Q_TILE
VMEM_LIMIT = 48 * 1024 * 1024


def _sigmoid(x):
    return 0.5 * jnp.tanh(0.5 * x) + 0.5


def _silu(x):
    return x * _sigmoid(x)


def _rms(x, g):
    return x * lax.rsqrt(jnp.mean(x * x, axis=-1, keepdims=True) + EPS) * g


def _reduce_keys(x, op):
    keys, queries = x.shape
    part = op(x.reshape(keys // PARTIAL_ROWS, PARTIAL_ROWS, queries), axis=0)
    return op(part, axis=0, keepdims=True)


def _mod_kernel(c_ref, w_ref, b_ref, o_ref):
    c_act = _silu(c_ref[...])
    o_ref[...] = jnp.dot(c_act, w_ref[...], precision=lax.Precision.HIGHEST,
                         preferred_element_type=jnp.float32) + b_ref[...]


def _modulation(c, ada_w, ada_b):
    depth = ada_w.shape[0]
    batch = c.shape[0]
    out = pl.pallas_call(
        _mod_kernel,
        grid=(depth, 3),
        in_specs=[
            pl.BlockSpec((batch, D_MODEL), lambda l, j: (0, 0)),
            pl.BlockSpec((None, D_MODEL, D_MODEL), lambda l, j: (l, 0, j)),
            pl.BlockSpec((None, None, 1, D_MODEL), lambda l, j: (l, j, 0, 0)),
        ],
        out_specs=pl.BlockSpec((None, None, batch, D_MODEL), lambda l, j: (l, j, 0, 0)),
        out_shape=jax.ShapeDtypeStruct((depth, 3, batch, D_MODEL), jnp.float32),
        compiler_params=pltpu.CompilerParams(vmem_limit_bytes=VMEM_LIMIT),
        name="adaln_mod",
    )(c, ada_w, ada_b.reshape(depth, 3, 1, D_MODEL))
    return out.transpose(0, 2, 1, 3)


def _inproj_kernel(x_ref, mod_ref, g_ref, w_ref, cw_ref, cb_ref, wg_ref, bg_ref, ap_ref,
                   yr_ref, qt_ref, ga_ref, kv_ref, wi_ref, vt_ref, perm_ref, tail_ref, h_ref):
    @pl.when(pl.program_id(1) == 0)
    def _():
        tail_ref[...] = jnp.zeros((8, D_RNN), jnp.float32)
        h_ref[...] = jnp.zeros((1, D_RNN), jnp.float32)

    x = x_ref[...]
    shift = mod_ref[0:1, :]
    scale = mod_ref[1:2, :]
    h = _rms(x, g_ref[...]) * (1.0 + scale) + shift
    h = h.astype(jnp.bfloat16)
    z_rnn = jnp.dot(h, w_ref[:, Z_RNN:Z_Q], preferred_element_type=jnp.float32)
    y_r = _rg_lru_tile(z_rnn[:, 0:D_RNN], z_rnn[:, D_RNN:2 * D_RNN],
                       cw_ref, cb_ref, wg_ref, bg_ref, ap_ref, perm_ref, tail_ref, h_ref)
    yr_ref[...] = y_r.astype(yr_ref.dtype)
    z = jnp.dot(h, w_ref[:, Z_Q:Z_COLS], preferred_element_type=jnp.float32)
    col = lambda start, width: z[:, start - Z_Q:start - Z_Q + width]
    for h in range(N_HEADS):
        q_h = col(Z_Q + h * HEAD_DIM, HEAD_DIM) * (HEAD_DIM ** -0.5 * LOG2_E)
        _store_query_tiles(qt_ref, h, q_h.T.astype(jnp.bfloat16))
    for h in range(N_IDX_HEADS):
        _store_query_tiles(qt_ref, N_HEADS + h, col(Z_QI + h * IDX_DIM, IDX_DIM).T.astype(jnp.bfloat16))
    ga_ref[...] = col(Z_GA, D_ATTN)
    kv_ref[...] = jnp.concatenate([col(Z_KV, 128), col(Z_KI, 128)], axis=-1).astype(jnp.bfloat16)
    wi_ref[...] = col(Z_KI, 128)
    pad_rows = lax.broadcasted_iota(jnp.int32, (VT_ROWS - HEAD_DIM, K_TILE), 0)
    for j in range(ROW_TILE // K_TILE):
        v_t = col(Z_KV + HEAD_DIM, HEAD_DIM)[j * K_TILE:(j + 1) * K_TILE, :].T
        vt_ref[j, 0:HEAD_DIM, :] = v_t.astype(jnp.bfloat16)
        vt_ref[j, HEAD_DIM:VT_ROWS, :] = jnp.where(pad_rows == 0, 1.0, 0.0).astype(jnp.bfloat16)


def _store_query_tiles(qt_ref, head, q_t):
    for j in range(ROW_TILE // Q_TILE):
        qt_ref[j, head] = q_t[:, j * Q_TILE:(j + 1) * Q_TILE]


def _in_projection(x, mod_l, norm_g, w_pad, conv_w, conv_b, w_gates, b_gates, a_param):
    batch, seq, _ = x.shape
    row = lambda b, i: (b, i, 0)
    const = lambda b, i: (0, 0)
    return pl.pallas_call(
        _inproj_kernel,
        grid=(batch, seq // ROW_TILE),
        in_specs=[
            pl.BlockSpec((None, ROW_TILE, D_MODEL), row),
            pl.BlockSpec((None, 3, D_MODEL), lambda b, i: (b, 0, 0)),
            pl.BlockSpec((1, D_MODEL), const),
            pl.BlockSpec((D_MODEL, Z_COLS), const),
            pl.BlockSpec((CONV_WIDTH, D_RNN), const),
            pl.BlockSpec((1, D_RNN), const),
            pl.BlockSpec((D_RNN // 128, 128, 256), lambda b, i: (0, 0, 0)),
            pl.BlockSpec((1, 2 * D_RNN), const),
            pl.BlockSpec((1, D_RNN), const),
        ],
        out_specs=[
            pl.BlockSpec((None, ROW_TILE, D_RNN), row),
            pl.BlockSpec((None, ROW_TILE // Q_TILE, N_HEADS + N_IDX_HEADS, HEAD_DIM, Q_TILE),
                         lambda b, i: (b, i, 0, 0, 0)),
            pl.BlockSpec((None, ROW_TILE, 512), row),
            pl.BlockSpec((None, ROW_TILE, 256), row),
            pl.BlockSpec((None, ROW_TILE, 128), row),
            pl.BlockSpec((None, ROW_TILE // K_TILE, VT_ROWS, K_TILE), lambda b, i: (b, i, 0, 0)),
        ],
        out_shape=[
            jax.ShapeDtypeStruct((batch, seq, D_RNN), jnp.bfloat16),
            jax.ShapeDtypeStruct((batch, seq // Q_TILE, N_HEADS + N_IDX_HEADS, HEAD_DIM, Q_TILE), jnp.bfloat16),
            jax.ShapeDtypeStruct((batch, seq, 512), jnp.float32),
            jax.ShapeDtypeStruct((batch, seq, 256), jnp.bfloat16),
            jax.ShapeDtypeStruct((batch, seq, 128), jnp.float32),
            jax.ShapeDtypeStruct((batch, seq // K_TILE, VT_ROWS, K_TILE), jnp.bfloat16),
        ],
        scratch_shapes=[
            pltpu.VMEM((D_RNN // 128, ROW_TILE, 128), jnp.float32),
            pltpu.VMEM((8, D_RNN), jnp.float32),
            pltpu.VMEM((1, D_RNN), jnp.float32),
        ],
        compiler_params=pltpu.CompilerParams(
            dimension_semantics=("parallel", "arbitrary"), vmem_limit_bytes=VMEM_LIMIT),
        name="in_proj_rg_lru",
    )(x, mod_l, norm_g.reshape(1, D_MODEL), w_pad, conv_w, conv_b.reshape(1, D_RNN),
      w_gates, b_gates, a_param.reshape(1, D_RNN))


def _rg_lru_tile(xr, gr, cw_ref, cb_ref, wg_ref, bg_ref, ap_ref, perm_ref, tail_ref, h_ref):
    ts, steps, slabs = ROW_TILE, LRU_STEPS, D_RNN // 128
    lanes = lambda k: slice(k * 128, (k + 1) * 128)

    for s in range(8):
        for k in range(slabs):
            perm_ref[k, pl.ds(s, steps, stride=8), :] = xr[s * steps:(s + 1) * steps, lanes(k)]
    x = jnp.concatenate([perm_ref[k] for k in range(slabs)], axis=-1).reshape(steps, 8, D_RNN)

    first = lax.broadcasted_iota(jnp.int32, (8, D_RNN), 0) == 0
    before = [jnp.where(first, tail_ref[8 - m:9 - m, :], pltpu.roll(x[steps - m], 1, 0))
              for m in range(CONV_WIDTH - 1, 0, -1)]
    x_ext = jnp.concatenate([jnp.stack(before), x], axis=0)
    xc = cb_ref[...] + cw_ref[CONV_WIDTH - 1:CONV_WIDTH, :] * x
    for k in range(CONV_WIDTH - 1):
        xc = xc + cw_ref[k:k + 1, :] * x_ext[k:k + steps]
    tail_ref[...] = xr[ts - 8:ts, :]

    xc_bf = xc.reshape(ts, D_RNN).astype(jnp.bfloat16)
    slab_gates = [jnp.dot(xc_bf[:, lanes(k)], wg_ref[k], preferred_element_type=jnp.float32) for k in range(slabs)]
    gates_x = jnp.concatenate([g[:, 0:128] for g in slab_gates], axis=-1) + bg_ref[:, 0:D_RNN]
    gates_a = jnp.concatenate([g[:, 128:256] for g in slab_gates], axis=-1) + bg_ref[:, D_RNN:2 * D_RNN]
    gate_x = _sigmoid(gates_x.reshape(steps, 8, D_RNN))
    gate_a = _sigmoid(gates_a.reshape(steps, 8, D_RNN))
    neg_ap = -ap_ref[...]
    softplus = jnp.maximum(neg_ap, 0.0) + jnp.log(1.0 + jnp.exp(-jnp.abs(neg_ap)))
    log_a = (-LRU_C) * gate_a * softplus
    a = jnp.exp(log_a)
    gap = 1.0 - a * a
    mult = jnp.where(gap > 0.0, gap * lax.rsqrt(gap), 0.0)
    u = mult * gate_x * xc

    h_run, a_run = u[0], a[0]
    h_loc, a_cum = [h_run], [a_run]
    for j in range(1, steps):
        h_run = a[j] * h_run + u[j]
        a_run = a[j] * a_run
        h_loc.append(h_run)
        a_cum.append(a_run)
    state = h_ref[...]
    entering = []
    for s in range(8):
        entering.append(state)
        state = h_run[s:s + 1, :] + a_run[s:s + 1, :] * state
    h_ref[...] = state
    entering = jnp.concatenate(entering, axis=0)
    h = jnp.stack([h_loc[j] + a_cum[j] * entering for j in range(steps)]).reshape(ts, D_RNN)

    for k in range(slabs):
        perm_ref[k] = h[:, lanes(k)]
    h = jnp.concatenate(
        [jnp.concatenate([perm_ref[k, pl.ds(s, steps, stride=8), :] for k in range(slabs)], axis=-1)
         for s in range(8)], axis=0)
    return h * _silu(gr)


def _attn_kernel(qt_ref, kv_ref, vt_ref, wi_ref, ga_ref, tri_ref, y_ref,
                 sc_ref, sc16_ref, bias_ref, eqb_ref, s_ref, m_ref, mnew_ref, acc_ref, out_ref):
    tq, tk = Q_TILE, K_TILE
    topk = float(MAX_TOPK)
    qb = pl.program_id(1)
    n_keys = (qb + 1) * tq
    n_full = n_keys // tk
    has_tail = n_keys - n_full * tk > 0

    w_t = wi_ref[...].T[WI_LANE:WI_LANE + N_IDX_HEADS, :] * ((IDX_DIM ** -0.5) * (N_IDX_HEADS ** -0.5))

    def score_chunk(c, carry, size, diagonal):
        rmax, rmin = carry
        k0 = pl.multiple_of(c * tk, tk)
        ki = kv_ref[pl.ds(k0, size), KI_LANE:KI_LANE + IDX_DIM]
        score = jnp.zeros((size, tq), jnp.float32)
        for h in range(N_IDX_HEADS):
            logits = jnp.dot(ki, qt_ref[N_HEADS + h], preferred_element_type=jnp.float32)
            score = score + jnp.maximum(logits, 0.0) * w_t[h:h + 1, :]
        if diagonal:
            key_pos = k0 + lax.broadcasted_iota(jnp.int32, (size, tq), 0)
            causal = key_pos <= qb * tq + lax.broadcasted_iota(jnp.int32, (size, tq), 1)
            low, high = jnp.where(causal, score, -jnp.inf), jnp.where(causal, score, jnp.inf)
        else:
            low = high = score
        sc_ref[pl.ds(k0, size), :] = low
        sc16_ref[pl.ds(k0, size), :] = low.astype(jnp.bfloat16)
        return jnp.maximum(rmax, _reduce_keys(low, jnp.max)), jnp.minimum(rmin, _reduce_keys(high, jnp.min))

    n_before = jnp.where(has_tail, n_full, n_full - 1)
    extremes = lax.fori_loop(
        0, n_before, functools.partial(score_chunk, size=tk, diagonal=False),
        (jnp.full((1, tq), -jnp.inf, jnp.float32), jnp.full((1, tq), jnp.inf, jnp.float32)))

    def last_chunk(e):
        return lax.cond(has_tail,
                        lambda e: score_chunk(n_full, e, TAIL_KEYS, True),
                        lambda e: score_chunk(n_full - 1, e, tk, True), e)

    def no_scores(e):
        key_pos = lax.broadcasted_iota(jnp.int32, (tq, tq), 0)
        causal = key_pos <= lax.broadcasted_iota(jnp.int32, (tq, tq), 1)
        sc_ref[0:tq, :] = jnp.where(causal, 0.0, -jnp.inf)
        return jnp.zeros((1, tq), jnp.float32), jnp.zeros((1, tq), jnp.float32)

    all_selected = n_keys <= MAX_TOPK
    rmax, rmin = lax.cond(all_selected, no_scores, last_chunk, extremes)

    n_sweep_tiles = jnp.where(all_selected, 0, n_keys // SWEEP_TILE)

    def sweep(mid, want):
        kinds = {"gt": (jnp.sum, 0.0), "ge": (jnp.sum, 0.0), "above": (jnp.min, jnp.inf), "below": (jnp.max, -jnp.inf)}

        def body(c, carry):
            k0 = pl.multiple_of(c * SWEEP_TILE, SWEEP_TILE)
            s = sc_ref[pl.ds(k0, SWEEP_TILE), :].reshape(SWEEP_TILE // PARTIAL_ROWS, PARTIAL_ROWS, tq)
            gt = s > mid
            terms = {"gt": lambda: jnp.where(gt, 1.0, 0.0), "ge": lambda: jnp.where(s >= mid, 1.0, 0.0),
                     "above": lambda: jnp.where(gt, s, jnp.inf), "below": lambda: jnp.where(gt, -jnp.inf, s)}
            out = []
            for name, acc in zip(want, carry):
                op = kinds[name][0]
                part = op(terms[name](), axis=0)
                out.append(acc + part if op is jnp.sum else
                           (jnp.minimum(acc, part) if op is jnp.min else jnp.maximum(acc, part)))
            return tuple(out)

        init = tuple(jnp.full((PARTIAL_ROWS, tq), kinds[name][1], jnp.float32) for name in want)
        res = lax.fori_loop(0, n_sweep_tiles, body, init)
        return [kinds[name][0](r, axis=0, keepdims=True) for name, r in zip(want, res)]

    def update(state, mid, cnt, tie, new_lo, new_hi):
        lo, hi, chi, act = state
        on = act > 0.5
        fin = jnp.logical_and(on, jnp.logical_or(cnt == topk, tie))
        go_on = jnp.logical_and(on, jnp.logical_not(fin))
        up = jnp.logical_and(go_on, cnt > topk)
        dn = jnp.logical_and(go_on, cnt < topk)
        lo = jnp.where(fin, mid, jnp.where(up, new_lo, lo))
        hi = jnp.where(fin, mid, jnp.where(dn, new_hi, hi))
        chi = jnp.where(jnp.logical_or(fin, dn), cnt, chi)
        act = jnp.where(jnp.logical_and(go_on, lo < hi), 1.0, 0.0)
        return lo, hi, chi, act

    def midpoint(state):
        return 0.5 * state[0] + 0.5 * state[1]

    n_valid = (qb * tq + 1 + lax.broadcasted_iota(jnp.int32, (1, tq), 1)).astype(jnp.float32)
    few = n_valid <= topk
    state = (rmin,
             jnp.where(few, -jnp.inf, rmax),
             jnp.where(few, topk, 0.0),
             jnp.where(jnp.logical_or(few, rmin >= rmax), 0.0, 1.0))

    def coarse_count(t16, also_ge=False):
        one, zero = jnp.ones((), jnp.bfloat16), jnp.zeros((), jnp.bfloat16)

        def tile_sum(hits):
            parts = [hits[g] for g in range(SWEEP_TILE // PARTIAL_ROWS)]
            while len(parts) > 1:
                parts = [parts[i] + parts[i + 1] for i in range(0, len(parts), 2)]
            return parts[0]

        def body(c, accs):
            k0 = pl.multiple_of(c * SWEEP_TILE, SWEEP_TILE)
            s = sc16_ref[pl.ds(k0, SWEEP_TILE), :].reshape(SWEEP_TILE // PARTIAL_ROWS, PARTIAL_ROWS, tq)
            out = [accs[0] + tile_sum(jnp.where(s > t16, one, zero))]
            if also_ge:
                out.append(accs[1] + tile_sum(jnp.where(s >= t16, one, zero)))
            return tuple(out)

        init = (jnp.zeros((PARTIAL_ROWS, tq), jnp.bfloat16),) * (2 if also_ge else 1)
        accs = lax.fori_loop(0, n_sweep_tiles, body, init)
        return [jnp.sum(a.astype(jnp.float32), axis=0, keepdims=True) for a in accs]

    lo, hi, chi, act = state
    on = act > 0.5
    pos, nonneg = coarse_count(jnp.zeros((1, tq), jnp.bfloat16), also_ge=True)
    at_zero = jnp.logical_and(on, jnp.logical_and(pos <= topk, nonneg >= topk))
    lo = jnp.where(at_zero, 0.0, jnp.where(jnp.logical_and(on, pos > topk), jnp.maximum(lo, 0.0), lo))
    hi = jnp.where(at_zero, 0.0, jnp.where(jnp.logical_and(on, nonneg < topk), jnp.minimum(hi, 0.0), hi))
    chi = jnp.where(at_zero, pos, chi)
    act = jnp.where(at_zero, 0.0, act)

    for _ in range(COARSE_PROBES):
        t16 = (0.5 * lo + 0.5 * hi).astype(jnp.bfloat16)
        t = t16.astype(jnp.float32)
        (cnt16,) = coarse_count(t16)
        on = act > 0.5
        lo = jnp.where(jnp.logical_and(on, cnt16 >= topk), jnp.maximum(lo, t), lo)
        hi = jnp.where(jnp.logical_and(on, cnt16 < topk), jnp.minimum(hi, t + jnp.abs(t) * BF16_STEP + 1e-30), hi)
    (cnt,) = sweep(hi, ("gt",))
    state = update((lo, hi, chi, act), hi, cnt, False, hi, hi)

    def two_largest_below(bound):
        groups = SWEEP_TILE // PARTIAL_ROWS

        def merge(a1, a2, b1, b2):
            return jnp.maximum(a1, b1), jnp.maximum(jnp.minimum(a1, b1), jnp.maximum(a2, b2))

        def body(c, carry):
            m1, m2 = carry
            k0 = pl.multiple_of(c * SWEEP_TILE, SWEEP_TILE)
            s = sc_ref[pl.ds(k0, SWEEP_TILE), :].reshape(groups, PARTIAL_ROWS, tq)
            x = jnp.where(s > bound, -jnp.inf, s)
            for g in range(groups):
                m1, m2 = jnp.maximum(m1, x[g]), jnp.maximum(m2, jnp.minimum(m1, x[g]))
            return m1, m2

        low = jnp.full((PARTIAL_ROWS, tq), -jnp.inf, jnp.float32)
        m1, m2 = lax.fori_loop(0, n_sweep_tiles, body, (low, low))
        rows = PARTIAL_ROWS
        while rows > 1:
            rows //= 2
            m1, m2 = merge(m1[:rows], m2[:rows], m1[rows:], m2[rows:])
        return m1, m2

    def probe_round(state, probes):
        for _ in range(probes):
            mid = midpoint(state)
            (cnt,) = sweep(mid, ("gt",))
            state = update(state, mid, cnt, False, mid, mid)
        lo, hi, chi, act = state
        largest, second = two_largest_below(hi)
        one_short = jnp.logical_and(act > 0.5, topk - chi == 1.0)
        separated = jnp.logical_and(one_short, second < largest)
        new_thr = jnp.where(separated, second, largest)
        return (jnp.where(one_short, new_thr, lo), jnp.where(one_short, new_thr, hi),
                jnp.where(separated, topk, chi), jnp.where(one_short, 0.0, act))

    def flags(state):
        return jnp.max(state[3]), jnp.max(topk - state[2])

    state = probe_round(state, PROBES_PER_ROUND)

    def more_cond(st):
        return jnp.logical_and(st[1] > 0.5, st[0] < PLAIN_BISECT_ROUNDS)

    def more_body(st):
        state = probe_round(st[3:], LATER_ROUND_PROBES)
        return (st[0] + 1,) + flags(state) + state

    st = lax.while_loop(more_cond, more_body, (jnp.int32(1),) + flags(state) + state)

    def snap_cond(st):
        return jnp.logical_and(st[1] > 0.5, st[0] < 4096)

    def snap_body(st):
        state = st[3:]
        mid = midpoint(state)
        mid = jnp.where(mid >= state[1], state[0], mid)
        cnt, above, below = sweep(mid, ("gt", "above", "below"))
        state = update(state, mid, cnt, False, above, below)
        return (st[0] + 1,) + flags(state) + state

    st = lax.while_loop(snap_cond, snap_body, (jnp.int32(0),) + st[1:])
    thr = st[4]
    need = topk - st[5]
    any_tie = st[2] > 0.5

    m_ref[...] = jnp.full(m_ref.shape, MASKED, jnp.float32)
    acc_ref[...] = jnp.zeros(acc_ref.shape, jnp.float32)
    eqb_ref[...] = jnp.zeros(eqb_ref.shape, jnp.float32)

    def attend_chunk(c, size):
        k0 = pl.multiple_of(c * tk, tk)
        s_idx = sc_ref[pl.ds(k0, size), :]
        bias_ref[0:size, :] = jnp.where(s_idx > thr, 0.0, MASKED)

        @pl.when(any_tie)
        def _():
            eq = s_idx == thr
            eq_f = jnp.where(eq, 1.0, 0.0)
            rank = jnp.dot(tri_ref[0:size, 0:size], eq_f.astype(jnp.bfloat16),
                           preferred_element_type=jnp.float32) + eqb_ref[...]
            tie_bias = jnp.where(jnp.logical_and(eq, rank < need), 0.0, MASKED)
            bias_ref[0:size, :] = jnp.where(s_idx > thr, 0.0, tie_bias)
            eqb_ref[...] = eqb_ref[...] + _reduce_keys(eq_f, jnp.sum)

        kc = kv_ref[pl.ds(k0, size), 0:HEAD_DIM]
        vt = vt_ref[c, :, 0:size]

        def logits(h):
            s = jnp.dot(kc, qt_ref[h], preferred_element_type=jnp.float32) + bias_ref[0:size, :]
            s_ref[h, 0:size, :] = s
            mnew_ref[h] = jnp.maximum(m_ref[h], _reduce_keys(s, jnp.max))

        def weigh(h):
            m_new = mnew_ref[h]
            alpha = jnp.exp2(m_ref[h] - m_new)
            p = jnp.exp2(s_ref[h, 0:size, :] - m_new)
            acc_ref[h] = alpha * acc_ref[h] + jnp.dot(
                vt, p.astype(jnp.bfloat16), preferred_element_type=jnp.float32)
            m_ref[h] = m_new

        for h in range(N_HEADS):
            logits(h)
        for h in range(N_HEADS):
            weigh(h)

    def attend_full(c, _):
        attend_chunk(c, tk)
        return 0

    lax.fori_loop(0, n_full, attend_full, 0)

    @pl.when(has_tail)
    def _():
        attend_chunk(n_full, TAIL_KEYS)

    for h in range(N_HEADS):
        rows = slice(h * HEAD_DIM, (h + 1) * HEAD_DIM)
        out_ref[rows, :] = acc_ref[h, 0:HEAD_DIM, :] / acc_ref[h, HEAD_DIM:HEAD_DIM + 1, :]
    y = out_ref[...].T * _silu(ga_ref[...])
    y_ref[...] = y.astype(y_ref.dtype)


def _dsa_attention(qt, kv, vt, wi, ga, tri):
    batch, seq, _ = kv.shape
    assert seq // PARTIAL_ROWS <= 256, "bf16 hit counters are exact only up to 256 per slot"
    row = lambda b, i: (b, i, 0)
    return pl.pallas_call(
        _attn_kernel,
        grid=(batch, seq // Q_TILE),
        in_specs=[
            pl.BlockSpec((None, None, N_HEADS + N_IDX_HEADS, HEAD_DIM, Q_TILE), lambda b, i: (b, i, 0, 0, 0)),
            pl.BlockSpec((None, seq, 256), lambda b, i: (b, 0, 0)),
            pl.BlockSpec((None, seq // K_TILE, VT_ROWS, K_TILE), lambda b, i: (b, 0, 0, 0)),
            pl.BlockSpec((None, Q_TILE, 128), row),
            pl.BlockSpec((None, Q_TILE, D_ATTN), row),
            pl.BlockSpec((K_TILE, K_TILE), lambda b, i: (0, 0)),
        ],
        out_specs=pl.BlockSpec((None, Q_TILE, D_ATTN), row),
        out_shape=jax.ShapeDtypeStruct((batch, seq, D_ATTN), jnp.bfloat16),
        scratch_shapes=[
            pltpu.VMEM((seq, Q_TILE), jnp.float32),
            pltpu.VMEM((seq, Q_TILE), jnp.bfloat16),
            pltpu.VMEM((K_TILE, Q_TILE), jnp.float32),
            pltpu.VMEM((1, Q_TILE), jnp.float32),
            pltpu.VMEM((N_HEADS, K_TILE, Q_TILE), jnp.float32),
            pltpu.VMEM((N_HEADS, 1, Q_TILE), jnp.float32),
            pltpu.VMEM((N_HEADS, 1, Q_TILE), jnp.float32),
            pltpu.VMEM((N_HEADS, VT_ROWS, Q_TILE), jnp.float32),
            pltpu.VMEM((D_ATTN, Q_TILE), jnp.float32),
        ],
        compiler_params=pltpu.CompilerParams(
            dimension_semantics=("parallel", "parallel"), vmem_limit_bytes=VMEM_LIMIT),
        name="dsa_attention",
    )(qt, kv, vt, wi, ga, tri)


def _outproj_kernel(yr_ref, ya_ref, x_ref, mod_ref, wr_ref, wa_ref, g_ref, o_ref, *, final_norm):
    y = jnp.dot(yr_ref[...], wr_ref[...], preferred_element_type=jnp.float32)
    y = y + jnp.dot(ya_ref[...], wa_ref[...], preferred_element_type=jnp.float32)
    x_new = x_ref[...] + mod_ref[2:3, :] * y
    if final_norm:
        x_new = _rms(x_new, g_ref[...])
    o_ref[...] = x_new


def _out_projection(yr, ya, x, mod_l, w_r, w_a, final_g, final_norm):
    batch, seq, _ = x.shape
    row = lambda b, i: (b, i, 0)
    const = lambda b, i: (0, 0)
    return pl.pallas_call(
        functools.partial(_outproj_kernel, final_norm=final_norm),
        grid=(batch, seq // OUT_ROW_TILE),
        in_specs=[
            pl.BlockSpec((None, OUT_ROW_TILE, D_RNN), row),
            pl.BlockSpec((None, OUT_ROW_TILE, D_ATTN), row),
            pl.BlockSpec((None, OUT_ROW_TILE, D_MODEL), row),
            pl.BlockSpec((None, 3, D_MODEL), lambda b, i: (b, 0, 0)),
            pl.BlockSpec((D_RNN, D_MODEL), const),
            pl.BlockSpec((D_ATTN, D_MODEL), const),
            pl.BlockSpec((1, D_MODEL), const),
        ],
        out_specs=pl.BlockSpec((None, OUT_ROW_TILE, D_MODEL), row),
        out_shape=jax.ShapeDtypeStruct((batch, seq, D_MODEL), jnp.float32),
        compiler_params=pltpu.CompilerParams(
            dimension_semantics=("parallel", "parallel"), vmem_limit_bytes=VMEM_LIMIT),
        name="out_proj",
    )(yr, ya, x, mod_l, w_r, w_a, final_g.reshape(1, D_MODEL))


def _pad_w_in(w):
    return jnp.pad(w.astype(jnp.bfloat16), ((0, 0), (0, 0), (0, Z_COLS - D_IN)))


def _block_diag(w):
    n, c, d = w.shape
    eye = jnp.eye(n, dtype=w.dtype)
    return (eye[:, None, :, None] * w[:, :, None, :]).reshape(n * c, n * d)


def _gate_slabs(w_x, w_a):
    per_slab = 128 // RNN_BLOCK
    slabs = [jnp.concatenate([_block_diag(w_x[k:k + per_slab]), _block_diag(w_a[k:k + per_slab])], axis=-1)
             for k in range(0, N_RNN_BLOCKS, per_slab)]
    return jnp.stack(slabs).astype(jnp.bfloat16)


def kernel(x, c, norm_g, ada_w, ada_b, w_in, conv_w, conv_b, lru_wx, lru_bx, lru_wa, lru_ba, lru_a, w_out, final_g):
    depth = w_in.shape[0]
    mod = _modulation(c, ada_w, ada_b)
    idx = jnp.arange(K_TILE)
    tri = (idx[None, :] < idx[:, None]).astype(jnp.bfloat16)
    w_pad = _pad_w_in(w_in)
    for l in range(depth):
        w_gates = _gate_slabs(lru_wx[l], lru_wa[l])
        b_gates = jnp.concatenate([lru_bx[l], lru_ba[l]]).reshape(1, 2 * D_RNN)
        w_o = w_out[l].astype(jnp.bfloat16)
        y_r, qt, ga, kv, wi, vt = _in_projection(x, mod[l], norm_g[l], w_pad[l], conv_w[l], conv_b[l],
                                                 w_gates, b_gates, lru_a[l])
        y_a = _dsa_attention(qt, kv, vt, wi, ga, tri)
        x = _out_projection(y_r, y_a, x, mod[l], w_o[:D_RNN], w_o[D_RNN:], final_g, l == depth - 1)
    return x
```

```python
import functools

import jax
import jax.numpy as jnp
from jax import lax
from jax.experimental import pallas as pl
from jax.experimental.pallas import tpu as pltpu

D_MODEL = 1024
D_RNN = 512
N_RNN_BLOCKS = 8
RNN_BLOCK = D_RNN // N_RNN_BLOCKS
CONV_WIDTH = 4
LRU_C = 8.0
N_HEADS = 8
HEAD_DIM = 64
D_ATTN = N_HEADS * HEAD_DIM
N_IDX_HEADS = 8
IDX_DIM = 64
MAX_TOPK = 256
EPS = 1e-6
LOG2_E = 1.4426950408889634

Z_RNN = 0
Z_Q = 1024
Z_KV = 1536
Z_GA = 1664
Z_QI = 2176
Z_KI = 2688
Z_COLS = 2816
D_IN = 2760
WI_LANE = 64
KI_LANE = 128
BF16_STEP = 2.0 ** -7

ROW_TILE = 512
OUT_ROW_TILE = 1024
LRU_STEPS = ROW_TILE // 8
Q_TILE = 256
K_TILE = 512
VT_ROWS = 80
MASKED = -1e30
COARSE_PROBES = 9
PROBES_PER_ROUND = 7
LATER_ROUND_PROBES = 3
PLAIN_BISECT_ROUNDS = 5
PARTIAL_ROWS = 32
SWEEP_TILE = 256
TAIL_KEYS = K_TILE - Q_TILE
assert K_TILE == 2 * Q_TILE
assert MAX_TOPK < 2 * Q_TILE
VMEM_LIMIT = 48 * 1024 * 1024


def _sigmoid(x):
    return 0.5 * jnp.tanh(0.5 * x) + 0.5


def _silu(x):
    return x * _sigmoid(x)


def _rms(x, g):
    return x * lax.rsqrt(jnp.mean(x * x, axis=-1, keepdims=True) + EPS) * g


def _reduce_keys(x, op):
    keys, queries = x.shape
    part = op(x.reshape(keys // PARTIAL_ROWS, PARTIAL_ROWS, queries), axis=0)
    return op(part, axis=0, keepdims=True)


def _mod_kernel(c_ref, w_ref, b_ref, o_ref):
    c_act = _silu(c_ref[...])
    o_ref[...] = jnp.dot(c_act, w_ref[...], precision=lax.Precision.HIGHEST,
                         preferred_element_type=jnp.float32) + b_ref[...]


def _modulation(c, ada_w, ada_b):
    depth = ada_w.shape[0]
    batch = c.shape[0]
    out = pl.pallas_call(
        _mod_kernel,
        grid=(depth, 3),
        in_specs=[
            pl.BlockSpec((batch, D_MODEL), lambda l, j: (0, 0)),
            pl.BlockSpec((None, D_MODEL, D_MODEL), lambda l, j: (l, 0, j)),
            pl.BlockSpec((None, None, 1, D_MODEL), lambda l, j: (l, j, 0, 0)),
        ],
        out_specs=pl.BlockSpec((None, None, batch, D_MODEL), lambda l, j: (l, j, 0, 0)),
        out_shape=jax.ShapeDtypeStruct((depth, 3, batch, D_MODEL), jnp.float32),
        compiler_params=pltpu.CompilerParams(vmem_limit_bytes=VMEM_LIMIT),
        name="adaln_mod",
    )(c, ada_w, ada_b.reshape(depth, 3, 1, D_MODEL))
    return out.transpose(0, 2, 1, 3)


def _inproj_kernel(x_ref, mod_ref, g_ref, w_ref, cw_ref, cb_ref, wg_ref, bg_ref, ap_ref,
                   yr_ref, qt_ref, ga_ref, kv_ref, wi_ref, vt_ref, perm_ref, tail_ref, h_ref):
    @pl.when(pl.program_id(1) == 0)
    def _():
        tail_ref[...] = jnp.zeros((8, D_RNN), jnp.float32)
        h_ref[...] = jnp.zeros((1, D_RNN), jnp.float32)

    x = x_ref[...]
    shift = mod_ref[0:1, :]
    scale = mod_ref[1:2, :]
    h = _rms(x, g_ref[...]) * (1.0 + scale) + shift
    h = h.astype(jnp.bfloat16)
    z_rnn = jnp.dot(h, w_ref[:, Z_RNN:Z_Q], preferred_element_type=jnp.float32)
    y_r = _rg_lru_tile(z_rnn[:, 0:D_RNN], z_rnn[:, D_RNN:2 * D_RNN],
                       cw_ref, cb_ref, wg_ref, bg_ref, ap_ref, perm_ref, tail_ref, h_ref)
    yr_ref[...] = y_r.astype(yr_ref.dtype)
    z = jnp.dot(h, w_ref[:, Z_Q:Z_COLS], preferred_element_type=jnp.float32)
    col = lambda start, width: z[:, start - Z_Q:start - Z_Q + width]
    for h in range(N_HEADS):
        q_h = col(Z_Q + h * HEAD_DIM, HEAD_DIM) * (HEAD_DIM ** -0.5 * LOG2_E)
        _store_query_tiles(qt_ref, h, q_h.T.astype(jnp.bfloat16))
    for h in range(N_IDX_HEADS):
        _store_query_tiles(qt_ref, N_HEADS + h, col(Z_QI + h * IDX_DIM, IDX_DIM).T.astype(jnp.bfloat16))
    ga_ref[...] = col(Z_GA, D_ATTN)
    kv_ref[...] = jnp.concatenate([col(Z_KV, 128), col(Z_KI, 128)], axis=-1).astype(jnp.bfloat16)
    wi_ref[...] = col(Z_KI, 128)
    pad_rows = lax.broadcasted_iota(jnp.int32, (VT_ROWS - HEAD_DIM, K_TILE), 0)
    for j in range(ROW_TILE // K_TILE):
        v_t = col(Z_KV + HEAD_DIM, HEAD_DIM)[j * K_TILE:(j + 1) * K_TILE, :].T
        vt_ref[j, 0:HEAD_DIM, :] = v_t.astype(jnp.bfloat16)
        vt_ref[j, HEAD_DIM:VT_ROWS, :] = jnp.where(pad_rows == 0, 1.0, 0.0).astype(jnp.bfloat16)


def _store_query_tiles(qt_ref, head, q_t):
    for j in range(ROW_TILE // Q_TILE):
        qt_ref[j, head] = q_t[:, j * Q_TILE:(j + 1) * Q_TILE]


def _in_projection(x, mod_l, norm_g, w_pad, conv_w, conv_b, w_gates, b_gates, a_param):
    batch, seq, _ = x.shape
    row = lambda b, i: (b, i, 0)
    const = lambda b, i: (0, 0)
    return pl.pallas_call(
        _inproj_kernel,
        grid=(batch, seq // ROW_TILE),
        in_specs=[
            pl.BlockSpec((None, ROW_TILE, D_MODEL), row),
            pl.BlockSpec((None, 3, D_MODEL), lambda b, i: (b, 0, 0)),
            pl.BlockSpec((1, D_MODEL), const),
            pl.BlockSpec((D_MODEL, Z_COLS), const),
            pl.BlockSpec((CONV_WIDTH, D_RNN), const),
            pl.BlockSpec((1, D_RNN), const),
            pl.BlockSpec((D_RNN // 128, 128, 256), lambda b, i: (0, 0, 0)),
            pl.BlockSpec((1, 2 * D_RNN), const),
            pl.BlockSpec((1, D_RNN), const),
        ],
        out_specs=[
            pl.BlockSpec((None, ROW_TILE, D_RNN), row),
            pl.BlockSpec((None, ROW_TILE // Q_TILE, N_HEADS + N_IDX_HEADS, HEAD_DIM, Q_TILE),
                         lambda b, i: (b, i, 0, 0, 0)),
            pl.BlockSpec((None, ROW_TILE, 512), row),
            pl.BlockSpec((None, ROW_TILE, 256), row),
            pl.BlockSpec((None, ROW_TILE, 128), row),
            pl.BlockSpec((None, ROW_TILE // K_TILE, VT_ROWS, K_TILE), lambda b, i: (b, i, 0, 0)),
        ],
        out_shape=[
            jax.ShapeDtypeStruct((batch, seq, D_RNN), jnp.bfloat16),
            jax.ShapeDtypeStruct((batch, seq // Q_TILE, N_HEADS + N_IDX_HEADS, HEAD_DIM, Q_TILE), jnp.bfloat16),
            jax.ShapeDtypeStruct((batch, seq, 512), jnp.float32),
            jax.ShapeDtypeStruct((batch, seq, 256), jnp.bfloat16),
            jax.ShapeDtypeStruct((batch, seq, 128), jnp.float32),
            jax.ShapeDtypeStruct((batch, seq // K_TILE, VT_ROWS, K_TILE), jnp.bfloat16),
        ],
        scratch_shapes=[
            pltpu.VMEM((D_RNN // 128, ROW_TILE, 128), jnp.float32),
            pltpu.VMEM((8, D_RNN), jnp.float32),
            pltpu.VMEM((1, D_RNN), jnp.float32),
        ],
        compiler_params=pltpu.CompilerParams(
            dimension_semantics=("parallel", "arbitrary"), vmem_limit_bytes=VMEM_LIMIT),
        name="in_proj_rg_lru",
    )(x, mod_l, norm_g.reshape(1, D_MODEL), w_pad, conv_w, conv_b.reshape(1, D_RNN),
      w_gates, b_gates, a_param.reshape(1, D_RNN))


def _rg_lru_tile(xr, gr, cw_ref, cb_ref, wg_ref, bg_ref, ap_ref, perm_ref, tail_ref, h_ref):
    ts, steps, slabs = ROW_TILE, LRU_STEPS, D_RNN // 128
    lanes = lambda k: slice(k * 128, (k + 1) * 128)

    for s in range(8):
        for k in range(slabs):
            perm_ref[k, pl.ds(s, steps, stride=8), :] = xr[s * steps:(s + 1) * steps, lanes(k)]
    x = jnp.concatenate([perm_ref[k] for k in range(slabs)], axis=-1).reshape(steps, 8, D_RNN)

    first = lax.broadcasted_iota(jnp.int32, (8, D_RNN), 0) == 0
    before = [jnp.where(first, tail_ref[8 - m:9 - m, :], pltpu.roll(x[steps - m], 1, 0))
              for m in range(CONV_WIDTH - 1, 0, -1)]
    x_ext = jnp.concatenate([jnp.stack(before), x], axis=0)
    xc = cb_ref[...] + cw_ref[CONV_WIDTH - 1:CONV_WIDTH, :] * x
    for k in range(CONV_WIDTH - 1):
        xc = xc + cw_ref[k:k + 1, :] * x_ext[k:k + steps]
    tail_ref[...] = xr[ts - 8:ts, :]

    xc_bf = xc.reshape(ts, D_RNN).astype(jnp.bfloat16)
    slab_gates = [jnp.dot(xc_bf[:, lanes(k)], wg_ref[k], preferred_element_type=jnp.float32) for k in range(slabs)]
    gates_x = jnp.concatenate([g[:, 0:128] for g in slab_gates], axis=-1) + bg_ref[:, 0:D_RNN]
    gates_a = jnp.concatenate([g[:, 128:256] for g in slab_gates], axis=-1) + bg_ref[:, D_RNN:2 * D_RNN]
    gate_x = _sigmoid(gates_x.reshape(steps, 8, D_RNN))
    gate_a = _sigmoid(gates_a.reshape(steps, 8, D_RNN))
    neg_ap = -ap_ref[...]
    softplus = jnp.maximum(neg_ap, 0.0) + jnp.log(1.0 + jnp.exp(-jnp.abs(neg_ap)))
    log_a = (-LRU_C) * gate_a * softplus
    a = jnp.exp(log_a)
    gap = 1.0 - a * a
    mult = jnp.where(gap > 0.0, gap * lax.rsqrt(gap), 0.0)
    u = mult * gate_x * xc

    h_run, a_run = u[0], a[0]
    h_loc, a_cum = [h_run], [a_run]
    for j in range(1, steps):
        h_run = a[j] * h_run + u[j]
        a_run = a[j] * a_run
        h_loc.append(h_run)
        a_cum.append(a_run)
    state = h_ref[...]
    entering = []
    for s in range(8):
        entering.append(state)
        state = h_run[s:s + 1, :] + a_run[s:s + 1, :] * state
    h_ref[...] = state
    entering = jnp.concatenate(entering, axis=0)
    h = jnp.stack([h_loc[j] + a_cum[j] * entering for j in range(steps)]).reshape(ts, D_RNN)

    for k in range(slabs):
        perm_ref[k] = h[:, lanes(k)]
    h = jnp.concatenate(
        [jnp.concatenate([perm_ref[k, pl.ds(s, steps, stride=8), :] for k in range(slabs)], axis=-1)
         for s in range(8)], axis=0)
    return h * _silu(gr)


def _attn_kernel(qt_ref, kv_ref, vt_ref, wi_ref, ga_ref, tri_ref, y_ref,
                 sc_ref, sc16_ref, bias_ref, eqb_ref, s_ref, m_ref, mnew_ref, acc_ref, out_ref):
    tq, tk = Q_TILE, K_TILE
    topk = float(MAX_TOPK)
    qb = pl.program_id(1)
    n_keys = (qb + 1) * tq
    n_full = n_keys // tk
    has_tail = n_keys - n_full * tk > 0

    w_t = wi_ref[...].T[WI_LANE:WI_LANE + N_IDX_HEADS, :] * ((IDX_DIM ** -0.5) * (N_IDX_HEADS ** -0.5))

    def score_chunk(c, carry, size, diagonal):
        rmax, rmin = carry
        k0 = pl.multiple_of(c * tk, tk)
        ki = kv_ref[pl.ds(k0, size), KI_LANE:KI_LANE + IDX_DIM]
        score = jnp.zeros((size, tq), jnp.float32)
        for h in range(N_IDX_HEADS):
            logits = jnp.dot(ki, qt_ref[N_HEADS + h], preferred_element_type=jnp.float32)
            score = score + jnp.maximum(logits, 0.0) * w_t[h:h + 1, :]
        if diagonal:
            key_pos = k0 + lax.broadcasted_iota(jnp.int32, (size, tq), 0)
            causal = key_pos <= qb * tq + lax.broadcasted_iota(jnp.int32, (size, tq), 1)
            low, high = jnp.where(causal, score, -jnp.inf), jnp.where(causal, score, jnp.inf)
        else:
            low = high = score
        sc_ref[pl.ds(k0, size), :] = low
        sc16_ref[pl.ds(k0, size), :] = low.astype(jnp.bfloat16)
        return jnp.maximum(rmax, _reduce_keys(low, jnp.max)), jnp.minimum(rmin, _reduce_keys(high, jnp.min))

    n_before = jnp.where(has_tail, n_full, n_full - 1)
    extremes = lax.fori_loop(
        0, n_before, functools.partial(score_chunk, size=tk, diagonal=False),
        (jnp.full((1, tq), -jnp.inf, jnp.float32), jnp.full((1, tq), jnp.inf, jnp.float32)))

    def last_chunk(e):
        return lax.cond(has_tail,
                        lambda e: score_chunk(n_full, e, TAIL_KEYS, True),
                        lambda e: score_chunk(n_full - 1, e, tk, True), e)

    def no_scores(e):
        key_pos = lax.broadcasted_iota(jnp.int32, (tq, tq), 0)
        causal = key_pos <= lax.broadcasted_iota(jnp.int32, (tq, tq), 1)
        sc_ref[0:tq, :] = jnp.where(causal, 0.0, -jnp.inf)
        return jnp.zeros((1, tq), jnp.float32), jnp.zeros((1, tq), jnp.float32)

    all_selected = n_keys <= MAX_TOPK
    rmax, rmin = lax.cond(all_selected, no_scores, last_chunk, extremes)

    n_sweep_tiles = jnp.where(all_selected, 0, n_keys // SWEEP_TILE)

    def sweep(mid, want):
        kinds = {"gt": (jnp.sum, 0.0), "ge": (jnp.sum, 0.0), "above": (jnp.min, jnp.inf), "below": (jnp.max, -jnp.inf)}

        def body(c, carry):
            k0 = pl.multiple_of(c * SWEEP_TILE, SWEEP_TILE)
            s = sc_ref[pl.ds(k0, SWEEP_TILE), :].reshape(SWEEP_TILE // PARTIAL_ROWS, PARTIAL_ROWS, tq)
            gt = s > mid
            terms = {"gt": lambda: jnp.where(gt, 1.0, 0.0), "ge": lambda: jnp.where(s >= mid, 1.0, 0.0),
                     "above": lambda: jnp.where(gt, s, jnp.inf), "below": lambda: jnp.where(gt, -jnp.inf, s)}
            out = []
            for name, acc in zip(want, carry):
                op = kinds[name][0]
                part = op(terms[name](), axis=0)
                out.append(acc + part if op is jnp.sum else
                           (jnp.minimum(acc, part) if op is jnp.min else jnp.maximum(acc, part)))
            return tuple(out)

        init = tuple(jnp.full((PARTIAL_ROWS, tq), kinds[name][1], jnp.float32) for name in want)
        res = lax.fori_loop(0, n_sweep_tiles, body, init)
        return [kinds[name][0](r, axis=0, keepdims=True) for name, r in zip(want, res)]

    def update(state, mid, cnt, tie, new_lo, new_hi):
        lo, hi, chi, act = state
        on = act > 0.5
        fin = jnp.logical_and(on, jnp.logical_or(cnt == topk, tie))
        go_on = jnp.logical_and(on, jnp.logical_not(fin))
        up = jnp.logical_and(go_on, cnt > topk)
        dn = jnp.logical_and(go_on, cnt < topk)
        lo = jnp.where(fin, mid, jnp.where(up, new_lo, lo))
        hi = jnp.where(fin, mid, jnp.where(dn, new_hi, hi))
        chi = jnp.where(jnp.logical_or(fin, dn), cnt, chi)
        act = jnp.where(jnp.logical_and(go_on, lo < hi), 1.0, 0.0)
        return lo, hi, chi, act

    def midpoint(state):
        return 0.5 * state[0] + 0.5 * state[1]

    n_valid = (qb * tq + 1 + lax.broadcasted_iota(jnp.int32, (1, tq), 1)).astype(jnp.float32)
    few = n_valid <= topk
    state = (rmin,
             jnp.where(few, -jnp.inf, rmax),
             jnp.where(few, topk, 0.0),
             jnp.where(jnp.logical_or(few, rmin >= rmax), 0.0, 1.0))

    def coarse_count(t16, with_zero=False):
        one, zero = jnp.ones((), jnp.bfloat16), jnp.zeros((), jnp.bfloat16)

        def tile_sum(hits):
            parts = [hits[g] for g in range(SWEEP_TILE // PARTIAL_ROWS)]
            while len(parts) > 1:
                parts = [parts[i] + parts[i + 1] for i in range(0, len(parts), 2)]
            return parts[0]

        def body(c, accs):
            k0 = pl.multiple_of(c * SWEEP_TILE, SWEEP_TILE)
            s = sc16_ref[pl.ds(k0, SWEEP_TILE), :].reshape(SWEEP_TILE // PARTIAL_ROWS, PARTIAL_ROWS, tq)
            out = [accs[0] + tile_sum(jnp.where(s > t16, one, zero))]
            if with_zero:
                out.append(accs[1] + tile_sum(jnp.where(s > zero, one, zero)))
                out.append(accs[2] + tile_sum(jnp.where(s >= zero, one, zero)))
            return tuple(out)

        init = (jnp.zeros((PARTIAL_ROWS, tq), jnp.bfloat16),) * (3 if with_zero else 1)
        accs = lax.fori_loop(0, n_sweep_tiles, body, init)
        return [jnp.sum(a.astype(jnp.float32), axis=0, keepdims=True) for a in accs]

    def coarse_update(lo, hi, act, t, cnt16):
        on = act > 0.5
        lo = jnp.where(jnp.logical_and(on, cnt16 >= topk), jnp.maximum(lo, t), lo)
        hi = jnp.where(jnp.logical_and(on, cnt16 < topk), jnp.minimum(hi, t + jnp.abs(t) * BF16_STEP + 1e-30), hi)
        return lo, hi

    lo, hi, chi, act = state
    for probe in range(COARSE_PROBES):
        t16 = (0.5 * lo + 0.5 * hi).astype(jnp.bfloat16)
        t = t16.astype(jnp.float32)
        if probe == 0:
            cnt16, pos, nonneg = coarse_count(t16, with_zero=True)
            on = act > 0.5
            at_zero = jnp.logical_and(on, jnp.logical_and(pos <= topk, nonneg >= topk))
            lo = jnp.where(at_zero, 0.0, jnp.where(jnp.logical_and(on, pos > topk), jnp.maximum(lo, 0.0), lo))
            hi = jnp.where(at_zero, 0.0, jnp.where(jnp.logical_and(on, nonneg < topk), jnp.minimum(hi, 0.0), hi))
            chi = jnp.where(at_zero, pos, chi)
            act = jnp.where(at_zero, 0.0, act)
        else:
            (cnt16,) = coarse_count(t16)
        lo, hi = coarse_update(lo, hi, act, t, cnt16)
    (cnt,) = sweep(hi, ("gt",))
    state = update((lo, hi, chi, act), hi, cnt, False, hi, hi)

    def two_largest_below(bound):
        groups = SWEEP_TILE // PARTIAL_ROWS

        def merge(a1, a2, b1, b2):
            return jnp.maximum(a1, b1), jnp.maximum(jnp.minimum(a1, b1), jnp.maximum(a2, b2))

        def body(c, carry):
            m1, m2 = carry
            k0 = pl.multiple_of(c * SWEEP_TILE, SWEEP_TILE)
            s = sc_ref[pl.ds(k0, SWEEP_TILE), :].reshape(groups, PARTIAL_ROWS, tq)
            x = jnp.where(s > bound, -jnp.inf, s)
            for g in range(groups):
                m1, m2 = jnp.maximum(m1, x[g]), jnp.maximum(m2, jnp.minimum(m1, x[g]))
            return m1, m2

        low = jnp.full((PARTIAL_ROWS, tq), -jnp.inf, jnp.float32)
        m1, m2 = lax.fori_loop(0, n_sweep_tiles, body, (low, low))
        rows = PARTIAL_ROWS
        while rows > 1:
            rows //= 2
            m1, m2 = merge(m1[:rows], m2[:rows], m1[rows:], m2[rows:])
        return m1, m2

    def probe_round(state, probes):
        for _ in range(probes):
            mid = midpoint(state)
            (cnt,) = sweep(mid, ("gt",))
            state = update(state, mid, cnt, False, mid, mid)
        lo, hi, chi, act = state
        largest, second = two_largest_below(hi)
        one_short = jnp.logical_and(act > 0.5, topk - chi == 1.0)
        separated = jnp.logical_and(one_short, second < largest)
        new_thr = jnp.where(separated, second, largest)
        return (jnp.where(one_short, new_thr, lo), jnp.where(one_short, new_thr, hi),
                jnp.where(separated, topk, chi), jnp.where(one_short, 0.0, act))

    def flags(state):
        return jnp.max(state[3]), jnp.max(topk - state[2])

    state = probe_round(state, PROBES_PER_ROUND)

    def more_cond(st):
        return jnp.logical_and(st[1] > 0.5, st[0] < PLAIN_BISECT_ROUNDS)

    def more_body(st):
        state = probe_round(st[3:], LATER_ROUND_PROBES)
        return (st[0] + 1,) + flags(state) + state

    st = lax.while_loop(more_cond, more_body, (jnp.int32(1),) + flags(state) + state)

    def snap_cond(st):
        return jnp.logical_and(st[1] > 0.5, st[0] < 4096)

    def snap_body(st):
        state = st[3:]
        mid = midpoint(state)
        mid = jnp.where(mid >= state[1], state[0], mid)
        cnt, above, below = sweep(mid, ("gt", "above", "below"))
        state = update(state, mid, cnt, False, above, below)
        return (st[0] + 1,) + flags(state) + state

    st = lax.while_loop(snap_cond, snap_body, (jnp.int32(0),) + st[1:])
    thr = st[4]
    need = topk - st[5]
    any_tie = st[2] > 0.5

    m_ref[...] = jnp.full(m_ref.shape, MASKED, jnp.float32)
    acc_ref[...] = jnp.zeros(acc_ref.shape, jnp.float32)
    eqb_ref[...] = jnp.zeros(eqb_ref.shape, jnp.float32)

    def attend_chunk(c, size):
        k0 = pl.multiple_of(c * tk, tk)
        s_idx = sc_ref[pl.ds(k0, size), :]
        bias_ref[0:size, :] = jnp.where(s_idx > thr, 0.0, MASKED)

        @pl.when(any_tie)
        def _():
            eq = s_idx == thr
            eq_f = jnp.where(eq, 1.0, 0.0)
            rank = jnp.dot(tri_ref[0:size, 0:size], eq_f.astype(jnp.bfloat16),
                           preferred_element_type=jnp.float32) + eqb_ref[...]
            tie_bias = jnp.where(jnp.logical_and(eq, rank < need), 0.0, MASKED)
            bias_ref[0:size, :] = jnp.where(s_idx > thr, 0.0, tie_bias)
            eqb_ref[...] = eqb_ref[...] + _reduce_keys(eq_f, jnp.sum)

        kc = kv_ref[pl.ds(k0, size), 0:HEAD_DIM]
        vt = vt_ref[c, :, 0:size]

        def logits(h):
            s = jnp.dot(kc, qt_ref[h], preferred_element_type=jnp.float32) + bias_ref[0:size, :]
            s_ref[h, 0:size, :] = s
            mnew_ref[h] = jnp.maximum(m_ref[h], _reduce_keys(s, jnp.max))

        def weigh(h):
            m_new = mnew_ref[h]
            alpha = jnp.exp2(m_ref[h] - m_new)
            p = jnp.exp2(s_ref[h, 0:size, :] - m_new)
            acc_ref[h] = alpha * acc_ref[h] + jnp.dot(
                vt, p.astype(jnp.bfloat16), preferred_element_type=jnp.float32)
            m_ref[h] = m_new

        for h in range(N_HEADS):
            logits(h)
        for h in range(N_HEADS):
            weigh(h)

    def attend_full(c, _):
        attend_chunk(c, tk)
        return 0

    lax.fori_loop(0, n_full, attend_full, 0)

    @pl.when(has_tail)
    def _():
        attend_chunk(n_full, TAIL_KEYS)

    for h in range(N_HEADS):
        rows = slice(h * HEAD_DIM, (h + 1) * HEAD_DIM)
        out_ref[rows, :] = acc_ref[h, 0:HEAD_DIM, :] / acc_ref[h, HEAD_DIM:HEAD_DIM + 1, :]
    y = out_ref[...].T * _silu(ga_ref[...])
    y_ref[...] = y.astype(y_ref.dtype)


def _dsa_attention(qt, kv, vt, wi, ga, tri):
    batch, seq, _ = kv.shape
    assert seq // PARTIAL_ROWS <= 256, "bf16 hit counters are exact only up to 256 per slot"
    row = lambda b, i: (b, i, 0)
    return pl.pallas_call(
        _attn_kernel,
        grid=(batch, seq // Q_TILE),
        in_specs=[
            pl.BlockSpec((None, None, N_HEADS + N_IDX_HEADS, HEAD_DIM, Q_TILE), lambda b, i: (b, i, 0, 0, 0)),
            pl.BlockSpec((None, seq, 256), lambda b, i: (b, 0, 0)),
            pl.BlockSpec((None, seq // K_TILE, VT_ROWS, K_TILE), lambda b, i: (b, 0, 0, 0)),
            pl.BlockSpec((None, Q_TILE, 128), row),
            pl.BlockSpec((None, Q_TILE, D_ATTN), row),
            pl.BlockSpec((K_TILE, K_TILE), lambda b, i: (0, 0)),
        ],
        out_specs=pl.BlockSpec((None, Q_TILE, D_ATTN), row),
        out_shape=jax.ShapeDtypeStruct((batch, seq, D_ATTN), jnp.bfloat16),
        scratch_shapes=[
            pltpu.VMEM((seq, Q_TILE), jnp.float32),
            pltpu.VMEM((seq, Q_TILE), jnp.bfloat16),
            pltpu.VMEM((K_TILE, Q_TILE), jnp.float32),
            pltpu.VMEM((1, Q_TILE), jnp.float32),
            pltpu.VMEM((N_HEADS, K_TILE, Q_TILE), jnp.float32),
            pltpu.VMEM((N_HEADS, 1, Q_TILE), jnp.float32),
            pltpu.VMEM((N_HEADS, 1, Q_TILE), jnp.float32),
            pltpu.VMEM((N_HEADS, VT_ROWS, Q_TILE), jnp.float32),
            pltpu.VMEM((D_ATTN, Q_TILE), jnp.float32),
        ],
        compiler_params=pltpu.CompilerParams(
            dimension_semantics=("parallel", "parallel"), vmem_limit_bytes=VMEM_LIMIT),
        name="dsa_attention",
    )(qt, kv, vt, wi, ga, tri)


def _outproj_kernel(yr_ref, ya_ref, x_ref, mod_ref, wr_ref, wa_ref, g_ref, o_ref, *, final_norm):
    y = jnp.dot(yr_ref[...], wr_ref[...], preferred_element_type=jnp.float32)
    y = y + jnp.dot(ya_ref[...], wa_ref[...], preferred_element_type=jnp.float32)
    x_new = x_ref[...] + mod_ref[2:3, :] * y
    if final_norm:
        x_new = _rms(x_new, g_ref[...])
    o_ref[...] = x_new


def _out_projection(yr, ya, x, mod_l, w_r, w_a, final_g, final_norm):
    batch, seq, _ = x.shape
    row = lambda b, i: (b, i, 0)
    const = lambda b, i: (0, 0)
    return pl.pallas_call(
        functools.partial(_outproj_kernel, final_norm=final_norm),
        grid=(batch, seq // OUT_ROW_TILE),
        in_specs=[
            pl.BlockSpec((None, OUT_ROW_TILE, D_RNN), row),
            pl.BlockSpec((None, OUT_ROW_TILE, D_ATTN), row),
            pl.BlockSpec((None, OUT_ROW_TILE, D_MODEL), row),
            pl.BlockSpec((None, 3, D_MODEL), lambda b, i: (b, 0, 0)),
            pl.BlockSpec((D_RNN, D_MODEL), const),
            pl.BlockSpec((D_ATTN, D_MODEL), const),
            pl.BlockSpec((1, D_MODEL), const),
        ],
        out_specs=pl.BlockSpec((None, OUT_ROW_TILE, D_MODEL), row),
        out_shape=jax.ShapeDtypeStruct((batch, seq, D_MODEL), jnp.float32),
        compiler_params=pltpu.CompilerParams(
            dimension_semantics=("parallel", "parallel"), vmem_limit_bytes=VMEM_LIMIT),
        name="out_proj",
    )(yr, ya, x, mod_l, w_r, w_a, final_g.reshape(1, D_MODEL))


def _pad_w_in(w):
    return jnp.pad(w.astype(jnp.bfloat16), ((0, 0), (0, 0), (0, Z_COLS - D_IN)))


def _block_diag(w):
    n, c, d = w.shape
    eye = jnp.eye(n, dtype=w.dtype)
    return (eye[:, None, :, None] * w[:, :, None, :]).reshape(n * c, n * d)


def _gate_slabs(w_x, w_a):
    per_slab = 128 // RNN_BLOCK
    slabs = [jnp.concatenate([_block_diag(w_x[k:k + per_slab]), _block_diag(w_a[k:k + per_slab])], axis=-1)
             for k in range(0, N_RNN_BLOCKS, per_slab)]
    return jnp.stack(slabs).astype(jnp.bfloat16)


def kernel(x, c, norm_g, ada_w, ada_b, w_in, conv_w, conv_b, lru_wx, lru_bx, lru_wa, lru_ba, lru_a, w_out, final_g):
    depth = w_in.shape[0]
    mod = _modulation(c, ada_w, ada_b)
    idx = jnp.arange(K_TILE)
    tri = (idx[None, :] < idx[:, None]).astype(jnp.bfloat16)
    w_pad = _pad_w_in(w_in)
    for l in range(depth):
        w_gates = _gate_slabs(lru_wx[l], lru_wa[l])
        b_gates = jnp.concatenate([lru_bx[l], lru_ba[l]]).reshape(1, 2 * D_RNN)
        w_o = w_out[l].astype(jnp.bfloat16)
        y_r, qt, ga, kv, wi, vt = _in_projection(x, mod[l], norm_g[l], w_pad[l], conv_w[l], conv_b[l],
                                                 w_gates, b_gates, lru_a[l])
        y_a = _dsa_attention(qt, kv, vt, wi, ga, tri)
        x = _out_projection(y_r, y_a, x, mod[l], w_o[:D_RNN], w_o[D_RNN:], final_g, l == depth - 1)
    return x
```

```python
import functools

import jax
import jax.numpy as jnp
from jax import lax
from jax.experimental import pallas as pl
from jax.experimental.pallas import tpu as pltpu

D_MODEL = 1024
D_RNN = 512
N_RNN_BLOCKS = 8
RNN_BLOCK = D_RNN // N_RNN_BLOCKS
CONV_WIDTH = 4
LRU_C = 8.0
N_HEADS = 8
HEAD_DIM = 64
D_ATTN = N_HEADS * HEAD_DIM
N_IDX_HEADS = 8
IDX_DIM = 64
MAX_TOPK = 256
EPS = 1e-6
LOG2_E = 1.4426950408889634

Z_RNN = 0
Z_Q = 1024
Z_KV = 1536
Z_GA = 1664
Z_QI = 2176
Z_KI = 2688
Z_COLS = 2816
D_IN = 2760
WI_LANE = 64
KI_LANE = 128
BF16_STEP = 2.0 ** -7

ROW_TILE = 512
OUT_ROW_TILE = 1024
LRU_STEPS = ROW_TILE // 8
Q_TILE = 256
K_TILE = 512
VT_ROWS = 80
MASKED = -1e30
COARSE_PROBES = 9
PROBES_PER_ROUND = 7
LATER_ROUND_PROBES = 3
PLAIN_BISECT_ROUNDS = 5
PARTIAL_ROWS = 32
SWEEP_TILE = 256
TAIL_KEYS = K_TILE - Q_TILE
assert K_TILE == 2 * Q_TILE
assert MAX_TOPK < 2 * Q_TILE
VMEM_LIMIT = 48 * 1024 * 1024


def _sigmoid(x):
    return 0.5 * jnp.tanh(0.5 * x) + 0.5


def _silu(x):
    return x * _sigmoid(x)


def _rms(x, g):
    return x * lax.rsqrt(jnp.mean(x * x, axis=-1, keepdims=True) + EPS) * g


def _reduce_keys(x, op):
    keys, queries = x.shape
    part = op(x.reshape(keys // PARTIAL_ROWS, PARTIAL_ROWS, queries), axis=0)
    return op(part, axis=0, keepdims=True)


def _mod_kernel(c_ref, w_ref, b_ref, o_ref):
    c_act = _silu(c_ref[...])
    o_ref[...] = jnp.dot(c_act, w_ref[...], precision=lax.Precision.HIGHEST,
                         preferred_element_type=jnp.float32) + b_ref[...]


def _modulation(c, ada_w, ada_b):
    depth = ada_w.shape[0]
    batch = c.shape[0]
    out = pl.pallas_call(
        _mod_kernel,
        grid=(depth, 3),
        in_specs=[
            pl.BlockSpec((batch, D_MODEL), lambda l, j: (0, 0)),
            pl.BlockSpec((None, D_MODEL, D_MODEL), lambda l, j: (l, 0, j)),
            pl.BlockSpec((None, None, 1, D_MODEL), lambda l, j: (l, j, 0, 0)),
        ],
        out_specs=pl.BlockSpec((None, None, batch, D_MODEL), lambda l, j: (l, j, 0, 0)),
        out_shape=jax.ShapeDtypeStruct((depth, 3, batch, D_MODEL), jnp.float32),
        compiler_params=pltpu.CompilerParams(vmem_limit_bytes=VMEM_LIMIT),
        name="adaln_mod",
    )(c, ada_w, ada_b.reshape(depth, 3, 1, D_MODEL))
    return out.transpose(0, 2, 1, 3)


def _inproj_kernel(x_ref, mod_ref, g_ref, w_ref, cw_ref, cb_ref, wg_ref, bg_ref, ap_ref,
                   yr_ref, qt_ref, ga_ref, kv_ref, wi_ref, vt_ref, perm_ref, tail_ref, h_ref):
    @pl.when(pl.program_id(1) == 0)
    def _():
        tail_ref[...] = jnp.zeros((8, D_RNN), jnp.float32)
        h_ref[...] = jnp.zeros((1, D_RNN), jnp.float32)

    x = x_ref[...]
    shift = mod_ref[0:1, :]
    scale = mod_ref[1:2, :]
    h = _rms(x, g_ref[...]) * (1.0 + scale) + shift
    h = h.astype(jnp.bfloat16)
    z_rnn = jnp.dot(h, w_ref[:, Z_RNN:Z_Q], preferred_element_type=jnp.float32)
    y_r = _rg_lru_tile(z_rnn[:, 0:D_RNN], z_rnn[:, D_RNN:2 * D_RNN],
                       cw_ref, cb_ref, wg_ref, bg_ref, ap_ref, perm_ref, tail_ref, h_ref)
    yr_ref[...] = y_r.astype(yr_ref.dtype)
    z = jnp.dot(h, w_ref[:, Z_Q:Z_COLS], preferred_element_type=jnp.float32)
    col = lambda start, width: z[:, start - Z_Q:start - Z_Q + width]
    for h in range(N_HEADS):
        q_h = col(Z_Q + h * HEAD_DIM, HEAD_DIM) * (HEAD_DIM ** -0.5 * LOG2_E)
        _store_query_tiles(qt_ref, h, q_h.T.astype(jnp.bfloat16))
    for h in range(N_IDX_HEADS):
        _store_query_tiles(qt_ref, N_HEADS + h, col(Z_QI + h * IDX_DIM, IDX_DIM).T.astype(jnp.bfloat16))
    ga_ref[...] = col(Z_GA, D_ATTN)
    kv_ref[...] = jnp.concatenate([col(Z_KV, 128), col(Z_KI, 128)], axis=-1).astype(jnp.bfloat16)
    wi_ref[...] = col(Z_KI, 128)
    pad_rows = lax.broadcasted_iota(jnp.int32, (VT_ROWS - HEAD_DIM, K_TILE), 0)
    for j in range(ROW_TILE // K_TILE):
        v_t = col(Z_KV + HEAD_DIM, HEAD_DIM)[j * K_TILE:(j + 1) * K_TILE, :].T
        vt_ref[j, 0:HEAD_DIM, :] = v_t.astype(jnp.bfloat16)
        vt_ref[j, HEAD_DIM:VT_ROWS, :] = jnp.where(pad_rows == 0, 1.0, 0.0).astype(jnp.bfloat16)


def _store_query_tiles(qt_ref, head, q_t):
    for j in range(ROW_TILE // Q_TILE):
        qt_ref[j, head] = q_t[:, j * Q_TILE:(j + 1) * Q_TILE]


def _in_projection(x, mod_l, norm_g, w_pad, conv_w, conv_b, w_gates, b_gates, a_param):
    batch, seq, _ = x.shape
    row = lambda b, i: (b, i, 0)
    const = lambda b, i: (0, 0)
    return pl.pallas_call(
        _inproj_kernel,
        grid=(batch, seq // ROW_TILE),
        in_specs=[
            pl.BlockSpec((None, ROW_TILE, D_MODEL), row),
            pl.BlockSpec((None, 3, D_MODEL), lambda b, i: (b, 0, 0)),
            pl.BlockSpec((1, D_MODEL), const),
            pl.BlockSpec((D_MODEL, Z_COLS), const),
            pl.BlockSpec((CONV_WIDTH, D_RNN), const),
            pl.BlockSpec((1, D_RNN), const),
            pl.BlockSpec((D_RNN // 128, 128, 256), lambda b, i: (0, 0, 0)),
            pl.BlockSpec((1, 2 * D_RNN), const),
            pl.BlockSpec((1, D_RNN), const),
        ],
        out_specs=[
            pl.BlockSpec((None, ROW_TILE, D_RNN), row),
            pl.BlockSpec((None, ROW_TILE // Q_TILE, N_HEADS + N_IDX_HEADS, HEAD_DIM, Q_TILE),
                         lambda b, i: (b, i, 0, 0, 0)),
            pl.BlockSpec((None, ROW_TILE, 512), row),
            pl.BlockSpec((None, ROW_TILE, 256), row),
            pl.BlockSpec((None, ROW_TILE, 128), row),
            pl.BlockSpec((None, ROW_TILE // K_TILE, VT_ROWS, K_TILE), lambda b, i: (b, i, 0, 0)),
        ],
        out_shape=[
            jax.ShapeDtypeStruct((batch, seq, D_RNN), jnp.bfloat16),
            jax.ShapeDtypeStruct((batch, seq // Q_TILE, N_HEADS + N_IDX_HEADS, HEAD_DIM, Q_TILE), jnp.bfloat16),
            jax.ShapeDtypeStruct((batch, seq, 512), jnp.float32),
            jax.ShapeDtypeStruct((batch, seq, 256), jnp.bfloat16),
            jax.ShapeDtypeStruct((batch, seq, 128), jnp.float32),
            jax.ShapeDtypeStruct((batch, seq // K_TILE, VT_ROWS, K_TILE), jnp.bfloat16),
        ],
        scratch_shapes=[
            pltpu.VMEM((D_RNN // 128, ROW_TILE, 128), jnp.float32),
            pltpu.VMEM((8, D_RNN), jnp.float32),
            pltpu.VMEM((1, D_RNN), jnp.float32),
        ],
        compiler_params=pltpu.CompilerParams(
            dimension_semantics=("parallel", "arbitrary"), vmem_limit_bytes=VMEM_LIMIT),
        name="in_proj_rg_lru",
    )(x, mod_l, norm_g.reshape(1, D_MODEL), w_pad, conv_w, conv_b.reshape(1, D_RNN),
      w_gates, b_gates, a_param.reshape(1, D_RNN))


def _rg_lru_tile(xr, gr, cw_ref, cb_ref, wg_ref, bg_ref, ap_ref, perm_ref, tail_ref, h_ref):
    ts, steps, slabs = ROW_TILE, LRU_STEPS, D_RNN // 128
    lanes = lambda k: slice(k * 128, (k + 1) * 128)

    for s in range(8):
        for k in range(slabs):
            perm_ref[k, pl.ds(s, steps, stride=8), :] = xr[s * steps:(s + 1) * steps, lanes(k)]
    x = jnp.concatenate([perm_ref[k] for k in range(slabs)], axis=-1).reshape(steps, 8, D_RNN)

    first = lax.broadcasted_iota(jnp.int32, (8, D_RNN), 0) == 0
    before = [jnp.where(first, tail_ref[8 - m:9 - m, :], pltpu.roll(x[steps - m], 1, 0))
              for m in range(CONV_WIDTH - 1, 0, -1)]
    x_ext = jnp.concatenate([jnp.stack(before), x], axis=0)
    xc = cb_ref[...] + cw_ref[CONV_WIDTH - 1:CONV_WIDTH, :] * x
    for k in range(CONV_WIDTH - 1):
        xc = xc + cw_ref[k:k + 1, :] * x_ext[k:k + steps]
    tail_ref[...] = xr[ts - 8:ts, :]

    xc_bf = xc.reshape(ts, D_RNN).astype(jnp.bfloat16)
    slab_gates = [jnp.dot(xc_bf[:, lanes(k)], wg_ref[k], preferred_element_type=jnp.float32) for k in range(slabs)]
    gates_x = jnp.concatenate([g[:, 0:128] for g in slab_gates], axis=-1) + bg_ref[:, 0:D_RNN]
    gates_a = jnp.concatenate([g[:, 128:256] for g in slab_gates], axis=-1) + bg_ref[:, D_RNN:2 * D_RNN]
    gate_x = _sigmoid(gates_x.reshape(steps, 8, D_RNN))
    gate_a = _sigmoid(gates_a.reshape(steps, 8, D_RNN))
    neg_ap = -ap_ref[...]
    softplus = jnp.maximum(neg_ap, 0.0) + jnp.log(1.0 + jnp.exp(-jnp.abs(neg_ap)))
    log_a = (-LRU_C) * gate_a * softplus
    a = jnp.exp(log_a)
    gap = 1.0 - a * a
    mult = jnp.where(gap > 0.0, gap * lax.rsqrt(gap), 0.0)
    u = mult * gate_x * xc

    h_run, a_run = u[0], a[0]
    h_loc, a_cum = [h_run], [a_run]
    for j in range(1, steps):
        h_run = a[j] * h_run + u[j]
        a_run = a[j] * a_run
        h_loc.append(h_run)
        a_cum.append(a_run)
    state = h_ref[...]
    entering = []
    for s in range(8):
        entering.append(state)
        state = h_run[s:s + 1, :] + a_run[s:s + 1, :] * state
    h_ref[...] = state
    entering = jnp.concatenate(entering, axis=0)
    h = jnp.stack([h_loc[j] + a_cum[j] * entering for j in range(steps)]).reshape(ts, D_RNN)

    for k in range(slabs):
        perm_ref[k] = h[:, lanes(k)]
    h = jnp.concatenate(
        [jnp.concatenate([perm_ref[k, pl.ds(s, steps, stride=8), :] for k in range(slabs)], axis=-1)
         for s in range(8)], axis=0)
    return h * _silu(gr)


def _attn_kernel(qt_ref, kv_ref, vt_ref, wi_ref, ga_ref, tri_ref, y_ref,
                 sc_ref, sc16_ref, bias_ref, eqb_ref, s_ref, m_ref, mnew_ref, acc_ref, out_ref):
    tq, tk = Q_TILE, K_TILE
    topk = float(MAX_TOPK)
    qb = pl.program_id(1)
    n_keys = (qb + 1) * tq
    n_full = n_keys // tk
    has_tail = n_keys - n_full * tk > 0

    w_t = wi_ref[...].T[WI_LANE:WI_LANE + N_IDX_HEADS, :] * ((IDX_DIM ** -0.5) * (N_IDX_HEADS ** -0.5))

    def score_chunk(c, carry, size, diagonal):
        rmax, rmin = carry
        k0 = pl.multiple_of(c * tk, tk)
        ki = kv_ref[pl.ds(k0, size), KI_LANE:KI_LANE + IDX_DIM]
        score = jnp.zeros((size, tq), jnp.float32)
        for h in range(N_IDX_HEADS):
            logits = jnp.dot(ki, qt_ref[N_HEADS + h], preferred_element_type=jnp.float32)
            score = score + jnp.maximum(logits, 0.0) * w_t[h:h + 1, :]
        if diagonal:
            key_pos = k0 + lax.broadcasted_iota(jnp.int32, (size, tq), 0)
            causal = key_pos <= qb * tq + lax.broadcasted_iota(jnp.int32, (size, tq), 1)
            low, high = jnp.where(causal, score, -jnp.inf), jnp.where(causal, score, jnp.inf)
        else:
            low = high = score
        sc_ref[pl.ds(k0, size), :] = low
        sc16_ref[pl.ds(k0, size), :] = low.astype(jnp.bfloat16)
        return jnp.maximum(rmax, _reduce_keys(low, jnp.max)), jnp.minimum(rmin, _reduce_keys(high, jnp.min))

    n_before = jnp.where(has_tail, n_full, n_full - 1)
    extremes = lax.fori_loop(
        0, n_before, functools.partial(score_chunk, size=tk, diagonal=False),
        (jnp.full((1, tq), -jnp.inf, jnp.float32), jnp.full((1, tq), jnp.inf, jnp.float32)))

    def last_chunk(e):
        return lax.cond(has_tail,
                        lambda e: score_chunk(n_full, e, TAIL_KEYS, True),
                        lambda e: score_chunk(n_full - 1, e, tk, True), e)

    def no_scores(e):
        key_pos = lax.broadcasted_iota(jnp.int32, (tq, tq), 0)
        causal = key_pos <= lax.broadcasted_iota(jnp.int32, (tq, tq), 1)
        sc_ref[0:tq, :] = jnp.where(causal, 0.0, -jnp.inf)
        return jnp.zeros((1, tq), jnp.float32), jnp.zeros((1, tq), jnp.float32)

    all_selected = n_keys <= MAX_TOPK
    rmax, rmin = lax.cond(all_selected, no_scores, last_chunk, extremes)

    n_sweep_tiles = jnp.where(all_selected, 0, n_keys // SWEEP_TILE)

    def sweep(mid, want):
        kinds = {"gt": (jnp.sum, 0.0), "ge": (jnp.sum, 0.0), "above": (jnp.min, jnp.inf), "below": (jnp.max, -jnp.inf)}

        def body(c, carry):
            k0 = pl.multiple_of(c * SWEEP_TILE, SWEEP_TILE)
            s = sc_ref[pl.ds(k0, SWEEP_TILE), :].reshape(SWEEP_TILE // PARTIAL_ROWS, PARTIAL_ROWS, tq)
            gt = s > mid
            terms = {"gt": lambda: jnp.where(gt, 1.0, 0.0), "ge": lambda: jnp.where(s >= mid, 1.0, 0.0),
                     "above": lambda: jnp.where(gt, s, jnp.inf), "below": lambda: jnp.where(gt, -jnp.inf, s)}
            out = []
            for name, acc in zip(want, carry):
                op = kinds[name][0]
                part = op(terms[name](), axis=0)
                out.append(acc + part if op is jnp.sum else
                           (jnp.minimum(acc, part) if op is jnp.min else jnp.maximum(acc, part)))
            return tuple(out)

        init = tuple(jnp.full((PARTIAL_ROWS, tq), kinds[name][1], jnp.float32) for name in want)
        res = lax.fori_loop(0, n_sweep_tiles, body, init)
        return [kinds[name][0](r, axis=0, keepdims=True) for name, r in zip(want, res)]

    def update(state, mid, cnt, tie, new_lo, new_hi):
        lo, hi, chi, act = state
        on = act > 0.5
        fin = jnp.logical_and(on, jnp.logical_or(cnt == topk, tie))
        go_on = jnp.logical_and(on, jnp.logical_not(fin))
        up = jnp.logical_and(go_on, cnt > topk)
        dn = jnp.logical_and(go_on, cnt < topk)
        lo = jnp.where(fin, mid, jnp.where(up, new_lo, lo))
        hi = jnp.where(fin, mid, jnp.where(dn, new_hi, hi))
        chi = jnp.where(jnp.logical_or(fin, dn), cnt, chi)
        act = jnp.where(jnp.logical_and(go_on, lo < hi), 1.0, 0.0)
        return lo, hi, chi, act

    def midpoint(state):
        return 0.5 * state[0] + 0.5 * state[1]

    n_valid = (qb * tq + 1 + lax.broadcasted_iota(jnp.int32, (1, tq), 1)).astype(jnp.float32)
    few = n_valid <= topk
    state = (rmin,
             jnp.where(few, -jnp.inf, rmax),
             jnp.where(few, topk, 0.0),
             jnp.where(jnp.logical_or(few, rmin >= rmax), 0.0, 1.0))

    def coarse_count(t16, also_ge=False):
        one, zero = jnp.ones((), jnp.bfloat16), jnp.zeros((), jnp.bfloat16)

        def tile_sum(hits):
            parts = [hits[g] for g in range(SWEEP_TILE // PARTIAL_ROWS)]
            while len(parts) > 1:
                parts = [parts[i] + parts[i + 1] for i in range(0, len(parts), 2)]
            return parts[0]

        def body(c, accs):
            k0 = pl.multiple_of(c * SWEEP_TILE, SWEEP_TILE)
            s = sc16_ref[pl.ds(k0, SWEEP_TILE), :].reshape(SWEEP_TILE // PARTIAL_ROWS, PARTIAL_ROWS, tq)
            out = [accs[0] + tile_sum(jnp.where(s > t16, one, zero))]
            if also_ge:
                out.append(accs[1] + tile_sum(jnp.where(s >= t16, one, zero)))
            return tuple(out)

        init = (jnp.zeros((PARTIAL_ROWS, tq), jnp.bfloat16),) * (2 if also_ge else 1)
        accs = lax.fori_loop(0, n_sweep_tiles, body, init)
        return [jnp.sum(a.astype(jnp.float32), axis=0, keepdims=True) for a in accs]

    lo, hi, chi, act = state
    on = act > 0.5
    pos, nonneg = coarse_count(jnp.zeros((1, tq), jnp.bfloat16), also_ge=True)
    at_zero = jnp.logical_and(on, jnp.logical_and(pos <= topk, nonneg >= topk))
    lo = jnp.where(at_zero, 0.0, jnp.where(jnp.logical_and(on, pos > topk), jnp.maximum(lo, 0.0), lo))
    hi = jnp.where(at_zero, 0.0, jnp.where(jnp.logical_and(on, nonneg < topk), jnp.minimum(hi, 0.0), hi))
    chi = jnp.where(at_zero, pos, chi)
    act = jnp.where(at_zero, 0.0, act)

    for _ in range(COARSE_PROBES):
        t16 = (0.5 * lo + 0.5 * hi).astype(jnp.bfloat16)
        t = t16.astype(jnp.float32)
        (cnt16,) = coarse_count(t16)
        on = act > 0.5
        lo = jnp.where(jnp.logical_and(on, cnt16 >= topk), jnp.maximum(lo, t), lo)
        hi = jnp.where(jnp.logical_and(on, cnt16 < topk), jnp.minimum(hi, t + jnp.abs(t) * BF16_STEP + 1e-30), hi)
    (cnt,) = sweep(hi, ("gt",))
    state = update((lo, hi, chi, act), hi, cnt, False, hi, hi)

    def two_largest_below(bound):
        groups = SWEEP_TILE // PARTIAL_ROWS

        def merge(a1, a2, b1, b2):
            return jnp.maximum(a1, b1), jnp.maximum(jnp.minimum(a1, b1), jnp.maximum(a2, b2))

        def body(c, carry):
            m1, m2 = carry
            k0 = pl.multiple_of(c * SWEEP_TILE, SWEEP_TILE)
            s = sc_ref[pl.ds(k0, SWEEP_TILE), :].reshape(groups, PARTIAL_ROWS, tq)
            x = jnp.where(s > bound, -jnp.inf, s)
            for g in range(groups):
                m1, m2 = jnp.maximum(m1, x[g]), jnp.maximum(m2, jnp.minimum(m1, x[g]))
            return m1, m2

        low = jnp.full((PARTIAL_ROWS, tq), -jnp.inf, jnp.float32)
        m1, m2 = lax.fori_loop(0, n_sweep_tiles, body, (low, low))
        rows = PARTIAL_ROWS
        while rows > 1:
            rows //= 2
            m1, m2 = merge(m1[:rows], m2[:rows], m1[rows:], m2[rows:])
        return m1, m2

    def probe_round(state, probes):
        for _ in range(probes):
            mid = midpoint(state)
            (cnt,) = sweep(mid, ("gt",))
            state = update(state, mid, cnt, False, mid, mid)
        lo, hi, chi, act = state
        largest, second = two_largest_below(hi)
        one_short = jnp.logical_and(act > 0.5, topk - chi == 1.0)
        separated = jnp.logical_and(one_short, second < largest)
        new_thr = jnp.where(separated, second, largest)
        return (jnp.where(one_short, new_thr, lo), jnp.where(one_short, new_thr, hi),
                jnp.where(separated, topk, chi), jnp.where(one_short, 0.0, act))

    def flags(state):
        return jnp.max(state[3]), jnp.max(topk - state[2])

    state = probe_round(state, PROBES_PER_ROUND)

    def more_cond(st):
        return jnp.logical_and(st[1] > 0.5, st[0] < PLAIN_BISECT_ROUNDS)

    def more_body(st):
        state = probe_round(st[3:], LATER_ROUND_PROBES)
        return (st[0] + 1,) + flags(state) + state

    st = lax.while_loop(more_cond, more_body, (jnp.int32(1),) + flags(state) + state)

    def snap_cond(st):
        return jnp.logical_and(st[1] > 0.5, st[0] < 4096)

    def snap_body(st):
        state = st[3:]
        mid = midpoint(state)
        mid = jnp.where(mid >= state[1], state[0], mid)
        cnt, above, below = sweep(mid, ("gt", "above", "below"))
        state = update(state, mid, cnt, False, above, below)
        return (st[0] + 1,) + flags(state) + state

    st = lax.while_loop(snap_cond, snap_body, (jnp.int32(0),) + st[1:])
    thr = st[4]
    need = topk - st[5]
    any_tie = st[2] > 0.5

    m_ref[...] = jnp.full(m_ref.shape, MASKED, jnp.float32)
    acc_ref[...] = jnp.zeros(acc_ref.shape, jnp.float32)
    eqb_ref[...] = jnp.zeros(eqb_ref.shape, jnp.float32)

    def attend_chunk(c, size):
        k0 = pl.multiple_of(c * tk, tk)
        s_idx = sc_ref[pl.ds(k0, size), :]
        bias_ref[0:size, :] = jnp.where(s_idx > thr, 0.0, MASKED)

        @pl.when(any_tie)
        def _():
            eq = s_idx == thr
            eq_f = jnp.where(eq, 1.0, 0.0)
            rank = jnp.dot(tri_ref[0:size, 0:size], eq_f.astype(jnp.bfloat16),
                           preferred_element_type=jnp.float32) + eqb_ref[...]
            tie_bias = jnp.where(jnp.logical_and(eq, rank < need), 0.0, MASKED)
            bias_ref[0:size, :] = jnp.where(s_idx > thr, 0.0, tie_bias)
            eqb_ref[...] = eqb_ref[...] + _reduce_keys(eq_f, jnp.sum)

        kc = kv_ref[pl.ds(k0, size), 0:HEAD_DIM]
        vt = vt_ref[c, :, 0:size]

        def logits(h):
            s = jnp.dot(kc, qt_ref[h], preferred_element_type=jnp.float32) + bias_ref[0:size, :]
            s_ref[h, 0:size, :] = s
            mnew_ref[h] = jnp.maximum(m_ref[h], _reduce_keys(s, jnp.max))

        def weigh(h):
            m_new = mnew_ref[h]
            alpha = jnp.exp2(m_ref[h] - m_new)
            p = jnp.exp2(s_ref[h, 0:size, :] - m_new)
            acc_ref[h] = alpha * acc_ref[h] + jnp.dot(
                vt, p.astype(jnp.bfloat16), preferred_element_type=jnp.float32)
            m_ref[h] = m_new

        for h in range(N_HEADS):
            logits(h)
        for h in range(N_HEADS):
            weigh(h)

    def attend_full(c, _):
        attend_chunk(c, tk)
        return 0

    lax.fori_loop(0, n_full, attend_full, 0)

    @pl.when(has_tail)
    def _():
        attend_chunk(n_full, TAIL_KEYS)

    for h in range(N_HEADS):
        rows = slice(h * HEAD_DIM, (h + 1) * HEAD_DIM)
        out_ref[rows, :] = acc_ref[h, 0:HEAD_DIM, :] / acc_ref[h, HEAD_DIM:HEAD_DIM + 1, :]
    y = out_ref[...].T * _silu(ga_ref[...])
    y_ref[...] = y.astype(y_ref.dtype)


def _dsa_attention(qt, kv, vt, wi, ga, tri):
    batch, seq, _ = kv.shape
    assert seq // PARTIAL_ROWS <= 256, "bf16 hit counters are exact only up to 256 per slot"
    row = lambda b, i: (b, i, 0)
    return pl.pallas_call(
        _attn_kernel,
        grid=(batch, seq // Q_TILE),
        in_specs=[
            pl.BlockSpec((None, None, N_HEADS + N_IDX_HEADS, HEAD_DIM, Q_TILE), lambda b, i: (b, i, 0, 0, 0)),
            pl.BlockSpec((None, seq, 256), lambda b, i: (b, 0, 0)),
            pl.BlockSpec((None, seq // K_TILE, VT_ROWS, K_TILE), lambda b, i: (b, 0, 0, 0)),
            pl.BlockSpec((None, Q_TILE, 128), row),
            pl.BlockSpec((None, Q_TILE, D_ATTN), row),
            pl.BlockSpec((K_TILE, K_TILE), lambda b, i: (0, 0)),
        ],
        out_specs=pl.BlockSpec((None, Q_TILE, D_ATTN), row),
        out_shape=jax.ShapeDtypeStruct((batch, seq, D_ATTN), jnp.bfloat16),
        scratch_shapes=[
            pltpu.VMEM((seq, Q_TILE), jnp.float32),
            pltpu.VMEM((seq, Q_TILE), jnp.bfloat16),
            pltpu.VMEM((K_TILE, Q_TILE), jnp.float32),
            pltpu.VMEM((1, Q_TILE), jnp.float32),
            pltpu.VMEM((N_HEADS, K_TILE, Q_TILE), jnp.float32),
            pltpu.VMEM((N_HEADS, 1, Q_TILE), jnp.float32),
            pltpu.VMEM((N_HEADS, 1, Q_TILE), jnp.float32),
            pltpu.VMEM((N_HEADS, VT_ROWS, Q_TILE), jnp.float32),
            pltpu.VMEM((D_ATTN, Q_TILE), jnp.float32),
        ],
        compiler_params=pltpu.CompilerParams(
            dimension_semantics=("parallel", "parallel"), vmem_limit_bytes=VMEM_LIMIT),
        name="dsa_attention",
    )(qt, kv, vt, wi, ga, tri)


OUT_IN_BUFFERS = 3
OUT_OUT_BUFFERS = 2


def _outproj_kernel(yr_hbm, ya_hbm, x_hbm, mod_ref, wr_ref, wa_ref, g_ref, o_hbm,
                    yr_buf, ya_buf, x_buf, o_buf, in_sem, out_sem, *, final_norm, tiles_per_row, n_tiles):
    tm = OUT_ROW_TILE

    def in_copies(step, slot):
        b, rows = step // tiles_per_row, pl.ds((step % tiles_per_row) * tm, tm)
        return (pltpu.make_async_copy(yr_hbm.at[b, rows], yr_buf.at[slot], in_sem.at[0, slot]),
                pltpu.make_async_copy(ya_hbm.at[b, rows], ya_buf.at[slot], in_sem.at[1, slot]),
                pltpu.make_async_copy(x_hbm.at[b, rows], x_buf.at[slot], in_sem.at[2, slot]))

    def out_copy(step, slot):
        b, rows = step // tiles_per_row, pl.ds((step % tiles_per_row) * tm, tm)
        return pltpu.make_async_copy(o_buf.at[slot], o_hbm.at[b, rows], out_sem.at[slot])

    for step in range(OUT_IN_BUFFERS - 1):
        for copy in in_copies(step, step):
            copy.start()

    def body(step, _):
        slot = step % OUT_IN_BUFFERS
        for copy in in_copies(step, slot):
            copy.wait()

        @pl.when(step + OUT_IN_BUFFERS - 1 < n_tiles)
        def _():
            ahead = step + OUT_IN_BUFFERS - 1
            for copy in in_copies(ahead, ahead % OUT_IN_BUFFERS):
                copy.start()

        o_slot = step % OUT_OUT_BUFFERS

        @pl.when(step >= OUT_OUT_BUFFERS)
        def _():
            out_copy(step - OUT_OUT_BUFFERS, o_slot).wait()

        y = jnp.dot(yr_buf[slot], wr_ref[...], preferred_element_type=jnp.float32)
        y = y + jnp.dot(ya_buf[slot], wa_ref[...], preferred_element_type=jnp.float32)
        x_new = x_buf[slot] + mod_ref[step // tiles_per_row, 2:3, :] * y
        if final_norm:
            x_new = _rms(x_new, g_ref[...])
        o_buf[o_slot] = x_new
        out_copy(step, o_slot).start()
        return 0

    lax.fori_loop(0, n_tiles, body, 0)
    for step in range(n_tiles - OUT_OUT_BUFFERS, n_tiles):
        out_copy(step, step % OUT_OUT_BUFFERS).wait()


def _out_projection(yr, ya, x, mod_l, w_r, w_a, final_g, final_norm):
    batch, seq, _ = x.shape
    tiles_per_row = seq // OUT_ROW_TILE
    n_tiles = batch * tiles_per_row
    assert n_tiles >= max(OUT_IN_BUFFERS, OUT_OUT_BUFFERS)
    hbm = pl.BlockSpec(memory_space=pl.ANY)
    vmem = pl.BlockSpec(memory_space=pltpu.VMEM)
    return pl.pallas_call(
        functools.partial(_outproj_kernel, final_norm=final_norm, tiles_per_row=tiles_per_row, n_tiles=n_tiles),
        in_specs=[hbm, hbm, hbm, vmem, vmem, vmem, vmem],
        out_specs=hbm,
        out_shape=jax.ShapeDtypeStruct((batch, seq, D_MODEL), jnp.float32),
        scratch_shapes=[
            pltpu.VMEM((OUT_IN_BUFFERS, OUT_ROW_TILE, D_RNN), jnp.bfloat16),
            pltpu.VMEM((OUT_IN_BUFFERS, OUT_ROW_TILE, D_ATTN), jnp.bfloat16),
            pltpu.VMEM((OUT_IN_BUFFERS, OUT_ROW_TILE, D_MODEL), jnp.float32),
            pltpu.VMEM((OUT_OUT_BUFFERS, OUT_ROW_TILE, D_MODEL), jnp.float32),
            pltpu.SemaphoreType.DMA((3, OUT_IN_BUFFERS)),
            pltpu.SemaphoreType.DMA((OUT_OUT_BUFFERS,)),
        ],
        compiler_params=pltpu.CompilerParams(vmem_limit_bytes=VMEM_LIMIT),
        name="out_proj",
    )(yr, ya, x, mod_l, w_r, w_a, final_g.reshape(1, D_MODEL))


def _pad_w_in(w):
    return jnp.pad(w.astype(jnp.bfloat16), ((0, 0), (0, 0), (0, Z_COLS - D_IN)))


def _block_diag(w):
    n, c, d = w.shape
    eye = jnp.eye(n, dtype=w.dtype)
    return (eye[:, None, :, None] * w[:, :, None, :]).reshape(n * c, n * d)


def _gate_slabs(w_x, w_a):
    per_slab = 128 // RNN_BLOCK
    slabs = [jnp.concatenate([_block_diag(w_x[k:k + per_slab]), _block_diag(w_a[k:k + per_slab])], axis=-1)
             for k in range(0, N_RNN_BLOCKS, per_slab)]
    return jnp.stack(slabs).astype(jnp.bfloat16)


def kernel(x, c, norm_g, ada_w, ada_b, w_in, conv_w, conv_b, lru_wx, lru_bx, lru_wa, lru_ba, lru_a, w_out, final_g):
    depth = w_in.shape[0]
    mod = _modulation(c, ada_w, ada_b)
    idx = jnp.arange(K_TILE)
    tri = (idx[None, :] < idx[:, None]).astype(jnp.bfloat16)
    w_pad = _pad_w_in(w_in)
    for l in range(depth):
        w_gates = _gate_slabs(lru_wx[l], lru_wa[l])
        b_gates = jnp.concatenate([lru_bx[l], lru_ba[l]]).reshape(1, 2 * D_RNN)
        w_o = w_out[l].astype(jnp.bfloat16)
        y_r, qt, ga, kv, wi, vt = _in_projection(x, mod[l], norm_g[l], w_pad[l], conv_w[l], conv_b[l],
                                                 w_gates, b_gates, lru_a[l])
        y_a = _dsa_attention(qt, kv, vt, wi, ga, tri)
        x = _out_projection(y_r, y_a, x, mod[l], w_o[:D_RNN], w_o[D_RNN:], final_g, l == depth - 1)
    return x
```

```python
import functools

import jax
import jax.numpy as jnp
from jax import lax
from jax.experimental import pallas as pl
from jax.experimental.pallas import tpu as pltpu

D_MODEL = 1024
D_RNN = 512
N_RNN_BLOCKS = 8
RNN_BLOCK = D_RNN // N_RNN_BLOCKS
CONV_WIDTH = 4
LRU_C = 8.0
N_HEADS = 8
HEAD_DIM = 64
D_ATTN = N_HEADS * HEAD_DIM
N_IDX_HEADS = 8
IDX_DIM = 64
MAX_TOPK = 256
EPS = 1e-6
LOG2_E = 1.4426950408889634

Z_RNN = 0
Z_Q = 1024
Z_KV = 1536
Z_GA = 1664
Z_QI = 2176
Z_KI = 2688
Z_COLS = 2816
D_IN = 2760
WI_LANE = 64
KI_LANE = 128
BF16_STEP = 2.0 ** -7

ROW_TILE = 512
OUT_ROW_TILE = 512
LRU_STEPS = ROW_TILE // 8
Q_TILE = 256
K_TILE = 512
VT_ROWS = 80
MASKED = -1e30
COARSE_PROBES = 9
PROBES_PER_ROUND = 7
LATER_ROUND_PROBES = 3
PLAIN_BISECT_ROUNDS = 5
PARTIAL_ROWS = 32
SWEEP_TILE = 256
TAIL_KEYS = K_TILE - Q_TILE
assert K_TILE == 2 * Q_TILE
assert MAX_TOPK < 2 * Q_TILE
VMEM_LIMIT = 48 * 1024 * 1024


def _sigmoid(x):
    return 0.5 * jnp.tanh(0.5 * x) + 0.5


def _silu(x):
    return x * _sigmoid(x)


def _rms(x, g):
    return x * lax.rsqrt(jnp.mean(x * x, axis=-1, keepdims=True) + EPS) * g


def _reduce_keys(x, op):
    keys, queries = x.shape
    part = op(x.reshape(keys // PARTIAL_ROWS, PARTIAL_ROWS, queries), axis=0)
    return op(part, axis=0, keepdims=True)


def _mod_kernel(c_ref, w_ref, b_ref, o_ref):
    c_act = _silu(c_ref[...])
    o_ref[...] = jnp.dot(c_act, w_ref[...], precision=lax.Precision.HIGHEST,
                         preferred_element_type=jnp.float32) + b_ref[...]


def _modulation(c, ada_w, ada_b):
    depth = ada_w.shape[0]
    batch = c.shape[0]
    out = pl.pallas_call(
        _mod_kernel,
        grid=(depth, 3),
        in_specs=[
            pl.BlockSpec((batch, D_MODEL), lambda l, j: (0, 0)),
            pl.BlockSpec((None, D_MODEL, D_MODEL), lambda l, j: (l, 0, j)),
            pl.BlockSpec((None, None, 1, D_MODEL), lambda l, j: (l, j, 0, 0)),
        ],
        out_specs=pl.BlockSpec((None, None, batch, D_MODEL), lambda l, j: (l, j, 0, 0)),
        out_shape=jax.ShapeDtypeStruct((depth, 3, batch, D_MODEL), jnp.float32),
        compiler_params=pltpu.CompilerParams(vmem_limit_bytes=VMEM_LIMIT),
        name="adaln_mod",
    )(c, ada_w, ada_b.reshape(depth, 3, 1, D_MODEL))
    return out.transpose(0, 2, 1, 3)


def _inproj_kernel(x_ref, mod_ref, g_ref, w_ref, cw_ref, cb_ref, wg_ref, bg_ref, ap_ref,
                   yr_ref, qt_ref, ga_ref, kv_ref, wi_ref, vt_ref, perm_ref, tail_ref, h_ref):
    @pl.when(pl.program_id(1) == 0)
    def _():
        tail_ref[...] = jnp.zeros((8, D_RNN), jnp.float32)
        h_ref[...] = jnp.zeros((1, D_RNN), jnp.float32)

    x = x_ref[...]
    shift = mod_ref[0:1, :]
    scale = mod_ref[1:2, :]
    h = _rms(x, g_ref[...]) * (1.0 + scale) + shift
    h = h.astype(jnp.bfloat16)
    z_rnn = jnp.dot(h, w_ref[:, Z_RNN:Z_Q], preferred_element_type=jnp.float32)
    y_r = _rg_lru_tile(z_rnn[:, 0:D_RNN], z_rnn[:, D_RNN:2 * D_RNN],
                       cw_ref, cb_ref, wg_ref, bg_ref, ap_ref, perm_ref, tail_ref, h_ref)
    yr_ref[...] = y_r.astype(yr_ref.dtype)
    z = jnp.dot(h, w_ref[:, Z_Q:Z_COLS], preferred_element_type=jnp.float32)
    col = lambda start, width: z[:, start - Z_Q:start - Z_Q + width]
    for h in range(N_HEADS):
        q_h = col(Z_Q + h * HEAD_DIM, HEAD_DIM) * (HEAD_DIM ** -0.5 * LOG2_E)
        _store_query_tiles(qt_ref, h, q_h.T.astype(jnp.bfloat16))
    for h in range(N_IDX_HEADS):
        _store_query_tiles(qt_ref, N_HEADS + h, col(Z_QI + h * IDX_DIM, IDX_DIM).T.astype(jnp.bfloat16))
    ga_ref[...] = col(Z_GA, D_ATTN)
    kv_ref[...] = jnp.concatenate([col(Z_KV, 128), col(Z_KI, 128)], axis=-1).astype(jnp.bfloat16)
    wi_ref[...] = col(Z_KI, 128)
    pad_rows = lax.broadcasted_iota(jnp.int32, (VT_ROWS - HEAD_DIM, K_TILE), 0)
    for j in range(ROW_TILE // K_TILE):
        v_t = col(Z_KV + HEAD_DIM, HEAD_DIM)[j * K_TILE:(j + 1) * K_TILE, :].T
        vt_ref[j, 0:HEAD_DIM, :] = v_t.astype(jnp.bfloat16)
        vt_ref[j, HEAD_DIM:VT_ROWS, :] = jnp.where(pad_rows == 0, 1.0, 0.0).astype(jnp.bfloat16)


def _store_query_tiles(qt_ref, head, q_t):
    for j in range(ROW_TILE // Q_TILE):
        qt_ref[j, head] = q_t[:, j * Q_TILE:(j + 1) * Q_TILE]


def _in_projection(x, mod_l, norm_g, w_pad, conv_w, conv_b, w_gates, b_gates, a_param):
    batch, seq, _ = x.shape
    row = lambda b, i: (b, i, 0)
    const = lambda b, i: (0, 0)
    return pl.pallas_call(
        _inproj_kernel,
        grid=(batch, seq // ROW_TILE),
        in_specs=[
            pl.BlockSpec((None, ROW_TILE, D_MODEL), row),
            pl.BlockSpec((None, 3, D_MODEL), lambda b, i: (b, 0, 0)),
            pl.BlockSpec((1, D_MODEL), const),
            pl.BlockSpec((D_MODEL, Z_COLS), const),
            pl.BlockSpec((CONV_WIDTH, D_RNN), const),
            pl.BlockSpec((1, D_RNN), const),
            pl.BlockSpec((D_RNN // 128, 128, 256), lambda b, i: (0, 0, 0)),
            pl.BlockSpec((1, 2 * D_RNN), const),
            pl.BlockSpec((1, D_RNN), const),
        ],
        out_specs=[
            pl.BlockSpec((None, ROW_TILE, D_RNN), row),
            pl.BlockSpec((None, ROW_TILE // Q_TILE, N_HEADS + N_IDX_HEADS, HEAD_DIM, Q_TILE),
                         lambda b, i: (b, i, 0, 0, 0)),
            pl.BlockSpec((None, ROW_TILE, 512), row),
            pl.BlockSpec((None, ROW_TILE, 256), row),
            pl.BlockSpec((None, ROW_TILE, 128), row),
            pl.BlockSpec((None, ROW_TILE // K_TILE, VT_ROWS, K_TILE), lambda b, i: (b, i, 0, 0)),
        ],
        out_shape=[
            jax.ShapeDtypeStruct((batch, seq, D_RNN), jnp.bfloat16),
            jax.ShapeDtypeStruct((batch, seq // Q_TILE, N_HEADS + N_IDX_HEADS, HEAD_DIM, Q_TILE), jnp.bfloat16),
            jax.ShapeDtypeStruct((batch, seq, 512), jnp.float32),
            jax.ShapeDtypeStruct((batch, seq, 256), jnp.bfloat16),
            jax.ShapeDtypeStruct((batch, seq, 128), jnp.float32),
            jax.ShapeDtypeStruct((batch, seq // K_TILE, VT_ROWS, K_TILE), jnp.bfloat16),
        ],
        scratch_shapes=[
            pltpu.VMEM((D_RNN // 128, ROW_TILE, 128), jnp.float32),
            pltpu.VMEM((8, D_RNN), jnp.float32),
            pltpu.VMEM((1, D_RNN), jnp.float32),
        ],
        compiler_params=pltpu.CompilerParams(
            dimension_semantics=("parallel", "arbitrary"), vmem_limit_bytes=VMEM_LIMIT),
        name="in_proj_rg_lru",
    )(x, mod_l, norm_g.reshape(1, D_MODEL), w_pad, conv_w, conv_b.reshape(1, D_RNN),
      w_gates, b_gates, a_param.reshape(1, D_RNN))


def _rg_lru_tile(xr, gr, cw_ref, cb_ref, wg_ref, bg_ref, ap_ref, perm_ref, tail_ref, h_ref):
    ts, steps, slabs = ROW_TILE, LRU_STEPS, D_RNN // 128
    lanes = lambda k: slice(k * 128, (k + 1) * 128)

    for s in range(8):
        for k in range(slabs):
            perm_ref[k, pl.ds(s, steps, stride=8), :] = xr[s * steps:(s + 1) * steps, lanes(k)]
    x = jnp.concatenate([perm_ref[k] for k in range(slabs)], axis=-1).reshape(steps, 8, D_RNN)

    first = lax.broadcasted_iota(jnp.int32, (8, D_RNN), 0) == 0
    before = [jnp.where(first, tail_ref[8 - m:9 - m, :], pltpu.roll(x[steps - m], 1, 0))
              for m in range(CONV_WIDTH - 1, 0, -1)]
    x_ext = jnp.concatenate([jnp.stack(before), x], axis=0)
    xc = cb_ref[...] + cw_ref[CONV_WIDTH - 1:CONV_WIDTH, :] * x
    for k in range(CONV_WIDTH - 1):
        xc = xc + cw_ref[k:k + 1, :] * x_ext[k:k + steps]
    tail_ref[...] = xr[ts - 8:ts, :]

    xc_bf = xc.reshape(ts, D_RNN).astype(jnp.bfloat16)
    slab_gates = [jnp.dot(xc_bf[:, lanes(k)], wg_ref[k], preferred_element_type=jnp.float32) for k in range(slabs)]
    gates_x = jnp.concatenate([g[:, 0:128] for g in slab_gates], axis=-1) + bg_ref[:, 0:D_RNN]
    gates_a = jnp.concatenate([g[:, 128:256] for g in slab_gates], axis=-1) + bg_ref[:, D_RNN:2 * D_RNN]
    gate_x = _sigmoid(gates_x.reshape(steps, 8, D_RNN))
    gate_a = _sigmoid(gates_a.reshape(steps, 8, D_RNN))
    neg_ap = -ap_ref[...]
    softplus = jnp.maximum(neg_ap, 0.0) + jnp.log(1.0 + jnp.exp(-jnp.abs(neg_ap)))
    log_a = (-LRU_C) * gate_a * softplus
    a = jnp.exp(log_a)
    gap = 1.0 - a * a
    mult = jnp.where(gap > 0.0, gap * lax.rsqrt(gap), 0.0)
    u = mult * gate_x * xc

    h_run, a_run = u[0], a[0]
    h_loc, a_cum = [h_run], [a_run]
    for j in range(1, steps):
        h_run = a[j] * h_run + u[j]
        a_run = a[j] * a_run
        h_loc.append(h_run)
        a_cum.append(a_run)
    state = h_ref[...]
    entering = []
    for s in range(8):
        entering.append(state)
        state = h_run[s:s + 1, :] + a_run[s:s + 1, :] * state
    h_ref[...] = state
    entering = jnp.concatenate(entering, axis=0)
    h = jnp.stack([h_loc[j] + a_cum[j] * entering for j in range(steps)]).reshape(ts, D_RNN)

    for k in range(slabs):
        perm_ref[k] = h[:, lanes(k)]
    h = jnp.concatenate(
        [jnp.concatenate([perm_ref[k, pl.ds(s, steps, stride=8), :] for k in range(slabs)], axis=-1)
         for s in range(8)], axis=0)
    return h * _silu(gr)


def _attn_kernel(qt_ref, kv_ref, vt_ref, wi_ref, ga_ref, tri_ref, y_ref,
                 sc_ref, sc16_ref, bias_ref, eqb_ref, s_ref, m_ref, mnew_ref, acc_ref, out_ref):
    tq, tk = Q_TILE, K_TILE
    topk = float(MAX_TOPK)
    qb = pl.program_id(1)
    n_keys = (qb + 1) * tq
    n_full = n_keys // tk
    has_tail = n_keys - n_full * tk > 0

    w_t = wi_ref[...].T[WI_LANE:WI_LANE + N_IDX_HEADS, :] * ((IDX_DIM ** -0.5) * (N_IDX_HEADS ** -0.5))

    def score_chunk(c, carry, size, diagonal):
        rmax, rmin = carry
        k0 = pl.multiple_of(c * tk, tk)
        ki = kv_ref[pl.ds(k0, size), KI_LANE:KI_LANE + IDX_DIM]
        score = jnp.zeros((size, tq), jnp.float32)
        for h in range(N_IDX_HEADS):
            logits = jnp.dot(ki, qt_ref[N_HEADS + h], preferred_element_type=jnp.float32)
            score = score + jnp.maximum(logits, 0.0) * w_t[h:h + 1, :]
        if diagonal:
            key_pos = k0 + lax.broadcasted_iota(jnp.int32, (size, tq), 0)
            causal = key_pos <= qb * tq + lax.broadcasted_iota(jnp.int32, (size, tq), 1)
            low, high = jnp.where(causal, score, -jnp.inf), jnp.where(causal, score, jnp.inf)
        else:
            low = high = score
        sc_ref[pl.ds(k0, size), :] = low
        sc16_ref[pl.ds(k0, size), :] = low.astype(jnp.bfloat16)
        return jnp.maximum(rmax, _reduce_keys(low, jnp.max)), jnp.minimum(rmin, _reduce_keys(high, jnp.min))

    n_before = jnp.where(has_tail, n_full, n_full - 1)
    extremes = lax.fori_loop(
        0, n_before, functools.partial(score_chunk, size=tk, diagonal=False),
        (jnp.full((1, tq), -jnp.inf, jnp.float32), jnp.full((1, tq), jnp.inf, jnp.float32)))

    def last_chunk(e):
        return lax.cond(has_tail,
                        lambda e: score_chunk(n_full, e, TAIL_KEYS, True),
                        lambda e: score_chunk(n_full - 1, e, tk, True), e)

    def no_scores(e):
        key_pos = lax.broadcasted_iota(jnp.int32, (tq, tq), 0)
        causal = key_pos <= lax.broadcasted_iota(jnp.int32, (tq, tq), 1)
        sc_ref[0:tq, :] = jnp.where(causal, 0.0, -jnp.inf)
        return jnp.zeros((1, tq), jnp.float32), jnp.zeros((1, tq), jnp.float32)

    all_selected = n_keys <= MAX_TOPK
    rmax, rmin = lax.cond(all_selected, no_scores, last_chunk, extremes)

    n_sweep_tiles = jnp.where(all_selected, 0, n_keys // SWEEP_TILE)

    def sweep(mid, want):
        kinds = {"gt": (jnp.sum, 0.0), "ge": (jnp.sum, 0.0), "above": (jnp.min, jnp.inf), "below": (jnp.max, -jnp.inf)}

        def body(c, carry):
            k0 = pl.multiple_of(c * SWEEP_TILE, SWEEP_TILE)
            s = sc_ref[pl.ds(k0, SWEEP_TILE), :].reshape(SWEEP_TILE // PARTIAL_ROWS, PARTIAL_ROWS, tq)
            gt = s > mid
            terms = {"gt": lambda: jnp.where(gt, 1.0, 0.0), "ge": lambda: jnp.where(s >= mid, 1.0, 0.0),
                     "above": lambda: jnp.where(gt, s, jnp.inf), "below": lambda: jnp.where(gt, -jnp.inf, s)}
            out = []
            for name, acc in zip(want, carry):
                op = kinds[name][0]
                part = op(terms[name](), axis=0)
                out.append(acc + part if op is jnp.sum else
                           (jnp.minimum(acc, part) if op is jnp.min else jnp.maximum(acc, part)))
            return tuple(out)

        init = tuple(jnp.full((PARTIAL_ROWS, tq), kinds[name][1], jnp.float32) for name in want)
        res = lax.fori_loop(0, n_sweep_tiles, body, init)
        return [kinds[name][0](r, axis=0, keepdims=True) for name, r in zip(want, res)]

    def update(state, mid, cnt, tie, new_lo, new_hi):
        lo, hi, chi, act = state
        on = act > 0.5
        fin = jnp.logical_and(on, jnp.logical_or(cnt == topk, tie))
        go_on = jnp.logical_and(on, jnp.logical_not(fin))
        up = jnp.logical_and(go_on, cnt > topk)
        dn = jnp.logical_and(go_on, cnt < topk)
        lo = jnp.where(fin, mid, jnp.where(up, new_lo, lo))
        hi = jnp.where(fin, mid, jnp.where(dn, new_hi, hi))
        chi = jnp.where(jnp.logical_or(fin, dn), cnt, chi)
        act = jnp.where(jnp.logical_and(go_on, lo < hi), 1.0, 0.0)
        return lo, hi, chi, act

    def midpoint(state):
        return 0.5 * state[0] + 0.5 * state[1]

    n_valid = (qb * tq + 1 + lax.broadcasted_iota(jnp.int32, (1, tq), 1)).astype(jnp.float32)
    few = n_valid <= topk
    state = (rmin,
             jnp.where(few, -jnp.inf, rmax),
             jnp.where(few, topk, 0.0),
             jnp.where(jnp.logical_or(few, rmin >= rmax), 0.0, 1.0))

    def coarse_count(t16, also_ge=False):
        one, zero = jnp.ones((), jnp.bfloat16), jnp.zeros((), jnp.bfloat16)

        def tile_sum(hits):
            parts = [hits[g] for g in range(SWEEP_TILE // PARTIAL_ROWS)]
            while len(parts) > 1:
                parts = [parts[i] + parts[i + 1] for i in range(0, len(parts), 2)]
            return parts[0]

        def body(c, accs):
            k0 = pl.multiple_of(c * SWEEP_TILE, SWEEP_TILE)
            s = sc16_ref[pl.ds(k0, SWEEP_TILE), :].reshape(SWEEP_TILE // PARTIAL_ROWS, PARTIAL_ROWS, tq)
            out = [accs[0] + tile_sum(jnp.where(s > t16, one, zero))]
            if also_ge:
                out.append(accs[1] + tile_sum(jnp.where(s >= t16, one, zero)))
            return tuple(out)

        init = (jnp.zeros((PARTIAL_ROWS, tq), jnp.bfloat16),) * (2 if also_ge else 1)
        accs = lax.fori_loop(0, n_sweep_tiles, body, init)
        return [jnp.sum(a.astype(jnp.float32), axis=0, keepdims=True) for a in accs]

    lo, hi, chi, act = state
    on = act > 0.5
    pos, nonneg = coarse_count(jnp.zeros((1, tq), jnp.bfloat16), also_ge=True)
    at_zero = jnp.logical_and(on, jnp.logical_and(pos <= topk, nonneg >= topk))
    lo = jnp.where(at_zero, 0.0, jnp.where(jnp.logical_and(on, pos > topk), jnp.maximum(lo, 0.0), lo))
    hi = jnp.where(at_zero, 0.0, jnp.where(jnp.logical_and(on, nonneg < topk), jnp.minimum(hi, 0.0), hi))
    chi = jnp.where(at_zero, pos, chi)
    act = jnp.where(at_zero, 0.0, act)

    for _ in range(COARSE_PROBES):
        t16 = (0.5 * lo + 0.5 * hi).astype(jnp.bfloat16)
        t = t16.astype(jnp.float32)
        (cnt16,) = coarse_count(t16)
        on = act > 0.5
        lo = jnp.where(jnp.logical_and(on, cnt16 >= topk), jnp.maximum(lo, t), lo)
        hi = jnp.where(jnp.logical_and(on, cnt16 < topk), jnp.minimum(hi, t + jnp.abs(t) * BF16_STEP + 1e-30), hi)
    (cnt,) = sweep(hi, ("gt",))
    state = update((lo, hi, chi, act), hi, cnt, False, hi, hi)

    def two_largest_below(bound):
        groups = SWEEP_TILE // PARTIAL_ROWS

        def merge(a1, a2, b1, b2):
            return jnp.maximum(a1, b1), jnp.maximum(jnp.minimum(a1, b1), jnp.maximum(a2, b2))

        def body(c, carry):
            m1, m2 = carry
            k0 = pl.multiple_of(c * SWEEP_TILE, SWEEP_TILE)
            s = sc_ref[pl.ds(k0, SWEEP_TILE), :].reshape(groups, PARTIAL_ROWS, tq)
            x = jnp.where(s > bound, -jnp.inf, s)
            for g in range(groups):
                m1, m2 = jnp.maximum(m1, x[g]), jnp.maximum(m2, jnp.minimum(m1, x[g]))
            return m1, m2

        low = jnp.full((PARTIAL_ROWS, tq), -jnp.inf, jnp.float32)
        m1, m2 = lax.fori_loop(0, n_sweep_tiles, body, (low, low))
        rows = PARTIAL_ROWS
        while rows > 1:
            rows //= 2
            m1, m2 = merge(m1[:rows], m2[:rows], m1[rows:], m2[rows:])
        return m1, m2

    def probe_round(state, probes):
        for _ in range(probes):
            mid = midpoint(state)
            (cnt,) = sweep(mid, ("gt",))
            state = update(state, mid, cnt, False, mid, mid)
        lo, hi, chi, act = state
        largest, second = two_largest_below(hi)
        one_short = jnp.logical_and(act > 0.5, topk - chi == 1.0)
        separated = jnp.logical_and(one_short, second < largest)
        new_thr = jnp.where(separated, second, largest)
        return (jnp.where(one_short, new_thr, lo), jnp.where(one_short, new_thr, hi),
                jnp.where(separated, topk, chi), jnp.where(one_short, 0.0, act))

    def flags(state):
        return jnp.max(state[3]), jnp.max(topk - state[2])

    state = probe_round(state, PROBES_PER_ROUND)

    def more_cond(st):
        return jnp.logical_and(st[1] > 0.5, st[0] < PLAIN_BISECT_ROUNDS)

    def more_body(st):
        state = probe_round(st[3:], LATER_ROUND_PROBES)
        return (st[0] + 1,) + flags(state) + state

    st = lax.while_loop(more_cond, more_body, (jnp.int32(1),) + flags(state) + state)

    def snap_cond(st):
        return jnp.logical_and(st[1] > 0.5, st[0] < 4096)

    def snap_body(st):
        state = st[3:]
        mid = midpoint(state)
        mid = jnp.where(mid >= state[1], state[0], mid)
        cnt, above, below = sweep(mid, ("gt", "above", "below"))
        state = update(state, mid, cnt, False, above, below)
        return (st[0] + 1,) + flags(state) + state

    st = lax.while_loop(snap_cond, snap_body, (jnp.int32(0),) + st[1:])
    thr = st[4]
    need = topk - st[5]
    any_tie = st[2] > 0.5

    m_ref[...] = jnp.full(m_ref.shape, MASKED, jnp.float32)
    acc_ref[...] = jnp.zeros(acc_ref.shape, jnp.float32)
    eqb_ref[...] = jnp.zeros(eqb_ref.shape, jnp.float32)

    def attend_chunk(c, size):
        k0 = pl.multiple_of(c * tk, tk)
        s_idx = sc_ref[pl.ds(k0, size), :]
        bias_ref[0:size, :] = jnp.where(s_idx > thr, 0.0, MASKED)

        @pl.when(any_tie)
        def _():
            eq = s_idx == thr
            eq_f = jnp.where(eq, 1.0, 0.0)
            rank = jnp.dot(tri_ref[0:size, 0:size], eq_f.astype(jnp.bfloat16),
                           preferred_element_type=jnp.float32) + eqb_ref[...]
            tie_bias = jnp.where(jnp.logical_and(eq, rank < need), 0.0, MASKED)
            bias_ref[0:size, :] = jnp.where(s_idx > thr, 0.0, tie_bias)
            eqb_ref[...] = eqb_ref[...] + _reduce_keys(eq_f, jnp.sum)

        kc = kv_ref[pl.ds(k0, size), 0:HEAD_DIM]
        vt = vt_ref[c, :, 0:size]

        def logits(h):
            s = jnp.dot(kc, qt_ref[h], preferred_element_type=jnp.float32) + bias_ref[0:size, :]
            s_ref[h, 0:size, :] = s
            mnew_ref[h] = jnp.maximum(m_ref[h], _reduce_keys(s, jnp.max))

        def weigh(h):
            m_new = mnew_ref[h]
            alpha = jnp.exp2(m_ref[h] - m_new)
            p = jnp.exp2(s_ref[h, 0:size, :] - m_new)
            acc_ref[h] = alpha * acc_ref[h] + jnp.dot(
                vt, p.astype(jnp.bfloat16), preferred_element_type=jnp.float32)
            m_ref[h] = m_new

        for h in range(N_HEADS):
            logits(h)
        for h in range(N_HEADS):
            weigh(h)

    def attend_full(c, _):
        attend_chunk(c, tk)
        return 0

    lax.fori_loop(0, n_full, attend_full, 0)

    @pl.when(has_tail)
    def _():
        attend_chunk(n_full, TAIL_KEYS)

    for h in range(N_HEADS):
        rows = slice(h * HEAD_DIM, (h + 1) * HEAD_DIM)
        out_ref[rows, :] = acc_ref[h, 0:HEAD_DIM, :] / acc_ref[h, HEAD_DIM:HEAD_DIM + 1, :]
    y = out_ref[...].T * _silu(ga_ref[...])
    y_ref[...] = y.astype(y_ref.dtype)


def _dsa_attention(qt, kv, vt, wi, ga, tri):
    batch, seq, _ = kv.shape
    assert seq // PARTIAL_ROWS <= 256, "bf16 hit counters are exact only up to 256 per slot"
    row = lambda b, i: (b, i, 0)
    return pl.pallas_call(
        _attn_kernel,
        grid=(batch, seq // Q_TILE),
        in_specs=[
            pl.BlockSpec((None, None, N_HEADS + N_IDX_HEADS, HEAD_DIM, Q_TILE), lambda b, i: (b, i, 0, 0, 0)),
            pl.BlockSpec((None, seq, 256), lambda b, i: (b, 0, 0)),
            pl.BlockSpec((None, seq // K_TILE, VT_ROWS, K_TILE), lambda b, i: (b, 0, 0, 0)),
            pl.BlockSpec((None, Q_TILE, 128), row),
            pl.BlockSpec((None, Q_TILE, D_ATTN), row),
            pl.BlockSpec((K_TILE, K_TILE), lambda b, i: (0, 0)),
        ],
        out_specs=pl.BlockSpec((None, Q_TILE, D_ATTN), row),
        out_shape=jax.ShapeDtypeStruct((batch, seq, D_ATTN), jnp.bfloat16),
        scratch_shapes=[
            pltpu.VMEM((seq, Q_TILE), jnp.float32),
            pltpu.VMEM((seq, Q_TILE), jnp.bfloat16),
            pltpu.VMEM((K_TILE, Q_TILE), jnp.float32),
            pltpu.VMEM((1, Q_TILE), jnp.float32),
            pltpu.VMEM((N_HEADS, K_TILE, Q_TILE), jnp.float32),
            pltpu.VMEM((N_HEADS, 1, Q_TILE), jnp.float32),
            pltpu.VMEM((N_HEADS, 1, Q_TILE), jnp.float32),
            pltpu.VMEM((N_HEADS, VT_ROWS, Q_TILE), jnp.float32),
            pltpu.VMEM((D_ATTN, Q_TILE), jnp.float32),
        ],
        compiler_params=pltpu.CompilerParams(
            dimension_semantics=("parallel", "parallel"), vmem_limit_bytes=VMEM_LIMIT),
        name="dsa_attention",
    )(qt, kv, vt, wi, ga, tri)


OUT_IN_BUFFERS = 4
OUT_OUT_BUFFERS = 3


def _outproj_kernel(yr_hbm, ya_hbm, x_hbm, mod_ref, wr_ref, wa_ref, g_ref, o_hbm,
                    yr_buf, ya_buf, x_buf, o_buf, in_sem, out_sem, *, final_norm, tiles_per_row, n_tiles):
    tm = OUT_ROW_TILE

    def in_copies(step, slot):
        b, rows = step // tiles_per_row, pl.ds((step % tiles_per_row) * tm, tm)
        return (pltpu.make_async_copy(yr_hbm.at[b, rows], yr_buf.at[slot], in_sem.at[0, slot]),
                pltpu.make_async_copy(ya_hbm.at[b, rows], ya_buf.at[slot], in_sem.at[1, slot]),
                pltpu.make_async_copy(x_hbm.at[b, rows], x_buf.at[slot], in_sem.at[2, slot]))

    def out_copy(step, slot):
        b, rows = step // tiles_per_row, pl.ds((step % tiles_per_row) * tm, tm)
        return pltpu.make_async_copy(o_buf.at[slot], o_hbm.at[b, rows], out_sem.at[slot])

    for step in range(OUT_IN_BUFFERS - 1):
        for copy in in_copies(step, step):
            copy.start()

    def body(step, _):
        slot = step % OUT_IN_BUFFERS
        for copy in in_copies(step, slot):
            copy.wait()

        @pl.when(step + OUT_IN_BUFFERS - 1 < n_tiles)
        def _():
            ahead = step + OUT_IN_BUFFERS - 1
            for copy in in_copies(ahead, ahead % OUT_IN_BUFFERS):
                copy.start()

        o_slot = step % OUT_OUT_BUFFERS

        @pl.when(step >= OUT_OUT_BUFFERS)
        def _():
            out_copy(step - OUT_OUT_BUFFERS, o_slot).wait()

        y = jnp.dot(yr_buf[slot], wr_ref[...], preferred_element_type=jnp.float32)
        y = y + jnp.dot(ya_buf[slot], wa_ref[...], preferred_element_type=jnp.float32)
        x_new = x_buf[slot] + mod_ref[step // tiles_per_row, 2:3, :] * y
        if final_norm:
            x_new = _rms(x_new, g_ref[...])
        o_buf[o_slot] = x_new
        out_copy(step, o_slot).start()
        return 0

    lax.fori_loop(0, n_tiles, body, 0)
    for step in range(n_tiles - OUT_OUT_BUFFERS, n_tiles):
        out_copy(step, step % OUT_OUT_BUFFERS).wait()


def _out_projection(yr, ya, x, mod_l, w_r, w_a, final_g, final_norm):
    batch, seq, _ = x.shape
    tiles_per_row = seq // OUT_ROW_TILE
    n_tiles = batch * tiles_per_row
    assert n_tiles >= max(OUT_IN_BUFFERS, OUT_OUT_BUFFERS)
    hbm = pl.BlockSpec(memory_space=pl.ANY)
    vmem = pl.BlockSpec(memory_space=pltpu.VMEM)
    return pl.pallas_call(
        functools.partial(_outproj_kernel, final_norm=final_norm, tiles_per_row=tiles_per_row, n_tiles=n_tiles),
        in_specs=[hbm, hbm, hbm, vmem, vmem, vmem, vmem],
        out_specs=hbm,
        out_shape=jax.ShapeDtypeStruct((batch, seq, D_MODEL), jnp.float32),
        scratch_shapes=[
            pltpu.VMEM((OUT_IN_BUFFERS, OUT_ROW_TILE, D_RNN), jnp.bfloat16),
            pltpu.VMEM((OUT_IN_BUFFERS, OUT_ROW_TILE, D_ATTN), jnp.bfloat16),
            pltpu.VMEM((OUT_IN_BUFFERS, OUT_ROW_TILE, D_MODEL), jnp.float32),
            pltpu.VMEM((OUT_OUT_BUFFERS, OUT_ROW_TILE, D_MODEL), jnp.float32),
            pltpu.SemaphoreType.DMA((3, OUT_IN_BUFFERS)),
            pltpu.SemaphoreType.DMA((OUT_OUT_BUFFERS,)),
        ],
        compiler_params=pltpu.CompilerParams(vmem_limit_bytes=VMEM_LIMIT),
        name="out_proj",
    )(yr, ya, x, mod_l, w_r, w_a, final_g.reshape(1, D_MODEL))


def _pad_w_in(w):
    return jnp.pad(w.astype(jnp.bfloat16), ((0, 0), (0, 0), (0, Z_COLS - D_IN)))


def _block_diag(w):
    n, c, d = w.shape
    eye = jnp.eye(n, dtype=w.dtype)
    return (eye[:, None, :, None] * w[:, :, None, :]).reshape(n * c, n * d)


def _gate_slabs(w_x, w_a):
    per_slab = 128 // RNN_BLOCK
    slabs = [jnp.concatenate([_block_diag(w_x[k:k + per_slab]), _block_diag(w_a[k:k + per_slab])], axis=-1)
             for k in range(0, N_RNN_BLOCKS, per_slab)]
    return jnp.stack(slabs).astype(jnp.bfloat16)


def kernel(x, c, norm_g, ada_w, ada_b, w_in, conv_w, conv_b, lru_wx, lru_bx, lru_wa, lru_ba, lru_a, w_out, final_g):
    depth = w_in.shape[0]
    mod = _modulation(c, ada_w, ada_b)
    idx = jnp.arange(K_TILE)
    tri = (idx[None, :] < idx[:, None]).astype(jnp.bfloat16)
    w_pad = _pad_w_in(w_in)
    for l in range(depth):
        w_gates = _gate_slabs(lru_wx[l], lru_wa[l])
        b_gates = jnp.concatenate([lru_bx[l], lru_ba[l]]).reshape(1, 2 * D_RNN)
        w_o = w_out[l].astype(jnp.bfloat16)
        y_r, qt, ga, kv, wi, vt = _in_projection(x, mod[l], norm_g[l], w_pad[l], conv_w[l], conv_b[l],
                                                 w_gates, b_gates, lru_a[l])
        y_a = _dsa_attention(qt, kv, vt, wi, ga, tri)
        x = _out_projection(y_r, y_a, x, mod[l], w_o[:D_RNN], w_o[D_RNN:], final_g, l == depth - 1)
    return x
```

```python
import functools

import jax
import jax.numpy as jnp
from jax import lax
from jax.experimental import pallas as pl
from jax.experimental.pallas import tpu as pltpu

D_MODEL = 1024
D_RNN = 512
N_RNN_BLOCKS = 8
RNN_BLOCK = D_RNN // N_RNN_BLOCKS
CONV_WIDTH = 4
LRU_C = 8.0
N_HEADS = 8
HEAD_DIM = 64
D_ATTN = N_HEADS * HEAD_DIM
N_IDX_HEADS = 8
IDX_DIM = 64
MAX_TOPK = 256
EPS = 1e-6
LOG2_E = 1.4426950408889634

Z_RNN = 0
Z_Q = 1024
Z_KV = 1536
Z_GA = 1664
Z_QI = 2176
Z_KI = 2688
Z_COLS = 2816
D_IN = 2760
WI_LANE = 64
KI_LANE = 128
BF16_STEP = 2.0 ** -7

ROW_TILE = 512
OUT_ROW_TILE = 512
LRU_STEPS = ROW_TILE // 8
Q_TILE = 256
K_TILE = 512
VT_ROWS = 80
MASKED = -1e30
COARSE_PROBES = 9
PROBES_PER_ROUND = 7
LATER_ROUND_PROBES = 3
PLAIN_BISECT_ROUNDS = 5
PARTIAL_ROWS = 32
SWEEP_TILE = 256
TAIL_KEYS = K_TILE - Q_TILE
assert K_TILE == 2 * Q_TILE
assert MAX_TOPK < 2 * Q_TILE
VMEM_LIMIT = 48 * 1024 * 1024


def _sigmoid(x):
    return 0.5 * jnp.tanh(0.5 * x) + 0.5


def _silu(x):
    return x * _sigmoid(x)


def _rms(x, g):
    return x * lax.rsqrt(jnp.mean(x * x, axis=-1, keepdims=True) + EPS) * g


def _reduce_keys(x, op):
    keys, queries = x.shape
    part = op(x.reshape(keys // PARTIAL_ROWS, PARTIAL_ROWS, queries), axis=0)
    return op(part, axis=0, keepdims=True)


def _mod_kernel(c_ref, w_ref, b_ref, o_ref):
    c_act = _silu(c_ref[...])
    w = w_ref[...]
    c_hi = c_act.astype(jnp.bfloat16)
    c_lo = (c_act - c_hi.astype(jnp.float32)).astype(jnp.bfloat16)
    w_hi = w.astype(jnp.bfloat16)
    w_lo = (w - w_hi.astype(jnp.float32)).astype(jnp.bfloat16)
    dot = functools.partial(jnp.dot, preferred_element_type=jnp.float32)
    o_ref[...] = dot(c_hi, w_hi) + (dot(c_lo, w_hi) + dot(c_hi, w_lo)) + b_ref[...]


def _modulation(c, ada_w, ada_b):
    depth = ada_w.shape[0]
    batch = c.shape[0]
    out = pl.pallas_call(
        _mod_kernel,
        grid=(depth, 3),
        in_specs=[
            pl.BlockSpec((batch, D_MODEL), lambda l, j: (0, 0)),
            pl.BlockSpec((None, D_MODEL, D_MODEL), lambda l, j: (l, 0, j)),
            pl.BlockSpec((None, None, 1, D_MODEL), lambda l, j: (l, j, 0, 0)),
        ],
        out_specs=pl.BlockSpec((None, None, batch, D_MODEL), lambda l, j: (l, j, 0, 0)),
        out_shape=jax.ShapeDtypeStruct((depth, 3, batch, D_MODEL), jnp.float32),
        compiler_params=pltpu.CompilerParams(vmem_limit_bytes=VMEM_LIMIT),
        name="adaln_mod",
    )(c, ada_w, ada_b.reshape(depth, 3, 1, D_MODEL))
    return out.transpose(0, 2, 1, 3)


def _inproj_kernel(x_ref, mod_ref, g_ref, w_ref, cw_ref, cb_ref, wg_ref, bg_ref, ap_ref,
                   yr_ref, qt_ref, ga_ref, kv_ref, wi_ref, vt_ref, perm_ref, tail_ref, h_ref):
    @pl.when(pl.program_id(1) == 0)
    def _():
        tail_ref[...] = jnp.zeros((8, D_RNN), jnp.float32)
        h_ref[...] = jnp.zeros((1, D_RNN), jnp.float32)

    x = x_ref[...]
    shift = mod_ref[0:1, :]
    scale = mod_ref[1:2, :]
    h = _rms(x, g_ref[...]) * (1.0 + scale) + shift
    h = h.astype(jnp.bfloat16)
    z_rnn = jnp.dot(h, w_ref[:, Z_RNN:Z_Q], preferred_element_type=jnp.float32)
    y_r = _rg_lru_tile(z_rnn[:, 0:D_RNN], z_rnn[:, D_RNN:2 * D_RNN],
                       cw_ref, cb_ref, wg_ref, bg_ref, ap_ref, perm_ref, tail_ref, h_ref)
    yr_ref[...] = y_r.astype(yr_ref.dtype)
    z = jnp.dot(h, w_ref[:, Z_Q:Z_COLS], preferred_element_type=jnp.float32)
    col = lambda start, width: z[:, start - Z_Q:start - Z_Q + width]
    for h in range(N_HEADS):
        q_h = col(Z_Q + h * HEAD_DIM, HEAD_DIM) * (HEAD_DIM ** -0.5 * LOG2_E)
        _store_query_tiles(qt_ref, h, q_h.T.astype(jnp.bfloat16))
    for h in range(N_IDX_HEADS):
        _store_query_tiles(qt_ref, N_HEADS + h, col(Z_QI + h * IDX_DIM, IDX_DIM).T.astype(jnp.bfloat16))
    ga_ref[...] = col(Z_GA, D_ATTN)
    kv_ref[...] = jnp.concatenate([col(Z_KV, 128), col(Z_KI, 128)], axis=-1).astype(jnp.bfloat16)
    wi_ref[...] = col(Z_KI, 128)
    pad_rows = lax.broadcasted_iota(jnp.int32, (VT_ROWS - HEAD_DIM, K_TILE), 0)
    for j in range(ROW_TILE // K_TILE):
        v_t = col(Z_KV + HEAD_DIM, HEAD_DIM)[j * K_TILE:(j + 1) * K_TILE, :].T
        vt_ref[j, 0:HEAD_DIM, :] = v_t.astype(jnp.bfloat16)
        vt_ref[j, HEAD_DIM:VT_ROWS, :] = jnp.where(pad_rows == 0, 1.0, 0.0).astype(jnp.bfloat16)


def _store_query_tiles(qt_ref, head, q_t):
    for j in range(ROW_TILE // Q_TILE):
        qt_ref[j, head] = q_t[:, j * Q_TILE:(j + 1) * Q_TILE]


def _in_projection(x, mod_l, norm_g, w_pad, conv_w, conv_b, w_gates, b_gates, a_param):
    batch, seq, _ = x.shape
    row = lambda b, i: (b, i, 0)
    const = lambda b, i: (0, 0)
    return pl.pallas_call(
        _inproj_kernel,
        grid=(batch, seq // ROW_TILE),
        in_specs=[
            pl.BlockSpec((None, ROW_TILE, D_MODEL), row),
            pl.BlockSpec((None, 3, D_MODEL), lambda b, i: (b, 0, 0)),
            pl.BlockSpec((1, D_MODEL), const),
            pl.BlockSpec((D_MODEL, Z_COLS), const),
            pl.BlockSpec((CONV_WIDTH, D_RNN), const),
            pl.BlockSpec((1, D_RNN), const),
            pl.BlockSpec((D_RNN // 128, 128, 256), lambda b, i: (0, 0, 0)),
            pl.BlockSpec((1, 2 * D_RNN), const),
            pl.BlockSpec((1, D_RNN), const),
        ],
        out_specs=[
            pl.BlockSpec((None, ROW_TILE, D_RNN), row),
            pl.BlockSpec((None, ROW_TILE // Q_TILE, N_HEADS + N_IDX_HEADS, HEAD_DIM, Q_TILE),
                         lambda b, i: (b, i, 0, 0, 0)),
            pl.BlockSpec((None, ROW_TILE, 512), row),
            pl.BlockSpec((None, ROW_TILE, 256), row),
            pl.BlockSpec((None, ROW_TILE, 128), row),
            pl.BlockSpec((None, ROW_TILE // K_TILE, VT_ROWS, K_TILE), lambda b, i: (b, i, 0, 0)),
        ],
        out_shape=[
            jax.ShapeDtypeStruct((batch, seq, D_RNN), jnp.bfloat16),
            jax.ShapeDtypeStruct((batch, seq // Q_TILE, N_HEADS + N_IDX_HEADS, HEAD_DIM, Q_TILE), jnp.bfloat16),
            jax.ShapeDtypeStruct((batch, seq, 512), jnp.float32),
            jax.ShapeDtypeStruct((batch, seq, 256), jnp.bfloat16),
            jax.ShapeDtypeStruct((batch, seq, 128), jnp.float32),
            jax.ShapeDtypeStruct((batch, seq // K_TILE, VT_ROWS, K_TILE), jnp.bfloat16),
        ],
        scratch_shapes=[
            pltpu.VMEM((D_RNN // 128, ROW_TILE, 128), jnp.float32),
            pltpu.VMEM((8, D_RNN), jnp.float32),
            pltpu.VMEM((1, D_RNN), jnp.float32),
        ],
        compiler_params=pltpu.CompilerParams(
            dimension_semantics=("parallel", "arbitrary"), vmem_limit_bytes=VMEM_LIMIT),
        name="in_proj_rg_lru",
    )(x, mod_l, norm_g.reshape(1, D_MODEL), w_pad, conv_w, conv_b.reshape(1, D_RNN),
      w_gates, b_gates, a_param.reshape(1, D_RNN))


def _rg_lru_tile(xr, gr, cw_ref, cb_ref, wg_ref, bg_ref, ap_ref, perm_ref, tail_ref, h_ref):
    ts, steps, slabs = ROW_TILE, LRU_STEPS, D_RNN // 128
    lanes = lambda k: slice(k * 128, (k + 1) * 128)

    for s in range(8):
        for k in range(slabs):
            perm_ref[k, pl.ds(s, steps, stride=8), :] = xr[s * steps:(s + 1) * steps, lanes(k)]
    x = jnp.concatenate([perm_ref[k] for k in range(slabs)], axis=-1).reshape(steps, 8, D_RNN)

    first = lax.broadcasted_iota(jnp.int32, (8, D_RNN), 0) == 0
    before = [jnp.where(first, tail_ref[8 - m:9 - m, :], pltpu.roll(x[steps - m], 1, 0))
              for m in range(CONV_WIDTH - 1, 0, -1)]
    x_ext = jnp.concatenate([jnp.stack(before), x], axis=0)
    xc = cb_ref[...] + cw_ref[CONV_WIDTH - 1:CONV_WIDTH, :] * x
    for k in range(CONV_WIDTH - 1):
        xc = xc + cw_ref[k:k + 1, :] * x_ext[k:k + steps]
    tail_ref[...] = xr[ts - 8:ts, :]

    xc_bf = xc.reshape(ts, D_RNN).astype(jnp.bfloat16)
    slab_gates = [jnp.dot(xc_bf[:, lanes(k)], wg_ref[k], preferred_element_type=jnp.float32) for k in range(slabs)]
    gates_x = jnp.concatenate([g[:, 0:128] for g in slab_gates], axis=-1) + bg_ref[:, 0:D_RNN]
    gates_a = jnp.concatenate([g[:, 128:256] for g in slab_gates], axis=-1) + bg_ref[:, D_RNN:2 * D_RNN]
    gate_x = _sigmoid(gates_x.reshape(steps, 8, D_RNN))
    gate_a = _sigmoid(gates_a.reshape(steps, 8, D_RNN))
    neg_ap = -ap_ref[...]
    softplus = jnp.maximum(neg_ap, 0.0) + jnp.log(1.0 + jnp.exp(-jnp.abs(neg_ap)))
    log_a = (-LRU_C) * gate_a * softplus
    a = jnp.exp(log_a)
    gap = 1.0 - a * a
    mult = jnp.where(gap > 0.0, gap * lax.rsqrt(gap), 0.0)
    u = mult * gate_x * xc

    h_run, a_run = u[0], a[0]
    h_loc, a_cum = [h_run], [a_run]
    for j in range(1, steps):
        h_run = a[j] * h_run + u[j]
        a_run = a[j] * a_run
        h_loc.append(h_run)
        a_cum.append(a_run)
    state = h_ref[...]
    entering = []
    for s in range(8):
        entering.append(state)
        state = h_run[s:s + 1, :] + a_run[s:s + 1, :] * state
    h_ref[...] = state
    entering = jnp.concatenate(entering, axis=0)
    h = jnp.stack([h_loc[j] + a_cum[j] * entering for j in range(steps)]).reshape(ts, D_RNN)

    for k in range(slabs):
        perm_ref[k] = h[:, lanes(k)]
    h = jnp.concatenate(
        [jnp.concatenate([perm_ref[k, pl.ds(s, steps, stride=8), :] for k in range(slabs)], axis=-1)
         for s in range(8)], axis=0)
    return h * _silu(gr)


def _attn_kernel(qt_ref, kv_ref, vt_ref, wi_ref, ga_ref, tri_ref, y_ref,
                 sc_ref, sc16_ref, bias_ref, eqb_ref, s_ref, m_ref, mnew_ref, acc_ref, out_ref):
    tq, tk = Q_TILE, K_TILE
    topk = float(MAX_TOPK)
    qb = pl.program_id(1)
    n_keys = (qb + 1) * tq
    n_full = n_keys // tk
    has_tail = n_keys - n_full * tk > 0

    w_t = wi_ref[...].T[WI_LANE:WI_LANE + N_IDX_HEADS, :] * ((IDX_DIM ** -0.5) * (N_IDX_HEADS ** -0.5))

    def score_chunk(c, carry, size, diagonal):
        rmax, rmin = carry
        k0 = pl.multiple_of(c * tk, tk)
        ki = kv_ref[pl.ds(k0, size), KI_LANE:KI_LANE + IDX_DIM]
        score = jnp.zeros((size, tq), jnp.float32)
        for h in range(N_IDX_HEADS):
            logits = jnp.dot(ki, qt_ref[N_HEADS + h], preferred_element_type=jnp.float32)
            score = score + jnp.maximum(logits, 0.0) * w_t[h:h + 1, :]
        if diagonal:
            key_pos = k0 + lax.broadcasted_iota(jnp.int32, (size, tq), 0)
            causal = key_pos <= qb * tq + lax.broadcasted_iota(jnp.int32, (size, tq), 1)
            low, high = jnp.where(causal, score, -jnp.inf), jnp.where(causal, score, jnp.inf)
        else:
            low = high = score
        sc_ref[pl.ds(k0, size), :] = low
        sc16_ref[pl.ds(k0, size), :] = low.astype(jnp.bfloat16)
        return jnp.maximum(rmax, _reduce_keys(low, jnp.max)), jnp.minimum(rmin, _reduce_keys(high, jnp.min))

    n_before = jnp.where(has_tail, n_full, n_full - 1)
    extremes = lax.fori_loop(
        0, n_before, functools.partial(score_chunk, size=tk, diagonal=False),
        (jnp.full((1, tq), -jnp.inf, jnp.float32), jnp.full((1, tq), jnp.inf, jnp.float32)))

    def last_chunk(e):
        return lax.cond(has_tail,
                        lambda e: score_chunk(n_full, e, TAIL_KEYS, True),
                        lambda e: score_chunk(n_full - 1, e, tk, True), e)

    def no_scores(e):
        key_pos = lax.broadcasted_iota(jnp.int32, (tq, tq), 0)
        causal = key_pos <= lax.broadcasted_iota(jnp.int32, (tq, tq), 1)
        sc_ref[0:tq, :] = jnp.where(causal, 0.0, -jnp.inf)
        return jnp.zeros((1, tq), jnp.float32), jnp.zeros((1, tq), jnp.float32)

    all_selected = n_keys <= MAX_TOPK
    rmax, rmin = lax.cond(all_selected, no_scores, last_chunk, extremes)

    n_sweep_tiles = jnp.where(all_selected, 0, n_keys // SWEEP_TILE)

    def sweep(mid, want):
        kinds = {"gt": (jnp.sum, 0.0), "ge": (jnp.sum, 0.0), "above": (jnp.min, jnp.inf), "below": (jnp.max, -jnp.inf)}

        def body(c, carry):
            k0 = pl.multiple_of(c * SWEEP_TILE, SWEEP_TILE)
            s = sc_ref[pl.ds(k0, SWEEP_TILE), :].reshape(SWEEP_TILE // PARTIAL_ROWS, PARTIAL_ROWS, tq)
            gt = s > mid
            terms = {"gt": lambda: jnp.where(gt, 1.0, 0.0), "ge": lambda: jnp.where(s >= mid, 1.0, 0.0),
                     "above": lambda: jnp.where(gt, s, jnp.inf), "below": lambda: jnp.where(gt, -jnp.inf, s)}
            out = []
            for name, acc in zip(want, carry):
                op = kinds[name][0]
                part = op(terms[name](), axis=0)
                out.append(acc + part if op is jnp.sum else
                           (jnp.minimum(acc, part) if op is jnp.min else jnp.maximum(acc, part)))
            return tuple(out)

        init = tuple(jnp.full((PARTIAL_ROWS, tq), kinds[name][1], jnp.float32) for name in want)
        res = lax.fori_loop(0, n_sweep_tiles, body, init)
        return [kinds[name][0](r, axis=0, keepdims=True) for name, r in zip(want, res)]

    def update(state, mid, cnt, tie, new_lo, new_hi):
        lo, hi, chi, act = state
        on = act > 0.5
        fin = jnp.logical_and(on, jnp.logical_or(cnt == topk, tie))
        go_on = jnp.logical_and(on, jnp.logical_not(fin))
        up = jnp.logical_and(go_on, cnt > topk)
        dn = jnp.logical_and(go_on, cnt < topk)
        lo = jnp.where(fin, mid, jnp.where(up, new_lo, lo))
        hi = jnp.where(fin, mid, jnp.where(dn, new_hi, hi))
        chi = jnp.where(jnp.logical_or(fin, dn), cnt, chi)
        act = jnp.where(jnp.logical_and(go_on, lo < hi), 1.0, 0.0)
        return lo, hi, chi, act

    def midpoint(state):
        return 0.5 * state[0] + 0.5 * state[1]

    n_valid = (qb * tq + 1 + lax.broadcasted_iota(jnp.int32, (1, tq), 1)).astype(jnp.float32)
    few = n_valid <= topk
    state = (rmin,
             jnp.where(few, -jnp.inf, rmax),
             jnp.where(few, topk, 0.0),
             jnp.where(jnp.logical_or(few, rmin >= rmax), 0.0, 1.0))

    def coarse_count(t16, also_ge=False):
        one, zero = jnp.ones((), jnp.bfloat16), jnp.zeros((), jnp.bfloat16)

        def tile_sum(hits):
            parts = [hits[g] for g in range(SWEEP_TILE // PARTIAL_ROWS)]
            while len(parts) > 1:
                parts = [parts[i] + parts[i + 1] for i in range(0, len(parts), 2)]
            return parts[0]

        def body(c, accs):
            k0 = pl.multiple_of(c * SWEEP_TILE, SWEEP_TILE)
            s = sc16_ref[pl.ds(k0, SWEEP_TILE), :].reshape(SWEEP_TILE // PARTIAL_ROWS, PARTIAL_ROWS, tq)
            out = [accs[0] + tile_sum(jnp.where(s > t16, one, zero))]
            if also_ge:
                out.append(accs[1] + tile_sum(jnp.where(s >= t16, one, zero)))
            return tuple(out)

        init = (jnp.zeros((PARTIAL_ROWS, tq), jnp.bfloat16),) * (2 if also_ge else 1)
        accs = lax.fori_loop(0, n_sweep_tiles, body, init)
        return [jnp.sum(a.astype(jnp.float32), axis=0, keepdims=True) for a in accs]

    lo, hi, chi, act = state
    on = act > 0.5
    pos, nonneg = coarse_count(jnp.zeros((1, tq), jnp.bfloat16), also_ge=True)
    at_zero = jnp.logical_and(on, jnp.logical_and(pos <= topk, nonneg >= topk))
    lo = jnp.where(at_zero, 0.0, jnp.where(jnp.logical_and(on, pos > topk), jnp.maximum(lo, 0.0), lo))
    hi = jnp.where(at_zero, 0.0, jnp.where(jnp.logical_and(on, nonneg < topk), jnp.minimum(hi, 0.0), hi))
    chi = jnp.where(at_zero, pos, chi)
    act = jnp.where(at_zero, 0.0, act)

    for _ in range(COARSE_PROBES):
        t16 = (0.5 * lo + 0.5 * hi).astype(jnp.bfloat16)
        t = t16.astype(jnp.float32)
        (cnt16,) = coarse_count(t16)
        on = act > 0.5
        lo = jnp.where(jnp.logical_and(on, cnt16 >= topk), jnp.maximum(lo, t), lo)
        hi = jnp.where(jnp.logical_and(on, cnt16 < topk), jnp.minimum(hi, t + jnp.abs(t) * BF16_STEP + 1e-30), hi)
    (cnt,) = sweep(hi, ("gt",))
    state = update((lo, hi, chi, act), hi, cnt, False, hi, hi)

    def two_largest_below(bound):
        groups = SWEEP_TILE // PARTIAL_ROWS

        def merge(a1, a2, b1, b2):
            return jnp.maximum(a1, b1), jnp.maximum(jnp.minimum(a1, b1), jnp.maximum(a2, b2))

        def body(c, carry):
            m1, m2 = carry
            k0 = pl.multiple_of(c * SWEEP_TILE, SWEEP_TILE)
            s = sc_ref[pl.ds(k0, SWEEP_TILE), :].reshape(groups, PARTIAL_ROWS, tq)
            x = jnp.where(s > bound, -jnp.inf, s)
            for g in range(groups):
                m1, m2 = jnp.maximum(m1, x[g]), jnp.maximum(m2, jnp.minimum(m1, x[g]))
            return m1, m2

        low = jnp.full((PARTIAL_ROWS, tq), -jnp.inf, jnp.float32)
        m1, m2 = lax.fori_loop(0, n_sweep_tiles, body, (low, low))
        rows = PARTIAL_ROWS
        while rows > 1:
            rows //= 2
            m1, m2 = merge(m1[:rows], m2[:rows], m1[rows:], m2[rows:])
        return m1, m2

    def probe_round(state, probes):
        for _ in range(probes):
            mid = midpoint(state)
            (cnt,) = sweep(mid, ("gt",))
            state = update(state, mid, cnt, False, mid, mid)
        lo, hi, chi, act = state
        largest, second = two_largest_below(hi)
        one_short = jnp.logical_and(act > 0.5, topk - chi == 1.0)
        separated = jnp.logical_and(one_short, second < largest)
        new_thr = jnp.where(separated, second, largest)
        return (jnp.where(one_short, new_thr, lo), jnp.where(one_short, new_thr, hi),
                jnp.where(separated, topk, chi), jnp.where(one_short, 0.0, act))

    def flags(state):
        return jnp.max(state[3]), jnp.max(topk - state[2])

    state = probe_round(state, PROBES_PER_ROUND)

    def more_cond(st):
        return jnp.logical_and(st[1] > 0.5, st[0] < PLAIN_BISECT_ROUNDS)

    def more_body(st):
        state = probe_round(st[3:], LATER_ROUND_PROBES)
        return (st[0] + 1,) + flags(state) + state

    st = lax.while_loop(more_cond, more_body, (jnp.int32(1),) + flags(state) + state)

    def snap_cond(st):
        return jnp.logical_and(st[1] > 0.5, st[0] < 4096)

    def snap_body(st):
        state = st[3:]
        mid = midpoint(state)
        mid = jnp.where(mid >= state[1], state[0], mid)
        cnt, above, below = sweep(mid, ("gt", "above", "below"))
        state = update(state, mid, cnt, False, above, below)
        return (st[0] + 1,) + flags(state) + state

    st = lax.while_loop(snap_cond, snap_body, (jnp.int32(0),) + st[1:])
    thr = st[4]
    need = topk - st[5]
    any_tie = st[2] > 0.5

    m_ref[...] = jnp.full(m_ref.shape, MASKED, jnp.float32)
    acc_ref[...] = jnp.zeros(acc_ref.shape, jnp.float32)
    eqb_ref[...] = jnp.zeros(eqb_ref.shape, jnp.float32)

    def attend_chunk(c, size):
        k0 = pl.multiple_of(c * tk, tk)
        s_idx = sc_ref[pl.ds(k0, size), :]
        bias_ref[0:size, :] = jnp.where(s_idx > thr, 0.0, MASKED)

        @pl.when(any_tie)
        def _():
            eq = s_idx == thr
            eq_f = jnp.where(eq, 1.0, 0.0)
            rank = jnp.dot(tri_ref[0:size, 0:size], eq_f.astype(jnp.bfloat16),
                           preferred_element_type=jnp.float32) + eqb_ref[...]
            tie_bias = jnp.where(jnp.logical_and(eq, rank < need), 0.0, MASKED)
            bias_ref[0:size, :] = jnp.where(s_idx > thr, 0.0, tie_bias)
            eqb_ref[...] = eqb_ref[...] + _reduce_keys(eq_f, jnp.sum)

        kc = kv_ref[pl.ds(k0, size), 0:HEAD_DIM]
        vt = vt_ref[c, :, 0:size]

        def logits(h):
            s = jnp.dot(kc, qt_ref[h], preferred_element_type=jnp.float32) + bias_ref[0:size, :]
            s_ref[h, 0:size, :] = s
            mnew_ref[h] = jnp.maximum(m_ref[h], _reduce_keys(s, jnp.max))

        def weigh(h):
            m_new = mnew_ref[h]
            alpha = jnp.exp2(m_ref[h] - m_new)
            p = jnp.exp2(s_ref[h, 0:size, :] - m_new)
            acc_ref[h] = alpha * acc_ref[h] + jnp.dot(
                vt, p.astype(jnp.bfloat16), preferred_element_type=jnp.float32)
            m_ref[h] = m_new

        for h in range(N_HEADS):
            logits(h)
        for h in range(N_HEADS):
            weigh(h)

    def attend_full(c, _):
        attend_chunk(c, tk)
        return 0

    lax.fori_loop(0, n_full, attend_full, 0)

    @pl.when(has_tail)
    def _():
        attend_chunk(n_full, TAIL_KEYS)

    for h in range(N_HEADS):
        rows = slice(h * HEAD_DIM, (h + 1) * HEAD_DIM)
        out_ref[rows, :] = acc_ref[h, 0:HEAD_DIM, :] / acc_ref[h, HEAD_DIM:HEAD_DIM + 1, :]
    y = out_ref[...].T * _silu(ga_ref[...])
    y_ref[...] = y.astype(y_ref.dtype)


def _dsa_attention(qt, kv, vt, wi, ga, tri):
    batch, seq, _ = kv.shape
    assert seq // PARTIAL_ROWS <= 256, "bf16 hit counters are exact only up to 256 per slot"
    row = lambda b, i: (b, i, 0)
    return pl.pallas_call(
        _attn_kernel,
        grid=(batch, seq // Q_TILE),
        in_specs=[
            pl.BlockSpec((None, None, N_HEADS + N_IDX_HEADS, HEAD_DIM, Q_TILE), lambda b, i: (b, i, 0, 0, 0)),
            pl.BlockSpec((None, seq, 256), lambda b, i: (b, 0, 0)),
            pl.BlockSpec((None, seq // K_TILE, VT_ROWS, K_TILE), lambda b, i: (b, 0, 0, 0)),
            pl.BlockSpec((None, Q_TILE, 128), row),
            pl.BlockSpec((None, Q_TILE, D_ATTN), row),
            pl.BlockSpec((K_TILE, K_TILE), lambda b, i: (0, 0)),
        ],
        out_specs=pl.BlockSpec((None, Q_TILE, D_ATTN), row),
        out_shape=jax.ShapeDtypeStruct((batch, seq, D_ATTN), jnp.bfloat16),
        scratch_shapes=[
            pltpu.VMEM((seq, Q_TILE), jnp.float32),
            pltpu.VMEM((seq, Q_TILE), jnp.bfloat16),
            pltpu.VMEM((K_TILE, Q_TILE), jnp.float32),
            pltpu.VMEM((1, Q_TILE), jnp.float32),
            pltpu.VMEM((N_HEADS, K_TILE, Q_TILE), jnp.float32),
            pltpu.VMEM((N_HEADS, 1, Q_TILE), jnp.float32),
            pltpu.VMEM((N_HEADS, 1, Q_TILE), jnp.float32),
            pltpu.VMEM((N_HEADS, VT_ROWS, Q_TILE), jnp.float32),
            pltpu.VMEM((D_ATTN, Q_TILE), jnp.float32),
        ],
        compiler_params=pltpu.CompilerParams(
            dimension_semantics=("parallel", "parallel"), vmem_limit_bytes=VMEM_LIMIT),
        name="dsa_attention",
    )(qt, kv, vt, wi, ga, tri)


OUT_IN_BUFFERS = 4
OUT_OUT_BUFFERS = 3


def _outproj_kernel(yr_hbm, ya_hbm, x_hbm, mod_ref, wr_ref, wa_ref, g_ref, o_hbm,
                    yr_buf, ya_buf, x_buf, o_buf, in_sem, out_sem, *, final_norm, tiles_per_row, n_tiles):
    tm = OUT_ROW_TILE

    def in_copies(step, slot):
        b, rows = step // tiles_per_row, pl.ds((step % tiles_per_row) * tm, tm)
        return (pltpu.make_async_copy(yr_hbm.at[b, rows], yr_buf.at[slot], in_sem.at[0, slot]),
                pltpu.make_async_copy(ya_hbm.at[b, rows], ya_buf.at[slot], in_sem.at[1, slot]),
                pltpu.make_async_copy(x_hbm.at[b, rows], x_buf.at[slot], in_sem.at[2, slot]))

    def out_copy(step, slot):
        b, rows = step // tiles_per_row, pl.ds((step % tiles_per_row) * tm, tm)
        return pltpu.make_async_copy(o_buf.at[slot], o_hbm.at[b, rows], out_sem.at[slot])

    for step in range(OUT_IN_BUFFERS - 1):
        for copy in in_copies(step, step):
            copy.start()

    def body(step, _):
        slot = step % OUT_IN_BUFFERS
        for copy in in_copies(step, slot):
            copy.wait()

        @pl.when(step + OUT_IN_BUFFERS - 1 < n_tiles)
        def _():
            ahead = step + OUT_IN_BUFFERS - 1
            for copy in in_copies(ahead, ahead % OUT_IN_BUFFERS):
                copy.start()

        o_slot = step % OUT_OUT_BUFFERS

        @pl.when(step >= OUT_OUT_BUFFERS)
        def _():
            out_copy(step - OUT_OUT_BUFFERS, o_slot).wait()

        y = jnp.dot(yr_buf[slot], wr_ref[...], preferred_element_type=jnp.float32)
        y = y + jnp.dot(ya_buf[slot], wa_ref[...], preferred_element_type=jnp.float32)
        x_new = x_buf[slot] + mod_ref[step // tiles_per_row, 2:3, :] * y
        if final_norm:
            x_new = _rms(x_new, g_ref[...])
        o_buf[o_slot] = x_new
        out_copy(step, o_slot).start()
        return 0

    lax.fori_loop(0, n_tiles, body, 0)
    for step in range(n_tiles - OUT_OUT_BUFFERS, n_tiles):
        out_copy(step, step % OUT_OUT_BUFFERS).wait()


def _out_projection(yr, ya, x, mod_l, w_r, w_a, final_g, final_norm):
    batch, seq, _ = x.shape
    tiles_per_row = seq // OUT_ROW_TILE
    n_tiles = batch * tiles_per_row
    assert n_tiles >= max(OUT_IN_BUFFERS, OUT_OUT_BUFFERS)
    hbm = pl.BlockSpec(memory_space=pl.ANY)
    vmem = pl.BlockSpec(memory_space=pltpu.VMEM)
    return pl.pallas_call(
        functools.partial(_outproj_kernel, final_norm=final_norm, tiles_per_row=tiles_per_row, n_tiles=n_tiles),
        in_specs=[hbm, hbm, hbm, vmem, vmem, vmem, vmem],
        out_specs=hbm,
        out_shape=jax.ShapeDtypeStruct((batch, seq, D_MODEL), jnp.float32),
        scratch_shapes=[
            pltpu.VMEM((OUT_IN_BUFFERS, OUT_ROW_TILE, D_RNN), jnp.bfloat16),
            pltpu.VMEM((OUT_IN_BUFFERS, OUT_ROW_TILE, D_ATTN), jnp.bfloat16),
            pltpu.VMEM((OUT_IN_BUFFERS, OUT_ROW_TILE, D_MODEL), jnp.float32),
            pltpu.VMEM((OUT_OUT_BUFFERS, OUT_ROW_TILE, D_MODEL), jnp.float32),
            pltpu.SemaphoreType.DMA((3, OUT_IN_BUFFERS)),
            pltpu.SemaphoreType.DMA((OUT_OUT_BUFFERS,)),
        ],
        compiler_params=pltpu.CompilerParams(vmem_limit_bytes=VMEM_LIMIT),
        name="out_proj",
    )(yr, ya, x, mod_l, w_r, w_a, final_g.reshape(1, D_MODEL))


def _pad_w_in(w):
    return jnp.pad(w.astype(jnp.bfloat16), ((0, 0), (0, 0), (0, Z_COLS - D_IN)))


def _block_diag(w):
    n, c, d = w.shape
    eye = jnp.eye(n, dtype=w.dtype)
    return (eye[:, None, :, None] * w[:, :, None, :]).reshape(n * c, n * d)


def _gate_slabs(w_x, w_a):
    per_slab = 128 // RNN_BLOCK
    slabs = [jnp.concatenate([_block_diag(w_x[k:k + per_slab]), _block_diag(w_a[k:k + per_slab])], axis=-1)
             for k in range(0, N_RNN_BLOCKS, per_slab)]
    return jnp.stack(slabs).astype(jnp.bfloat16)


def kernel(x, c, norm_g, ada_w, ada_b, w_in, conv_w, conv_b, lru_wx, lru_bx, lru_wa, lru_ba, lru_a, w_out, final_g):
    depth = w_in.shape[0]
    mod = _modulation(c, ada_w, ada_b)
    idx = jnp.arange(K_TILE)
    tri = (idx[None, :] < idx[:, None]).astype(jnp.bfloat16)
    w_pad = _pad_w_in(w_in)
    for l in range(depth):
        w_gates = _gate_slabs(lru_wx[l], lru_wa[l])
        b_gates = jnp.concatenate([lru_bx[l], lru_ba[l]]).reshape(1, 2 * D_RNN)
        w_o = w_out[l].astype(jnp.bfloat16)
        y_r, qt, ga, kv, wi, vt = _in_projection(x, mod[l], norm_g[l], w_pad[l], conv_w[l], conv_b[l],
                                                 w_gates, b_gates, lru_a[l])
        y_a = _dsa_attention(qt, kv, vt, wi, ga, tri)
        x = _out_projection(y_r, y_a, x, mod[l], w_o[:D_RNN], w_o[D_RNN:], final_g, l == depth - 1)
    return x
```
